```python
import math, functools
import jax, jax.numpy as jnp
from jax import lax
import numpy as np

D_MODEL = 2048
BATCH = 2
SEQ = 4096
DEPTH = 4
DEC_BATCH = 8
DEC_SEQ = 1
PAST_LEN = 16384
PAGE_SIZE = 128

N_MIXERS = 3
N_ATT_LAYERS = (DEPTH + 2) // 3
N_HG_LAYERS = (DEPTH + 1) // 3
N_SSM_LAYERS = DEPTH // 3

D_FF = 2 * D_MODEL
PLE_DIM = 256
DEEPNORM_ALPHA = (2 * DEPTH) ** 0.25
DEEPNORM_BETA = (8 * DEPTH) ** -0.25
LN_EPS = 1e-5
RMS_EPS = 1e-6

ATT_HEAD_DIM = 128
ATT_HEADS = D_MODEL // ATT_HEAD_DIM
ATT_KV_HEADS = 4
ATT_GROUP = ATT_HEADS // ATT_KV_HEADS
IDX_HEADS = 16
IDX_DIM = 64
IDX_W_SCALE = (IDX_HEADS ** -0.5) * (IDX_DIM ** -0.5)
TOPK_MAX = 256
Q_BLOCK = 128
ATT_SPLITS = [ATT_HEADS * ATT_HEAD_DIM,
              ATT_HEADS * ATT_HEAD_DIM + ATT_KV_HEADS * ATT_HEAD_DIM,
              ATT_HEADS * ATT_HEAD_DIM + 2 * ATT_KV_HEADS * ATT_HEAD_DIM,
              ATT_HEADS * ATT_HEAD_DIM + 2 * ATT_KV_HEADS * ATT_HEAD_DIM + IDX_HEADS * IDX_DIM,
              ATT_HEADS * ATT_HEAD_DIM + 2 * ATT_KV_HEADS * ATT_HEAD_DIM + IDX_HEADS * IDX_DIM + IDX_DIM]
ATT_IN = ATT_SPLITS[-1] + IDX_HEADS

HG_EXPAND = 128
HG_HEADS = D_MODEL // HG_EXPAND
HG_DK = HG_EXPAND
HG_DV = D_MODEL // HG_HEADS
HG_CHUNK = 32
HG_IN = 2 * HG_HEADS * HG_DK + 2 * HG_HEADS * HG_DV

SSM_EXPAND = 2
SSM_INNER = SSM_EXPAND * D_MODEL
SSM_HEAD_DIM = 64
SSM_HEADS = SSM_INNER // SSM_HEAD_DIM
SSM_GROUPS = 8
SSM_HPG = SSM_HEADS // SSM_GROUPS
SSM_STATE = 128
SSM_CONV = 4
SSM_CONV_CH = SSM_INNER + 2 * SSM_GROUPS * SSM_STATE
SSM_IN = SSM_INNER + SSM_CONV_CH + SSM_HEADS
SSM_CHUNK = 128

kernel_name = "hybrid_dsa_hgrn2_ssd_decoder_step"


def _standardize(x):
    xf = x.astype(jnp.float32)
    mu = jnp.mean(xf, axis=-1, keepdims=True)
    xc = xf - mu
    return xc * lax.rsqrt(jnp.mean(xc * xc, axis=-1, keepdims=True) + LN_EPS)


def _layernorm(x, g, b):
    return (_standardize(x) * g + b).astype(x.dtype)


def _rmsnorm(x, g):
    xf = x.astype(jnp.float32)
    return (xf * lax.rsqrt(jnp.mean(xf * xf, axis=-1, keepdims=True) + RMS_EPS) * g).astype(x.dtype)


def _swiglu(x, w_gu, w_d):
    gate, up = jnp.split(x @ w_gu, 2, axis=-1)
    return (jax.nn.silu(gate) * up) @ w_d


def _pad_time(a, pad):
    return jnp.pad(a, [(0, 0), (0, pad)] + [(0, 0)] * (a.ndim - 2))


def _to_chunks(a, c):
    b, tp = a.shape[:2]
    return jnp.moveaxis(a.reshape(b, tp // c, c, *a.shape[2:]), 1, 0)


def _from_chunks(a, t):
    n, b, c = a.shape[:3]
    return jnp.moveaxis(a, 0, 1).reshape(b, n * c, *a.shape[3:])[:, :t]


def _dsa_project(x, w_in, ik_g):
    b, t, _ = x.shape
    q, k, v, iq, ik, iw = jnp.split(x @ w_in, ATT_SPLITS, axis=-1)
    q = q.reshape(b, t, ATT_KV_HEADS, ATT_GROUP, ATT_HEAD_DIM)
    k = k.reshape(b, t, ATT_KV_HEADS, ATT_HEAD_DIM)
    v = v.reshape(b, t, ATT_KV_HEADS, ATT_HEAD_DIM)
    iq = iq.reshape(b, t, IDX_HEADS, IDX_DIM)
    ik = (_standardize(ik) * ik_g).astype(x.dtype)
    iw = iw * IDX_W_SCALE
    return q, k, v, iq, ik, iw


def _index_scores(iq, iw, ik):
    dots = jnp.einsum('bqhd,bld->bqlh', iq, ik, preferred_element_type=jnp.float32)
    return jnp.einsum('bqlh,bqh->bql', jax.nn.relu(dots), iw.astype(jnp.float32))


def _sparse_attend(q, k_sel, v_sel, valid):
    logits = jnp.einsum('bqhgd,bqkhd->bqhgk', q, k_sel, preferred_element_type=jnp.float32) * (ATT_HEAD_DIM ** -0.5)
    logits = jnp.where(valid[:, :, None, None, :], logits, -jnp.inf)
    probs = jax.nn.softmax(logits, axis=-1).astype(v_sel.dtype)
    o = jnp.einsum('bqhgk,bqkhd->bqhgd', probs, v_sel)
    return o.reshape(q.shape[0], q.shape[1], ATT_HEADS * ATT_HEAD_DIM)


def dsa_prompt(x, w_in, ik_g, w_o):
    b, s, _ = x.shape
    q, k, v, iq, ik, iw = _dsa_project(x, w_in, ik_g)
    topk = min(TOPK_MAX, s // 4)
    gather = jax.vmap(lambda a, i: a[i])
    key_pos = jnp.arange(s)

    def block(bi):
        t0 = bi * Q_BLOCK
        sl = lambda a: lax.dynamic_slice_in_dim(a, t0, Q_BLOCK, axis=1)
        pos_q = t0 + jnp.arange(Q_BLOCK)
        causal = key_pos[None, :] <= pos_q[:, None]
        scores = jnp.where(causal[None], _index_scores(sl(iq), sl(iw), ik), -jnp.inf)
        _, idx = lax.top_k(scores, topk)
        valid = idx <= pos_q[None, :, None]
        return _sparse_attend(sl(q), gather(k, idx), gather(v, idx), valid)

    o = lax.map(block, jnp.arange(s // Q_BLOCK))
    o = jnp.moveaxis(o, 0, 1).reshape(b, s, ATT_HEADS * ATT_HEAD_DIM)
    return o @ w_o, (k, v, ik)


def dsa_sample(x, cache_k, cache_v, cache_ik, page_table, w_in, ik_g, w_o):
    bd, t, _ = x.shape
    n_pages = page_table.shape[1]
    past = n_pages * PAGE_SIZE
    total = past + t
    q, k, v, iq, ik, iw = _dsa_project(x, w_in, ik_g)
    ik_past = cache_ik[page_table].reshape(bd, past, IDX_DIM)
    ik_all = jnp.concatenate([ik_past.astype(ik.dtype), ik], axis=1)
    pos_q = past + jnp.arange(t)
    causal = jnp.arange(total)[None, :] <= pos_q[:, None]
    scores = jnp.where(causal[None], _index_scores(iq, iw, ik_all), -jnp.inf)
    _, idx = lax.top_k(scores, min(TOPK_MAX, total // 4))
    valid = idx <= pos_q[None, :, None]
    in_past = idx < past
    page = jnp.minimum(idx // PAGE_SIZE, n_pages - 1)
    phys = jax.vmap(lambda pt, pg: pt[pg])(page_table, page)
    off = idx % PAGE_SIZE
    new_i = jnp.clip(idx - past, 0, t - 1)
    gather_new = jax.vmap(lambda a, i: a[i])

    def select(pool, new):
        return jnp.where(in_past[..., None, None], pool[phys, off].astype(new.dtype), gather_new(new, new_i))

    o = _sparse_attend(q, select(cache_k, k), select(cache_v, v), valid)
    return o @ w_o, (k, v, ik)


def _gla_chunked(q, k, v, log_f, s0, chunk):
    t = q.shape[1]
    c = min(chunk, t)
    pad = (-t) % c
    q, k, v, log_f = (_to_chunks(_pad_time(a, pad), c) for a in (q, k, v, log_f))
    tri = jnp.tril(jnp.ones((c, c), dtype=bool))

    def step(s, inp):
        qc, kc, vc, lc = inp
        bcum = jnp.cumsum(lc, axis=1)
        o_inter = jnp.einsum('bthk,bhkv->bthv', qc * jnp.exp(bcum), s)
        diff = bcum[:, :, None] - bcum[:, None, :]
        dec = jnp.exp(jnp.where(tri[None, :, :, None, None], diff, -jnp.inf))
        att = jnp.einsum('bthk,btshk,bshk->btsh', qc, dec, kc)
        o_intra = jnp.einsum('btsh,bshv->bthv', att, vc)
        b_last = bcum[:, -1]
        s_new = jnp.exp(b_last)[..., None] * s + jnp.einsum('bshk,bshv->bhkv', kc * jnp.exp(b_last[:, None] - bcum), vc)
        return s_new, o_inter + o_intra

    s_fin, o = lax.scan(step, s0, (q, k, v, log_f))
    return _from_chunks(o, t), s_fin


def hgrn2_mixer(x, s0, lb, w_in, norm_g, w_o):
    b, t, _ = x.shape
    hk = HG_HEADS * HG_DK
    q, f, i, g = jnp.split(x @ w_in, [hk, 2 * hk, 2 * hk + HG_HEADS * HG_DV], axis=-1)
    heads = lambda a, d: a.reshape(b, t, HG_HEADS, d).astype(jnp.float32)
    lb = lb.reshape(HG_HEADS, HG_DK)
    fg = lb + (1.0 - lb) * jax.nn.sigmoid(heads(f, HG_DK))
    o, s = _gla_chunked(jax.nn.silu(heads(q, HG_DK)), 1.0 - fg, heads(i, HG_DV), jnp.log(fg),
                        s0.astype(jnp.float32), HG_CHUNK)
    o = _rmsnorm(o, norm_g) * jax.nn.silu(heads(g, HG_DV))
    return o.reshape(b, t, HG_HEADS * HG_DV).astype(x.dtype) @ w_o, s.astype(s0.dtype)


def _ssd_chunked(xs, dt, a, bm, cm, s0, chunk):
    b, t = xs.shape[:2]
    c = min(chunk, t)
    pad = (-t) % c
    xs, dt, bm, cm = (_to_chunks(_pad_time(arr, pad), c) for arr in (xs, dt, bm, cm))
    tri = jnp.tril(jnp.ones((c, c), dtype=bool))
    a_g = a.reshape(SSM_GROUPS, SSM_HPG)

    def step(s, inp):
        xc, dtc, bc, cc = inp
        xc = xc.reshape(b, c, SSM_GROUPS, SSM_HPG, SSM_HEAD_DIM)
        dtc = dtc.reshape(b, c, SSM_GROUPS, SSM_HPG)
        bcum = jnp.cumsum(dtc * a_g, axis=1)
        y_inter = jnp.einsum('btgn,bgjpn->btgjp', cc, s) * jnp.exp(bcum)[..., None]
        cb = jnp.einsum('btgn,bsgn->btsg', cc, bc)
        dec = jnp.exp(jnp.where(tri[None, :, :, None, None], bcum[:, :, None] - bcum[:, None], -jnp.inf))
        w = cb[..., None] * dec * dtc[:, None]
        y_intra = jnp.einsum('btsgj,bsgjp->btgjp', w, xc)
        b_last = bcum[:, -1]
        wts = jnp.exp(b_last[:, None] - bcum) * dtc
        s_new = jnp.exp(b_last)[..., None, None] * s + jnp.einsum('bsgj,bsgn,bsgjp->bgjpn', wts, bc, xc)
        return s_new, (y_inter + y_intra).reshape(b, c, SSM_HEADS, SSM_HEAD_DIM)

    s_init = s0.reshape(b, SSM_GROUPS, SSM_HPG, SSM_HEAD_DIM, SSM_STATE)
    s_fin, y = lax.scan(step, s_init, (xs, dt, bm, cm))
    return _from_chunks(y, t), s_fin.reshape(b, SSM_HEADS, SSM_HEAD_DIM, SSM_STATE)


def mamba2_mixer(x, conv_state, ssm_state, w_in, conv_w, conv_b, dt_bias, a_log, d_skip, norm_g, w_o):
    b, t, _ = x.shape
    z, xbc, dt = jnp.split(x @ w_in, [SSM_INNER, SSM_INNER + SSM_CONV_CH], axis=-1)
    xbc_full = jnp.concatenate([conv_state.astype(xbc.dtype), xbc], axis=1)
    conv = conv_b + sum(xbc_full[:, w:w + t] * conv_w[w] for w in range(SSM_CONV))
    new_conv = xbc_full[:, t:]
    xs, bm, cm = jnp.split(jax.nn.silu(conv), [SSM_INNER, SSM_INNER + SSM_GROUPS * SSM_STATE], axis=-1)
    xs = xs.reshape(b, t, SSM_HEADS, SSM_HEAD_DIM).astype(jnp.float32)
    bm = bm.reshape(b, t, SSM_GROUPS, SSM_STATE).astype(jnp.float32)
    cm = cm.reshape(b, t, SSM_GROUPS, SSM_STATE).astype(jnp.float32)
    dt = jax.nn.softplus(dt.astype(jnp.float32) + dt_bias.astype(jnp.float32))
    a = -jnp.exp(a_log.astype(jnp.float32))
    y, s = _ssd_chunked(xs, dt, a, bm, cm, ssm_state.astype(jnp.float32), SSM_CHUNK)
    y = y + d_skip.astype(jnp.float32)[:, None] * xs
    y = y.reshape(b, t, SSM_INNER) * jax.nn.silu(z.astype(jnp.float32))
    y = _rmsnorm(y.reshape(b, t, SSM_GROUPS, SSM_INNER // SSM_GROUPS),
                 norm_g.reshape(SSM_GROUPS, SSM_INNER // SSM_GROUPS)).reshape(b, t, SSM_INNER)
    return y.astype(x.dtype) @ w_o, (new_conv, s.astype(ssm_state.dtype))


def setup_inputs(seed: int = 0) -> dict:
    key = jax.random.key(seed)
    ks = iter(jax.random.split(key, 40))
    nrm = lambda shape, scale: jax.random.normal(next(ks), shape, jnp.float32) * scale
    n_pages = PAST_LEN // PAGE_SIZE
    n_pool = (5 * DEC_BATCH * n_pages + 3) // 4
    d = D_MODEL
    inp = {}
    inp['x_prompt'] = nrm((BATCH, SEQ, d), 1.0)
    inp['x_sample'] = nrm((DEC_BATCH, DEC_SEQ, d), 1.0)
    inp['cache_k'] = nrm((N_ATT_LAYERS, n_pool, PAGE_SIZE, ATT_KV_HEADS, ATT_HEAD_DIM), 1.0)
    inp['cache_v'] = nrm((N_ATT_LAYERS, n_pool, PAGE_SIZE, ATT_KV_HEADS, ATT_HEAD_DIM), 1.0)
    inp['cache_idx_k'] = nrm((N_ATT_LAYERS, n_pool, PAGE_SIZE, IDX_DIM), 1.0)
    inp['state_hgrn'] = nrm((N_HG_LAYERS, DEC_BATCH, HG_HEADS, HG_DK, HG_DV), 0.5)
    inp['state_ssm'] = nrm((N_SSM_LAYERS, DEC_BATCH, SSM_HEADS, SSM_HEAD_DIM, SSM_STATE), 0.5)
    inp['state_conv'] = nrm((N_SSM_LAYERS, DEC_BATCH, SSM_CONV - 1, SSM_CONV_CH), 1.0)
    perm = jax.random.permutation(next(ks), n_pool)
    inp['page_table'] = perm[:DEC_BATCH * n_pages].reshape(DEC_BATCH, n_pages).astype(jnp.int32)
    inp['p_prompt'] = nrm((DEPTH, BATCH, SEQ, PLE_DIM), 1.0)
    inp['p_sample'] = nrm((DEPTH, DEC_BATCH, DEC_SEQ, PLE_DIM), 1.0)
    inp['ln_g'] = 1.0 + nrm((DEPTH, 3, d), 0.02)
    inp['ln_b'] = nrm((DEPTH, 3, d), 0.02)
    inp['ffn_w_gate_up'] = nrm((DEPTH, 2, d, 2 * D_FF), d ** -0.5)
    inp['ffn_w_down'] = nrm((DEPTH, 2, D_FF, d), DEEPNORM_BETA * D_FF ** -0.5)
    inp['ple_w_proj'] = nrm((DEPTH, PLE_DIM, d), PLE_DIM ** -0.5)
    inp['ple_w_gate'] = nrm((DEPTH, d, d), d ** -0.5)
    inp['att_w_in'] = nrm((N_ATT_LAYERS, d, ATT_IN), d ** -0.5)
    inp['att_idx_k_norm'] = 1.0 + nrm((N_ATT_LAYERS, IDX_DIM), 0.02)
    inp['att_w_o'] = nrm((N_ATT_LAYERS, ATT_HEADS * ATT_HEAD_DIM, d), DEEPNORM_BETA * (ATT_HEADS * ATT_HEAD_DIM) ** -0.5)
    inp['hg_w_in'] = nrm((N_HG_LAYERS, d, HG_IN), d ** -0.5)
    inp['hg_lb_logits'] = nrm((DEPTH, HG_HEADS * HG_DK), 0.1)
    inp['hg_norm_g'] = 1.0 + nrm((N_HG_LAYERS, HG_DV), 0.02)
    inp['hg_w_o'] = nrm((N_HG_LAYERS, HG_HEADS * HG_DV, d), DEEPNORM_BETA * (HG_HEADS * HG_DV) ** -0.5)
    inp['ssm_w_in'] = nrm((N_SSM_LAYERS, d, SSM_IN), d ** -0.5)
    inp['ssm_conv_w'] = nrm((N_SSM_LAYERS, SSM_CONV, SSM_CONV_CH), SSM_CONV ** -0.5)
    inp['ssm_conv_b'] = nrm((N_SSM_LAYERS, SSM_CONV_CH), 0.02)
    dt0 = jnp.exp(jax.random.uniform(next(ks), (N_SSM_LAYERS, SSM_HEADS), jnp.float32,
                                     minval=math.log(1e-3), maxval=math.log(1e-1)))
    inp['ssm_dt_bias'] = dt0 + jnp.log(-jnp.expm1(-dt0))
    inp['ssm_a_log'] = jnp.log(jax.random.uniform(next(ks), (N_SSM_LAYERS, SSM_HEADS), jnp.float32,
                                                  minval=1.0, maxval=16.0))
    inp['ssm_d'] = 1.0 + nrm((N_SSM_LAYERS, SSM_HEADS), 0.02)
    inp['ssm_norm_g'] = 1.0 + nrm((N_SSM_LAYERS, SSM_INNER), 0.02)
    inp['ssm_w_o'] = nrm((N_SSM_LAYERS, SSM_INNER, d), DEEPNORM_BETA * SSM_INNER ** -0.5)
    return inp


def reference(x_prompt, x_sample, cache_k, cache_v, cache_idx_k, state_hgrn, state_ssm, state_conv,
              page_table, p_prompt, p_sample,
              ln_g, ln_b, ffn_w_gate_up, ffn_w_down, ple_w_proj, ple_w_gate,
              att_w_in, att_idx_k_norm, att_w_o,
              hg_w_in, hg_lb_logits, hg_norm_g, hg_w_o,
              ssm_w_in, ssm_conv_w, ssm_conv_b, ssm_dt_bias, ssm_a_log, ssm_d, ssm_norm_g, ssm_w_o):
    lb_soft = jax.nn.softmax(hg_lb_logits.astype(jnp.float32), axis=0)
    lower_bounds = jnp.cumsum(lb_soft, axis=0) - lb_soft[0]

    def pre(x, i):
        return _layernorm(DEEPNORM_ALPHA * x + 0.5 * _swiglu(x, ffn_w_gate_up[i, 0], ffn_w_down[i, 0]),
                          ln_g[i, 0], ln_b[i, 0])

    def post(x, m, p_i, i):
        x = _layernorm(DEEPNORM_ALPHA * x + m, ln_g[i, 1], ln_b[i, 1])
        x = _layernorm(DEEPNORM_ALPHA * x + 0.5 * _swiglu(x, ffn_w_gate_up[i, 1], ffn_w_down[i, 1]),
                       ln_g[i, 2], ln_b[i, 2])
        return x + jax.nn.sigmoid(x @ ple_w_gate[i]) * (p_i @ ple_w_proj[i])

    hp, hs = x_prompt, x_sample
    bp = x_prompt.shape[0]
    att_p, att_s, hg_p, hg_s, ssm_p, ssm_s, conv_p, conv_s = [], [], [], [], [], [], [], []
    for i in range(DEPTH):
        j = i // N_MIXERS
        hp, hs = pre(hp, i), pre(hs, i)
        if i % N_MIXERS == 0:
            mp, st_p = dsa_prompt(hp, att_w_in[j], att_idx_k_norm[j], att_w_o[j])
            ms, st_s = dsa_sample(hs, cache_k[j], cache_v[j], cache_idx_k[j], page_table,
                                  att_w_in[j], att_idx_k_norm[j], att_w_o[j])
            att_p.append(st_p)
            att_s.append(st_s)
        elif i % N_MIXERS == 1:
            s0 = jnp.zeros((bp, HG_HEADS, HG_DK, HG_DV), hp.dtype)
            mp, st_p = hgrn2_mixer(hp, s0, lower_bounds[i], hg_w_in[j], hg_norm_g[j], hg_w_o[j])
            ms, st_s = hgrn2_mixer(hs, state_hgrn[j], lower_bounds[i], hg_w_in[j], hg_norm_g[j], hg_w_o[j])
            hg_p.append(st_p)
            hg_s.append(st_s)
        else:
            c0 = jnp.zeros((bp, SSM_CONV - 1, SSM_CONV_CH), hp.dtype)
            h0 = jnp.zeros((bp, SSM_HEADS, SSM_HEAD_DIM, SSM_STATE), hp.dtype)
            mp, (cp, sp) = mamba2_mixer(hp, c0, h0, ssm_w_in[j], ssm_conv_w[j], ssm_conv_b[j], ssm_dt_bias[j],
                                        ssm_a_log[j], ssm_d[j], ssm_norm_g[j], ssm_w_o[j])
            ms, (cs, ss) = mamba2_mixer(hs, state_conv[j], state_ssm[j], ssm_w_in[j], ssm_conv_w[j], ssm_conv_b[j],
                                        ssm_dt_bias[j], ssm_a_log[j], ssm_d[j], ssm_norm_g[j], ssm_w_o[j])
            conv_p.append(cp)
            ssm_p.append(sp)
            conv_s.append(cs)
            ssm_s.append(ss)
        hp, hs = post(hp, mp, p_prompt[i], i), post(hs, ms, p_sample[i], i)

    k_prompt = jnp.stack([e[0] for e in att_p])
    v_prompt = jnp.stack([e[1] for e in att_p])
    idx_k_prompt = jnp.stack([e[2] for e in att_p])
    k_sample = jnp.stack([e[0] for e in att_s])
    v_sample = jnp.stack([e[1] for e in att_s])
    idx_k_sample = jnp.stack([e[2] for e in att_s])
    hgrn_prompt = jnp.stack(hg_p)
    hgrn_sample = jnp.stack(hg_s)
    ssm_prompt = jnp.stack(ssm_p)
    ssm_sample = jnp.stack(ssm_s)
    conv_prompt = jnp.stack(conv_p)
    conv_sample = jnp.stack(conv_s)
    return (hp, hs, k_prompt, v_prompt, idx_k_prompt, k_sample, v_sample, idx_k_sample,
            hgrn_prompt, hgrn_sample, ssm_prompt, ssm_sample, conv_prompt, conv_sample)
```

```python
import functools
import math

import jax
import jax.numpy as jnp
import numpy as np
from jax import lax
from jax.experimental import pallas as pl
from jax.experimental.pallas import tpu as pltpu

F32 = jnp.float32
BF16 = jnp.bfloat16

D_MODEL = 2048
DEPTH = 4
N_MIXERS = 3
D_FF = 2 * D_MODEL
PLE_DIM = 256
ALPHA = (2 * DEPTH) ** 0.25
LN_EPS = 1e-5
RMS_EPS = 1e-6
PAGE = 128

ATT_HD = 128
ATT_HEADS = 16
ATT_KV = 4
ATT_GROUP = ATT_HEADS // ATT_KV
IDX_HEADS = 16
IDX_DIM = 64
IDX_W_SCALE = (IDX_HEADS ** -0.5) * (IDX_DIM ** -0.5)
TOPK_MAX = 256
ATT_SCALE = ATT_HD ** -0.5
ATT_MAIN = ATT_HEADS * ATT_HD + 2 * ATT_KV * ATT_HD + IDX_HEADS * IDX_DIM

HG_HEADS = 16
HG_DK = 128
HG_DV = 128

SSM_INNER = 2 * D_MODEL
SSM_P = 64
SSM_HEADS = SSM_INNER // SSM_P
SSM_GROUPS = 8
SSM_HPG = SSM_HEADS // SSM_GROUPS
SSM_N = 128
SSM_CONV = 4
SSM_CH = SSM_INNER + 2 * SSM_GROUPS * SSM_N
SSM_MAIN = SSM_INNER + SSM_CH

LANES = 128
SUBLANES = 8
VMEM_LIMIT_BYTES = 56 * 1024 * 1024

INT_MIN = np.int32(-2 ** 31)
NEG_BIG = -1e30


def _cp(*sem):
    return pltpu.CompilerParams(dimension_semantics=sem, vmem_limit_bytes=VMEM_LIMIT_BYTES)


def _nt(a, b):
    return lax.dot_general(a, b, (((1,), (1,)), ((), ())), preferred_element_type=F32)


def _tn(a, b):
    return lax.dot_general(a, b, (((0,), (0,)), ((), ())), preferred_element_type=F32)


def _silu(x):
    return x * jax.nn.sigmoid(x)


def _mm_body(x_ref, w_ref, o_ref, *rest):
    acc = jnp.dot(x_ref[...], w_ref[...], preferred_element_type=F32)
    o_ref[...] = acc
    if rest:
        rest[0][...] = acc.astype(BF16)


def _mm(xb, wb, tm, tn, with_bf16=False, name="proj"):
    m, k = xb.shape
    n = wb.shape[1]
    out_shape = [jax.ShapeDtypeStruct((m, n), F32)]
    out_specs = [pl.BlockSpec((tm, tn), lambda i, j: (i, j))]
    if with_bf16:
        out_shape.append(jax.ShapeDtypeStruct((m, n), BF16))
        out_specs.append(pl.BlockSpec((tm, tn), lambda i, j: (i, j)))
    res = pl.pallas_call(
        _mm_body,
        grid=(m // tm, n // tn),
        in_specs=[pl.BlockSpec((tm, k), lambda i, j: (i, 0)),
                  pl.BlockSpec((k, tn), lambda i, j: (0, j))],
        out_specs=out_specs,
        out_shape=out_shape,
        compiler_params=_cp("parallel", "arbitrary"),
        name=name,
    )(xb, wb)
    return res if with_bf16 else res[0]


def _swiglu_body(x_ref, wg_ref, wu_ref, o_ref):
    x = x_ref[...]
    g = jnp.dot(x, wg_ref[...], preferred_element_type=F32)
    u = jnp.dot(x, wu_ref[...], preferred_element_type=F32)
    o_ref[...] = (_silu(g) * u).astype(o_ref.dtype)


def _mm_swiglu(xb, wgu, tm, tn):
    m, k = xb.shape
    f = wgu.shape[1] // 2
    nj = f // tn
    return pl.pallas_call(
        _swiglu_body,
        grid=(m // tm, nj),
        in_specs=[pl.BlockSpec((tm, k), lambda i, j: (i, 0)),
                  pl.BlockSpec((k, tn), lambda i, j: (0, j)),
                  pl.BlockSpec((k, tn), lambda i, j: (0, j + nj))],
        out_specs=pl.BlockSpec((tm, tn), lambda i, j: (i, j)),
        out_shape=jax.ShapeDtypeStruct((m, f), BF16),
        compiler_params=_cp("parallel", "arbitrary"),
        name="ffn_up",
    )(xb, wgu, wgu)


def _mm_ln_body(a_ref, w_ref, r_ref, g_ref, b_ref, o_ref, ob_ref, acc_ref, *, scale, nk):
    k = pl.program_id(1)

    @pl.when(k == 0)
    def _():
        acc_ref[...] = jnp.zeros_like(acc_ref)

    acc_ref[...] += jnp.dot(a_ref[...], w_ref[...], preferred_element_type=F32)

    @pl.when(k == nk - 1)
    def _():
        y = ALPHA * r_ref[...] + scale * acc_ref[...]
        mu = jnp.mean(y, axis=-1, keepdims=True)
        yc = y - mu
        var = jnp.mean(yc * yc, axis=-1, keepdims=True)
        out = yc * lax.rsqrt(var + LN_EPS) * g_ref[...] + b_ref[...]
        o_ref[...] = out
        ob_ref[...] = out.astype(BF16)


def _mm_ln(ab, wb, res, g, b, scale, tm, tk, name):
    m, kdim = ab.shape
    n = wb.shape[1]
    nk = kdim // tk
    return pl.pallas_call(
        functools.partial(_mm_ln_body, scale=scale, nk=nk),
        grid=(m // tm, nk),
        in_specs=[pl.BlockSpec((tm, tk), lambda i, k: (i, k)),
                  pl.BlockSpec((tk, n), lambda i, k: (k, 0)),
                  pl.BlockSpec((tm, n), lambda i, k: (i, 0)),
                  pl.BlockSpec((1, n), lambda i, k: (0, 0)),
                  pl.BlockSpec((1, n), lambda i, k: (0, 0))],
        out_specs=[pl.BlockSpec((tm, n), lambda i, k: (i, 0)),
                   pl.BlockSpec((tm, n), lambda i, k: (i, 0))],
        out_shape=[jax.ShapeDtypeStruct((m, n), F32), jax.ShapeDtypeStruct((m, n), BF16)],
        scratch_shapes=[pltpu.VMEM((tm, n), F32)],
        compiler_params=_cp("parallel", "arbitrary"),
        name=name,
    )(ab, wb, res, g.reshape(1, n), b.reshape(1, n))


def _ple_body(xb_ref, p_ref, wg_ref, wp_ref, x_ref, o_ref, ob_ref):
    gate = jax.nn.sigmoid(jnp.dot(xb_ref[...], wg_ref[...], preferred_element_type=F32))
    proj = jnp.dot(p_ref[...], wp_ref[...], preferred_element_type=F32)
    out = x_ref[...] + gate * proj
    o_ref[...] = out
    ob_ref[...] = out.astype(BF16)


def _mm_ple(x32, xb, pb, wg, wp, tm, tn):
    m, d = xb.shape
    pd = pb.shape[1]
    return pl.pallas_call(
        _ple_body,
        grid=(m // tm, d // tn),
        in_specs=[pl.BlockSpec((tm, d), lambda i, j: (i, 0)),
                  pl.BlockSpec((tm, pd), lambda i, j: (i, 0)),
                  pl.BlockSpec((d, tn), lambda i, j: (0, j)),
                  pl.BlockSpec((pd, tn), lambda i, j: (0, j)),
                  pl.BlockSpec((tm, tn), lambda i, j: (i, j))],
        out_specs=[pl.BlockSpec((tm, tn), lambda i, j: (i, j)),
                   pl.BlockSpec((tm, tn), lambda i, j: (i, j))],
        out_shape=[jax.ShapeDtypeStruct((m, d), F32), jax.ShapeDtypeStruct((m, d), BF16)],
        compiler_params=_cp("parallel", "arbitrary"),
        name="ple",
    )(xb, pb, wg, wp, x32)


def _ik_norm_body(p_ref, g_ref, o_ref, ob_ref):
    x = p_ref[...][:, :IDX_DIM]
    mu = jnp.mean(x, axis=-1, keepdims=True)
    xc = x - mu
    out = xc * lax.rsqrt(jnp.mean(xc * xc, axis=-1, keepdims=True) + LN_EPS) * g_ref[...]
    o_ref[...] = out
    ob_ref[...] = out.astype(BF16)


def _ik_norm(proj2, ik_g, tm):
    m = proj2.shape[0]
    return pl.pallas_call(
        _ik_norm_body,
        grid=(m // tm,),
        in_specs=[pl.BlockSpec((tm, LANES), lambda i: (i, 0)),
                  pl.BlockSpec((1, IDX_DIM), lambda i: (0, 0))],
        out_specs=[pl.BlockSpec((tm, IDX_DIM), lambda i: (i, 0)),
                   pl.BlockSpec((tm, IDX_DIM), lambda i: (i, 0))],
        out_shape=[jax.ShapeDtypeStruct((m, IDX_DIM), F32), jax.ShapeDtypeStruct((m, IDX_DIM), BF16)],
        compiler_params=_cp("parallel"),
        name="idx_k_norm",
    )(proj2, ik_g.reshape(1, IDX_DIM))


def _order_key(x):
    bits = pltpu.bitcast(x, jnp.int32)
    return jnp.where(bits < 0, bits ^ jnp.int32(0x7FFFFFFF), bits)


_RADIX_BITS = [INT_MIN] + [np.int32(1 << s) for s in range(30, -1, -1)]


def _dsa_prompt_body(q_ref, iq_ref, iw_ref, k_ref, v_ref, ik_ref, o_ref, keys_ref, *, qb, kc, topk):
    i = pl.program_id(1)
    nck = ((i + 1) * qb + kc - 1) // kc
    row_pos = i * qb + lax.broadcasted_iota(jnp.int32, (qb, 1), 0)
    iq = iq_ref[...]
    iw = iw_ref[...][:, IDX_DIM:IDX_DIM + IDX_HEADS] * IDX_W_SCALE
    iq_h = [iq[:, h * IDX_DIM:(h + 1) * IDX_DIM] for h in range(IDX_HEADS)]
    iw_h = [iw[:, h:h + 1] for h in range(IDX_HEADS)]
    col0 = lax.broadcasted_iota(jnp.int32, (1, kc), 1)

    def score_chunk(c, carry):
        off = pl.multiple_of(c * kc, kc)
        ikc = ik_ref[pl.ds(off, kc), :]
        sc = jnp.zeros((qb, kc), F32)
        for h in range(IDX_HEADS):
            sc = sc + iw_h[h] * jnp.maximum(_nt(iq_h[h], ikc), 0.0)
        key = jnp.where(col0 + off <= row_pos, _order_key(sc), INT_MIN)
        keys_ref[:, pl.ds(off, kc)] = key
        return carry

    lax.fori_loop(0, nck, score_chunk, 0)

    lo = jnp.full((qb, 1), INT_MIN, jnp.int32)
    for bit in _RADIX_BITS:
        cand = lo + bit

        def count_chunk(c, acc, cand=cand):
            off = pl.multiple_of(c * kc, kc)
            ge = jnp.where(keys_ref[:, pl.ds(off, kc)] >= cand, 1.0, 0.0)
            for t in range(kc // LANES):
                acc = acc + ge[:, t * LANES:(t + 1) * LANES]
            return acc

        cnt = jnp.sum(lax.fori_loop(0, nck, count_chunk, jnp.zeros((qb, LANES), F32)), axis=-1, keepdims=True)
        lo = jnp.where(cnt >= topk, cand, lo)
    thr = jnp.maximum(lo, INT_MIN + 1)

    for g in range(ATT_KV):
        qg = jnp.concatenate(
            [q_ref[:, (g * ATT_GROUP + j) * ATT_HD:(g * ATT_GROUP + j + 1) * ATT_HD] for j in range(ATT_GROUP)], axis=0)

        def att_chunk(c, carry, qg=qg, g=g):
            m, l, acc = carry
            off = pl.multiple_of(c * kc, kc)
            kch = k_ref[pl.ds(off, kc), g * ATT_HD:(g + 1) * ATT_HD]
            vch = v_ref[pl.ds(off, kc), g * ATT_HD:(g + 1) * ATT_HD]
            sel = (keys_ref[:, pl.ds(off, kc)] >= thr)[None]
            s = (_nt(qg, kch) * ATT_SCALE).reshape(ATT_GROUP, qb, kc)
            s = jnp.where(sel, s, NEG_BIG)
            m_new = jnp.maximum(m, jnp.max(s, axis=-1, keepdims=True))
            p = jnp.where(sel, jnp.exp(s - m_new), 0.0)
            a = jnp.exp(m - m_new)
            l = a * l + jnp.sum(p, axis=-1, keepdims=True)
            pv = jnp.dot(p.reshape(ATT_GROUP * qb, kc).astype(BF16), vch, preferred_element_type=F32)
            acc = a * acc + pv.reshape(ATT_GROUP, qb, ATT_HD)
            return m_new, l, acc

        init = (jnp.full((ATT_GROUP, qb, 1), NEG_BIG, F32), jnp.zeros((ATT_GROUP, qb, 1), F32),
                jnp.zeros((ATT_GROUP, qb, ATT_HD), F32))
        _, l, acc = lax.fori_loop(0, nck, att_chunk, init)
        out = acc / l
        for j in range(ATT_GROUP):
            h = g * ATT_GROUP + j
            o_ref[:, h * ATT_HD:(h + 1) * ATT_HD] = out[j].astype(o_ref.dtype)


def _dsa_prompt(projb, proj2, ikb, nb, s, qb=128, kc=512):
    nq = s // qb
    topk = min(TOPK_MAX, s // 4)
    qw = ATT_HEADS * ATT_HD
    kvw = ATT_KV * ATT_HD
    iqw = IDX_HEADS * IDX_DIM
    return pl.pallas_call(
        functools.partial(_dsa_prompt_body, qb=qb, kc=kc, topk=topk),
        grid=(nb, nq),
        in_specs=[pl.BlockSpec((qb, qw), lambda b, i: (b * nq + i, 0)),
                  pl.BlockSpec((qb, iqw), lambda b, i: (b * nq + i, (qw + 2 * kvw) // iqw)),
                  pl.BlockSpec((qb, LANES), lambda b, i: (b * nq + i, 0)),
                  pl.BlockSpec((s, kvw), lambda b, i: (b, qw // kvw)),
                  pl.BlockSpec((s, kvw), lambda b, i: (b, qw // kvw + 1)),
                  pl.BlockSpec((s, IDX_DIM), lambda b, i: (b, 0))],
        out_specs=pl.BlockSpec((qb, qw), lambda b, i: (b * nq + i, 0)),
        out_shape=jax.ShapeDtypeStruct((nb * s, qw), BF16),
        scratch_shapes=[pltpu.VMEM((qb, s), jnp.int32)],
        compiler_params=_cp("parallel", "arbitrary"),
        name="dsa_prompt",
    )(projb, projb, proj2, projb, projb, ikb)


def _dsa_s_scores_body(pt_ref, iq_ref, iw_ref, ikn_ref, ikp_ref, o_ref):
    p = pl.program_id(1)
    iq = iq_ref[...].astype(BF16)
    iw = iw_ref[...] * IDX_W_SCALE

    @pl.when(p == 0)
    def _():
        own = jnp.broadcast_to(ikn_ref[...], (SUBLANES, IDX_DIM)).astype(BF16)
        d = jnp.maximum(_nt(iq, own), 0.0)
        sc = jnp.sum(iw * d, axis=0, keepdims=True)
        o_ref[PAGE:PAGE + SUBLANES, :] = jnp.broadcast_to(sc[:, 0:1], (SUBLANES, LANES))

    d = jnp.maximum(_nt(iq, ikp_ref[...].astype(BF16)), 0.0)
    o_ref[pl.ds(p, 1), :] = jnp.sum(iw * d, axis=0, keepdims=True)


def _dsa_s_scores(page_table, iq, iw, ik_new, cache_ik):
    nb, n_pages = page_table.shape
    assert n_pages == PAGE
    return pl.pallas_call(
        _dsa_s_scores_body,
        grid_spec=pltpu.PrefetchScalarGridSpec(
            num_scalar_prefetch=1,
            grid=(nb, n_pages),
            in_specs=[pl.BlockSpec((None, IDX_HEADS, IDX_DIM), lambda b, p, pt: (b, 0, 0)),
                      pl.BlockSpec((None, IDX_HEADS, 1), lambda b, p, pt: (b, 0, 0)),
                      pl.BlockSpec((None, 1, IDX_DIM), lambda b, p, pt: (b, 0, 0)),
                      pl.BlockSpec((None, PAGE, IDX_DIM), lambda b, p, pt: (pt[b, p], 0, 0))],
            out_specs=pl.BlockSpec((None, n_pages + SUBLANES, LANES), lambda b, p, pt: (b, 0, 0)),
        ),
        out_shape=jax.ShapeDtypeStruct((nb, n_pages + SUBLANES, LANES), F32),
        compiler_params=_cp("parallel", "arbitrary"),
        name="dsa_sample_scores",
    )(page_table, iq, iw, ik_new, cache_ik)


def _dsa_s_attend_body(pt_ref, sc_ref, q_ref, kn_ref, vn_ref, kp_ref, vp_ref, o_ref,
                       keys_ref, thr_ref, qbd_ref, m_ref, l_ref, acc_ref, *, n_pages, topk):
    p = pl.program_id(1)
    kvw = ATT_KV * ATT_HD
    blk = (lax.broadcasted_iota(jnp.int32, (ATT_HEADS, kvw), 1) // ATT_HD
           == lax.broadcasted_iota(jnp.int32, (ATT_HEADS, kvw), 0) // ATT_GROUP)

    @pl.when(p == 0)
    def _():
        rows = lax.broadcasted_iota(jnp.int32, (n_pages + SUBLANES, LANES), 0)
        cols = lax.broadcasted_iota(jnp.int32, (n_pages + SUBLANES, LANES), 1)
        live = (rows < n_pages) | ((rows == n_pages) & (cols == 0))
        keys = jnp.where(live, _order_key(sc_ref[...]), INT_MIN)
        keys_ref[...] = keys
        lo = jnp.full((1, 1), INT_MIN, jnp.int32)
        for bit in _RADIX_BITS:
            cand = lo + bit
            cnt = jnp.sum(jnp.sum(jnp.where(keys >= cand, 1.0, 0.0), axis=0, keepdims=True), axis=1, keepdims=True)
            lo = jnp.where(cnt >= topk, cand, lo)
        thr_ref[...] = jnp.broadcast_to(jnp.maximum(lo, INT_MIN + 1), thr_ref.shape)
        q = q_ref[...].astype(BF16)
        qbd_ref[...] = jnp.where(blk, jnp.concatenate([q] * ATT_KV, axis=1), 0.0).astype(BF16)
        m_ref[...] = jnp.full_like(m_ref, NEG_BIG)
        l_ref[...] = jnp.zeros_like(l_ref)
        acc_ref[...] = jnp.zeros_like(acc_ref)

    thr = thr_ref[0:1, 0:1]
    qbd = qbd_ref[...]

    def update(s, sel, v_rows):
        s = jnp.where(sel, s, NEG_BIG)
        m = m_ref[...]
        m_new = jnp.maximum(m, jnp.max(s, axis=-1, keepdims=True))
        pr = jnp.where(sel, jnp.exp(s - m_new), 0.0)
        a = jnp.exp(m - m_new)
        l_ref[...] = a * l_ref[...] + jnp.sum(pr, axis=-1, keepdims=True)
        acc_ref[...] = a * acc_ref[...] + v_rows(pr)
        m_ref[...] = m_new

    sel = keys_ref[pl.ds(p, 1), :] >= thr
    vpage = vp_ref[...].astype(BF16)
    update(_nt(qbd, kp_ref[...].astype(BF16)) * ATT_SCALE, sel,
           lambda pr: jnp.dot(pr.astype(BF16), vpage, preferred_element_type=F32))

    @pl.when(p == n_pages - 1)
    def _():
        kn = kn_ref[...].astype(BF16).astype(F32)
        vn = vn_ref[...].astype(BF16).astype(F32)
        s_own = jnp.sum(qbd.astype(F32) * kn, axis=-1, keepdims=True) * ATT_SCALE
        sel_own = keys_ref[n_pages:n_pages + 1, 0:1] >= thr
        update(s_own, sel_own, lambda pr: pr.astype(BF16).astype(F32) * vn)
        out = jnp.where(blk, acc_ref[...] / l_ref[...], 0.0)
        o = out[:, 0:ATT_HD]
        for g in range(1, ATT_KV):
            o = o + out[:, g * ATT_HD:(g + 1) * ATT_HD]
        o_ref[...] = o


def _dsa_s_attend(page_table, scores, q, k_new, v_new, cache_k, cache_v):
    nb, n_pages = page_table.shape
    kvw = ATT_KV * ATT_HD
    topk = min(TOPK_MAX, (n_pages * PAGE + 1) // 4)
    return pl.pallas_call(
        functools.partial(_dsa_s_attend_body, n_pages=n_pages, topk=topk),
        grid_spec=pltpu.PrefetchScalarGridSpec(
            num_scalar_prefetch=1,
            grid=(nb, n_pages),
            in_specs=[pl.BlockSpec((None, n_pages + SUBLANES, LANES), lambda b, p, pt: (b, 0, 0)),
                      pl.BlockSpec((None, ATT_HEADS, ATT_HD), lambda b, p, pt: (b, 0, 0)),
                      pl.BlockSpec((None, 1, kvw), lambda b, p, pt: (b, 0, 0)),
                      pl.BlockSpec((None, 1, kvw), lambda b, p, pt: (b, 0, 0)),
                      pl.BlockSpec((None, PAGE, kvw), lambda b, p, pt: (pt[b, p], 0, 0)),
                      pl.BlockSpec((None, PAGE, kvw), lambda b, p, pt: (pt[b, p], 0, 0))],
            out_specs=pl.BlockSpec((None, ATT_HEADS, ATT_HD), lambda b, p, pt: (b, 0, 0)),
            scratch_shapes=[pltpu.VMEM((n_pages + SUBLANES, LANES), jnp.int32),
                            pltpu.VMEM((SUBLANES, LANES), jnp.int32),
                            pltpu.VMEM((ATT_HEADS, kvw), BF16),
                            pltpu.VMEM((ATT_HEADS, 1), F32),
                            pltpu.VMEM((ATT_HEADS, 1), F32),
                            pltpu.VMEM((ATT_HEADS, kvw), F32)],
        ),
        out_shape=jax.ShapeDtypeStruct((nb, ATT_HEADS, ATT_HD), F32),
        compiler_params=_cp("parallel", "arbitrary"),
        name="dsa_sample_attend",
    )(page_table, scores, q, k_new, v_new, cache_k, cache_v)


HG_SUB = 16


def _hgrn_body(lbl_ref, q_ref, f_ref, i_ref, g_ref, ng_ref, s0_ref, o_ref, so_ref, st_ref, *, layer, c, t, nc):
    ci = pl.program_id(2)

    @pl.when(ci == 0)
    def _():
        st_ref[...] = s0_ref[...].T

    logits = lbl_ref[...]
    e = jnp.exp(logits - jnp.max(logits, axis=0, keepdims=True))
    soft = e / jnp.sum(e, axis=0, keepdims=True)
    lb = jnp.zeros((1, HG_DK), F32)
    for r in range(1, layer + 1):
        lb = lb + soft[r:r + 1, :]

    rows = lax.broadcasted_iota(jnp.int32, (HG_SUB, 1), 0)
    tri = (lax.broadcasted_iota(jnp.int32, (HG_SUB, HG_SUB), 0)
           >= lax.broadcasted_iota(jnp.int32, (HG_SUB, HG_SUB), 1)).astype(F32)
    ng = ng_ref[...]
    nsub = min(c, -(-t // HG_SUB) * HG_SUB) // HG_SUB if nc == 1 else c // HG_SUB
    for sb in range(nsub):
        sl = pl.ds(sb * HG_SUB, HG_SUB)
        fg = lb + (1.0 - lb) * jax.nn.sigmoid(f_ref[sl, :])
        lf = jnp.log(fg)
        kk = 1.0 - fg
        if t % c:
            valid = (ci * c + sb * HG_SUB + rows) < t
            lf = jnp.where(valid, lf, 0.0)
            kk = jnp.where(valid, kk, 0.0)
        qq = _silu(q_ref[sl, :])
        vv = i_ref[sl, :]
        b = jnp.dot(tri, lf, precision=lax.Precision.HIGHEST, preferred_element_type=F32)
        st = st_ref[...]
        o = _nt((qq * jnp.exp(b)).astype(BF16), st.astype(BF16))
        for s in range(HG_SUB):
            dec = jnp.exp(jnp.where(rows >= s, b - b[s:s + 1, :], -jnp.inf))
            att = jnp.sum(qq * dec * kk[s:s + 1, :], axis=-1, keepdims=True)
            o = o + att * vv[s:s + 1, :]
        bl = b[HG_SUB - 1:HG_SUB, :]
        kt = kk * jnp.exp(bl - b)
        st_ref[...] = st * jnp.exp(bl) + _tn(vv.astype(BF16), kt.astype(BF16))
        on = o * lax.rsqrt(jnp.mean(o * o, axis=-1, keepdims=True) + RMS_EPS) * ng
        o_ref[sl, :] = (on * _silu(g_ref[sl, :])).astype(o_ref.dtype)
    if nsub * HG_SUB < c:
        o_ref[pl.ds(nsub * HG_SUB, c - nsub * HG_SUB), :] = jnp.zeros((c - nsub * HG_SUB, HG_DV), o_ref.dtype)

    @pl.when(ci == nc - 1)
    def _():
        so_ref[...] = st_ref[...].T


def _hgrn(proj, lb_logits, norm_g, s0, layer, t, c=128):
    nb, tpad, _ = proj.shape
    nc = tpad // c
    h_ = HG_HEADS
    return pl.pallas_call(
        functools.partial(_hgrn_body, layer=layer, c=c, t=t, nc=nc),
        grid=(nb, h_, nc),
        in_specs=[pl.BlockSpec((DEPTH, HG_DK), lambda b, h, ci: (0, h)),
                  pl.BlockSpec((None, c, HG_DK), lambda b, h, ci: (b, ci, h)),
                  pl.BlockSpec((None, c, HG_DK), lambda b, h, ci: (b, ci, h_ + h)),
                  pl.BlockSpec((None, c, HG_DV), lambda b, h, ci: (b, ci, 2 * h_ + h)),
                  pl.BlockSpec((None, c, HG_DV), lambda b, h, ci: (b, ci, 3 * h_ + h)),
                  pl.BlockSpec((1, HG_DV), lambda b, h, ci: (0, 0)),
                  pl.BlockSpec((None, None, HG_DK, HG_DV), lambda b, h, ci: (b, h, 0, 0))],
        out_specs=[pl.BlockSpec((None, c, HG_DV), lambda b, h, ci: (b, ci, h)),
                   pl.BlockSpec((None, None, HG_DK, HG_DV), lambda b, h, ci: (b, h, 0, 0))],
        out_shape=[jax.ShapeDtypeStruct((nb, tpad, h_ * HG_DV), BF16),
                   jax.ShapeDtypeStruct((nb, h_, HG_DK, HG_DV), F32)],
        scratch_shapes=[pltpu.VMEM((HG_DV, HG_DK), F32)],
        compiler_params=_cp("parallel", "parallel", "arbitrary"),
        name="hgrn2",
    )(lb_logits, proj, proj, proj, proj, norm_g.reshape(1, HG_DV), s0)


def _conv_body(x_ref, halo_ref, cs_ref, w_ref, b_ref, o_ref, nc_ref, *, c, t, nc):
    ti = pl.program_id(2)
    halo = jnp.where(ti == 0, cs_ref[...], halo_ref[...])
    full = jnp.concatenate([halo, x_ref[...]], axis=0)
    w = w_ref[...]
    conv = b_ref[...]
    for k in range(SSM_CONV):
        lo = SUBLANES - (SSM_CONV - 1) + k
        conv = conv + full[lo:lo + c, :] * w[k:k + 1, :]
    o_ref[...] = _silu(conv)

    @pl.when(ti == nc - 1)
    def _():
        tv = t - (nc - 1) * c
        tail = full[SUBLANES + tv - (SSM_CONV - 1):SUBLANES + tv, :]
        nc_ref[...] = jnp.concatenate([tail, jnp.zeros((SUBLANES - (SSM_CONV - 1), tail.shape[1]), F32)], axis=0)


def _ssd_conv(zx, cs_pad, conv_w, conv_b, t, c, cw=512):
    nb, tpad, _ = zx.shape
    nc = tpad // c
    ncol = SSM_CH // cw
    col0 = SSM_INNER // cw
    hb = c // SUBLANES
    return pl.pallas_call(
        functools.partial(_conv_body, c=c, t=t, nc=nc),
        grid=(nb, ncol, nc),
        in_specs=[pl.BlockSpec((None, c, cw), lambda b, j, ti: (b, ti, col0 + j)),
                  pl.BlockSpec((None, SUBLANES, cw), lambda b, j, ti: (b, jnp.maximum(ti * hb - 1, 0), col0 + j)),
                  pl.BlockSpec((None, SUBLANES, cw), lambda b, j, ti: (b, 0, j)),
                  pl.BlockSpec((SSM_CONV, cw), lambda b, j, ti: (0, j)),
                  pl.BlockSpec((1, cw), lambda b, j, ti: (0, j))],
        out_specs=[pl.BlockSpec((None, c, cw), lambda b, j, ti: (b, ti, j)),
                   pl.BlockSpec((None, SUBLANES, cw), lambda b, j, ti: (b, 0, j))],
        out_shape=[jax.ShapeDtypeStruct((nb, tpad, SSM_CH), F32),
                   jax.ShapeDtypeStruct((nb, SUBLANES, SSM_CH), F32)],
        compiler_params=_cp("parallel", "parallel", "arbitrary"),
        name="ssd_conv",
    )(zx, zx, cs_pad, conv_w, conv_b.reshape(1, SSM_CH))


def _expand_heads(v, e):
    hi = v.astype(BF16)
    r1 = v - hi.astype(F32)
    mid = r1.astype(BF16)
    lo = (r1 - mid.astype(F32)).astype(BF16)
    return (jnp.dot(hi, e, preferred_element_type=F32) + jnp.dot(mid, e, preferred_element_type=F32)
            + jnp.dot(lo, e, preferred_element_type=F32))


def _ssd_body(xc_ref, z_ref, dt_ref, e_ref, dtb_ref, alog_ref, dx_ref, ng_ref, s0_ref, o_ref, so_ref, st_ref,
              *, c, t, nc):
    ci = pl.program_id(1)
    gw = SSM_HPG * SSM_P

    @pl.when(ci == 0)
    def _():
        for blk in range(SSM_INNER // LANES):
            st_ref[:, blk * LANES:(blk + 1) * LANES] = s0_ref[blk * LANES:(blk + 1) * LANES, :].T

    e = e_ref[...]
    rows = lax.broadcasted_iota(jnp.int32, (c, 1), 0)
    tri_b = lax.broadcasted_iota(jnp.int32, (c, c), 0) >= lax.broadcasted_iota(jnp.int32, (c, c), 1)
    lane_lo = lax.broadcasted_iota(jnp.int32, (1, LANES), 1) < SSM_P

    dt = jax.nn.softplus(dt_ref[...] + dtb_ref[...])
    if t % c:
        dt = jnp.where(ci * c + rows < t, dt, 0.0)
    da = dt * (-jnp.exp(alog_ref[...]))
    bcum = jnp.dot(tri_b.astype(F32), da, precision=lax.Precision.HIGHEST, preferred_element_type=F32)
    bcum_t = bcum.T
    bl = bcum[c - 1:c, :]
    dt_x = _expand_heads(dt, e)
    eb_x = _expand_heads(jnp.exp(bcum), e)
    w_x = _expand_heads(jnp.exp(bl - bcum) * dt, e)
    decay_x = _expand_heads(jnp.broadcast_to(jnp.exp(bl), (SUBLANES, LANES)), e)[0:1, :]

    xs = xc_ref[:, 0:SSM_INNER]
    xdt = (xs * dt_x).astype(BF16)
    xw = (xs * w_x).astype(BF16)
    y = xs * dx_ref[...]
    zg = _silu(z_ref[...])
    for g in range(SSM_GROUPS):
        bg = xc_ref[:, SSM_INNER + g * SSM_N:SSM_INNER + (g + 1) * SSM_N]
        cg = xc_ref[:, SSM_INNER + (SSM_GROUPS + g) * SSM_N:SSM_INNER + (SSM_GROUPS + g + 1) * SSM_N].astype(BF16)
        cb = _nt(cg, bg.astype(BF16))
        st_g = st_ref[:, g * gw:(g + 1) * gw]
        yg = jnp.dot(cg, st_g.astype(BF16), preferred_element_type=F32) * eb_x[:, g * gw:(g + 1) * gw]
        parts = []
        for jp in range(SSM_HPG // 2):
            xpair = xdt[:, g * gw + jp * LANES:g * gw + (jp + 1) * LANES]
            acc = None
            for half in range(2):
                h = g * SSM_HPG + jp * 2 + half
                dec = jnp.exp(jnp.where(tri_b, bcum[:, h:h + 1] - bcum_t[h:h + 1, :], -jnp.inf))
                w = (cb * dec).astype(BF16)
                xh = jnp.where(lane_lo if half == 0 else jnp.logical_not(lane_lo), xpair, 0.0).astype(BF16)
                r = jnp.dot(w, xh, preferred_element_type=F32)
                acc = r if acc is None else acc + r
            parts.append(acc)
        yg = yg + jnp.concatenate(parts, axis=1)
        st_ref[:, g * gw:(g + 1) * gw] = (st_g * decay_x[:, g * gw:(g + 1) * gw]
                                          + jnp.dot(bg.T.astype(BF16), xw[:, g * gw:(g + 1) * gw],
                                                    preferred_element_type=F32))
        yg = (yg + y[:, g * gw:(g + 1) * gw]) * zg[:, g * gw:(g + 1) * gw]
        yg = yg * lax.rsqrt(jnp.mean(yg * yg, axis=-1, keepdims=True) + RMS_EPS) * ng_ref[:, g * gw:(g + 1) * gw]
        o_ref[:, g * gw:(g + 1) * gw] = yg.astype(o_ref.dtype)

    @pl.when(ci == nc - 1)
    def _():
        for blk in range(SSM_INNER // LANES):
            so_ref[blk * LANES:(blk + 1) * LANES, :] = st_ref[:, blk * LANES:(blk + 1) * LANES].T


def _ssd(xc, z, dt_raw, expand, dt_bias, a_log, d_x, norm_g, s0, t, c=128):
    nb, tpad, _ = xc.shape
    nc = tpad // c
    return pl.pallas_call(
        functools.partial(_ssd_body, c=c, t=t, nc=nc),
        grid=(nb, nc),
        in_specs=[pl.BlockSpec((None, c, SSM_CH), lambda b, ci: (b, ci, 0)),
                  pl.BlockSpec((None, c, SSM_INNER), lambda b, ci: (b, ci, 0)),
                  pl.BlockSpec((None, c, LANES), lambda b, ci: (b, ci, 0)),
                  pl.BlockSpec((LANES, SSM_INNER), lambda b, ci: (0, 0)),
                  pl.BlockSpec((1, LANES), lambda b, ci: (0, 0)),
                  pl.BlockSpec((1, LANES), lambda b, ci: (0, 0)),
                  pl.BlockSpec((1, SSM_INNER), lambda b, ci: (0, 0)),
                  pl.BlockSpec((1, SSM_INNER), lambda b, ci: (0, 0)),
                  pl.BlockSpec((None, SSM_INNER, SSM_N), lambda b, ci: (b, 0, 0))],
        out_specs=[pl.BlockSpec((None, c, SSM_INNER), lambda b, ci: (b, ci, 0)),
                   pl.BlockSpec((None, SSM_INNER, SSM_N), lambda b, ci: (b, 0, 0))],
        out_shape=[jax.ShapeDtypeStruct((nb, tpad, SSM_INNER), BF16),
                   jax.ShapeDtypeStruct((nb, SSM_INNER, SSM_N), F32)],
        scratch_shapes=[pltpu.VMEM((SSM_N, SSM_INNER), F32)],
        compiler_params=_cp("parallel", "arbitrary"),
        name="ssd",
    )(xc, z, dt_raw, expand, dt_bias, a_log, d_x, norm_g.reshape(1, SSM_INNER), s0)


def _pad_cols(w, n):
    return jnp.pad(w, ((0, 0), (0, n - w.shape[1])))


def _pad_time(a, tpad):
    return jnp.pad(a, ((0, 0), (0, tpad - a.shape[1]), (0, 0)))


def kernel(x_prompt, x_sample, cache_k, cache_v, cache_idx_k, state_hgrn, state_ssm, state_conv, page_table, p_prompt, p_sample, ln_g, ln_b, ffn_w_gate_up, ffn_w_down, ple_w_proj, ple_w_gate, att_w_in, att_idx_k_norm, att_w_o, hg_w_in, hg_lb_logits, hg_norm_g, hg_w_o, ssm_w_in, ssm_conv_w, ssm_conv_b, ssm_dt_bias, ssm_a_log, ssm_d, ssm_norm_g, ssm_w_o):
    nbp, seq, d = x_prompt.shape
    nbs = x_sample.shape[0]
    mp = nbp * seq
    ms = 16
    chunk = 128

    streams = {
        "p": dict(x=x_prompt.reshape(mp, d), m=mp, tm=1024, tml=512),
        "s": dict(x=jnp.pad(x_sample.reshape(nbs, d), ((0, ms - nbs), (0, 0))), m=ms, tm=ms, tml=ms),
    }
    for st in streams.values():
        st["xb"] = st["x"].astype(BF16)
    p_in = {"p": p_prompt.reshape(DEPTH, mp, PLE_DIM),
            "s": jnp.pad(p_sample.reshape(DEPTH, nbs, PLE_DIM), ((0, 0), (0, ms - nbs), (0, 0)))}

    expand = jnp.asarray(np.kron(np.eye(LANES, SSM_HEADS, dtype=np.float32),
                                 np.ones((1, SSM_P), np.float32)), BF16)
    outs = {}

    def ffn_ln(st, i, which, ln_idx):
        wgu = ffn_w_gate_up[i, which].astype(BF16)
        wd = ffn_w_down[i, which].astype(BF16)
        h = _mm_swiglu(st["xb"], wgu, st["tm"], 512)
        st["x"], st["xb"] = _mm_ln(h, wd, st["x"], ln_g[i, ln_idx], ln_b[i, ln_idx], 0.5, st["tml"], 1024, "ffn_down_ln")

    for i in range(DEPTH):
        j = i // N_MIXERS
        for st in streams.values():
            ffn_ln(st, i, 0, 0)

        if i % N_MIXERS == 0:
            w_main = att_w_in[j][:, :ATT_MAIN].astype(BF16)
            w_small = _pad_cols(att_w_in[j][:, ATT_MAIN:], LANES).astype(BF16)
            w_o = att_w_o[j].astype(BF16)
            qw, kvw = ATT_HEADS * ATT_HD, ATT_KV * ATT_HD
            mix = {}
            for name, st in streams.items():
                proj, projb = _mm(st["xb"], w_main, st["tm"], 512, with_bf16=True, name="att_in")
                proj2 = _mm(st["xb"], w_small, st["tm"], LANES, name="att_in_idx")
                ik, ikb = _ik_norm(proj2, att_idx_k_norm[j], st["tm"])
                if name == "p":
                    o = _dsa_prompt(projb, proj2, ikb, nbp, seq)
                    outs.setdefault("k_p", []).append(proj[:, qw:qw + kvw].reshape(nbp, seq, ATT_KV, ATT_HD))
                    outs.setdefault("v_p", []).append(proj[:, qw + kvw:qw + 2 * kvw].reshape(nbp, seq, ATT_KV, ATT_HD))
                    outs.setdefault("ik_p", []).append(ik.reshape(nbp, seq, IDX_DIM))
                else:
                    pr = proj[:nbs]
                    k_new = pr[:, qw:qw + kvw]
                    v_new = pr[:, qw + kvw:qw + 2 * kvw]
                    ik_new = ik[:nbs]
                    n_pool = cache_k.shape[1]
                    scores = _dsa_s_scores(page_table,
                                           pr[:, qw + 2 * kvw:].reshape(nbs, IDX_HEADS, IDX_DIM),
                                           proj2[:nbs, IDX_DIM:IDX_DIM + IDX_HEADS].reshape(nbs, IDX_HEADS, 1),
                                           ik_new.reshape(nbs, 1, IDX_DIM), cache_idx_k[j])
                    o = _dsa_s_attend(page_table, scores, pr[:, :qw].reshape(nbs, ATT_HEADS, ATT_HD),
                                      k_new.reshape(nbs, 1, kvw), v_new.reshape(nbs, 1, kvw),
                                      cache_k[j].reshape(n_pool, PAGE, kvw), cache_v[j].reshape(n_pool, PAGE, kvw))
                    o = jnp.pad(o.reshape(nbs, qw), ((0, ms - nbs), (0, 0))).astype(BF16)
                    outs.setdefault("k_s", []).append(k_new.reshape(nbs, 1, ATT_KV, ATT_HD))
                    outs.setdefault("v_s", []).append(v_new.reshape(nbs, 1, ATT_KV, ATT_HD))
                    outs.setdefault("ik_s", []).append(ik_new.reshape(nbs, 1, IDX_DIM))
                mix[name] = (o, w_o)
        elif i % N_MIXERS == 1:
            w_in = hg_w_in[j].astype(BF16)
            w_o = hg_w_o[j].astype(BF16)
            mix = {}
            for name, st in streams.items():
                proj = _mm(st["xb"], w_in, st["tm"], 512, name="hg_in")
                if name == "p":
                    s0 = jnp.zeros((nbp, HG_HEADS, HG_DK, HG_DV), F32)
                    o, s_fin = _hgrn(proj.reshape(nbp, seq, -1), hg_lb_logits, hg_norm_g[j], s0, i, seq, chunk)
                    o = o.reshape(mp, -1)
                    outs.setdefault("hg_p", []).append(s_fin)
                else:
                    pr = _pad_time(proj[:nbs].reshape(nbs, 1, -1), chunk)
                    o, s_fin = _hgrn(pr, hg_lb_logits, hg_norm_g[j], state_hgrn[j], i, 1, chunk)
                    o = jnp.pad(o[:, 0, :], ((0, ms - nbs), (0, 0)))
                    outs.setdefault("hg_s", []).append(s_fin)
                mix[name] = (o, w_o)
        else:
            w_main = ssm_w_in[j][:, :SSM_MAIN].astype(BF16)
            w_small = _pad_cols(ssm_w_in[j][:, SSM_MAIN:], LANES).astype(BF16)
            w_o = ssm_w_o[j].astype(BF16)
            dt_bias = _pad_cols(ssm_dt_bias[j].reshape(1, SSM_HEADS), LANES)
            a_log = _pad_cols(ssm_a_log[j].reshape(1, SSM_HEADS), LANES)
            d_x = jnp.repeat(ssm_d[j], SSM_P).reshape(1, SSM_INNER)
            mix = {}
            for name, st in streams.items():
                zx = _mm(st["xb"], w_main, st["tm"], 512, name="ssm_in")
                dtr = _mm(st["xb"], w_small, st["tm"], LANES, name="ssm_in_dt")
                if name == "p":
                    nb_, t_, c_conv = nbp, seq, chunk
                    zx3 = zx.reshape(nbp, seq, -1)
                    dt3 = dtr.reshape(nbp, seq, LANES)
                    cs = jnp.zeros((nbp, SUBLANES, SSM_CH), F32)
                    s0 = jnp.zeros((nbp, SSM_INNER, SSM_N), F32)
                else:
                    nb_, t_, c_conv = nbs, 1, SUBLANES
                    zx3 = _pad_time(zx[:nbs].reshape(nbs, 1, -1), SUBLANES)
                    dt3 = _pad_time(dtr[:nbs].reshape(nbs, 1, LANES), chunk)
                    cs = jnp.pad(state_conv[j], ((0, 0), (SUBLANES - (SSM_CONV - 1), 0), (0, 0)))
                    s0 = state_ssm[j].reshape(nbs, SSM_INNER, SSM_N)
                xc, new_conv = _ssd_conv(zx3, cs, ssm_conv_w[j], ssm_conv_b[j], t_, c_conv)
                if name == "p":
                    z3 = zx3
                else:
                    xc = _pad_time(xc, chunk)
                    z3 = _pad_time(zx3[:, :, :SSM_INNER], chunk)
                y, s_fin = _ssd(xc, z3, dt3, expand, dt_bias, a_log, d_x, ssm_norm_g[j], s0, t_, chunk)
                s_fin = s_fin.reshape(nb_, SSM_HEADS, SSM_P, SSM_N)
                new_conv = new_conv[:, :SSM_CONV - 1, :]
                if name == "p":
                    o = y.reshape(mp, SSM_INNER)
                    outs.setdefault("ssm_p", []).append(s_fin)
                    outs.setdefault("conv_p", []).append(new_conv)
                else:
                    o = jnp.pad(y[:, 0, :], ((0, ms - nbs), (0, 0)))
                    outs.setdefault("ssm_s", []).append(s_fin)
                    outs.setdefault("conv_s", []).append(new_conv)
                mix[name] = (o, w_o)

        wg = ple_w_gate[i].astype(BF16)
        wp = ple_w_proj[i].astype(BF16)
        for name, st in streams.items():
            o, w_o = mix[name]
            tk = 1024
            st["x"], st["xb"] = _mm_ln(o, w_o, st["x"], ln_g[i, 1], ln_b[i, 1], 1.0, st["tml"], tk, "mixer_out_ln")
            ffn_ln(st, i, 1, 2)
            st["x"], st["xb"] = _mm_ple(st["x"], st["xb"], p_in[name][i].astype(BF16), wg, wp, st["tm"], 512)

    y_prompt = streams["p"]["x"].reshape(nbp, seq, d)
    y_sample = streams["s"]["x"][:nbs].reshape(nbs, 1, d)
    stack = lambda key: jnp.stack(outs[key])
    return (y_prompt, y_sample, stack("k_p"), stack("v_p"), stack("ik_p"), stack("k_s"), stack("v_s"), stack("ik_s"),
            stack("hg_p"), stack("hg_s"), stack("ssm_p"), stack("ssm_s"), stack("conv_p"), stack("conv_s"))
```

```python
import functools
import math

import jax
import jax.numpy as jnp
import numpy as np
from jax import lax
from jax.experimental import pallas as pl
from jax.experimental.pallas import tpu as pltpu

F32 = jnp.float32
BF16 = jnp.bfloat16

D_MODEL = 2048
DEPTH = 4
N_MIXERS = 3
D_FF = 2 * D_MODEL
PLE_DIM = 256
ALPHA = (2 * DEPTH) ** 0.25
LN_EPS = 1e-5
RMS_EPS = 1e-6
PAGE = 128

ATT_HD = 128
ATT_HEADS = 16
ATT_KV = 4
ATT_GROUP = ATT_HEADS // ATT_KV
IDX_HEADS = 16
IDX_DIM = 64
IDX_W_SCALE = (IDX_HEADS ** -0.5) * (IDX_DIM ** -0.5)
TOPK_MAX = 256
ATT_SCALE = ATT_HD ** -0.5
ATT_MAIN = ATT_HEADS * ATT_HD + 2 * ATT_KV * ATT_HD + IDX_HEADS * IDX_DIM

HG_HEADS = 16
HG_DK = 128
HG_DV = 128

SSM_INNER = 2 * D_MODEL
SSM_P = 64
SSM_HEADS = SSM_INNER // SSM_P
SSM_GROUPS = 8
SSM_HPG = SSM_HEADS // SSM_GROUPS
SSM_N = 128
SSM_CONV = 4
SSM_CH = SSM_INNER + 2 * SSM_GROUPS * SSM_N
SSM_MAIN = SSM_INNER + SSM_CH

LANES = 128
SUBLANES = 8
VMEM_LIMIT_BYTES = 56 * 1024 * 1024

INT_MIN = np.int32(-2 ** 31)
NEG_BIG = -1e30


def _cp(*sem):
    return pltpu.CompilerParams(dimension_semantics=sem, vmem_limit_bytes=VMEM_LIMIT_BYTES)


def _nt(a, b):
    return lax.dot_general(a, b, (((1,), (1,)), ((), ())), preferred_element_type=F32)


def _tn(a, b):
    return lax.dot_general(a, b, (((0,), (0,)), ((), ())), preferred_element_type=F32)


def _silu(x):
    return x * jax.nn.sigmoid(x)


def _wspec(k, tn, idx, col_block):
    lead = (None,) * len(idx)
    return pl.BlockSpec(lead + (k, tn), lambda n, i: idx + (0, col_block(n)))


def _proj_body(x_ref, w_ref, o_ref, *rest):
    wb_ref = rest[-1]

    @pl.when(pl.program_id(1) == 0)
    def _():
        wb_ref[...] = w_ref[...].astype(BF16)

    acc = jnp.dot(x_ref[...], wb_ref[...], preferred_element_type=F32)
    o_ref[...] = acc
    if len(rest) == 2:
        rest[0][...] = acc.astype(BF16)


def _proj(xb, w, idx, n_cols, tm, tn, with_bf16=False, name="proj"):
    m, k = xb.shape
    out_shape = [jax.ShapeDtypeStruct((m, n_cols), F32)]
    out_specs = [pl.BlockSpec((tm, tn), lambda n, i: (i, n))]
    if with_bf16:
        out_shape.append(jax.ShapeDtypeStruct((m, n_cols), BF16))
        out_specs.append(pl.BlockSpec((tm, tn), lambda n, i: (i, n)))
    res = pl.pallas_call(
        _proj_body,
        grid=(n_cols // tn, m // tm),
        in_specs=[pl.BlockSpec((tm, k), lambda n, i: (i, 0)), _wspec(k, tn, idx, lambda n: n)],
        out_specs=out_specs,
        out_shape=out_shape,
        scratch_shapes=[pltpu.VMEM((k, tn), BF16)],
        compiler_params=_cp("parallel", "arbitrary"),
        name=name,
    )(xb, w)
    return res if with_bf16 else res[0]


def _swiglu_body(x_ref, wg_ref, wu_ref, o_ref, wgb_ref, wub_ref):
    @pl.when(pl.program_id(1) == 0)
    def _():
        wgb_ref[...] = wg_ref[...].astype(BF16)
        wub_ref[...] = wu_ref[...].astype(BF16)

    x = x_ref[...]
    g = jnp.dot(x, wgb_ref[...], preferred_element_type=F32)
    u = jnp.dot(x, wub_ref[...], preferred_element_type=F32)
    o_ref[...] = (_silu(g) * u).astype(o_ref.dtype)


def _mm_swiglu(xb, wgu, idx, tm, tn):
    m, k = xb.shape
    f = wgu.shape[-1] // 2
    nj = f // tn
    return pl.pallas_call(
        _swiglu_body,
        grid=(nj, m // tm),
        in_specs=[pl.BlockSpec((tm, k), lambda n, i: (i, 0)),
                  _wspec(k, tn, idx, lambda n: n),
                  _wspec(k, tn, idx, lambda n: n + nj)],
        out_specs=pl.BlockSpec((tm, tn), lambda n, i: (i, n)),
        out_shape=jax.ShapeDtypeStruct((m, f), BF16),
        scratch_shapes=[pltpu.VMEM((k, tn), BF16), pltpu.VMEM((k, tn), BF16)],
        compiler_params=_cp("parallel", "arbitrary"),
        name="ffn_up",
    )(xb, wgu, wgu)


def _cast_body(w_ref, o_ref):
    o_ref[...] = w_ref[...].astype(BF16)


def _cast_w(w, idx, tk=512):
    k, n = w.shape[-2:]
    lead = (None,) * len(idx)
    return pl.pallas_call(
        _cast_body,
        grid=(k // tk,),
        in_specs=[pl.BlockSpec(lead + (tk, n), lambda i: idx + (i, 0))],
        out_specs=pl.BlockSpec((tk, n), lambda i: (i, 0)),
        out_shape=jax.ShapeDtypeStruct((k, n), BF16),
        compiler_params=_cp("parallel"),
        name="cast_w",
    )(w)


LN_ROWS = 128
MM_ROWS = 256


def _mm_ln_body(a_ref, w_ref, r_ref, g_ref, b_ref, o_ref, ob_ref, *, scale, nk, tm):
    k = pl.program_id(1)

    @pl.when(k == 0)
    def _():
        o_ref[...] = jnp.zeros_like(o_ref)

    w = w_ref[...]
    for r in range(0, tm, MM_ROWS):
        sl = pl.ds(r, min(MM_ROWS, tm - r))
        o_ref[sl, :] += jnp.dot(a_ref[sl, :], w, preferred_element_type=F32)

    @pl.when(k == nk - 1)
    def _():
        g = g_ref[...]
        b = b_ref[...]
        for r in range(0, tm, LN_ROWS):
            sl = pl.ds(r, min(LN_ROWS, tm - r))
            y = ALPHA * r_ref[sl, :] + scale * o_ref[sl, :]
            mu = jnp.mean(y, axis=-1, keepdims=True)
            yc = y - mu
            var = jnp.mean(yc * yc, axis=-1, keepdims=True)
            out = yc * lax.rsqrt(var + LN_EPS) * g + b
            o_ref[sl, :] = out
            ob_ref[sl, :] = out.astype(BF16)


def _mm_ln(ab, wb, res, g, b, scale, tm, tk, name):
    m, kdim = ab.shape
    n = wb.shape[1]
    nk = kdim // tk
    return pl.pallas_call(
        functools.partial(_mm_ln_body, scale=scale, nk=nk, tm=tm),
        grid=(m // tm, nk),
        in_specs=[pl.BlockSpec((tm, tk), lambda i, k: (i, k)),
                  pl.BlockSpec((tk, n), lambda i, k: (k, 0)),
                  pl.BlockSpec((tm, n), lambda i, k: (i, 0)),
                  pl.BlockSpec((1, n), lambda i, k: (0, 0)),
                  pl.BlockSpec((1, n), lambda i, k: (0, 0))],
        out_specs=[pl.BlockSpec((tm, n), lambda i, k: (i, 0)),
                   pl.BlockSpec((tm, n), lambda i, k: (i, 0))],
        out_shape=[jax.ShapeDtypeStruct((m, n), F32), jax.ShapeDtypeStruct((m, n), BF16)],
        compiler_params=_cp("parallel", "arbitrary"),
        name=name,
    )(ab, wb, res, g.reshape(1, n), b.reshape(1, n))


def _ple_body(xb_ref, p_ref, wg_ref, wp_ref, x_ref, o_ref, ob_ref, wgb_ref, wpb_ref):
    @pl.when(pl.program_id(1) == 0)
    def _():
        wgb_ref[...] = wg_ref[...].astype(BF16)
        wpb_ref[...] = wp_ref[...].astype(BF16)

    gate = jax.nn.sigmoid(jnp.dot(xb_ref[...], wgb_ref[...], preferred_element_type=F32))
    proj = jnp.dot(p_ref[...], wpb_ref[...], preferred_element_type=F32)
    out = x_ref[...] + gate * proj
    o_ref[...] = out
    ob_ref[...] = out.astype(BF16)


def _mm_ple(x32, xb, pb, wg, wp, idx, tm, tn):
    m, d = xb.shape
    pd = pb.shape[1]
    return pl.pallas_call(
        _ple_body,
        grid=(d // tn, m // tm),
        in_specs=[pl.BlockSpec((tm, d), lambda n, i: (i, 0)),
                  pl.BlockSpec((tm, pd), lambda n, i: (i, 0)),
                  _wspec(d, tn, idx, lambda n: n),
                  _wspec(pd, tn, idx, lambda n: n),
                  pl.BlockSpec((tm, tn), lambda n, i: (i, n))],
        out_specs=[pl.BlockSpec((tm, tn), lambda n, i: (i, n)),
                   pl.BlockSpec((tm, tn), lambda n, i: (i, n))],
        out_shape=[jax.ShapeDtypeStruct((m, d), F32), jax.ShapeDtypeStruct((m, d), BF16)],
        scratch_shapes=[pltpu.VMEM((d, tn), BF16), pltpu.VMEM((pd, tn), BF16)],
        compiler_params=_cp("parallel", "arbitrary"),
        name="ple",
    )(xb, pb, wg, wp, x32)


def _ik_norm_body(p_ref, g_ref, o_ref, ob_ref):
    x = p_ref[...][:, :IDX_DIM]
    mu = jnp.mean(x, axis=-1, keepdims=True)
    xc = x - mu
    out = xc * lax.rsqrt(jnp.mean(xc * xc, axis=-1, keepdims=True) + LN_EPS) * g_ref[...]
    o_ref[...] = out
    ob_ref[...] = out.astype(BF16)


def _ik_norm(proj2, ik_g, tm):
    m = proj2.shape[0]
    return pl.pallas_call(
        _ik_norm_body,
        grid=(m // tm,),
        in_specs=[pl.BlockSpec((tm, LANES), lambda i: (i, 0)),
                  pl.BlockSpec((1, IDX_DIM), lambda i: (0, 0))],
        out_specs=[pl.BlockSpec((tm, IDX_DIM), lambda i: (i, 0)),
                   pl.BlockSpec((tm, IDX_DIM), lambda i: (i, 0))],
        out_shape=[jax.ShapeDtypeStruct((m, IDX_DIM), F32), jax.ShapeDtypeStruct((m, IDX_DIM), BF16)],
        compiler_params=_cp("parallel"),
        name="idx_k_norm",
    )(proj2, ik_g.reshape(1, IDX_DIM))


def _order_key(x):
    bits = pltpu.bitcast(x, jnp.int32)
    return jnp.where(bits < 0, bits ^ jnp.int32(0x7FFFFFFF), bits)


_RADIX_BITS = [INT_MIN] + [np.int32(1 << s) for s in range(30, -1, -1)]


def _dsa_prompt_body(q_ref, iq_ref, iw_ref, k_ref, v_ref, ik_ref, o_ref, keys_ref, *, qb, kc, topk):
    i = pl.program_id(1)
    nck = ((i + 1) * qb + kc - 1) // kc
    row_pos = i * qb + lax.broadcasted_iota(jnp.int32, (qb, 1), 0)
    iq = iq_ref[...]
    iw = iw_ref[...][:, IDX_DIM:IDX_DIM + IDX_HEADS] * IDX_W_SCALE
    iq_h = [iq[:, h * IDX_DIM:(h + 1) * IDX_DIM] for h in range(IDX_HEADS)]
    iw_h = [iw[:, h:h + 1] for h in range(IDX_HEADS)]
    col0 = lax.broadcasted_iota(jnp.int32, (1, kc), 1)

    def score_chunk(c, carry):
        off = pl.multiple_of(c * kc, kc)
        ikc = ik_ref[pl.ds(off, kc), :]
        sc = jnp.zeros((qb, kc), F32)
        for h in range(IDX_HEADS):
            sc = sc + iw_h[h] * jnp.maximum(_nt(iq_h[h], ikc), 0.0)
        key = jnp.where(col0 + off <= row_pos, _order_key(sc), INT_MIN)
        keys_ref[:, pl.ds(off, kc)] = key
        return carry

    lax.fori_loop(0, nck, score_chunk, 0)

    lo = jnp.full((qb, 1), INT_MIN, jnp.int32)
    for bit in _RADIX_BITS:
        cand = lo + bit

        def count_chunk(c, acc, cand=cand):
            off = pl.multiple_of(c * kc, kc)
            ge = jnp.where(keys_ref[:, pl.ds(off, kc)] >= cand, 1.0, 0.0)
            for t in range(kc // LANES):
                acc = acc + ge[:, t * LANES:(t + 1) * LANES]
            return acc

        cnt = jnp.sum(lax.fori_loop(0, nck, count_chunk, jnp.zeros((qb, LANES), F32)), axis=-1, keepdims=True)
        lo = jnp.where(cnt >= topk, cand, lo)
    thr = jnp.maximum(lo, INT_MIN + 1)

    for g in range(ATT_KV):
        qg = jnp.concatenate(
            [q_ref[:, (g * ATT_GROUP + j) * ATT_HD:(g * ATT_GROUP + j + 1) * ATT_HD] for j in range(ATT_GROUP)], axis=0)

        def att_chunk(c, carry, qg=qg, g=g):
            m, l, acc = carry
            off = pl.multiple_of(c * kc, kc)
            kch = k_ref[pl.ds(off, kc), g * ATT_HD:(g + 1) * ATT_HD]
            vch = v_ref[pl.ds(off, kc), g * ATT_HD:(g + 1) * ATT_HD]
            sel = (keys_ref[:, pl.ds(off, kc)] >= thr)[None]
            s = (_nt(qg, kch) * ATT_SCALE).reshape(ATT_GROUP, qb, kc)
            s = jnp.where(sel, s, NEG_BIG)
            m_new = jnp.maximum(m, jnp.max(s, axis=-1, keepdims=True))
            p = jnp.where(sel, jnp.exp(s - m_new), 0.0)
            a = jnp.exp(m - m_new)
            l = a * l + jnp.sum(p, axis=-1, keepdims=True)
            pv = jnp.dot(p.reshape(ATT_GROUP * qb, kc).astype(BF16), vch, preferred_element_type=F32)
            acc = a * acc + pv.reshape(ATT_GROUP, qb, ATT_HD)
            return m_new, l, acc

        init = (jnp.full((ATT_GROUP, qb, 1), NEG_BIG, F32), jnp.zeros((ATT_GROUP, qb, 1), F32),
                jnp.zeros((ATT_GROUP, qb, ATT_HD), F32))
        _, l, acc = lax.fori_loop(0, nck, att_chunk, init)
        out = acc / l
        for j in range(ATT_GROUP):
            h = g * ATT_GROUP + j
            o_ref[:, h * ATT_HD:(h + 1) * ATT_HD] = out[j].astype(o_ref.dtype)


def _dsa_prompt(projb, proj2, ikb, nb, s, qb=128, kc=512):
    nq = s // qb
    topk = min(TOPK_MAX, s // 4)
    qw = ATT_HEADS * ATT_HD
    kvw = ATT_KV * ATT_HD
    iqw = IDX_HEADS * IDX_DIM
    return pl.pallas_call(
        functools.partial(_dsa_prompt_body, qb=qb, kc=kc, topk=topk),
        grid=(nb, nq),
        in_specs=[pl.BlockSpec((qb, qw), lambda b, i: (b * nq + i, 0)),
                  pl.BlockSpec((qb, iqw), lambda b, i: (b * nq + i, (qw + 2 * kvw) // iqw)),
                  pl.BlockSpec((qb, LANES), lambda b, i: (b * nq + i, 0)),
                  pl.BlockSpec((s, kvw), lambda b, i: (b, qw // kvw)),
                  pl.BlockSpec((s, kvw), lambda b, i: (b, qw // kvw + 1)),
                  pl.BlockSpec((s, IDX_DIM), lambda b, i: (b, 0))],
        out_specs=pl.BlockSpec((qb, qw), lambda b, i: (b * nq + i, 0)),
        out_shape=jax.ShapeDtypeStruct((nb * s, qw), BF16),
        scratch_shapes=[pltpu.VMEM((qb, s), jnp.int32)],
        compiler_params=_cp("parallel", "arbitrary"),
        name="dsa_prompt",
    )(projb, projb, proj2, projb, projb, ikb)


SCORE_PAGES = 16


def _dsa_s_scores_body(pt_ref, iq_ref, iw_ref, ikn_ref, *rest, n_pages):
    page_refs, o_ref = rest[:SCORE_PAGES], rest[SCORE_PAGES]
    p = pl.program_id(1)
    iq = iq_ref[...].astype(BF16)
    iw = iw_ref[...] * IDX_W_SCALE

    @pl.when(p == 0)
    def _():
        own = jnp.broadcast_to(ikn_ref[...], (SUBLANES, IDX_DIM)).astype(BF16)
        d = jnp.maximum(_nt(iq, own), 0.0)
        sc = jnp.sum(iw * d, axis=0, keepdims=True)
        o_ref[n_pages:n_pages + SUBLANES, :] = jnp.broadcast_to(sc[:, 0:1], (SUBLANES, LANES))

    for r in range(SCORE_PAGES):
        d = jnp.maximum(_nt(iq, page_refs[r][...].astype(BF16)), 0.0)
        o_ref[pl.ds(p * SCORE_PAGES + r, 1), :] = jnp.sum(iw * d, axis=0, keepdims=True)


def _dsa_s_scores(page_table, iq, iw, ik_new, cache_ik, layer):
    nb, n_pages = page_table.shape
    page_spec = lambda r: pl.BlockSpec((None, None, PAGE, IDX_DIM),
                                       lambda b, p, pt: (layer, pt[b, p * SCORE_PAGES + r], 0, 0))
    return pl.pallas_call(
        functools.partial(_dsa_s_scores_body, n_pages=n_pages),
        grid_spec=pltpu.PrefetchScalarGridSpec(
            num_scalar_prefetch=1,
            grid=(nb, n_pages // SCORE_PAGES),
            in_specs=[pl.BlockSpec((None, IDX_HEADS, IDX_DIM), lambda b, p, pt: (b, 0, 0)),
                      pl.BlockSpec((None, IDX_HEADS, 1), lambda b, p, pt: (b, 0, 0)),
                      pl.BlockSpec((None, 1, IDX_DIM), lambda b, p, pt: (b, 0, 0))]
                     + [page_spec(r) for r in range(SCORE_PAGES)],
            out_specs=pl.BlockSpec((None, n_pages + SUBLANES, LANES), lambda b, p, pt: (b, 0, 0)),
        ),
        out_shape=jax.ShapeDtypeStruct((nb, n_pages + SUBLANES, LANES), F32),
        compiler_params=_cp("parallel", "arbitrary"),
        name="dsa_sample_scores",
    )(page_table, iq, iw, ik_new, *([cache_ik] * SCORE_PAGES))


def _dsa_s_select_body(sc_ref, ptc_ref, idx_ref, meta_ref, rank_ref, phys_ref, *, n_pages, topk):
    shape = (n_pages + SUBLANES, LANES)
    rows = lax.broadcasted_iota(jnp.int32, shape, 0)
    cols = lax.broadcasted_iota(jnp.int32, shape, 1)
    live = (rows < n_pages) | ((rows == n_pages) & (cols == 0))
    keys = jnp.where(live, _order_key(sc_ref[...]), INT_MIN)
    lo = jnp.full((1, 1), INT_MIN, jnp.int32)
    for bit in _RADIX_BITS:
        cand = lo + bit
        cnt = jnp.sum(jnp.sum(jnp.where(keys >= cand, 1.0, 0.0), axis=0, keepdims=True), axis=1, keepdims=True)
        lo = jnp.where(cnt >= topk, cand, lo)
    sel = keys >= jnp.maximum(lo, INT_MIN + 1)
    sel_c = sel[:n_pages, :]
    own = jnp.where(sel[n_pages:n_pages + 1, 0:1], 1, 0)

    r_i = lax.broadcasted_iota(jnp.int32, (LANES, LANES), 0)
    c_i = lax.broadcasted_iota(jnp.int32, (LANES, LANES), 1)
    ones_le = jnp.where(r_i <= c_i, 1.0, 0.0).astype(BF16)
    ones_gt = jnp.where(r_i > c_i, 1.0, 0.0).astype(BF16)
    m = jnp.where(sel_c, 1.0, 0.0).astype(BF16)
    within = jnp.dot(m, ones_le, preferred_element_type=F32)
    tot = jnp.broadcast_to(within[:, LANES - 1:LANES], (n_pages, LANES)).astype(BF16)
    before = jnp.dot(ones_gt, tot, preferred_element_type=F32)
    rank_ref[...] = jnp.where(sel_c, (within + before).astype(jnp.int32) - 1, -1)
    phys_ref[...] = (ptc_ref[...] * PAGE + c_i[:n_pages, :]).astype(F32)
    n_sel = (before[n_pages - 1:n_pages, 0:1] + within[n_pages - 1:n_pages, LANES - 1:LANES]).astype(jnp.int32)

    slot = lax.broadcasted_iota(jnp.int32, (topk, 1), 0)

    def place(r, acc):
        return acc + jnp.where(rank_ref[pl.ds(r, 1), :] == slot, phys_ref[pl.ds(r, 1), :], 0.0)

    acc = lax.fori_loop(0, n_pages, place, jnp.zeros((topk, LANES), F32), unroll=8)
    idx_ref[...] = jnp.sum(acc, axis=-1, keepdims=True).astype(jnp.int32)
    mrow = lax.broadcasted_iota(jnp.int32, (SUBLANES, LANES), 0)
    meta_ref[...] = jnp.where(mrow == 0, jnp.minimum(n_sel, topk), own)


def _dsa_s_select(scores, page_table):
    nb, n_pages = page_table.shape
    assert n_pages == LANES
    topk = min(TOPK_MAX, (n_pages * PAGE + 1) // 4)
    idx, meta = pl.pallas_call(
        functools.partial(_dsa_s_select_body, n_pages=n_pages, topk=topk),
        grid=(nb,),
        in_specs=[pl.BlockSpec((None, n_pages + SUBLANES, LANES), lambda b: (b, 0, 0)),
                  pl.BlockSpec((None, n_pages, 1), lambda b: (b, 0, 0))],
        out_specs=[pl.BlockSpec((None, topk, 1), lambda b: (b, 0, 0)),
                   pl.BlockSpec((None, SUBLANES, LANES), lambda b: (b, 0, 0))],
        out_shape=[jax.ShapeDtypeStruct((nb, topk, 1), jnp.int32),
                   jax.ShapeDtypeStruct((nb, SUBLANES, LANES), jnp.int32)],
        scratch_shapes=[pltpu.VMEM((n_pages, LANES), jnp.int32), pltpu.VMEM((n_pages, LANES), F32)],
        compiler_params=_cp("parallel"),
        name="dsa_sample_select",
    )(scores, page_table.reshape(nb, n_pages, 1))
    return idx.reshape(nb, topk), meta[:, 0:2, 0]


def _dsa_s_attend_body(idx_ref, meta_ref, q_ref, kn_ref, vn_ref, ck_ref, cv_ref, o_ref, kbuf, vbuf, sem,
                       *, layer, topk):
    b = pl.program_id(0)
    nb = pl.num_programs(0)
    slot = b % 2

    def row_copies(tok, buf_slot, j):
        row = idx_ref[tok, j]
        page, off = row // PAGE, row % PAGE
        dst = pl.ds(j * ATT_KV, ATT_KV)
        return (pltpu.make_async_copy(ck_ref.at[layer, page, off], kbuf.at[buf_slot, dst, :], sem.at[buf_slot, 0]),
                pltpu.make_async_copy(cv_ref.at[layer, page, off], vbuf.at[buf_slot, dst, :], sem.at[buf_slot, 1]))

    def start_gather(tok, buf_slot):
        def body(j, carry):
            for cp in row_copies(tok, buf_slot, j):
                cp.start()
            return carry
        lax.fori_loop(0, topk, body, 0)

    @pl.when(b == 0)
    def _():
        start_gather(0, 0)

    @pl.when(b + 1 < nb)
    def _():
        start_gather(b + 1, 1 - slot)

    def wait_body(j, carry):
        for cp in row_copies(b, slot, j):
            cp.wait()
        return carry

    lax.fori_loop(0, topk, wait_body, 0)

    n_sel = meta_ref[b, 0]
    own = meta_ref[b, 1] > 0
    qb = q_ref[...].astype(BF16)
    ncol = topk * ATT_KV
    col = lax.broadcasted_iota(jnp.int32, (ATT_HEADS, ncol), 1)
    head = lax.broadcasted_iota(jnp.int32, (ATT_HEADS, ncol), 0)
    valid = (col % ATT_KV == head // ATT_GROUP) & (col // ATT_KV < n_sel)
    s = jnp.where(valid, _nt(qb, kbuf[slot].astype(BF16)) * ATT_SCALE, NEG_BIG)
    kn = kn_ref[...].astype(BF16).astype(F32)
    vn = vn_ref[...].astype(BF16).astype(F32)
    expand = lambda a: jnp.concatenate(
        [jnp.broadcast_to(a[:, g * ATT_HD:(g + 1) * ATT_HD], (ATT_GROUP, ATT_HD)) for g in range(ATT_KV)], axis=0)
    s_own = jnp.where(own, jnp.sum(qb.astype(F32) * expand(kn), axis=-1, keepdims=True) * ATT_SCALE, NEG_BIG)
    m = jnp.maximum(jnp.max(s, axis=-1, keepdims=True), s_own)
    p = jnp.where(valid, jnp.exp(s - m), 0.0)
    p_own = jnp.where(own, jnp.exp(s_own - m), 0.0)
    l = jnp.sum(p, axis=-1, keepdims=True) + p_own
    acc = (jnp.dot(p.astype(BF16), vbuf[slot].astype(BF16), preferred_element_type=F32)
           + p_own.astype(BF16).astype(F32) * expand(vn))
    o_ref[...] = acc / l


def _dsa_s_attend(idx, meta, q, k_new, v_new, cache_k, cache_v, layer):
    nb, topk = idx.shape
    kvw = ATT_KV * ATT_HD
    return pl.pallas_call(
        functools.partial(_dsa_s_attend_body, layer=layer, topk=topk),
        grid_spec=pltpu.PrefetchScalarGridSpec(
            num_scalar_prefetch=2,
            grid=(nb,),
            in_specs=[pl.BlockSpec((None, ATT_HEADS, ATT_HD), lambda b, idx, meta: (b, 0, 0)),
                      pl.BlockSpec((None, 1, kvw), lambda b, idx, meta: (b, 0, 0)),
                      pl.BlockSpec((None, 1, kvw), lambda b, idx, meta: (b, 0, 0)),
                      pl.BlockSpec(memory_space=pl.ANY),
                      pl.BlockSpec(memory_space=pl.ANY)],
            out_specs=pl.BlockSpec((None, ATT_HEADS, ATT_HD), lambda b, idx, meta: (b, 0, 0)),
            scratch_shapes=[pltpu.VMEM((2, topk * ATT_KV, ATT_HD), F32),
                            pltpu.VMEM((2, topk * ATT_KV, ATT_HD), F32),
                            pltpu.SemaphoreType.DMA((2, 2))],
        ),
        out_shape=jax.ShapeDtypeStruct((nb, ATT_HEADS, ATT_HD), F32),
        compiler_params=_cp("arbitrary"),
        name="dsa_sample_attend",
    )(idx, meta, q, k_new, v_new, cache_k, cache_v)


HG_SUB = 16


def _hgrn_body(lbl_ref, q_ref, f_ref, i_ref, g_ref, ng_ref, s0_ref, o_ref, so_ref, st_ref, *, layer, c, t, nc):
    ci = pl.program_id(2)

    @pl.when(ci == 0)
    def _():
        st_ref[...] = s0_ref[...].T

    logits = lbl_ref[...]
    e = jnp.exp(logits - jnp.max(logits, axis=0, keepdims=True))
    soft = e / jnp.sum(e, axis=0, keepdims=True)
    lb = jnp.zeros((1, HG_DK), F32)
    for r in range(1, layer + 1):
        lb = lb + soft[r:r + 1, :]

    rows = lax.broadcasted_iota(jnp.int32, (HG_SUB, 1), 0)
    tri = (lax.broadcasted_iota(jnp.int32, (HG_SUB, HG_SUB), 0)
           >= lax.broadcasted_iota(jnp.int32, (HG_SUB, HG_SUB), 1)).astype(F32)
    ng = ng_ref[...]
    nsub = min(c, -(-t // HG_SUB) * HG_SUB) // HG_SUB if nc == 1 else c // HG_SUB
    for sb in range(nsub):
        sl = pl.ds(sb * HG_SUB, HG_SUB)
        fg = lb + (1.0 - lb) * jax.nn.sigmoid(f_ref[sl, :])
        lf = jnp.log(fg)
        kk = 1.0 - fg
        if t % c:
            valid = (ci * c + sb * HG_SUB + rows) < t
            lf = jnp.where(valid, lf, 0.0)
            kk = jnp.where(valid, kk, 0.0)
        qq = _silu(q_ref[sl, :])
        vv = i_ref[sl, :]
        b = jnp.dot(tri, lf, precision=lax.Precision.HIGHEST, preferred_element_type=F32)
        st = st_ref[...]
        o = _nt((qq * jnp.exp(b)).astype(BF16), st.astype(BF16))
        for s in range(HG_SUB):
            dec = jnp.exp(jnp.where(rows >= s, b - b[s:s + 1, :], -jnp.inf))
            att = jnp.sum(qq * dec * kk[s:s + 1, :], axis=-1, keepdims=True)
            o = o + att * vv[s:s + 1, :]
        bl = b[HG_SUB - 1:HG_SUB, :]
        kt = kk * jnp.exp(bl - b)
        st_ref[...] = st * jnp.exp(bl) + _tn(vv.astype(BF16), kt.astype(BF16))
        on = o * lax.rsqrt(jnp.mean(o * o, axis=-1, keepdims=True) + RMS_EPS) * ng
        o_ref[sl, :] = (on * _silu(g_ref[sl, :])).astype(o_ref.dtype)
    if nsub * HG_SUB < c:
        o_ref[pl.ds(nsub * HG_SUB, c - nsub * HG_SUB), :] = jnp.zeros((c - nsub * HG_SUB, HG_DV), o_ref.dtype)

    @pl.when(ci == nc - 1)
    def _():
        so_ref[...] = st_ref[...].T


def _hgrn(proj, lb_logits, norm_g, s0, layer, t, c=128):
    nb, tpad, _ = proj.shape
    nc = tpad // c
    h_ = HG_HEADS
    return pl.pallas_call(
        functools.partial(_hgrn_body, layer=layer, c=c, t=t, nc=nc),
        grid=(nb, h_, nc),
        in_specs=[pl.BlockSpec((DEPTH, HG_DK), lambda b, h, ci: (0, h)),
                  pl.BlockSpec((None, c, HG_DK), lambda b, h, ci: (b, ci, h)),
                  pl.BlockSpec((None, c, HG_DK), lambda b, h, ci: (b, ci, h_ + h)),
                  pl.BlockSpec((None, c, HG_DV), lambda b, h, ci: (b, ci, 2 * h_ + h)),
                  pl.BlockSpec((None, c, HG_DV), lambda b, h, ci: (b, ci, 3 * h_ + h)),
                  pl.BlockSpec((1, HG_DV), lambda b, h, ci: (0, 0)),
                  pl.BlockSpec((None, None, HG_DK, HG_DV), lambda b, h, ci: (b, h, 0, 0))],
        out_specs=[pl.BlockSpec((None, c, HG_DV), lambda b, h, ci: (b, ci, h)),
                   pl.BlockSpec((None, None, HG_DK, HG_DV), lambda b, h, ci: (b, h, 0, 0))],
        out_shape=[jax.ShapeDtypeStruct((nb, tpad, h_ * HG_DV), BF16),
                   jax.ShapeDtypeStruct((nb, h_, HG_DK, HG_DV), F32)],
        scratch_shapes=[pltpu.VMEM((HG_DV, HG_DK), F32)],
        compiler_params=_cp("parallel", "parallel", "arbitrary"),
        name="hgrn2",
    )(lb_logits, proj, proj, proj, proj, norm_g.reshape(1, HG_DV), s0)


def _conv_body(x_ref, halo_ref, cs_ref, w_ref, b_ref, o_ref, nc_ref, *, c, t, nc):
    ti = pl.program_id(2)
    halo = jnp.where(ti == 0, cs_ref[...], halo_ref[...])
    full = jnp.concatenate([halo, x_ref[...]], axis=0)
    w = w_ref[...]
    conv = b_ref[...]
    for k in range(SSM_CONV):
        lo = SUBLANES - (SSM_CONV - 1) + k
        conv = conv + full[lo:lo + c, :] * w[k:k + 1, :]
    o_ref[...] = _silu(conv)

    @pl.when(ti == nc - 1)
    def _():
        tv = t - (nc - 1) * c
        tail = full[SUBLANES + tv - (SSM_CONV - 1):SUBLANES + tv, :]
        nc_ref[...] = jnp.concatenate([tail, jnp.zeros((SUBLANES - (SSM_CONV - 1), tail.shape[1]), F32)], axis=0)


def _ssd_conv(zx, cs_pad, conv_w, conv_b, t, c, cw=512):
    nb, tpad, _ = zx.shape
    nc = tpad // c
    ncol = SSM_CH // cw
    col0 = SSM_INNER // cw
    hb = c // SUBLANES
    return pl.pallas_call(
        functools.partial(_conv_body, c=c, t=t, nc=nc),
        grid=(nb, ncol, nc),
        in_specs=[pl.BlockSpec((None, c, cw), lambda b, j, ti: (b, ti, col0 + j)),
                  pl.BlockSpec((None, SUBLANES, cw), lambda b, j, ti: (b, jnp.maximum(ti * hb - 1, 0), col0 + j)),
                  pl.BlockSpec((None, SUBLANES, cw), lambda b, j, ti: (b, 0, j)),
                  pl.BlockSpec((SSM_CONV, cw), lambda b, j, ti: (0, j)),
                  pl.BlockSpec((1, cw), lambda b, j, ti: (0, j))],
        out_specs=[pl.BlockSpec((None, c, cw), lambda b, j, ti: (b, ti, j)),
                   pl.BlockSpec((None, SUBLANES, cw), lambda b, j, ti: (b, 0, j))],
        out_shape=[jax.ShapeDtypeStruct((nb, tpad, SSM_CH), F32),
                   jax.ShapeDtypeStruct((nb, SUBLANES, SSM_CH), F32)],
        compiler_params=_cp("parallel", "parallel", "arbitrary"),
        name="ssd_conv",
    )(zx, zx, cs_pad, conv_w, conv_b.reshape(1, SSM_CH))


def _expand_heads(v, e):
    hi = v.astype(BF16)
    r1 = v - hi.astype(F32)
    mid = r1.astype(BF16)
    lo = (r1 - mid.astype(F32)).astype(BF16)
    return (jnp.dot(hi, e, preferred_element_type=F32) + jnp.dot(mid, e, preferred_element_type=F32)
            + jnp.dot(lo, e, preferred_element_type=F32))


def _ssd_body(xc_ref, z_ref, dt_ref, e_ref, dtb_ref, alog_ref, dx_ref, ng_ref, s0_ref, o_ref, so_ref, st_ref,
              *, c, t, nc):
    ci = pl.program_id(1)
    gw = SSM_HPG * SSM_P

    @pl.when(ci == 0)
    def _():
        for blk in range(SSM_INNER // LANES):
            st_ref[:, blk * LANES:(blk + 1) * LANES] = s0_ref[blk * LANES:(blk + 1) * LANES, :].T

    e = e_ref[...]
    rows = lax.broadcasted_iota(jnp.int32, (c, 1), 0)
    tri_b = lax.broadcasted_iota(jnp.int32, (c, c), 0) >= lax.broadcasted_iota(jnp.int32, (c, c), 1)
    lane_lo = lax.broadcasted_iota(jnp.int32, (1, LANES), 1) < SSM_P

    dt = jax.nn.softplus(dt_ref[...] + dtb_ref[...])
    if t % c:
        dt = jnp.where(ci * c + rows < t, dt, 0.0)
    da = dt * (-jnp.exp(alog_ref[...]))
    bcum = jnp.dot(tri_b.astype(F32), da, precision=lax.Precision.HIGHEST, preferred_element_type=F32)
    bcum_t = bcum.T
    bl = bcum[c - 1:c, :]
    dt_x = _expand_heads(dt, e)
    eb_x = _expand_heads(jnp.exp(bcum), e)
    w_x = _expand_heads(jnp.exp(bl - bcum) * dt, e)
    decay_x = _expand_heads(jnp.broadcast_to(jnp.exp(bl), (SUBLANES, LANES)), e)[0:1, :]

    xs = xc_ref[:, 0:SSM_INNER]
    xdt = (xs * dt_x).astype(BF16)
    xw = (xs * w_x).astype(BF16)
    y = xs * dx_ref[...]
    zg = _silu(z_ref[...])
    for g in range(SSM_GROUPS):
        bg = xc_ref[:, SSM_INNER + g * SSM_N:SSM_INNER + (g + 1) * SSM_N]
        cg = xc_ref[:, SSM_INNER + (SSM_GROUPS + g) * SSM_N:SSM_INNER + (SSM_GROUPS + g + 1) * SSM_N].astype(BF16)
        cb = _nt(cg, bg.astype(BF16))
        st_g = st_ref[:, g * gw:(g + 1) * gw]
        yg = jnp.dot(cg, st_g.astype(BF16), preferred_element_type=F32) * eb_x[:, g * gw:(g + 1) * gw]
        parts = []
        for jp in range(SSM_HPG // 2):
            xpair = xdt[:, g * gw + jp * LANES:g * gw + (jp + 1) * LANES]
            acc = None
            for half in range(2):
                h = g * SSM_HPG + jp * 2 + half
                dec = jnp.exp(jnp.where(tri_b, bcum[:, h:h + 1] - bcum_t[h:h + 1, :], -jnp.inf))
                w = (cb * dec).astype(BF16)
                xh = jnp.where(lane_lo if half == 0 else jnp.logical_not(lane_lo), xpair, 0.0).astype(BF16)
                r = jnp.dot(w, xh, preferred_element_type=F32)
                acc = r if acc is None else acc + r
            parts.append(acc)
        yg = yg + jnp.concatenate(parts, axis=1)
        st_ref[:, g * gw:(g + 1) * gw] = (st_g * decay_x[:, g * gw:(g + 1) * gw]
                                          + jnp.dot(bg.T.astype(BF16), xw[:, g * gw:(g + 1) * gw],
                                                    preferred_element_type=F32))
        yg = (yg + y[:, g * gw:(g + 1) * gw]) * zg[:, g * gw:(g + 1) * gw]
        yg = yg * lax.rsqrt(jnp.mean(yg * yg, axis=-1, keepdims=True) + RMS_EPS) * ng_ref[:, g * gw:(g + 1) * gw]
        o_ref[:, g * gw:(g + 1) * gw] = yg.astype(o_ref.dtype)

    @pl.when(ci == nc - 1)
    def _():
        for blk in range(SSM_INNER // LANES):
            so_ref[blk * LANES:(blk + 1) * LANES, :] = st_ref[:, blk * LANES:(blk + 1) * LANES].T


def _ssd(xc, z, dt_raw, expand, dt_bias, a_log, d_x, norm_g, s0, t, c=128):
    nb, tpad, _ = xc.shape
    nc = tpad // c
    return pl.pallas_call(
        functools.partial(_ssd_body, c=c, t=t, nc=nc),
        grid=(nb, nc),
        in_specs=[pl.BlockSpec((None, c, SSM_CH), lambda b, ci: (b, ci, 0)),
                  pl.BlockSpec((None, c, SSM_INNER), lambda b, ci: (b, ci, 0)),
                  pl.BlockSpec((None, c, LANES), lambda b, ci: (b, ci, 0)),
                  pl.BlockSpec((LANES, SSM_INNER), lambda b, ci: (0, 0)),
                  pl.BlockSpec((1, LANES), lambda b, ci: (0, 0)),
                  pl.BlockSpec((1, LANES), lambda b, ci: (0, 0)),
                  pl.BlockSpec((1, SSM_INNER), lambda b, ci: (0, 0)),
                  pl.BlockSpec((1, SSM_INNER), lambda b, ci: (0, 0)),
                  pl.BlockSpec((None, SSM_INNER, SSM_N), lambda b, ci: (b, 0, 0))],
        out_specs=[pl.BlockSpec((None, c, SSM_INNER), lambda b, ci: (b, ci, 0)),
                   pl.BlockSpec((None, SSM_INNER, SSM_N), lambda b, ci: (b, 0, 0))],
        out_shape=[jax.ShapeDtypeStruct((nb, tpad, SSM_INNER), BF16),
                   jax.ShapeDtypeStruct((nb, SSM_INNER, SSM_N), F32)],
        scratch_shapes=[pltpu.VMEM((SSM_N, SSM_INNER), F32)],
        compiler_params=_cp("parallel", "arbitrary"),
        name="ssd",
    )(xc, z, dt_raw, expand, dt_bias, a_log, d_x, norm_g.reshape(1, SSM_INNER), s0)


def _pad_cols(w, n):
    return jnp.pad(w, ((0, 0), (0, n - w.shape[1])))


def _pad_time(a, tpad):
    return jnp.pad(a, ((0, 0), (0, tpad - a.shape[1]), (0, 0)))


def kernel(x_prompt, x_sample, cache_k, cache_v, cache_idx_k, state_hgrn, state_ssm, state_conv, page_table, p_prompt, p_sample, ln_g, ln_b, ffn_w_gate_up, ffn_w_down, ple_w_proj, ple_w_gate, att_w_in, att_idx_k_norm, att_w_o, hg_w_in, hg_lb_logits, hg_norm_g, hg_w_o, ssm_w_in, ssm_conv_w, ssm_conv_b, ssm_dt_bias, ssm_a_log, ssm_d, ssm_norm_g, ssm_w_o):
    nbp, seq, d = x_prompt.shape
    nbs = x_sample.shape[0]
    mp = nbp * seq
    ms = 16
    chunk = 128

    streams = {
        "p": dict(x=x_prompt.reshape(mp, d), m=mp, tm=1024, tml=1024),
        "s": dict(x=jnp.pad(x_sample.reshape(nbs, d), ((0, ms - nbs), (0, 0))), m=ms, tm=ms, tml=ms),
    }
    for st in streams.values():
        st["xb"] = st["x"].astype(BF16)
    p_in = {"p": p_prompt.reshape(DEPTH, mp, PLE_DIM),
            "s": jnp.pad(p_sample.reshape(DEPTH, nbs, PLE_DIM), ((0, 0), (0, ms - nbs), (0, 0)))}

    expand = jnp.asarray(np.kron(np.eye(LANES, SSM_HEADS, dtype=np.float32),
                                 np.ones((1, SSM_P), np.float32)), BF16)
    outs = {}

    def ffn_ln(i, which, ln_idx):
        wd = _cast_w(ffn_w_down, (i, which))
        for st in streams.values():
            h = _mm_swiglu(st["xb"], ffn_w_gate_up, (i, which), st["tm"], 512)
            st["x"], st["xb"] = _mm_ln(h, wd, st["x"], ln_g[i, ln_idx], ln_b[i, ln_idx], 0.5, st["tml"], 512,
                                       "ffn_down_ln")

    for i in range(DEPTH):
        j = i // N_MIXERS
        ffn_ln(i, 0, 0)

        if i % N_MIXERS == 0:
            w_small = _pad_cols(att_w_in[j][:, ATT_MAIN:], LANES)
            w_o = _cast_w(att_w_o, (j,))
            qw, kvw = ATT_HEADS * ATT_HD, ATT_KV * ATT_HD
            mix = {}
            for name, st in streams.items():
                proj, projb = _proj(st["xb"], att_w_in, (j,), ATT_MAIN, st["tm"], 512, with_bf16=True, name="att_in")
                proj2 = _proj(st["xb"], w_small, (), LANES, st["tm"], LANES, name="att_in_idx")
                ik, ikb = _ik_norm(proj2, att_idx_k_norm[j], st["tm"])
                if name == "p":
                    o = _dsa_prompt(projb, proj2, ikb, nbp, seq)
                    outs.setdefault("k_p", []).append(proj[:, qw:qw + kvw].reshape(nbp, seq, ATT_KV, ATT_HD))
                    outs.setdefault("v_p", []).append(proj[:, qw + kvw:qw + 2 * kvw].reshape(nbp, seq, ATT_KV, ATT_HD))
                    outs.setdefault("ik_p", []).append(ik.reshape(nbp, seq, IDX_DIM))
                else:
                    pr = proj[:nbs]
                    k_new = pr[:, qw:qw + kvw]
                    v_new = pr[:, qw + kvw:qw + 2 * kvw]
                    ik_new = ik[:nbs]
                    scores = _dsa_s_scores(page_table,
                                           pr[:, qw + 2 * kvw:].reshape(nbs, IDX_HEADS, IDX_DIM),
                                           proj2[:nbs, IDX_DIM:IDX_DIM + IDX_HEADS].reshape(nbs, IDX_HEADS, 1),
                                           ik_new.reshape(nbs, 1, IDX_DIM), cache_idx_k, j)
                    idx, meta = _dsa_s_select(scores, page_table)
                    o = _dsa_s_attend(idx, meta, pr[:, :qw].reshape(nbs, ATT_HEADS, ATT_HD),
                                      k_new.reshape(nbs, 1, kvw), v_new.reshape(nbs, 1, kvw), cache_k, cache_v, j)
                    o = jnp.pad(o.reshape(nbs, qw), ((0, ms - nbs), (0, 0))).astype(BF16)
                    outs.setdefault("k_s", []).append(k_new.reshape(nbs, 1, ATT_KV, ATT_HD))
                    outs.setdefault("v_s", []).append(v_new.reshape(nbs, 1, ATT_KV, ATT_HD))
                    outs.setdefault("ik_s", []).append(ik_new.reshape(nbs, 1, IDX_DIM))
                mix[name] = (o, w_o)
        elif i % N_MIXERS == 1:
            w_o = _cast_w(hg_w_o, (j,))
            mix = {}
            for name, st in streams.items():
                proj = _proj(st["xb"], hg_w_in, (j,), hg_w_in.shape[-1], st["tm"], 512, name="hg_in")
                if name == "p":
                    s0 = jnp.zeros((nbp, HG_HEADS, HG_DK, HG_DV), F32)
                    o, s_fin = _hgrn(proj.reshape(nbp, seq, -1), hg_lb_logits, hg_norm_g[j], s0, i, seq, chunk)
                    o = o.reshape(mp, -1)
                    outs.setdefault("hg_p", []).append(s_fin)
                else:
                    pr = _pad_time(proj[:nbs].reshape(nbs, 1, -1), chunk)
                    o, s_fin = _hgrn(pr, hg_lb_logits, hg_norm_g[j], state_hgrn[j], i, 1, chunk)
                    o = jnp.pad(o[:, 0, :], ((0, ms - nbs), (0, 0)))
                    outs.setdefault("hg_s", []).append(s_fin)
                mix[name] = (o, w_o)
        else:
            w_small = _pad_cols(ssm_w_in[j][:, SSM_MAIN:], LANES)
            w_o = _cast_w(ssm_w_o, (j,))
            dt_bias = _pad_cols(ssm_dt_bias[j].reshape(1, SSM_HEADS), LANES)
            a_log = _pad_cols(ssm_a_log[j].reshape(1, SSM_HEADS), LANES)
            d_x = jnp.repeat(ssm_d[j], SSM_P).reshape(1, SSM_INNER)
            mix = {}
            for name, st in streams.items():
                zx = _proj(st["xb"], ssm_w_in, (j,), SSM_MAIN, st["tm"], 512, name="ssm_in")
                dtr = _proj(st["xb"], w_small, (), LANES, st["tm"], LANES, name="ssm_in_dt")
                if name == "p":
                    nb_, t_, c_conv = nbp, seq, 4 * chunk
                    zx3 = zx.reshape(nbp, seq, -1)
                    dt3 = dtr.reshape(nbp, seq, LANES)
                    cs = jnp.zeros((nbp, SUBLANES, SSM_CH), F32)
                    s0 = jnp.zeros((nbp, SSM_INNER, SSM_N), F32)
                else:
                    nb_, t_, c_conv = nbs, 1, SUBLANES
                    zx3 = _pad_time(zx[:nbs].reshape(nbs, 1, -1), SUBLANES)
                    dt3 = _pad_time(dtr[:nbs].reshape(nbs, 1, LANES), chunk)
                    cs = jnp.pad(state_conv[j], ((0, 0), (SUBLANES - (SSM_CONV - 1), 0), (0, 0)))
                    s0 = state_ssm[j].reshape(nbs, SSM_INNER, SSM_N)
                xc, new_conv = _ssd_conv(zx3, cs, ssm_conv_w[j], ssm_conv_b[j], t_, c_conv, cw=1024)
                if name == "p":
                    z3 = zx3
                else:
                    xc = _pad_time(xc, chunk)
                    z3 = _pad_time(zx3[:, :, :SSM_INNER], chunk)
                y, s_fin = _ssd(xc, z3, dt3, expand, dt_bias, a_log, d_x, ssm_norm_g[j], s0, t_, chunk)
                s_fin = s_fin.reshape(nb_, SSM_HEADS, SSM_P, SSM_N)
                new_conv = new_conv[:, :SSM_CONV - 1, :]
                if name == "p":
                    o = y.reshape(mp, SSM_INNER)
                    outs.setdefault("ssm_p", []).append(s_fin)
                    outs.setdefault("conv_p", []).append(new_conv)
                else:
                    o = jnp.pad(y[:, 0, :], ((0, ms - nbs), (0, 0)))
                    outs.setdefault("ssm_s", []).append(s_fin)
                    outs.setdefault("conv_s", []).append(new_conv)
                mix[name] = (o, w_o)

        for name, st in streams.items():
            o, w_o = mix[name]
            st["x"], st["xb"] = _mm_ln(o, w_o, st["x"], ln_g[i, 1], ln_b[i, 1], 1.0, st["tml"], 512, "mixer_out_ln")
        ffn_ln(i, 1, 2)
        for name, st in streams.items():
            st["x"], st["xb"] = _mm_ple(st["x"], st["xb"], p_in[name][i].astype(BF16), ple_w_gate, ple_w_proj, (i,),
                                        st["tm"], 512)

    y_prompt = streams["p"]["x"].reshape(nbp, seq, d)
    y_sample = streams["s"]["x"][:nbs].reshape(nbs, 1, d)
    stack = lambda key: jnp.stack(outs[key])
    return (y_prompt, y_sample, stack("k_p"), stack("v_p"), stack("ik_p"), stack("k_s"), stack("v_s"), stack("ik_s"),
            stack("hg_p"), stack("hg_s"), stack("ssm_p"), stack("ssm_s"), stack("conv_p"), stack("conv_s"))
```

```python
import functools
import math

import jax
import jax.numpy as jnp
import numpy as np
from jax import lax
from jax.experimental import pallas as pl
from jax.experimental.pallas import tpu as pltpu

F32 = jnp.float32
BF16 = jnp.bfloat16

D_MODEL = 2048
DEPTH = 4
N_MIXERS = 3
D_FF = 2 * D_MODEL
PLE_DIM = 256
ALPHA = (2 * DEPTH) ** 0.25
LN_EPS = 1e-5
RMS_EPS = 1e-6
PAGE = 128

ATT_HD = 128
ATT_HEADS = 16
ATT_KV = 4
ATT_GROUP = ATT_HEADS // ATT_KV
IDX_HEADS = 16
IDX_DIM = 64
IDX_W_SCALE = (IDX_HEADS ** -0.5) * (IDX_DIM ** -0.5)
TOPK_MAX = 256
ATT_SCALE = ATT_HD ** -0.5
LOG2E = math.log2(math.e)
ATT_MAIN = ATT_HEADS * ATT_HD + 2 * ATT_KV * ATT_HD + IDX_HEADS * IDX_DIM

HG_HEADS = 16
HG_DK = 128
HG_DV = 128

SSM_INNER = 2 * D_MODEL
SSM_P = 64
SSM_HEADS = SSM_INNER // SSM_P
SSM_GROUPS = 8
SSM_HPG = SSM_HEADS // SSM_GROUPS
SSM_N = 128
SSM_CONV = 4
SSM_CH = SSM_INNER + 2 * SSM_GROUPS * SSM_N
SSM_MAIN = SSM_INNER + SSM_CH

LANES = 128
SUBLANES = 8
VMEM_LIMIT_BYTES = 56 * 1024 * 1024

INT_MIN = np.int32(-2 ** 31)
NEG_BIG = -1e30


def _cp(*sem):
    return pltpu.CompilerParams(dimension_semantics=sem, vmem_limit_bytes=VMEM_LIMIT_BYTES)


def _nt(a, b):
    return lax.dot_general(a, b, (((1,), (1,)), ((), ())), preferred_element_type=F32)


def _tn(a, b):
    return lax.dot_general(a, b, (((0,), (0,)), ((), ())), preferred_element_type=F32)


def _silu(x):
    return x * jax.nn.sigmoid(x)


def _wspec(k, tn, idx, col_block):
    lead = (None,) * len(idx)
    return pl.BlockSpec(lead + (k, tn), lambda n, i: idx + (0, col_block(n)))


def _proj_body(x_ref, w_ref, o_ref, *rest):
    wb_ref = rest[-1]

    @pl.when(pl.program_id(1) == 0)
    def _():
        wb_ref[...] = w_ref[...].astype(BF16)

    acc = jnp.dot(x_ref[...], wb_ref[...], preferred_element_type=F32)
    o_ref[...] = acc
    if len(rest) == 2:
        rest[0][...] = acc.astype(BF16)


def _proj(xb, w, idx, n_cols, tm, tn, with_bf16=False, name="proj"):
    m, k = xb.shape
    out_shape = [jax.ShapeDtypeStruct((m, n_cols), F32)]
    out_specs = [pl.BlockSpec((tm, tn), lambda n, i: (i, n))]
    if with_bf16:
        out_shape.append(jax.ShapeDtypeStruct((m, n_cols), BF16))
        out_specs.append(pl.BlockSpec((tm, tn), lambda n, i: (i, n)))
    res = pl.pallas_call(
        _proj_body,
        grid=(n_cols // tn, m // tm),
        in_specs=[pl.BlockSpec((tm, k), lambda n, i: (i, 0)), _wspec(k, tn, idx, lambda n: n)],
        out_specs=out_specs,
        out_shape=out_shape,
        scratch_shapes=[pltpu.VMEM((k, tn), BF16)],
        compiler_params=_cp("parallel", "arbitrary"),
        name=name,
    )(xb, w)
    return res if with_bf16 else res[0]


def _swiglu_body(x_ref, wg_ref, wu_ref, o_ref, wgb_ref, wub_ref):
    @pl.when(pl.program_id(1) == 0)
    def _():
        wgb_ref[...] = wg_ref[...].astype(BF16)
        wub_ref[...] = wu_ref[...].astype(BF16)

    x = x_ref[...]
    g = jnp.dot(x, wgb_ref[...], preferred_element_type=F32)
    u = jnp.dot(x, wub_ref[...], preferred_element_type=F32)
    o_ref[...] = (_silu(g) * u).astype(o_ref.dtype)


def _mm_swiglu(xb, wgu, idx, tm, tn):
    m, k = xb.shape
    f = wgu.shape[-1] // 2
    nj = f // tn
    return pl.pallas_call(
        _swiglu_body,
        grid=(nj, m // tm),
        in_specs=[pl.BlockSpec((tm, k), lambda n, i: (i, 0)),
                  _wspec(k, tn, idx, lambda n: n),
                  _wspec(k, tn, idx, lambda n: n + nj)],
        out_specs=pl.BlockSpec((tm, tn), lambda n, i: (i, n)),
        out_shape=jax.ShapeDtypeStruct((m, f), BF16),
        scratch_shapes=[pltpu.VMEM((k, tn), BF16), pltpu.VMEM((k, tn), BF16)],
        compiler_params=_cp("parallel", "arbitrary"),
        name="ffn_up",
    )(xb, wgu, wgu)


def _cast_body(w_ref, o_ref):
    o_ref[...] = w_ref[...].astype(BF16)


def _cast_w(w, idx, tk=512):
    k, n = w.shape[-2:]
    lead = (None,) * len(idx)
    return pl.pallas_call(
        _cast_body,
        grid=(k // tk,),
        in_specs=[pl.BlockSpec(lead + (tk, n), lambda i: idx + (i, 0))],
        out_specs=pl.BlockSpec((tk, n), lambda i: (i, 0)),
        out_shape=jax.ShapeDtypeStruct((k, n), BF16),
        compiler_params=_cp("parallel"),
        name="cast_w",
    )(w)


LN_ROWS = 128
MM_ROWS = 256


def _mm_ln_body(a_ref, w_ref, r_ref, g_ref, b_ref, o_ref, ob_ref, *, scale, nk, tm):
    k = pl.program_id(1)

    @pl.when(k == 0)
    def _():
        o_ref[...] = jnp.zeros_like(o_ref)

    w = w_ref[...]
    for r in range(0, tm, MM_ROWS):
        sl = pl.ds(r, min(MM_ROWS, tm - r))
        o_ref[sl, :] += jnp.dot(a_ref[sl, :], w, preferred_element_type=F32)

    @pl.when(k == nk - 1)
    def _():
        g = g_ref[...]
        b = b_ref[...]
        for r in range(0, tm, LN_ROWS):
            sl = pl.ds(r, min(LN_ROWS, tm - r))
            y = ALPHA * r_ref[sl, :] + scale * o_ref[sl, :]
            mu = jnp.mean(y, axis=-1, keepdims=True)
            yc = y - mu
            var = jnp.mean(yc * yc, axis=-1, keepdims=True)
            out = yc * lax.rsqrt(var + LN_EPS) * g + b
            o_ref[sl, :] = out
            ob_ref[sl, :] = out.astype(BF16)


def _mm_ln(ab, wb, res, g, b, scale, tm, tk, name):
    m, kdim = ab.shape
    n = wb.shape[1]
    nk = kdim // tk
    return pl.pallas_call(
        functools.partial(_mm_ln_body, scale=scale, nk=nk, tm=tm),
        grid=(m // tm, nk),
        in_specs=[pl.BlockSpec((tm, tk), lambda i, k: (i, k)),
                  pl.BlockSpec((tk, n), lambda i, k: (k, 0)),
                  pl.BlockSpec((tm, n), lambda i, k: (i, 0)),
                  pl.BlockSpec((1, n), lambda i, k: (0, 0)),
                  pl.BlockSpec((1, n), lambda i, k: (0, 0))],
        out_specs=[pl.BlockSpec((tm, n), lambda i, k: (i, 0)),
                   pl.BlockSpec((tm, n), lambda i, k: (i, 0))],
        out_shape=[jax.ShapeDtypeStruct((m, n), F32), jax.ShapeDtypeStruct((m, n), BF16)],
        compiler_params=_cp("parallel", "arbitrary"),
        name=name,
    )(ab, wb, res, g.reshape(1, n), b.reshape(1, n))


def _ple_body(xb_ref, p_ref, wg_ref, wp_ref, x_ref, o_ref, ob_ref, wgb_ref, wpb_ref):
    @pl.when(pl.program_id(1) == 0)
    def _():
        wgb_ref[...] = wg_ref[...].astype(BF16)
        wpb_ref[...] = wp_ref[...].astype(BF16)

    gate = jax.nn.sigmoid(jnp.dot(xb_ref[...], wgb_ref[...], preferred_element_type=F32))
    proj = jnp.dot(p_ref[...], wpb_ref[...], preferred_element_type=F32)
    out = x_ref[...] + gate * proj
    o_ref[...] = out
    ob_ref[...] = out.astype(BF16)


def _mm_ple(x32, xb, pb, wg, wp, idx, tm, tn):
    m, d = xb.shape
    pd = pb.shape[1]
    return pl.pallas_call(
        _ple_body,
        grid=(d // tn, m // tm),
        in_specs=[pl.BlockSpec((tm, d), lambda n, i: (i, 0)),
                  pl.BlockSpec((tm, pd), lambda n, i: (i, 0)),
                  _wspec(d, tn, idx, lambda n: n),
                  _wspec(pd, tn, idx, lambda n: n),
                  pl.BlockSpec((tm, tn), lambda n, i: (i, n))],
        out_specs=[pl.BlockSpec((tm, tn), lambda n, i: (i, n)),
                   pl.BlockSpec((tm, tn), lambda n, i: (i, n))],
        out_shape=[jax.ShapeDtypeStruct((m, d), F32), jax.ShapeDtypeStruct((m, d), BF16)],
        scratch_shapes=[pltpu.VMEM((d, tn), BF16), pltpu.VMEM((pd, tn), BF16)],
        compiler_params=_cp("parallel", "arbitrary"),
        name="ple",
    )(xb, pb, wg, wp, x32)


def _ik_norm_body(p_ref, g_ref, o_ref, ob_ref):
    x = p_ref[...][:, :IDX_DIM]
    mu = jnp.mean(x, axis=-1, keepdims=True)
    xc = x - mu
    out = xc * lax.rsqrt(jnp.mean(xc * xc, axis=-1, keepdims=True) + LN_EPS) * g_ref[...]
    o_ref[...] = out
    ob_ref[...] = out.astype(BF16)


def _ik_norm(proj2, ik_g, tm):
    m = proj2.shape[0]
    return pl.pallas_call(
        _ik_norm_body,
        grid=(m // tm,),
        in_specs=[pl.BlockSpec((tm, LANES), lambda i: (i, 0)),
                  pl.BlockSpec((1, IDX_DIM), lambda i: (0, 0))],
        out_specs=[pl.BlockSpec((tm, IDX_DIM), lambda i: (i, 0)),
                   pl.BlockSpec((tm, IDX_DIM), lambda i: (i, 0))],
        out_shape=[jax.ShapeDtypeStruct((m, IDX_DIM), F32), jax.ShapeDtypeStruct((m, IDX_DIM), BF16)],
        compiler_params=_cp("parallel"),
        name="idx_k_norm",
    )(proj2, ik_g.reshape(1, IDX_DIM))


def _order_key(x):
    bits = pltpu.bitcast(x, jnp.int32)
    return jnp.where(bits < 0, bits ^ jnp.int32(0x7FFFFFFF), bits)


_RADIX_BITS = [INT_MIN] + [np.int32(1 << s) for s in range(30, -1, -1)]


def _dsa_prompt_body(q_ref, iq_ref, iw_ref, k_ref, v_ref, ik_ref, o_ref, keys_ref, *, qb, kc, topk):
    i = pl.program_id(1)
    nck = ((i + 1) * qb + kc - 1) // kc
    row_pos = i * qb + lax.broadcasted_iota(jnp.int32, (qb, 1), 0)
    iq = iq_ref[...]
    iw = iw_ref[...][:, IDX_DIM:IDX_DIM + IDX_HEADS] * IDX_W_SCALE
    iq_h = [iq[:, h * IDX_DIM:(h + 1) * IDX_DIM] for h in range(IDX_HEADS)]
    iw_h = [iw[:, h:h + 1] for h in range(IDX_HEADS)]
    col0 = lax.broadcasted_iota(jnp.int32, (1, kc), 1)

    def score_chunk(c, carry):
        off = pl.multiple_of(c * kc, kc)
        ikc = ik_ref[pl.ds(off, kc), :]
        sc = jnp.zeros((qb, kc), F32)
        for h in range(IDX_HEADS):
            sc = sc + iw_h[h] * jnp.maximum(_nt(iq_h[h], ikc), 0.0)
        key = jnp.where(col0 + off <= row_pos, _order_key(sc), INT_MIN)
        keys_ref[:, pl.ds(off, kc)] = key
        return carry

    lax.fori_loop(0, nck, score_chunk, 0)

    lo = jnp.full((qb, 1), INT_MIN, jnp.int32)
    for bit in _RADIX_BITS:
        cand = lo + bit

        def count_chunk(c, acc, cand=cand):
            off = pl.multiple_of(c * kc, kc)
            ge = jnp.where(keys_ref[:, pl.ds(off, kc)] >= cand, 1.0, 0.0)
            for t in range(kc // LANES):
                acc = acc + ge[:, t * LANES:(t + 1) * LANES]
            return acc

        cnt = jnp.sum(lax.fori_loop(0, nck, count_chunk, jnp.zeros((qb, LANES), F32)), axis=-1, keepdims=True)
        lo = jnp.where(cnt >= topk, cand, lo)
    thr = jnp.maximum(lo, INT_MIN + 1)

    for g in range(ATT_KV):
        qg = jnp.concatenate(
            [q_ref[:, (g * ATT_GROUP + j) * ATT_HD:(g * ATT_GROUP + j + 1) * ATT_HD] for j in range(ATT_GROUP)], axis=0)
        qg = (qg * (ATT_SCALE * LOG2E)).astype(BF16)

        def att_chunk(c, carry, qg=qg, g=g):
            m, l, acc = carry
            off = pl.multiple_of(c * kc, kc)
            kch = k_ref[pl.ds(off, kc), g * ATT_HD:(g + 1) * ATT_HD]
            vch = v_ref[pl.ds(off, kc), g * ATT_HD:(g + 1) * ATT_HD]
            sel = (keys_ref[:, pl.ds(off, kc)] >= thr)[None]
            s = jnp.where(sel, _nt(qg, kch).reshape(ATT_GROUP, qb, kc), -jnp.inf)
            m_new = jnp.maximum(m, jnp.max(s, axis=-1, keepdims=True))
            p = jnp.exp2(s - m_new)
            a = jnp.exp2(m - m_new)
            l = a * l + jnp.sum(p, axis=-1, keepdims=True)
            pv = jnp.dot(p.reshape(ATT_GROUP * qb, kc).astype(BF16), vch, preferred_element_type=F32)
            acc = a * acc + pv.reshape(ATT_GROUP, qb, ATT_HD)
            return m_new, l, acc

        init = (jnp.full((ATT_GROUP, qb, 1), NEG_BIG, F32), jnp.zeros((ATT_GROUP, qb, 1), F32),
                jnp.zeros((ATT_GROUP, qb, ATT_HD), F32))
        _, l, acc = lax.fori_loop(0, nck, att_chunk, init)
        out = acc / l
        for j in range(ATT_GROUP):
            h = g * ATT_GROUP + j
            o_ref[:, h * ATT_HD:(h + 1) * ATT_HD] = out[j].astype(o_ref.dtype)


def _dsa_prompt(proj, projb, proj2, ikb, nb, s, qb=256, kc=512):
    nq = s // qb
    topk = min(TOPK_MAX, s // 4)
    qw = ATT_HEADS * ATT_HD
    kvw = ATT_KV * ATT_HD
    iqw = IDX_HEADS * IDX_DIM
    return pl.pallas_call(
        functools.partial(_dsa_prompt_body, qb=qb, kc=kc, topk=topk),
        grid=(nb, nq),
        in_specs=[pl.BlockSpec((qb, qw), lambda b, i: (b * nq + i, 0)),
                  pl.BlockSpec((qb, iqw), lambda b, i: (b * nq + i, (qw + 2 * kvw) // iqw)),
                  pl.BlockSpec((qb, LANES), lambda b, i: (b * nq + i, 0)),
                  pl.BlockSpec((s, kvw), lambda b, i: (b, qw // kvw)),
                  pl.BlockSpec((s, kvw), lambda b, i: (b, qw // kvw + 1)),
                  pl.BlockSpec((s, IDX_DIM), lambda b, i: (b, 0))],
        out_specs=pl.BlockSpec((qb, qw), lambda b, i: (b * nq + i, 0)),
        out_shape=jax.ShapeDtypeStruct((nb * s, qw), BF16),
        scratch_shapes=[pltpu.VMEM((qb, s), jnp.int32)],
        compiler_params=_cp("parallel", "arbitrary"),
        name="dsa_prompt",
    )(proj, projb, proj2, projb, projb, ikb)


SCORE_PAGES = 16


def _dsa_s_scores_body(pt_ref, iq_ref, iw_ref, ikn_ref, *rest, n_pages):
    page_refs, o_ref = rest[:SCORE_PAGES], rest[SCORE_PAGES]
    p = pl.program_id(1)
    iq = iq_ref[...].astype(BF16)
    iw = iw_ref[...] * IDX_W_SCALE

    @pl.when(p == 0)
    def _():
        own = jnp.broadcast_to(ikn_ref[...], (SUBLANES, IDX_DIM)).astype(BF16)
        d = jnp.maximum(_nt(iq, own), 0.0)
        sc = jnp.sum(iw * d, axis=0, keepdims=True)
        o_ref[n_pages:n_pages + SUBLANES, :] = jnp.broadcast_to(sc[:, 0:1], (SUBLANES, LANES))

    for r in range(SCORE_PAGES):
        d = jnp.maximum(_nt(iq, page_refs[r][...].astype(BF16)), 0.0)
        o_ref[pl.ds(p * SCORE_PAGES + r, 1), :] = jnp.sum(iw * d, axis=0, keepdims=True)


def _dsa_s_scores(page_table, iq, iw, ik_new, cache_ik, layer):
    nb, n_pages = page_table.shape
    page_spec = lambda r: pl.BlockSpec((None, None, PAGE, IDX_DIM),
                                       lambda b, p, pt: (layer, pt[b, p * SCORE_PAGES + r], 0, 0))
    return pl.pallas_call(
        functools.partial(_dsa_s_scores_body, n_pages=n_pages),
        grid_spec=pltpu.PrefetchScalarGridSpec(
            num_scalar_prefetch=1,
            grid=(nb, n_pages // SCORE_PAGES),
            in_specs=[pl.BlockSpec((None, IDX_HEADS, IDX_DIM), lambda b, p, pt: (b, 0, 0)),
                      pl.BlockSpec((None, IDX_HEADS, 1), lambda b, p, pt: (b, 0, 0)),
                      pl.BlockSpec((None, 1, IDX_DIM), lambda b, p, pt: (b, 0, 0))]
                     + [page_spec(r) for r in range(SCORE_PAGES)],
            out_specs=pl.BlockSpec((None, n_pages + SUBLANES, LANES), lambda b, p, pt: (b, 0, 0)),
        ),
        out_shape=jax.ShapeDtypeStruct((nb, n_pages + SUBLANES, LANES), F32),
        compiler_params=_cp("parallel", "arbitrary"),
        name="dsa_sample_scores",
    )(page_table, iq, iw, ik_new, *([cache_ik] * SCORE_PAGES))


def _dsa_s_select_body(sc_ref, ptc_ref, idx_ref, meta_ref, rank_ref, phys_ref, *, n_pages, topk):
    shape = (n_pages + SUBLANES, LANES)
    rows = lax.broadcasted_iota(jnp.int32, shape, 0)
    cols = lax.broadcasted_iota(jnp.int32, shape, 1)
    live = (rows < n_pages) | ((rows == n_pages) & (cols == 0))
    keys = jnp.where(live, _order_key(sc_ref[...]), INT_MIN)
    lo = jnp.full((1, 1), INT_MIN, jnp.int32)
    for bit in _RADIX_BITS:
        cand = lo + bit
        cnt = jnp.sum(jnp.sum(jnp.where(keys >= cand, 1.0, 0.0), axis=0, keepdims=True), axis=1, keepdims=True)
        lo = jnp.where(cnt >= topk, cand, lo)
    sel = keys >= jnp.maximum(lo, INT_MIN + 1)
    sel_c = sel[:n_pages, :]
    own = jnp.where(sel[n_pages:n_pages + 1, 0:1], 1, 0)

    r_i = lax.broadcasted_iota(jnp.int32, (LANES, LANES), 0)
    c_i = lax.broadcasted_iota(jnp.int32, (LANES, LANES), 1)
    ones_le = jnp.where(r_i <= c_i, 1.0, 0.0).astype(BF16)
    ones_gt = jnp.where(r_i > c_i, 1.0, 0.0).astype(BF16)
    m = jnp.where(sel_c, 1.0, 0.0).astype(BF16)
    within = jnp.dot(m, ones_le, preferred_element_type=F32)
    tot = jnp.broadcast_to(within[:, LANES - 1:LANES], (n_pages, LANES)).astype(BF16)
    before = jnp.dot(ones_gt, tot, preferred_element_type=F32)
    rank_ref[...] = jnp.where(sel_c, (within + before).astype(jnp.int32) - 1, -1)
    phys_ref[...] = (ptc_ref[...] * PAGE + c_i[:n_pages, :]).astype(F32)
    n_sel = (before[n_pages - 1:n_pages, 0:1] + within[n_pages - 1:n_pages, LANES - 1:LANES]).astype(jnp.int32)

    slot = lax.broadcasted_iota(jnp.int32, (topk, 1), 0)

    def place(r, acc):
        return acc + jnp.where(rank_ref[pl.ds(r, 1), :] == slot, phys_ref[pl.ds(r, 1), :], 0.0)

    acc = lax.fori_loop(0, n_pages, place, jnp.zeros((topk, LANES), F32), unroll=8)
    idx_ref[...] = jnp.sum(acc, axis=-1, keepdims=True).astype(jnp.int32)
    mrow = lax.broadcasted_iota(jnp.int32, (SUBLANES, LANES), 0)
    meta_ref[...] = jnp.where(mrow == 0, jnp.minimum(n_sel, topk), own)


def _dsa_s_select(scores, page_table):
    nb, n_pages = page_table.shape
    assert n_pages == LANES
    topk = min(TOPK_MAX, (n_pages * PAGE + 1) // 4)
    idx, meta = pl.pallas_call(
        functools.partial(_dsa_s_select_body, n_pages=n_pages, topk=topk),
        grid=(nb,),
        in_specs=[pl.BlockSpec((None, n_pages + SUBLANES, LANES), lambda b: (b, 0, 0)),
                  pl.BlockSpec((None, n_pages, 1), lambda b: (b, 0, 0))],
        out_specs=[pl.BlockSpec((None, topk, 1), lambda b: (b, 0, 0)),
                   pl.BlockSpec((None, SUBLANES, LANES), lambda b: (b, 0, 0))],
        out_shape=[jax.ShapeDtypeStruct((nb, topk, 1), jnp.int32),
                   jax.ShapeDtypeStruct((nb, SUBLANES, LANES), jnp.int32)],
        scratch_shapes=[pltpu.VMEM((n_pages, LANES), jnp.int32), pltpu.VMEM((n_pages, LANES), F32)],
        compiler_params=_cp("parallel"),
        name="dsa_sample_select",
    )(scores, page_table.reshape(nb, n_pages, 1))
    return idx.reshape(nb, topk), meta[:, 0:2, 0]


def _dsa_s_attend_body(idx_ref, meta_ref, q_ref, kn_ref, vn_ref, ck_ref, cv_ref, o_ref, kbuf, vbuf, sem,
                       *, layer, topk):
    b = pl.program_id(0)
    nb = pl.num_programs(0)
    slot = b % 2

    def row_copies(tok, buf_slot, j):
        row = idx_ref[tok, j]
        page, off = row // PAGE, row % PAGE
        dst = pl.ds(j * ATT_KV, ATT_KV)
        return (pltpu.make_async_copy(ck_ref.at[layer, page, off], kbuf.at[buf_slot, dst, :], sem.at[buf_slot, 0]),
                pltpu.make_async_copy(cv_ref.at[layer, page, off], vbuf.at[buf_slot, dst, :], sem.at[buf_slot, 1]))

    def start_gather(tok, buf_slot):
        def body(j, carry):
            for cp in row_copies(tok, buf_slot, j):
                cp.start()
            return carry
        lax.fori_loop(0, topk, body, 0)

    @pl.when(b == 0)
    def _():
        start_gather(0, 0)

    @pl.when(b + 1 < nb)
    def _():
        start_gather(b + 1, 1 - slot)

    def wait_body(j, carry):
        for cp in row_copies(b, slot, j):
            cp.wait()
        return carry

    lax.fori_loop(0, topk, wait_body, 0)

    n_sel = meta_ref[b, 0]
    own = meta_ref[b, 1] > 0
    qb = q_ref[...].astype(BF16)
    ncol = topk * ATT_KV
    col = lax.broadcasted_iota(jnp.int32, (ATT_HEADS, ncol), 1)
    head = lax.broadcasted_iota(jnp.int32, (ATT_HEADS, ncol), 0)
    valid = (col % ATT_KV == head // ATT_GROUP) & (col // ATT_KV < n_sel)
    s = jnp.where(valid, _nt(qb, kbuf[slot].astype(BF16)) * ATT_SCALE, NEG_BIG)
    kn = kn_ref[...].astype(BF16).astype(F32)
    vn = vn_ref[...].astype(BF16).astype(F32)
    expand = lambda a: jnp.concatenate(
        [jnp.broadcast_to(a[:, g * ATT_HD:(g + 1) * ATT_HD], (ATT_GROUP, ATT_HD)) for g in range(ATT_KV)], axis=0)
    s_own = jnp.where(own, jnp.sum(qb.astype(F32) * expand(kn), axis=-1, keepdims=True) * ATT_SCALE, NEG_BIG)
    m = jnp.maximum(jnp.max(s, axis=-1, keepdims=True), s_own)
    p = jnp.where(valid, jnp.exp(s - m), 0.0)
    p_own = jnp.where(own, jnp.exp(s_own - m), 0.0)
    l = jnp.sum(p, axis=-1, keepdims=True) + p_own
    acc = (jnp.dot(p.astype(BF16), vbuf[slot].astype(BF16), preferred_element_type=F32)
           + p_own.astype(BF16).astype(F32) * expand(vn))
    o_ref[...] = acc / l


def _dsa_s_attend(idx, meta, q, k_new, v_new, cache_k, cache_v, layer):
    nb, topk = idx.shape
    kvw = ATT_KV * ATT_HD
    return pl.pallas_call(
        functools.partial(_dsa_s_attend_body, layer=layer, topk=topk),
        grid_spec=pltpu.PrefetchScalarGridSpec(
            num_scalar_prefetch=2,
            grid=(nb,),
            in_specs=[pl.BlockSpec((None, ATT_HEADS, ATT_HD), lambda b, idx, meta: (b, 0, 0)),
                      pl.BlockSpec((None, 1, kvw), lambda b, idx, meta: (b, 0, 0)),
                      pl.BlockSpec((None, 1, kvw), lambda b, idx, meta: (b, 0, 0)),
                      pl.BlockSpec(memory_space=pl.ANY),
                      pl.BlockSpec(memory_space=pl.ANY)],
            out_specs=pl.BlockSpec((None, ATT_HEADS, ATT_HD), lambda b, idx, meta: (b, 0, 0)),
            scratch_shapes=[pltpu.VMEM((2, topk * ATT_KV, ATT_HD), F32),
                            pltpu.VMEM((2, topk * ATT_KV, ATT_HD), F32),
                            pltpu.SemaphoreType.DMA((2, 2))],
        ),
        out_shape=jax.ShapeDtypeStruct((nb, ATT_HEADS, ATT_HD), F32),
        compiler_params=_cp("arbitrary"),
        name="dsa_sample_attend",
    )(idx, meta, q, k_new, v_new, cache_k, cache_v)


HG_SUB = 16


HG_HB = 4
HG_EXP_LIMIT = 80.0


def _hgrn_body(lbl_ref, q_ref, f_ref, i_ref, g_ref, ng_ref, s0_ref, o_ref, so_ref, st_ref, *, layer, c, t, nc):
    ci = pl.program_id(2)
    mid = c // 2

    @pl.when(ci == 0)
    def _():
        for hh in range(HG_HB):
            st_ref[hh] = s0_ref[hh].T

    logits = lbl_ref[...]
    e = jnp.exp(logits - jnp.max(logits, axis=0, keepdims=True))
    soft = e / jnp.sum(e, axis=0, keepdims=True)
    lb_all = jnp.zeros((1, HG_HB * HG_DK), F32)
    for r in range(1, layer + 1):
        lb_all = lb_all + soft[r:r + 1, :]
    ng = ng_ref[...]

    def gates(hh, sl, row0):
        cs = slice(hh * HG_DK, (hh + 1) * HG_DK)
        lb = lb_all[:, cs]
        fg = lb + (1.0 - lb) * jax.nn.sigmoid(f_ref[sl, cs])
        lf = jnp.log(fg)
        kk = 1.0 - fg
        if t % c:
            n = lf.shape[0]
            valid = (ci * c + row0 + lax.broadcasted_iota(jnp.int32, (n, 1), 0)) < t
            lf = jnp.where(valid, lf, 0.0)
            kk = jnp.where(valid, kk, 0.0)
        return lf, kk

    def finish(hh, sl, o):
        cs = slice(hh * HG_DV, (hh + 1) * HG_DV)
        on = o * lax.rsqrt(jnp.mean(o * o, axis=-1, keepdims=True) + RMS_EPS) * ng
        o_ref[sl, cs] = (on * _silu(g_ref[sl, cs])).astype(o_ref.dtype)

    tri_c = lax.broadcasted_iota(jnp.int32, (c, c), 0) >= lax.broadcasted_iota(jnp.int32, (c, c), 1)
    full = pl.ds(0, c)
    lfs, kks, bs = [], [], []
    safe = None
    for hh in range(HG_HB):
        lf, kk = gates(hh, full, 0)
        b = jnp.dot(tri_c.astype(F32), lf, precision=lax.Precision.HIGHEST, preferred_element_type=F32)
        bm = b[mid - 1:mid, :]
        ok = jnp.min(jnp.minimum(bm, b[c - 1:c, :] - bm)) > -HG_EXP_LIMIT
        safe = ok if safe is None else jnp.logical_and(safe, ok)
        lfs.append(lf)
        kks.append(kk)
        bs.append(b)

    @pl.when(safe)
    def _():
        for hh in range(HG_HB):
            cs = slice(hh * HG_DK, (hh + 1) * HG_DK)
            b, kk = bs[hh], kks[hh]
            bm = b[mid - 1:mid, :]
            bl = b[c - 1:c, :]
            qq = _silu(q_ref[:, cs])
            vv = i_ref[:, cs].astype(BF16)
            att = jnp.where(tri_c, _nt((qq * jnp.exp(b - bm)).astype(BF16), (kk * jnp.exp(bm - b)).astype(BF16)), 0.0)
            st = st_ref[hh]
            o = (jnp.dot(att.astype(BF16), vv, preferred_element_type=F32)
                 + _nt((qq * jnp.exp(b)).astype(BF16), st.astype(BF16)))
            st_ref[hh] = st * jnp.exp(bl) + _tn(vv, (kk * jnp.exp(bl - b)).astype(BF16))
            finish(hh, full, o)

    @pl.when(jnp.logical_not(safe))
    def _():
        rows = lax.broadcasted_iota(jnp.int32, (HG_SUB, 1), 0)
        tri = (lax.broadcasted_iota(jnp.int32, (HG_SUB, HG_SUB), 0)
               >= lax.broadcasted_iota(jnp.int32, (HG_SUB, HG_SUB), 1)).astype(F32)
        for hh in range(HG_HB):
            cs = slice(hh * HG_DK, (hh + 1) * HG_DK)

            def sub_block(sb, carry, hh=hh, cs=cs):
                row0 = pl.multiple_of(sb * HG_SUB, HG_SUB)
                sl = pl.ds(row0, HG_SUB)
                lf, kk = gates(hh, sl, row0)
                qq = _silu(q_ref[sl, cs])
                vv = i_ref[sl, cs]
                b = jnp.dot(tri, lf, precision=lax.Precision.HIGHEST, preferred_element_type=F32)
                st = st_ref[hh]
                o = _nt((qq * jnp.exp(b)).astype(BF16), st.astype(BF16))
                for s in range(HG_SUB):
                    dec = jnp.exp(jnp.where(rows >= s, b - b[s:s + 1, :], -jnp.inf))
                    att = jnp.sum(qq * dec * kk[s:s + 1, :], axis=-1, keepdims=True)
                    o = o + att * vv[s:s + 1, :]
                bl = b[HG_SUB - 1:HG_SUB, :]
                st_ref[hh] = st * jnp.exp(bl) + _tn(vv.astype(BF16), (kk * jnp.exp(bl - b)).astype(BF16))
                finish(hh, sl, o)
                return carry

            lax.fori_loop(0, c // HG_SUB, sub_block, 0)

    @pl.when(ci == nc - 1)
    def _():
        for hh in range(HG_HB):
            so_ref[hh] = st_ref[hh].T


def _hgrn(proj, lb_logits, norm_g, s0, layer, t, c=128):
    nb, tpad, _ = proj.shape
    nc = tpad // c
    nhb = HG_HEADS // HG_HB
    w = HG_HB * HG_DK
    return pl.pallas_call(
        functools.partial(_hgrn_body, layer=layer, c=c, t=t, nc=nc),
        grid=(nb, nhb, nc),
        in_specs=[pl.BlockSpec((DEPTH, w), lambda b, h, ci: (0, h)),
                  pl.BlockSpec((None, c, w), lambda b, h, ci: (b, ci, h)),
                  pl.BlockSpec((None, c, w), lambda b, h, ci: (b, ci, nhb + h)),
                  pl.BlockSpec((None, c, w), lambda b, h, ci: (b, ci, 2 * nhb + h)),
                  pl.BlockSpec((None, c, w), lambda b, h, ci: (b, ci, 3 * nhb + h)),
                  pl.BlockSpec((1, HG_DV), lambda b, h, ci: (0, 0)),
                  pl.BlockSpec((None, HG_HB, HG_DK, HG_DV), lambda b, h, ci: (b, h, 0, 0))],
        out_specs=[pl.BlockSpec((None, c, w), lambda b, h, ci: (b, ci, h)),
                   pl.BlockSpec((None, HG_HB, HG_DK, HG_DV), lambda b, h, ci: (b, h, 0, 0))],
        out_shape=[jax.ShapeDtypeStruct((nb, tpad, HG_HEADS * HG_DV), BF16),
                   jax.ShapeDtypeStruct((nb, HG_HEADS, HG_DK, HG_DV), F32)],
        scratch_shapes=[pltpu.VMEM((HG_HB, HG_DV, HG_DK), F32)],
        compiler_params=_cp("parallel", "parallel", "arbitrary"),
        name="hgrn2",
    )(lb_logits, proj, proj, proj, proj, norm_g.reshape(1, HG_DV), s0)


def _conv_body(x_ref, halo_ref, cs_ref, w_ref, b_ref, o_ref, nc_ref, *, c, t, nc):
    ti = pl.program_id(2)
    halo = jnp.where(ti == 0, cs_ref[...], halo_ref[...])
    full = jnp.concatenate([halo, x_ref[...]], axis=0)
    w = w_ref[...]
    conv = b_ref[...]
    for k in range(SSM_CONV):
        lo = SUBLANES - (SSM_CONV - 1) + k
        conv = conv + full[lo:lo + c, :] * w[k:k + 1, :]
    o_ref[...] = _silu(conv)

    @pl.when(ti == nc - 1)
    def _():
        tv = t - (nc - 1) * c
        tail = full[SUBLANES + tv - (SSM_CONV - 1):SUBLANES + tv, :]
        nc_ref[...] = jnp.concatenate([tail, jnp.zeros((SUBLANES - (SSM_CONV - 1), tail.shape[1]), F32)], axis=0)


def _ssd_conv(zx, cs_pad, conv_w, conv_b, t, c, cw=512):
    nb, tpad, _ = zx.shape
    nc = tpad // c
    ncol = SSM_CH // cw
    col0 = SSM_INNER // cw
    hb = c // SUBLANES
    return pl.pallas_call(
        functools.partial(_conv_body, c=c, t=t, nc=nc),
        grid=(nb, ncol, nc),
        in_specs=[pl.BlockSpec((None, c, cw), lambda b, j, ti: (b, ti, col0 + j)),
                  pl.BlockSpec((None, SUBLANES, cw), lambda b, j, ti: (b, jnp.maximum(ti * hb - 1, 0), col0 + j)),
                  pl.BlockSpec((None, SUBLANES, cw), lambda b, j, ti: (b, 0, j)),
                  pl.BlockSpec((SSM_CONV, cw), lambda b, j, ti: (0, j)),
                  pl.BlockSpec((1, cw), lambda b, j, ti: (0, j))],
        out_specs=[pl.BlockSpec((None, c, cw), lambda b, j, ti: (b, ti, j)),
                   pl.BlockSpec((None, SUBLANES, cw), lambda b, j, ti: (b, 0, j))],
        out_shape=[jax.ShapeDtypeStruct((nb, tpad, SSM_CH), F32),
                   jax.ShapeDtypeStruct((nb, SUBLANES, SSM_CH), F32)],
        compiler_params=_cp("parallel", "parallel", "arbitrary"),
        name="ssd_conv",
    )(zx, zx, cs_pad, conv_w, conv_b.reshape(1, SSM_CH))


def _expand_heads(v, e):
    hi = v.astype(BF16)
    r1 = v - hi.astype(F32)
    mid = r1.astype(BF16)
    lo = (r1 - mid.astype(F32)).astype(BF16)
    return (jnp.dot(hi, e, preferred_element_type=F32) + jnp.dot(mid, e, preferred_element_type=F32)
            + jnp.dot(lo, e, preferred_element_type=F32))


def _ssd_body(xc_ref, z_ref, dt_ref, e_ref, dtb_ref, alog_ref, dx_ref, ng_ref, s0_ref, o_ref, so_ref, st_ref,
              *, c, t, nc):
    ci = pl.program_id(1)
    gw = SSM_HPG * SSM_P

    @pl.when(ci == 0)
    def _():
        for blk in range(SSM_INNER // LANES):
            st_ref[:, blk * LANES:(blk + 1) * LANES] = s0_ref[blk * LANES:(blk + 1) * LANES, :].T

    e = e_ref[...]
    rows = lax.broadcasted_iota(jnp.int32, (c, 1), 0)
    tri_b = lax.broadcasted_iota(jnp.int32, (c, c), 0) >= lax.broadcasted_iota(jnp.int32, (c, c), 1)
    lane_lo = lax.broadcasted_iota(jnp.int32, (1, LANES), 1) < SSM_P

    dt = jax.nn.softplus(dt_ref[...] + dtb_ref[...])
    if t % c:
        dt = jnp.where(ci * c + rows < t, dt, 0.0)
    da = dt * (-jnp.exp(alog_ref[...]))
    bcum = jnp.dot(tri_b.astype(F32), da, precision=lax.Precision.HIGHEST, preferred_element_type=F32)
    bcum_t = bcum.T
    bl = bcum[c - 1:c, :]
    dt_x = _expand_heads(dt, e)
    eb_x = _expand_heads(jnp.exp(bcum), e)
    w_x = _expand_heads(jnp.exp(bl - bcum) * dt, e)
    decay_x = _expand_heads(jnp.broadcast_to(jnp.exp(bl), (SUBLANES, LANES)), e)[0:1, :]

    xs = xc_ref[:, 0:SSM_INNER]
    xdt = (xs * dt_x).astype(BF16)
    xw = (xs * w_x).astype(BF16)
    y = xs * dx_ref[...]
    zg = _silu(z_ref[...])
    for g in range(SSM_GROUPS):
        bg = xc_ref[:, SSM_INNER + g * SSM_N:SSM_INNER + (g + 1) * SSM_N]
        cg = xc_ref[:, SSM_INNER + (SSM_GROUPS + g) * SSM_N:SSM_INNER + (SSM_GROUPS + g + 1) * SSM_N].astype(BF16)
        cb = _nt(cg, bg.astype(BF16))
        st_g = st_ref[:, g * gw:(g + 1) * gw]
        yg = jnp.dot(cg, st_g.astype(BF16), preferred_element_type=F32) * eb_x[:, g * gw:(g + 1) * gw]
        parts = []
        for jp in range(SSM_HPG // 2):
            xpair = xdt[:, g * gw + jp * LANES:g * gw + (jp + 1) * LANES]
            acc = None
            for half in range(2):
                h = g * SSM_HPG + jp * 2 + half
                dec = jnp.exp(jnp.where(tri_b, bcum[:, h:h + 1] - bcum_t[h:h + 1, :], -jnp.inf))
                w = (cb * dec).astype(BF16)
                xh = jnp.where(lane_lo if half == 0 else jnp.logical_not(lane_lo), xpair, 0.0).astype(BF16)
                r = jnp.dot(w, xh, preferred_element_type=F32)
                acc = r if acc is None else acc + r
            parts.append(acc)
        yg = yg + jnp.concatenate(parts, axis=1)
        st_ref[:, g * gw:(g + 1) * gw] = (st_g * decay_x[:, g * gw:(g + 1) * gw]
                                          + jnp.dot(bg.T.astype(BF16), xw[:, g * gw:(g + 1) * gw],
                                                    preferred_element_type=F32))
        yg = (yg + y[:, g * gw:(g + 1) * gw]) * zg[:, g * gw:(g + 1) * gw]
        yg = yg * lax.rsqrt(jnp.mean(yg * yg, axis=-1, keepdims=True) + RMS_EPS) * ng_ref[:, g * gw:(g + 1) * gw]
        o_ref[:, g * gw:(g + 1) * gw] = yg.astype(o_ref.dtype)

    @pl.when(ci == nc - 1)
    def _():
        for blk in range(SSM_INNER // LANES):
            so_ref[blk * LANES:(blk + 1) * LANES, :] = st_ref[:, blk * LANES:(blk + 1) * LANES].T


def _ssd(xc, z, dt_raw, expand, dt_bias, a_log, d_x, norm_g, s0, t, c=128):
    nb, tpad, _ = xc.shape
    nc = tpad // c
    return pl.pallas_call(
        functools.partial(_ssd_body, c=c, t=t, nc=nc),
        grid=(nb, nc),
        in_specs=[pl.BlockSpec((None, c, SSM_CH), lambda b, ci: (b, ci, 0)),
                  pl.BlockSpec((None, c, SSM_INNER), lambda b, ci: (b, ci, 0)),
                  pl.BlockSpec((None, c, LANES), lambda b, ci: (b, ci, 0)),
                  pl.BlockSpec((LANES, SSM_INNER), lambda b, ci: (0, 0)),
                  pl.BlockSpec((1, LANES), lambda b, ci: (0, 0)),
                  pl.BlockSpec((1, LANES), lambda b, ci: (0, 0)),
                  pl.BlockSpec((1, SSM_INNER), lambda b, ci: (0, 0)),
                  pl.BlockSpec((1, SSM_INNER), lambda b, ci: (0, 0)),
                  pl.BlockSpec((None, SSM_INNER, SSM_N), lambda b, ci: (b, 0, 0))],
        out_specs=[pl.BlockSpec((None, c, SSM_INNER), lambda b, ci: (b, ci, 0)),
                   pl.BlockSpec((None, SSM_INNER, SSM_N), lambda b, ci: (b, 0, 0))],
        out_shape=[jax.ShapeDtypeStruct((nb, tpad, SSM_INNER), BF16),
                   jax.ShapeDtypeStruct((nb, SSM_INNER, SSM_N), F32)],
        scratch_shapes=[pltpu.VMEM((SSM_N, SSM_INNER), F32)],
        compiler_params=_cp("parallel", "arbitrary"),
        name="ssd",
    )(xc, z, dt_raw, expand, dt_bias, a_log, d_x, norm_g.reshape(1, SSM_INNER), s0)


def _pad_cols(w, n):
    return jnp.pad(w, ((0, 0), (0, n - w.shape[1])))


def _pad_time(a, tpad):
    return jnp.pad(a, ((0, 0), (0, tpad - a.shape[1]), (0, 0)))


def kernel(x_prompt, x_sample, cache_k, cache_v, cache_idx_k, state_hgrn, state_ssm, state_conv, page_table, p_prompt, p_sample, ln_g, ln_b, ffn_w_gate_up, ffn_w_down, ple_w_proj, ple_w_gate, att_w_in, att_idx_k_norm, att_w_o, hg_w_in, hg_lb_logits, hg_norm_g, hg_w_o, ssm_w_in, ssm_conv_w, ssm_conv_b, ssm_dt_bias, ssm_a_log, ssm_d, ssm_norm_g, ssm_w_o):
    nbp, seq, d = x_prompt.shape
    nbs = x_sample.shape[0]
    mp = nbp * seq
    ms = 16
    chunk = 128

    streams = {
        "p": dict(x=x_prompt.reshape(mp, d), m=mp, tm=1024, tml=1024),
        "s": dict(x=jnp.pad(x_sample.reshape(nbs, d), ((0, ms - nbs), (0, 0))), m=ms, tm=ms, tml=ms),
    }
    for st in streams.values():
        st["xb"] = st["x"].astype(BF16)
    p_in = {"p": p_prompt.reshape(DEPTH, mp, PLE_DIM),
            "s": jnp.pad(p_sample.reshape(DEPTH, nbs, PLE_DIM), ((0, 0), (0, ms - nbs), (0, 0)))}

    expand = jnp.asarray(np.kron(np.eye(LANES, SSM_HEADS, dtype=np.float32),
                                 np.ones((1, SSM_P), np.float32)), BF16)
    outs = {}

    def ffn_ln(i, which, ln_idx):
        wd = _cast_w(ffn_w_down, (i, which))
        for st in streams.values():
            h = _mm_swiglu(st["xb"], ffn_w_gate_up, (i, which), st["tm"], 512)
            st["x"], st["xb"] = _mm_ln(h, wd, st["x"], ln_g[i, ln_idx], ln_b[i, ln_idx], 0.5, st["tml"], 512,
                                       "ffn_down_ln")

    for i in range(DEPTH):
        j = i // N_MIXERS
        ffn_ln(i, 0, 0)

        if i % N_MIXERS == 0:
            w_small = _pad_cols(att_w_in[j][:, ATT_MAIN:], LANES)
            w_o = _cast_w(att_w_o, (j,))
            qw, kvw = ATT_HEADS * ATT_HD, ATT_KV * ATT_HD
            mix = {}
            for name, st in streams.items():
                proj, projb = _proj(st["xb"], att_w_in, (j,), ATT_MAIN, st["tm"], 1024, with_bf16=True, name="att_in")
                proj2 = _proj(st["xb"], w_small, (), LANES, st["tm"], LANES, name="att_in_idx")
                ik, ikb = _ik_norm(proj2, att_idx_k_norm[j], st["tm"])
                if name == "p":
                    o = _dsa_prompt(proj, projb, proj2, ikb, nbp, seq)
                    outs.setdefault("k_p", []).append(proj[:, qw:qw + kvw].reshape(nbp, seq, ATT_KV, ATT_HD))
                    outs.setdefault("v_p", []).append(proj[:, qw + kvw:qw + 2 * kvw].reshape(nbp, seq, ATT_KV, ATT_HD))
                    outs.setdefault("ik_p", []).append(ik.reshape(nbp, seq, IDX_DIM))
                else:
                    pr = proj[:nbs]
                    k_new = pr[:, qw:qw + kvw]
                    v_new = pr[:, qw + kvw:qw + 2 * kvw]
                    ik_new = ik[:nbs]
                    scores = _dsa_s_scores(page_table,
                                           pr[:, qw + 2 * kvw:].reshape(nbs, IDX_HEADS, IDX_DIM),
                                           proj2[:nbs, IDX_DIM:IDX_DIM + IDX_HEADS].reshape(nbs, IDX_HEADS, 1),
                                           ik_new.reshape(nbs, 1, IDX_DIM), cache_idx_k, j)
                    idx, meta = _dsa_s_select(scores, page_table)
                    o = _dsa_s_attend(idx, meta, pr[:, :qw].reshape(nbs, ATT_HEADS, ATT_HD),
                                      k_new.reshape(nbs, 1, kvw), v_new.reshape(nbs, 1, kvw), cache_k, cache_v, j)
                    o = jnp.pad(o.reshape(nbs, qw), ((0, ms - nbs), (0, 0))).astype(BF16)
                    outs.setdefault("k_s", []).append(k_new.reshape(nbs, 1, ATT_KV, ATT_HD))
                    outs.setdefault("v_s", []).append(v_new.reshape(nbs, 1, ATT_KV, ATT_HD))
                    outs.setdefault("ik_s", []).append(ik_new.reshape(nbs, 1, IDX_DIM))
                mix[name] = (o, w_o)
        elif i % N_MIXERS == 1:
            w_o = _cast_w(hg_w_o, (j,))
            mix = {}
            for name, st in streams.items():
                proj = _proj(st["xb"], hg_w_in, (j,), hg_w_in.shape[-1], st["tm"], 1024, name="hg_in")
                if name == "p":
                    s0 = jnp.zeros((nbp, HG_HEADS, HG_DK, HG_DV), F32)
                    o, s_fin = _hgrn(proj.reshape(nbp, seq, -1), hg_lb_logits, hg_norm_g[j], s0, i, seq, chunk)
                    o = o.reshape(mp, -1)
                    outs.setdefault("hg_p", []).append(s_fin)
                else:
                    pr = _pad_time(proj[:nbs].reshape(nbs, 1, -1), chunk)
                    o, s_fin = _hgrn(pr, hg_lb_logits, hg_norm_g[j], state_hgrn[j], i, 1, chunk)
                    o = jnp.pad(o[:, 0, :], ((0, ms - nbs), (0, 0)))
                    outs.setdefault("hg_s", []).append(s_fin)
                mix[name] = (o, w_o)
        else:
            w_small = _pad_cols(ssm_w_in[j][:, SSM_MAIN:], LANES)
            w_o = _cast_w(ssm_w_o, (j,))
            dt_bias = _pad_cols(ssm_dt_bias[j].reshape(1, SSM_HEADS), LANES)
            a_log = _pad_cols(ssm_a_log[j].reshape(1, SSM_HEADS), LANES)
            d_x = jnp.repeat(ssm_d[j], SSM_P).reshape(1, SSM_INNER)
            mix = {}
            for name, st in streams.items():
                zx = _proj(st["xb"], ssm_w_in, (j,), SSM_MAIN, st["tm"], 1024, name="ssm_in")
                dtr = _proj(st["xb"], w_small, (), LANES, st["tm"], LANES, name="ssm_in_dt")
                if name == "p":
                    nb_, t_, c_conv = nbp, seq, 4 * chunk
                    zx3 = zx.reshape(nbp, seq, -1)
                    dt3 = dtr.reshape(nbp, seq, LANES)
                    cs = jnp.zeros((nbp, SUBLANES, SSM_CH), F32)
                    s0 = jnp.zeros((nbp, SSM_INNER, SSM_N), F32)
                else:
                    nb_, t_, c_conv = nbs, 1, SUBLANES
                    zx3 = _pad_time(zx[:nbs].reshape(nbs, 1, -1), SUBLANES)
                    dt3 = _pad_time(dtr[:nbs].reshape(nbs, 1, LANES), chunk)
                    cs = jnp.pad(state_conv[j], ((0, 0), (SUBLANES - (SSM_CONV - 1), 0), (0, 0)))
                    s0 = state_ssm[j].reshape(nbs, SSM_INNER, SSM_N)
                xc, new_conv = _ssd_conv(zx3, cs, ssm_conv_w[j], ssm_conv_b[j], t_, c_conv, cw=1024)
                if name == "p":
                    z3 = zx3
                else:
                    xc = _pad_time(xc, chunk)
                    z3 = _pad_time(zx3[:, :, :SSM_INNER], chunk)
                y, s_fin = _ssd(xc, z3, dt3, expand, dt_bias, a_log, d_x, ssm_norm_g[j], s0, t_, chunk)
                s_fin = s_fin.reshape(nb_, SSM_HEADS, SSM_P, SSM_N)
                new_conv = new_conv[:, :SSM_CONV - 1, :]
                if name == "p":
                    o = y.reshape(mp, SSM_INNER)
                    outs.setdefault("ssm_p", []).append(s_fin)
                    outs.setdefault("conv_p", []).append(new_conv)
                else:
                    o = jnp.pad(y[:, 0, :], ((0, ms - nbs), (0, 0)))
                    outs.setdefault("ssm_s", []).append(s_fin)
                    outs.setdefault("conv_s", []).append(new_conv)
                mix[name] = (o, w_o)

        for name, st in streams.items():
            o, w_o = mix[name]
            st["x"], st["xb"] = _mm_ln(o, w_o, st["x"], ln_g[i, 1], ln_b[i, 1], 1.0, st["tml"], 512, "mixer_out_ln")
        ffn_ln(i, 1, 2)
        for name, st in streams.items():
            st["x"], st["xb"] = _mm_ple(st["x"], st["xb"], p_in[name][i].astype(BF16), ple_w_gate, ple_w_proj, (i,),
                                        st["tm"], 512)

    y_prompt = streams["p"]["x"].reshape(nbp, seq, d)
    y_sample = streams["s"]["x"][:nbs].reshape(nbs, 1, d)
    stack = lambda key: jnp.stack(outs[key])
    return (y_prompt, y_sample, stack("k_p"), stack("v_p"), stack("ik_p"), stack("k_s"), stack("v_s"), stack("ik_s"),
            stack("hg_p"), stack("hg_s"), stack("ssm_p"), stack("ssm_s"), stack("conv_p"), stack("conv_s"))
```

```python
import functools
import math

import jax
import jax.numpy as jnp
import numpy as np
from jax import lax
from jax.experimental import pallas as pl
from jax.experimental.pallas import tpu as pltpu

F32 = jnp.float32
BF16 = jnp.bfloat16

D_MODEL = 2048
DEPTH = 4
N_MIXERS = 3
D_FF = 2 * D_MODEL
PLE_DIM = 256
ALPHA = (2 * DEPTH) ** 0.25
LN_EPS = 1e-5
RMS_EPS = 1e-6
PAGE = 128

ATT_HD = 128
ATT_HEADS = 16
ATT_KV = 4
ATT_GROUP = ATT_HEADS // ATT_KV
IDX_HEADS = 16
IDX_DIM = 64
IDX_W_SCALE = (IDX_HEADS ** -0.5) * (IDX_DIM ** -0.5)
TOPK_MAX = 256
ATT_SCALE = ATT_HD ** -0.5
LOG2E = math.log2(math.e)
ATT_MAIN = ATT_HEADS * ATT_HD + 2 * ATT_KV * ATT_HD + IDX_HEADS * IDX_DIM

HG_HEADS = 16
HG_DK = 128
HG_DV = 128

SSM_INNER = 2 * D_MODEL
SSM_P = 64
SSM_HEADS = SSM_INNER // SSM_P
SSM_GROUPS = 8
SSM_HPG = SSM_HEADS // SSM_GROUPS
SSM_N = 128
SSM_CONV = 4
SSM_CH = SSM_INNER + 2 * SSM_GROUPS * SSM_N
SSM_MAIN = SSM_INNER + SSM_CH

LANES = 128
SUBLANES = 8
VMEM_LIMIT_BYTES = 56 * 1024 * 1024

INT_MIN = np.int32(-2 ** 31)
NEG_BIG = -1e30


def _cp(*sem):
    return pltpu.CompilerParams(dimension_semantics=sem, vmem_limit_bytes=VMEM_LIMIT_BYTES)


def _nt(a, b):
    return lax.dot_general(a, b, (((1,), (1,)), ((), ())), preferred_element_type=F32)


def _tn(a, b):
    return lax.dot_general(a, b, (((0,), (0,)), ((), ())), preferred_element_type=F32)


def _silu(x):
    return x * jax.nn.sigmoid(x)


def _wspec(k, tn, idx, col_block):
    lead = (None,) * len(idx)
    return pl.BlockSpec(lead + (k, tn), lambda n, i: idx + (0, col_block(n)))


def _proj_body(x_ref, w_ref, o_ref, *rest):
    wb_ref = rest[-1]

    @pl.when(pl.program_id(1) == 0)
    def _():
        wb_ref[...] = w_ref[...].astype(BF16)

    acc = jnp.dot(x_ref[...], wb_ref[...], preferred_element_type=F32)
    o_ref[...] = acc
    if len(rest) == 2:
        rest[0][...] = acc.astype(BF16)


def _proj(xb, w, idx, n_cols, tm, tn, with_bf16=False, name="proj"):
    m, k = xb.shape
    out_shape = [jax.ShapeDtypeStruct((m, n_cols), F32)]
    out_specs = [pl.BlockSpec((tm, tn), lambda n, i: (i, n))]
    if with_bf16:
        out_shape.append(jax.ShapeDtypeStruct((m, n_cols), BF16))
        out_specs.append(pl.BlockSpec((tm, tn), lambda n, i: (i, n)))
    res = pl.pallas_call(
        _proj_body,
        grid=(n_cols // tn, m // tm),
        in_specs=[pl.BlockSpec((tm, k), lambda n, i: (i, 0)), _wspec(k, tn, idx, lambda n: n)],
        out_specs=out_specs,
        out_shape=out_shape,
        scratch_shapes=[pltpu.VMEM((k, tn), BF16)],
        compiler_params=_cp("parallel", "arbitrary"),
        name=name,
    )(xb, w)
    return res if with_bf16 else res[0]


def _swiglu_body(x_ref, wg_ref, wu_ref, o_ref, wgb_ref, wub_ref):
    @pl.when(pl.program_id(1) == 0)
    def _():
        wgb_ref[...] = wg_ref[...].astype(BF16)
        wub_ref[...] = wu_ref[...].astype(BF16)

    x = x_ref[...]
    g = jnp.dot(x, wgb_ref[...], preferred_element_type=F32)
    u = jnp.dot(x, wub_ref[...], preferred_element_type=F32)
    o_ref[...] = (_silu(g) * u).astype(o_ref.dtype)


def _mm_swiglu(xb, wgu, idx, tm, tn):
    m, k = xb.shape
    f = wgu.shape[-1] // 2
    nj = f // tn
    return pl.pallas_call(
        _swiglu_body,
        grid=(nj, m // tm),
        in_specs=[pl.BlockSpec((tm, k), lambda n, i: (i, 0)),
                  _wspec(k, tn, idx, lambda n: n),
                  _wspec(k, tn, idx, lambda n: n + nj)],
        out_specs=pl.BlockSpec((tm, tn), lambda n, i: (i, n)),
        out_shape=jax.ShapeDtypeStruct((m, f), BF16),
        scratch_shapes=[pltpu.VMEM((k, tn), BF16), pltpu.VMEM((k, tn), BF16)],
        compiler_params=_cp("parallel", "arbitrary"),
        name="ffn_up",
    )(xb, wgu, wgu)


def _cast_body(w_ref, o_ref):
    o_ref[...] = w_ref[...].astype(BF16)


def _cast_w(w, idx, tk=512):
    k, n = w.shape[-2:]
    lead = (None,) * len(idx)
    return pl.pallas_call(
        _cast_body,
        grid=(k // tk,),
        in_specs=[pl.BlockSpec(lead + (tk, n), lambda i: idx + (i, 0))],
        out_specs=pl.BlockSpec((tk, n), lambda i: (i, 0)),
        out_shape=jax.ShapeDtypeStruct((k, n), BF16),
        compiler_params=_cp("parallel"),
        name="cast_w",
    )(w)


LN_SPLIT = 2


def _mm_ln_body(a_ref, w_ref, r_ref, g_ref, b_ref, o_ref, ob_ref, *, scale, tm):
    g = g_ref[...]
    b = b_ref[...]
    rows = max(tm // LN_SPLIT, min(tm, 2 * SUBLANES))
    for r in range(0, tm, rows):
        sl = pl.ds(r, rows)
        y = ALPHA * r_ref[sl, :] + scale * jnp.dot(a_ref[sl, :], w_ref[...], preferred_element_type=F32)
        mu = jnp.mean(y, axis=-1, keepdims=True)
        yc = y - mu
        var = jnp.mean(yc * yc, axis=-1, keepdims=True)
        out = yc * lax.rsqrt(var + LN_EPS) * g + b
        o_ref[sl, :] = out
        ob_ref[sl, :] = out.astype(BF16)


def _mm_ln(ab, wb, res, g, b, scale, tm, name):
    m, kdim = ab.shape
    n = wb.shape[1]
    return pl.pallas_call(
        functools.partial(_mm_ln_body, scale=scale, tm=tm),
        grid=(m // tm,),
        in_specs=[pl.BlockSpec((tm, kdim), lambda i: (i, 0)),
                  pl.BlockSpec((kdim, n), lambda i: (0, 0), pipeline_mode=pl.Buffered(1)),
                  pl.BlockSpec((tm, n), lambda i: (i, 0)),
                  pl.BlockSpec((1, n), lambda i: (0, 0)),
                  pl.BlockSpec((1, n), lambda i: (0, 0))],
        out_specs=[pl.BlockSpec((tm, n), lambda i: (i, 0)),
                   pl.BlockSpec((tm, n), lambda i: (i, 0))],
        out_shape=[jax.ShapeDtypeStruct((m, n), F32), jax.ShapeDtypeStruct((m, n), BF16)],
        compiler_params=_cp("parallel"),
        name=name,
    )(ab, wb, res, g.reshape(1, n), b.reshape(1, n))


def _ple_body(xb_ref, p_ref, wg_ref, wp_ref, x_ref, o_ref, ob_ref, wgb_ref, wpb_ref):
    @pl.when(pl.program_id(1) == 0)
    def _():
        wgb_ref[...] = wg_ref[...].astype(BF16)
        wpb_ref[...] = wp_ref[...].astype(BF16)

    gate = jax.nn.sigmoid(jnp.dot(xb_ref[...], wgb_ref[...], preferred_element_type=F32))
    proj = jnp.dot(p_ref[...], wpb_ref[...], preferred_element_type=F32)
    out = x_ref[...] + gate * proj
    o_ref[...] = out
    ob_ref[...] = out.astype(BF16)


def _mm_ple(x32, xb, pb, wg, wp, idx, tm, tn):
    m, d = xb.shape
    pd = pb.shape[1]
    return pl.pallas_call(
        _ple_body,
        grid=(d // tn, m // tm),
        in_specs=[pl.BlockSpec((tm, d), lambda n, i: (i, 0)),
                  pl.BlockSpec((tm, pd), lambda n, i: (i, 0)),
                  _wspec(d, tn, idx, lambda n: n),
                  _wspec(pd, tn, idx, lambda n: n),
                  pl.BlockSpec((tm, tn), lambda n, i: (i, n))],
        out_specs=[pl.BlockSpec((tm, tn), lambda n, i: (i, n)),
                   pl.BlockSpec((tm, tn), lambda n, i: (i, n))],
        out_shape=[jax.ShapeDtypeStruct((m, d), F32), jax.ShapeDtypeStruct((m, d), BF16)],
        scratch_shapes=[pltpu.VMEM((d, tn), BF16), pltpu.VMEM((pd, tn), BF16)],
        compiler_params=_cp("parallel", "arbitrary"),
        name="ple",
    )(xb, pb, wg, wp, x32)


def _ik_norm_body(p_ref, g_ref, o_ref, ob_ref):
    x = p_ref[...][:, :IDX_DIM]
    mu = jnp.mean(x, axis=-1, keepdims=True)
    xc = x - mu
    out = xc * lax.rsqrt(jnp.mean(xc * xc, axis=-1, keepdims=True) + LN_EPS) * g_ref[...]
    o_ref[...] = out
    ob_ref[...] = out.astype(BF16)


def _ik_norm(proj2, ik_g, tm):
    m = proj2.shape[0]
    return pl.pallas_call(
        _ik_norm_body,
        grid=(m // tm,),
        in_specs=[pl.BlockSpec((tm, LANES), lambda i: (i, 0)),
                  pl.BlockSpec((1, IDX_DIM), lambda i: (0, 0))],
        out_specs=[pl.BlockSpec((tm, IDX_DIM), lambda i: (i, 0)),
                   pl.BlockSpec((tm, IDX_DIM), lambda i: (i, 0))],
        out_shape=[jax.ShapeDtypeStruct((m, IDX_DIM), F32), jax.ShapeDtypeStruct((m, IDX_DIM), BF16)],
        compiler_params=_cp("parallel"),
        name="idx_k_norm",
    )(proj2, ik_g.reshape(1, IDX_DIM))


def _order_key(x):
    bits = pltpu.bitcast(x, jnp.int32)
    return jnp.where(bits < 0, bits ^ jnp.int32(0x7FFFFFFF), bits)


_RADIX_BITS = [INT_MIN] + [np.int32(1 << s) for s in range(30, -1, -1)]


RADIX_ROWS = 128


def _dsa_prompt_body(q_ref, iq_ref, iw_ref, k_ref, v_ref, ik_ref, o_ref, keys_ref, *, qb, kc, topk):
    i = pl.program_id(1)
    nck = ((i + 1) * qb + kc - 1) // kc
    row_pos = i * qb + lax.broadcasted_iota(jnp.int32, (qb, 1), 0)
    iq = iq_ref[...]
    iw = iw_ref[...][:, IDX_DIM:IDX_DIM + IDX_HEADS] * IDX_W_SCALE
    iq_h = [iq[:, h * IDX_DIM:(h + 1) * IDX_DIM] for h in range(IDX_HEADS)]
    iw_h = [iw[:, h:h + 1] for h in range(IDX_HEADS)]
    col0 = lax.broadcasted_iota(jnp.int32, (1, kc), 1)

    def score_chunk(c, carry):
        off = pl.multiple_of(c * kc, kc)
        ikc = ik_ref[pl.ds(off, kc), :]
        sc = jnp.zeros((qb, kc), F32)
        for h in range(IDX_HEADS):
            sc = sc + iw_h[h] * jnp.maximum(_nt(iq_h[h], ikc), 0.0)
        key = jnp.where(col0 + off <= row_pos, _order_key(sc), INT_MIN)
        keys_ref[:, pl.ds(off, kc)] = key
        return carry

    lax.fori_loop(0, nck, score_chunk, 0)

    nblk = qb // RADIX_ROWS
    los = [jnp.full((RADIX_ROWS, 1), INT_MIN, jnp.int32) for _ in range(nblk)]
    for bit in _RADIX_BITS:
        cands = [lo + bit for lo in los]
        accs = []
        for blk in range(nblk):
            cand_b = jnp.broadcast_to(cands[blk], (RADIX_ROWS, LANES))

            def count_chunk(c, acc, cand_b=cand_b, r0=blk * RADIX_ROWS):
                off = pl.multiple_of(c * kc, kc)
                for t in range(kc // LANES):
                    tile = keys_ref[r0:r0 + RADIX_ROWS, pl.ds(off + t * LANES, LANES)]
                    acc = acc + jnp.where(tile >= cand_b, 1.0, 0.0)
                return acc

            accs.append(lax.fori_loop(0, nck, count_chunk, jnp.zeros((RADIX_ROWS, LANES), F32)))
        cnts = [jnp.sum(acc, axis=-1, keepdims=True) for acc in accs]
        los = [jnp.where(cnts[blk] >= topk, cands[blk], los[blk]) for blk in range(nblk)]
    thr = jnp.maximum(jnp.concatenate(los, axis=0), INT_MIN + 1)

    for g in range(ATT_KV):
        qg = jnp.concatenate(
            [q_ref[:, (g * ATT_GROUP + j) * ATT_HD:(g * ATT_GROUP + j + 1) * ATT_HD] for j in range(ATT_GROUP)], axis=0)
        qg = (qg * (ATT_SCALE * LOG2E)).astype(BF16)

        def att_chunk(c, carry, qg=qg, g=g):
            m, l, acc = carry
            off = pl.multiple_of(c * kc, kc)
            kch = k_ref[pl.ds(off, kc), g * ATT_HD:(g + 1) * ATT_HD]
            vch = v_ref[pl.ds(off, kc), g * ATT_HD:(g + 1) * ATT_HD]
            sel = (keys_ref[:, pl.ds(off, kc)] >= thr)[None]
            s = jnp.where(sel, _nt(qg, kch).reshape(ATT_GROUP, qb, kc), -jnp.inf)
            m_new = jnp.maximum(m, jnp.max(s, axis=-1, keepdims=True))
            p = jnp.exp2(s - m_new)
            a = jnp.exp2(m - m_new)
            l = a * l + jnp.sum(p, axis=-1, keepdims=True)
            pv = jnp.dot(p.reshape(ATT_GROUP * qb, kc).astype(BF16), vch, preferred_element_type=F32)
            acc = a * acc + pv.reshape(ATT_GROUP, qb, ATT_HD)
            return m_new, l, acc

        init = (jnp.full((ATT_GROUP, qb, 1), NEG_BIG, F32), jnp.zeros((ATT_GROUP, qb, 1), F32),
                jnp.zeros((ATT_GROUP, qb, ATT_HD), F32))
        _, l, acc = lax.fori_loop(0, nck, att_chunk, init)
        out = acc / l
        for j in range(ATT_GROUP):
            h = g * ATT_GROUP + j
            o_ref[:, h * ATT_HD:(h + 1) * ATT_HD] = out[j].astype(o_ref.dtype)


def _dsa_prompt(proj, projb, proj2, ikb, nb, s, qb=256, kc=512):
    nq = s // qb
    topk = min(TOPK_MAX, s // 4)
    qw = ATT_HEADS * ATT_HD
    kvw = ATT_KV * ATT_HD
    iqw = IDX_HEADS * IDX_DIM
    return pl.pallas_call(
        functools.partial(_dsa_prompt_body, qb=qb, kc=kc, topk=topk),
        grid=(nb, nq),
        in_specs=[pl.BlockSpec((qb, qw), lambda b, i: (b * nq + i, 0)),
                  pl.BlockSpec((qb, iqw), lambda b, i: (b * nq + i, (qw + 2 * kvw) // iqw)),
                  pl.BlockSpec((qb, LANES), lambda b, i: (b * nq + i, 0)),
                  pl.BlockSpec((s, kvw), lambda b, i: (b, qw // kvw)),
                  pl.BlockSpec((s, kvw), lambda b, i: (b, qw // kvw + 1)),
                  pl.BlockSpec((s, IDX_DIM), lambda b, i: (b, 0))],
        out_specs=pl.BlockSpec((qb, qw), lambda b, i: (b * nq + i, 0)),
        out_shape=jax.ShapeDtypeStruct((nb * s, qw), BF16),
        scratch_shapes=[pltpu.VMEM((qb, s), jnp.int32)],
        compiler_params=_cp("parallel", "arbitrary"),
        name="dsa_prompt",
    )(proj, projb, proj2, projb, projb, ikb)


SCORE_PAGES = 16


def _dsa_s_scores_body(pt_ref, iq_ref, iw_ref, ikn_ref, *rest, n_pages):
    page_refs, o_ref = rest[:SCORE_PAGES], rest[SCORE_PAGES]
    p = pl.program_id(1)
    iq = iq_ref[...].astype(BF16)
    iw = iw_ref[...] * IDX_W_SCALE

    @pl.when(p == 0)
    def _():
        own = jnp.broadcast_to(ikn_ref[...], (SUBLANES, IDX_DIM)).astype(BF16)
        d = jnp.maximum(_nt(iq, own), 0.0)
        sc = jnp.sum(iw * d, axis=0, keepdims=True)
        o_ref[n_pages:n_pages + SUBLANES, :] = jnp.broadcast_to(sc[:, 0:1], (SUBLANES, LANES))

    for r in range(SCORE_PAGES):
        d = jnp.maximum(_nt(iq, page_refs[r][...].astype(BF16)), 0.0)
        o_ref[pl.ds(p * SCORE_PAGES + r, 1), :] = jnp.sum(iw * d, axis=0, keepdims=True)


def _dsa_s_scores(page_table, iq, iw, ik_new, cache_ik, layer):
    nb, n_pages = page_table.shape
    page_spec = lambda r: pl.BlockSpec((None, None, PAGE, IDX_DIM),
                                       lambda b, p, pt: (layer, pt[b, p * SCORE_PAGES + r], 0, 0))
    return pl.pallas_call(
        functools.partial(_dsa_s_scores_body, n_pages=n_pages),
        grid_spec=pltpu.PrefetchScalarGridSpec(
            num_scalar_prefetch=1,
            grid=(nb, n_pages // SCORE_PAGES),
            in_specs=[pl.BlockSpec((None, IDX_HEADS, IDX_DIM), lambda b, p, pt: (b, 0, 0)),
                      pl.BlockSpec((None, IDX_HEADS, 1), lambda b, p, pt: (b, 0, 0)),
                      pl.BlockSpec((None, 1, IDX_DIM), lambda b, p, pt: (b, 0, 0))]
                     + [page_spec(r) for r in range(SCORE_PAGES)],
            out_specs=pl.BlockSpec((None, n_pages + SUBLANES, LANES), lambda b, p, pt: (b, 0, 0)),
        ),
        out_shape=jax.ShapeDtypeStruct((nb, n_pages + SUBLANES, LANES), F32),
        compiler_params=_cp("parallel", "arbitrary"),
        name="dsa_sample_scores",
    )(page_table, iq, iw, ik_new, *([cache_ik] * SCORE_PAGES))


def _dsa_s_select_body(sc_ref, ptc_ref, idx_ref, meta_ref, rank_ref, phys_ref, *, n_pages, topk):
    shape = (n_pages + SUBLANES, LANES)
    rows = lax.broadcasted_iota(jnp.int32, shape, 0)
    cols = lax.broadcasted_iota(jnp.int32, shape, 1)
    live = (rows < n_pages) | ((rows == n_pages) & (cols == 0))
    keys = jnp.where(live, _order_key(sc_ref[...]), INT_MIN)
    lo = jnp.full((1, 1), INT_MIN, jnp.int32)
    for bit in _RADIX_BITS:
        cand = lo + bit
        cnt = jnp.sum(jnp.sum(jnp.where(keys >= cand, 1.0, 0.0), axis=0, keepdims=True), axis=1, keepdims=True)
        lo = jnp.where(cnt >= topk, cand, lo)
    sel = keys >= jnp.maximum(lo, INT_MIN + 1)
    sel_c = sel[:n_pages, :]
    own = jnp.where(sel[n_pages:n_pages + 1, 0:1], 1, 0)

    r_i = lax.broadcasted_iota(jnp.int32, (LANES, LANES), 0)
    c_i = lax.broadcasted_iota(jnp.int32, (LANES, LANES), 1)
    ones_le = jnp.where(r_i <= c_i, 1.0, 0.0).astype(BF16)
    ones_gt = jnp.where(r_i > c_i, 1.0, 0.0).astype(BF16)
    m = jnp.where(sel_c, 1.0, 0.0).astype(BF16)
    within = jnp.dot(m, ones_le, preferred_element_type=F32)
    tot = jnp.broadcast_to(within[:, LANES - 1:LANES], (n_pages, LANES)).astype(BF16)
    before = jnp.dot(ones_gt, tot, preferred_element_type=F32)
    rank_ref[...] = jnp.where(sel_c, (within + before).astype(jnp.int32) - 1, -1)
    phys_ref[...] = (ptc_ref[...] * PAGE + c_i[:n_pages, :]).astype(F32)
    n_sel = (before[n_pages - 1:n_pages, 0:1] + within[n_pages - 1:n_pages, LANES - 1:LANES]).astype(jnp.int32)

    slot = lax.broadcasted_iota(jnp.int32, (topk, 1), 0)

    def place(r, acc):
        return acc + jnp.where(rank_ref[pl.ds(r, 1), :] == slot, phys_ref[pl.ds(r, 1), :], 0.0)

    acc = lax.fori_loop(0, n_pages, place, jnp.zeros((topk, LANES), F32), unroll=8)
    idx_ref[...] = jnp.sum(acc, axis=-1, keepdims=True).astype(jnp.int32)
    mrow = lax.broadcasted_iota(jnp.int32, (SUBLANES, LANES), 0)
    meta_ref[...] = jnp.where(mrow == 0, jnp.minimum(n_sel, topk), own)


def _dsa_s_select(scores, page_table):
    nb, n_pages = page_table.shape
    assert n_pages == LANES
    topk = min(TOPK_MAX, (n_pages * PAGE + 1) // 4)
    idx, meta = pl.pallas_call(
        functools.partial(_dsa_s_select_body, n_pages=n_pages, topk=topk),
        grid=(nb,),
        in_specs=[pl.BlockSpec((None, n_pages + SUBLANES, LANES), lambda b: (b, 0, 0)),
                  pl.BlockSpec((None, n_pages, 1), lambda b: (b, 0, 0))],
        out_specs=[pl.BlockSpec((None, topk, 1), lambda b: (b, 0, 0)),
                   pl.BlockSpec((None, SUBLANES, LANES), lambda b: (b, 0, 0))],
        out_shape=[jax.ShapeDtypeStruct((nb, topk, 1), jnp.int32),
                   jax.ShapeDtypeStruct((nb, SUBLANES, LANES), jnp.int32)],
        scratch_shapes=[pltpu.VMEM((n_pages, LANES), jnp.int32), pltpu.VMEM((n_pages, LANES), F32)],
        compiler_params=_cp("parallel"),
        name="dsa_sample_select",
    )(scores, page_table.reshape(nb, n_pages, 1))
    return idx.reshape(nb, topk), meta[:, 0:2, 0]


def _dsa_s_attend_body(idx_ref, meta_ref, q_ref, kn_ref, vn_ref, ck_ref, cv_ref, o_ref, kbuf, vbuf, sem,
                       *, layer, topk):
    b = pl.program_id(0)
    nb = pl.num_programs(0)
    slot = b % 2

    def row_copies(tok, buf_slot, j):
        row = idx_ref[tok, j]
        page, off = row // PAGE, row % PAGE
        dst = pl.ds(j * ATT_KV, ATT_KV)
        return (pltpu.make_async_copy(ck_ref.at[layer, page, off], kbuf.at[buf_slot, dst, :], sem.at[buf_slot, 0]),
                pltpu.make_async_copy(cv_ref.at[layer, page, off], vbuf.at[buf_slot, dst, :], sem.at[buf_slot, 1]))

    def start_gather(tok, buf_slot):
        def body(j, carry):
            for cp in row_copies(tok, buf_slot, j):
                cp.start()
            return carry
        lax.fori_loop(0, topk, body, 0)

    @pl.when(b == 0)
    def _():
        start_gather(0, 0)

    @pl.when(b + 1 < nb)
    def _():
        start_gather(b + 1, 1 - slot)

    def wait_body(j, carry):
        for cp in row_copies(b, slot, j):
            cp.wait()
        return carry

    lax.fori_loop(0, topk, wait_body, 0)

    n_sel = meta_ref[b, 0]
    own = meta_ref[b, 1] > 0
    qb = q_ref[...].astype(BF16)
    ncol = topk * ATT_KV
    col = lax.broadcasted_iota(jnp.int32, (ATT_HEADS, ncol), 1)
    head = lax.broadcasted_iota(jnp.int32, (ATT_HEADS, ncol), 0)
    valid = (col % ATT_KV == head // ATT_GROUP) & (col // ATT_KV < n_sel)
    s = jnp.where(valid, _nt(qb, kbuf[slot].astype(BF16)) * ATT_SCALE, NEG_BIG)
    kn = kn_ref[...].astype(BF16).astype(F32)
    vn = vn_ref[...].astype(BF16).astype(F32)
    expand = lambda a: jnp.concatenate(
        [jnp.broadcast_to(a[:, g * ATT_HD:(g + 1) * ATT_HD], (ATT_GROUP, ATT_HD)) for g in range(ATT_KV)], axis=0)
    s_own = jnp.where(own, jnp.sum(qb.astype(F32) * expand(kn), axis=-1, keepdims=True) * ATT_SCALE, NEG_BIG)
    m = jnp.maximum(jnp.max(s, axis=-1, keepdims=True), s_own)
    p = jnp.where(valid, jnp.exp(s - m), 0.0)
    p_own = jnp.where(own, jnp.exp(s_own - m), 0.0)
    l = jnp.sum(p, axis=-1, keepdims=True) + p_own
    acc = (jnp.dot(p.astype(BF16), vbuf[slot].astype(BF16), preferred_element_type=F32)
           + p_own.astype(BF16).astype(F32) * expand(vn))
    o_ref[...] = acc / l


def _dsa_s_attend(idx, meta, q, k_new, v_new, cache_k, cache_v, layer):
    nb, topk = idx.shape
    kvw = ATT_KV * ATT_HD
    return pl.pallas_call(
        functools.partial(_dsa_s_attend_body, layer=layer, topk=topk),
        grid_spec=pltpu.PrefetchScalarGridSpec(
            num_scalar_prefetch=2,
            grid=(nb,),
            in_specs=[pl.BlockSpec((None, ATT_HEADS, ATT_HD), lambda b, idx, meta: (b, 0, 0)),
                      pl.BlockSpec((None, 1, kvw), lambda b, idx, meta: (b, 0, 0)),
                      pl.BlockSpec((None, 1, kvw), lambda b, idx, meta: (b, 0, 0)),
                      pl.BlockSpec(memory_space=pl.ANY),
                      pl.BlockSpec(memory_space=pl.ANY)],
            out_specs=pl.BlockSpec((None, ATT_HEADS, ATT_HD), lambda b, idx, meta: (b, 0, 0)),
            scratch_shapes=[pltpu.VMEM((2, topk * ATT_KV, ATT_HD), F32),
                            pltpu.VMEM((2, topk * ATT_KV, ATT_HD), F32),
                            pltpu.SemaphoreType.DMA((2, 2))],
        ),
        out_shape=jax.ShapeDtypeStruct((nb, ATT_HEADS, ATT_HD), F32),
        compiler_params=_cp("arbitrary"),
        name="dsa_sample_attend",
    )(idx, meta, q, k_new, v_new, cache_k, cache_v)


HG_SUB = 16


HG_HB = 4
HG_EXP_LIMIT = 80.0


def _hgrn_body(lbl_ref, q_ref, f_ref, i_ref, g_ref, ng_ref, s0_ref, o_ref, so_ref, st_ref, *, layer, c, t, nc):
    ci = pl.program_id(2)
    mid = c // 2

    @pl.when(ci == 0)
    def _():
        for hh in range(HG_HB):
            st_ref[hh] = s0_ref[hh].T

    logits = lbl_ref[...]
    e = jnp.exp(logits - jnp.max(logits, axis=0, keepdims=True))
    soft = e / jnp.sum(e, axis=0, keepdims=True)
    lb_all = jnp.zeros((1, HG_HB * HG_DK), F32)
    for r in range(1, layer + 1):
        lb_all = lb_all + soft[r:r + 1, :]
    ng = ng_ref[...]

    def gates(hh, sl, row0):
        cs = slice(hh * HG_DK, (hh + 1) * HG_DK)
        lb = lb_all[:, cs]
        fg = lb + (1.0 - lb) * jax.nn.sigmoid(f_ref[sl, cs])
        lf = jnp.log(fg)
        kk = 1.0 - fg
        if t % c:
            n = lf.shape[0]
            valid = (ci * c + row0 + lax.broadcasted_iota(jnp.int32, (n, 1), 0)) < t
            lf = jnp.where(valid, lf, 0.0)
            kk = jnp.where(valid, kk, 0.0)
        return lf, kk

    def finish(hh, sl, o):
        cs = slice(hh * HG_DV, (hh + 1) * HG_DV)
        on = o * lax.rsqrt(jnp.mean(o * o, axis=-1, keepdims=True) + RMS_EPS) * ng
        o_ref[sl, cs] = (on * _silu(g_ref[sl, cs])).astype(o_ref.dtype)

    tri_c = lax.broadcasted_iota(jnp.int32, (c, c), 0) >= lax.broadcasted_iota(jnp.int32, (c, c), 1)
    full = pl.ds(0, c)
    lfs, kks, bs = [], [], []
    safe = None
    for hh in range(HG_HB):
        lf, kk = gates(hh, full, 0)
        b = jnp.dot(tri_c.astype(F32), lf, precision=lax.Precision.HIGHEST, preferred_element_type=F32)
        bm = b[mid - 1:mid, :]
        ok = jnp.min(jnp.minimum(bm, b[c - 1:c, :] - bm)) > -HG_EXP_LIMIT
        safe = ok if safe is None else jnp.logical_and(safe, ok)
        lfs.append(lf)
        kks.append(kk)
        bs.append(b)

    @pl.when(safe)
    def _():
        for hh in range(HG_HB):
            cs = slice(hh * HG_DK, (hh + 1) * HG_DK)
            b, kk = bs[hh], kks[hh]
            bm = b[mid - 1:mid, :]
            bl = b[c - 1:c, :]
            qq = _silu(q_ref[:, cs])
            vv = i_ref[:, cs].astype(BF16)
            att = jnp.where(tri_c, _nt((qq * jnp.exp(b - bm)).astype(BF16), (kk * jnp.exp(bm - b)).astype(BF16)), 0.0)
            st = st_ref[hh]
            o = (jnp.dot(att.astype(BF16), vv, preferred_element_type=F32)
                 + _nt((qq * jnp.exp(b)).astype(BF16), st.astype(BF16)))
            st_ref[hh] = st * jnp.exp(bl) + _tn(vv, (kk * jnp.exp(bl - b)).astype(BF16))
            finish(hh, full, o)

    @pl.when(jnp.logical_not(safe))
    def _():
        rows = lax.broadcasted_iota(jnp.int32, (HG_SUB, 1), 0)
        tri = (lax.broadcasted_iota(jnp.int32, (HG_SUB, HG_SUB), 0)
               >= lax.broadcasted_iota(jnp.int32, (HG_SUB, HG_SUB), 1)).astype(F32)
        for hh in range(HG_HB):
            cs = slice(hh * HG_DK, (hh + 1) * HG_DK)

            def sub_block(sb, carry, hh=hh, cs=cs):
                row0 = pl.multiple_of(sb * HG_SUB, HG_SUB)
                sl = pl.ds(row0, HG_SUB)
                lf, kk = gates(hh, sl, row0)
                qq = _silu(q_ref[sl, cs])
                vv = i_ref[sl, cs]
                b = jnp.dot(tri, lf, precision=lax.Precision.HIGHEST, preferred_element_type=F32)
                st = st_ref[hh]
                o = _nt((qq * jnp.exp(b)).astype(BF16), st.astype(BF16))
                for s in range(HG_SUB):
                    dec = jnp.exp(jnp.where(rows >= s, b - b[s:s + 1, :], -jnp.inf))
                    att = jnp.sum(qq * dec * kk[s:s + 1, :], axis=-1, keepdims=True)
                    o = o + att * vv[s:s + 1, :]
                bl = b[HG_SUB - 1:HG_SUB, :]
                st_ref[hh] = st * jnp.exp(bl) + _tn(vv.astype(BF16), (kk * jnp.exp(bl - b)).astype(BF16))
                finish(hh, sl, o)
                return carry

            lax.fori_loop(0, c // HG_SUB, sub_block, 0)

    @pl.when(ci == nc - 1)
    def _():
        for hh in range(HG_HB):
            so_ref[hh] = st_ref[hh].T


def _hgrn(proj, lb_logits, norm_g, s0, layer, t, c=128):
    nb, tpad, _ = proj.shape
    nc = tpad // c
    nhb = HG_HEADS // HG_HB
    w = HG_HB * HG_DK
    return pl.pallas_call(
        functools.partial(_hgrn_body, layer=layer, c=c, t=t, nc=nc),
        grid=(nb, nhb, nc),
        in_specs=[pl.BlockSpec((DEPTH, w), lambda b, h, ci: (0, h)),
                  pl.BlockSpec((None, c, w), lambda b, h, ci: (b, ci, h)),
                  pl.BlockSpec((None, c, w), lambda b, h, ci: (b, ci, nhb + h)),
                  pl.BlockSpec((None, c, w), lambda b, h, ci: (b, ci, 2 * nhb + h)),
                  pl.BlockSpec((None, c, w), lambda b, h, ci: (b, ci, 3 * nhb + h)),
                  pl.BlockSpec((1, HG_DV), lambda b, h, ci: (0, 0)),
                  pl.BlockSpec((None, HG_HB, HG_DK, HG_DV), lambda b, h, ci: (b, h, 0, 0))],
        out_specs=[pl.BlockSpec((None, c, w), lambda b, h, ci: (b, ci, h)),
                   pl.BlockSpec((None, HG_HB, HG_DK, HG_DV), lambda b, h, ci: (b, h, 0, 0))],
        out_shape=[jax.ShapeDtypeStruct((nb, tpad, HG_HEADS * HG_DV), BF16),
                   jax.ShapeDtypeStruct((nb, HG_HEADS, HG_DK, HG_DV), F32)],
        scratch_shapes=[pltpu.VMEM((HG_HB, HG_DV, HG_DK), F32)],
        compiler_params=_cp("parallel", "parallel", "arbitrary"),
        name="hgrn2",
    )(lb_logits, proj, proj, proj, proj, norm_g.reshape(1, HG_DV), s0)


def _conv_body(x_ref, halo_ref, cs_ref, w_ref, b_ref, o_ref, nc_ref, *, c, t, nc):
    ti = pl.program_id(2)
    halo = jnp.where(ti == 0, cs_ref[...], halo_ref[...])
    full = jnp.concatenate([halo, x_ref[...]], axis=0)
    w = w_ref[...]
    conv = b_ref[...]
    for k in range(SSM_CONV):
        lo = SUBLANES - (SSM_CONV - 1) + k
        conv = conv + full[lo:lo + c, :] * w[k:k + 1, :]
    o_ref[...] = _silu(conv)

    @pl.when(ti == nc - 1)
    def _():
        tv = t - (nc - 1) * c
        tail = full[SUBLANES + tv - (SSM_CONV - 1):SUBLANES + tv, :]
        nc_ref[...] = jnp.concatenate([tail, jnp.zeros((SUBLANES - (SSM_CONV - 1), tail.shape[1]), F32)], axis=0)


def _ssd_conv(zx, cs_pad, conv_w, conv_b, t, c, cw=512):
    nb, tpad, _ = zx.shape
    nc = tpad // c
    ncol = SSM_CH // cw
    col0 = SSM_INNER // cw
    hb = c // SUBLANES
    return pl.pallas_call(
        functools.partial(_conv_body, c=c, t=t, nc=nc),
        grid=(nb, ncol, nc),
        in_specs=[pl.BlockSpec((None, c, cw), lambda b, j, ti: (b, ti, col0 + j)),
                  pl.BlockSpec((None, SUBLANES, cw), lambda b, j, ti: (b, jnp.maximum(ti * hb - 1, 0), col0 + j)),
                  pl.BlockSpec((None, SUBLANES, cw), lambda b, j, ti: (b, 0, j)),
                  pl.BlockSpec((SSM_CONV, cw), lambda b, j, ti: (0, j)),
                  pl.BlockSpec((1, cw), lambda b, j, ti: (0, j))],
        out_specs=[pl.BlockSpec((None, c, cw), lambda b, j, ti: (b, ti, j)),
                   pl.BlockSpec((None, SUBLANES, cw), lambda b, j, ti: (b, 0, j))],
        out_shape=[jax.ShapeDtypeStruct((nb, tpad, SSM_CH), F32),
                   jax.ShapeDtypeStruct((nb, SUBLANES, SSM_CH), F32)],
        compiler_params=_cp("parallel", "parallel", "arbitrary"),
        name="ssd_conv",
    )(zx, zx, cs_pad, conv_w, conv_b.reshape(1, SSM_CH))


def _expand_heads(v, e):
    hi = v.astype(BF16)
    r1 = v - hi.astype(F32)
    mid = r1.astype(BF16)
    lo = (r1 - mid.astype(F32)).astype(BF16)
    return (jnp.dot(hi, e, preferred_element_type=F32) + jnp.dot(mid, e, preferred_element_type=F32)
            + jnp.dot(lo, e, preferred_element_type=F32))


def _ssd_body(xc_ref, z_ref, dt_ref, e_ref, dtb_ref, alog_ref, dx_ref, ng_ref, s0_ref, o_ref, so_ref, st_ref,
              *, c, t, nc):
    ci = pl.program_id(1)
    gw = SSM_HPG * SSM_P

    @pl.when(ci == 0)
    def _():
        for blk in range(SSM_INNER // LANES):
            st_ref[:, blk * LANES:(blk + 1) * LANES] = s0_ref[blk * LANES:(blk + 1) * LANES, :].T

    e = e_ref[...]
    rows = lax.broadcasted_iota(jnp.int32, (c, 1), 0)
    tri_b = lax.broadcasted_iota(jnp.int32, (c, c), 0) >= lax.broadcasted_iota(jnp.int32, (c, c), 1)
    lane_lo = lax.broadcasted_iota(jnp.int32, (1, LANES), 1) < SSM_P

    dt = jax.nn.softplus(dt_ref[...] + dtb_ref[...])
    if t % c:
        dt = jnp.where(ci * c + rows < t, dt, 0.0)
    da = dt * (-jnp.exp(alog_ref[...]))
    bcum = jnp.dot(tri_b.astype(F32), da, precision=lax.Precision.HIGHEST, preferred_element_type=F32)
    bcum_t = bcum.T
    bl = bcum[c - 1:c, :]
    dt_x = _expand_heads(dt, e)
    eb_x = _expand_heads(jnp.exp(bcum), e)
    w_x = _expand_heads(jnp.exp(bl - bcum) * dt, e)
    decay_x = _expand_heads(jnp.broadcast_to(jnp.exp(bl), (SUBLANES, LANES)), e)[0:1, :]

    xs = xc_ref[:, 0:SSM_INNER]
    xdt = (xs * dt_x).astype(BF16)
    xw = (xs * w_x).astype(BF16)
    y = xs * dx_ref[...]
    zg = _silu(z_ref[...])
    for g in range(SSM_GROUPS):
        bg = xc_ref[:, SSM_INNER + g * SSM_N:SSM_INNER + (g + 1) * SSM_N]
        cg = xc_ref[:, SSM_INNER + (SSM_GROUPS + g) * SSM_N:SSM_INNER + (SSM_GROUPS + g + 1) * SSM_N].astype(BF16)
        cb = _nt(cg, bg.astype(BF16))
        st_g = st_ref[:, g * gw:(g + 1) * gw]
        yg = jnp.dot(cg, st_g.astype(BF16), preferred_element_type=F32) * eb_x[:, g * gw:(g + 1) * gw]
        parts = []
        for jp in range(SSM_HPG // 2):
            xpair = xdt[:, g * gw + jp * LANES:g * gw + (jp + 1) * LANES]
            acc = None
            for half in range(2):
                h = g * SSM_HPG + jp * 2 + half
                dec = jnp.exp(jnp.where(tri_b, bcum[:, h:h + 1] - bcum_t[h:h + 1, :], -jnp.inf))
                w = (cb * dec).astype(BF16)
                xh = jnp.where(lane_lo if half == 0 else jnp.logical_not(lane_lo), xpair, 0.0).astype(BF16)
                r = jnp.dot(w, xh, preferred_element_type=F32)
                acc = r if acc is None else acc + r
            parts.append(acc)
        yg = yg + jnp.concatenate(parts, axis=1)
        st_ref[:, g * gw:(g + 1) * gw] = (st_g * decay_x[:, g * gw:(g + 1) * gw]
                                          + jnp.dot(bg.T.astype(BF16), xw[:, g * gw:(g + 1) * gw],
                                                    preferred_element_type=F32))
        yg = (yg + y[:, g * gw:(g + 1) * gw]) * zg[:, g * gw:(g + 1) * gw]
        yg = yg * lax.rsqrt(jnp.mean(yg * yg, axis=-1, keepdims=True) + RMS_EPS) * ng_ref[:, g * gw:(g + 1) * gw]
        o_ref[:, g * gw:(g + 1) * gw] = yg.astype(o_ref.dtype)

    @pl.when(ci == nc - 1)
    def _():
        for blk in range(SSM_INNER // LANES):
            so_ref[blk * LANES:(blk + 1) * LANES, :] = st_ref[:, blk * LANES:(blk + 1) * LANES].T


def _ssd(xc, z, dt_raw, expand, dt_bias, a_log, d_x, norm_g, s0, t, c=128):
    nb, tpad, _ = xc.shape
    nc = tpad // c
    return pl.pallas_call(
        functools.partial(_ssd_body, c=c, t=t, nc=nc),
        grid=(nb, nc),
        in_specs=[pl.BlockSpec((None, c, SSM_CH), lambda b, ci: (b, ci, 0)),
                  pl.BlockSpec((None, c, SSM_INNER), lambda b, ci: (b, ci, 0)),
                  pl.BlockSpec((None, c, LANES), lambda b, ci: (b, ci, 0)),
                  pl.BlockSpec((LANES, SSM_INNER), lambda b, ci: (0, 0)),
                  pl.BlockSpec((1, LANES), lambda b, ci: (0, 0)),
                  pl.BlockSpec((1, LANES), lambda b, ci: (0, 0)),
                  pl.BlockSpec((1, SSM_INNER), lambda b, ci: (0, 0)),
                  pl.BlockSpec((1, SSM_INNER), lambda b, ci: (0, 0)),
                  pl.BlockSpec((None, SSM_INNER, SSM_N), lambda b, ci: (b, 0, 0))],
        out_specs=[pl.BlockSpec((None, c, SSM_INNER), lambda b, ci: (b, ci, 0)),
                   pl.BlockSpec((None, SSM_INNER, SSM_N), lambda b, ci: (b, 0, 0))],
        out_shape=[jax.ShapeDtypeStruct((nb, tpad, SSM_INNER), BF16),
                   jax.ShapeDtypeStruct((nb, SSM_INNER, SSM_N), F32)],
        scratch_shapes=[pltpu.VMEM((SSM_N, SSM_INNER), F32)],
        compiler_params=_cp("parallel", "arbitrary"),
        name="ssd",
    )(xc, z, dt_raw, expand, dt_bias, a_log, d_x, norm_g.reshape(1, SSM_INNER), s0)


def _pad_cols(w, n):
    return jnp.pad(w, ((0, 0), (0, n - w.shape[1])))


def _pad_time(a, tpad):
    return jnp.pad(a, ((0, 0), (0, tpad - a.shape[1]), (0, 0)))


def kernel(x_prompt, x_sample, cache_k, cache_v, cache_idx_k, state_hgrn, state_ssm, state_conv, page_table, p_prompt, p_sample, ln_g, ln_b, ffn_w_gate_up, ffn_w_down, ple_w_proj, ple_w_gate, att_w_in, att_idx_k_norm, att_w_o, hg_w_in, hg_lb_logits, hg_norm_g, hg_w_o, ssm_w_in, ssm_conv_w, ssm_conv_b, ssm_dt_bias, ssm_a_log, ssm_d, ssm_norm_g, ssm_w_o):
    nbp, seq, d = x_prompt.shape
    nbs = x_sample.shape[0]
    mp = nbp * seq
    ms = 16
    chunk = 128

    streams = {
        "p": dict(x=x_prompt.reshape(mp, d), m=mp, tm=1024, tml=512),
        "s": dict(x=jnp.pad(x_sample.reshape(nbs, d), ((0, ms - nbs), (0, 0))), m=ms, tm=ms, tml=ms),
    }
    for st in streams.values():
        st["xb"] = st["x"].astype(BF16)
    p_in = {"p": p_prompt.reshape(DEPTH, mp, PLE_DIM),
            "s": jnp.pad(p_sample.reshape(DEPTH, nbs, PLE_DIM), ((0, 0), (0, ms - nbs), (0, 0)))}

    expand = jnp.asarray(np.kron(np.eye(LANES, SSM_HEADS, dtype=np.float32),
                                 np.ones((1, SSM_P), np.float32)), BF16)
    outs = {}

    def ffn_ln(i, which, ln_idx):
        wd = _cast_w(ffn_w_down, (i, which))
        for st in streams.values():
            h = _mm_swiglu(st["xb"], ffn_w_gate_up, (i, which), st["tm"], 512)
            st["x"], st["xb"] = _mm_ln(h, wd, st["x"], ln_g[i, ln_idx], ln_b[i, ln_idx], 0.5, st["tml"], "ffn_down_ln")

    for i in range(DEPTH):
        j = i // N_MIXERS
        ffn_ln(i, 0, 0)

        if i % N_MIXERS == 0:
            w_small = _pad_cols(att_w_in[j][:, ATT_MAIN:], LANES)
            w_o = _cast_w(att_w_o, (j,))
            qw, kvw = ATT_HEADS * ATT_HD, ATT_KV * ATT_HD
            mix = {}
            for name, st in streams.items():
                proj, projb = _proj(st["xb"], att_w_in, (j,), ATT_MAIN, st["tm"], 1024, with_bf16=True, name="att_in")
                proj2 = _proj(st["xb"], w_small, (), LANES, st["tm"], LANES, name="att_in_idx")
                ik, ikb = _ik_norm(proj2, att_idx_k_norm[j], st["tm"])
                if name == "p":
                    o = _dsa_prompt(proj, projb, proj2, ikb, nbp, seq)
                    outs.setdefault("k_p", []).append(proj[:, qw:qw + kvw].reshape(nbp, seq, ATT_KV, ATT_HD))
                    outs.setdefault("v_p", []).append(proj[:, qw + kvw:qw + 2 * kvw].reshape(nbp, seq, ATT_KV, ATT_HD))
                    outs.setdefault("ik_p", []).append(ik.reshape(nbp, seq, IDX_DIM))
                else:
                    pr = proj[:nbs]
                    k_new = pr[:, qw:qw + kvw]
                    v_new = pr[:, qw + kvw:qw + 2 * kvw]
                    ik_new = ik[:nbs]
                    scores = _dsa_s_scores(page_table,
                                           pr[:, qw + 2 * kvw:].reshape(nbs, IDX_HEADS, IDX_DIM),
                                           proj2[:nbs, IDX_DIM:IDX_DIM + IDX_HEADS].reshape(nbs, IDX_HEADS, 1),
                                           ik_new.reshape(nbs, 1, IDX_DIM), cache_idx_k, j)
                    idx, meta = _dsa_s_select(scores, page_table)
                    o = _dsa_s_attend(idx, meta, pr[:, :qw].reshape(nbs, ATT_HEADS, ATT_HD),
                                      k_new.reshape(nbs, 1, kvw), v_new.reshape(nbs, 1, kvw), cache_k, cache_v, j)
                    o = jnp.pad(o.reshape(nbs, qw), ((0, ms - nbs), (0, 0))).astype(BF16)
                    outs.setdefault("k_s", []).append(k_new.reshape(nbs, 1, ATT_KV, ATT_HD))
                    outs.setdefault("v_s", []).append(v_new.reshape(nbs, 1, ATT_KV, ATT_HD))
                    outs.setdefault("ik_s", []).append(ik_new.reshape(nbs, 1, IDX_DIM))
                mix[name] = (o, w_o)
        elif i % N_MIXERS == 1:
            w_o = _cast_w(hg_w_o, (j,))
            mix = {}
            for name, st in streams.items():
                proj = _proj(st["xb"], hg_w_in, (j,), hg_w_in.shape[-1], st["tm"], 1024, name="hg_in")
                if name == "p":
                    s0 = jnp.zeros((nbp, HG_HEADS, HG_DK, HG_DV), F32)
                    o, s_fin = _hgrn(proj.reshape(nbp, seq, -1), hg_lb_logits, hg_norm_g[j], s0, i, seq, chunk)
                    o = o.reshape(mp, -1)
                    outs.setdefault("hg_p", []).append(s_fin)
                else:
                    pr = _pad_time(proj[:nbs].reshape(nbs, 1, -1), chunk)
                    o, s_fin = _hgrn(pr, hg_lb_logits, hg_norm_g[j], state_hgrn[j], i, 1, chunk)
                    o = jnp.pad(o[:, 0, :], ((0, ms - nbs), (0, 0)))
                    outs.setdefault("hg_s", []).append(s_fin)
                mix[name] = (o, w_o)
        else:
            w_small = _pad_cols(ssm_w_in[j][:, SSM_MAIN:], LANES)
            w_o = _cast_w(ssm_w_o, (j,))
            dt_bias = _pad_cols(ssm_dt_bias[j].reshape(1, SSM_HEADS), LANES)
            a_log = _pad_cols(ssm_a_log[j].reshape(1, SSM_HEADS), LANES)
            d_x = jnp.repeat(ssm_d[j], SSM_P).reshape(1, SSM_INNER)
            mix = {}
            for name, st in streams.items():
                zx = _proj(st["xb"], ssm_w_in, (j,), SSM_MAIN, st["tm"], 1024, name="ssm_in")
                dtr = _proj(st["xb"], w_small, (), LANES, st["tm"], LANES, name="ssm_in_dt")
                if name == "p":
                    nb_, t_, c_conv = nbp, seq, 4 * chunk
                    zx3 = zx.reshape(nbp, seq, -1)
                    dt3 = dtr.reshape(nbp, seq, LANES)
                    cs = jnp.zeros((nbp, SUBLANES, SSM_CH), F32)
                    s0 = jnp.zeros((nbp, SSM_INNER, SSM_N), F32)
                else:
                    nb_, t_, c_conv = nbs, 1, SUBLANES
                    zx3 = _pad_time(zx[:nbs].reshape(nbs, 1, -1), SUBLANES)
                    dt3 = _pad_time(dtr[:nbs].reshape(nbs, 1, LANES), chunk)
                    cs = jnp.pad(state_conv[j], ((0, 0), (SUBLANES - (SSM_CONV - 1), 0), (0, 0)))
                    s0 = state_ssm[j].reshape(nbs, SSM_INNER, SSM_N)
                xc, new_conv = _ssd_conv(zx3, cs, ssm_conv_w[j], ssm_conv_b[j], t_, c_conv, cw=1024)
                if name == "p":
                    z3 = zx3
                else:
                    xc = _pad_time(xc, chunk)
                    z3 = _pad_time(zx3[:, :, :SSM_INNER], chunk)
                y, s_fin = _ssd(xc, z3, dt3, expand, dt_bias, a_log, d_x, ssm_norm_g[j], s0, t_, chunk)
                s_fin = s_fin.reshape(nb_, SSM_HEADS, SSM_P, SSM_N)
                new_conv = new_conv[:, :SSM_CONV - 1, :]
                if name == "p":
                    o = y.reshape(mp, SSM_INNER)
                    outs.setdefault("ssm_p", []).append(s_fin)
                    outs.setdefault("conv_p", []).append(new_conv)
                else:
                    o = jnp.pad(y[:, 0, :], ((0, ms - nbs), (0, 0)))
                    outs.setdefault("ssm_s", []).append(s_fin)
                    outs.setdefault("conv_s", []).append(new_conv)
                mix[name] = (o, w_o)

        for name, st in streams.items():
            o, w_o = mix[name]
            st["x"], st["xb"] = _mm_ln(o, w_o, st["x"], ln_g[i, 1], ln_b[i, 1], 1.0, st["tml"], "mixer_out_ln")
        ffn_ln(i, 1, 2)
        for name, st in streams.items():
            st["x"], st["xb"] = _mm_ple(st["x"], st["xb"], p_in[name][i].astype(BF16), ple_w_gate, ple_w_proj, (i,),
                                        st["tm"], 512)

    y_prompt = streams["p"]["x"].reshape(nbp, seq, d)
    y_sample = streams["s"]["x"][:nbs].reshape(nbs, 1, d)
    stack = lambda key: jnp.stack(outs[key])
    return (y_prompt, y_sample, stack("k_p"), stack("v_p"), stack("ik_p"), stack("k_s"), stack("v_s"), stack("ik_s"),
            stack("hg_p"), stack("hg_s"), stack("ssm_p"), stack("ssm_s"), stack("conv_p"), stack("conv_s"))
```

```python
import functools
import math

import jax
import jax.numpy as jnp
import numpy as np
from jax import lax
from jax.experimental import pallas as pl
from jax.experimental.pallas import tpu as pltpu

F32 = jnp.float32
BF16 = jnp.bfloat16

D_MODEL = 2048
DEPTH = 4
N_MIXERS = 3
D_FF = 2 * D_MODEL
PLE_DIM = 256
ALPHA = (2 * DEPTH) ** 0.25
LN_EPS = 1e-5
RMS_EPS = 1e-6
PAGE = 128

ATT_HD = 128
ATT_HEADS = 16
ATT_KV = 4
ATT_GROUP = ATT_HEADS // ATT_KV
IDX_HEADS = 16
IDX_DIM = 64
IDX_W_SCALE = (IDX_HEADS ** -0.5) * (IDX_DIM ** -0.5)
TOPK_MAX = 256
ATT_SCALE = ATT_HD ** -0.5
LOG2E = math.log2(math.e)
ATT_MAIN = ATT_HEADS * ATT_HD + 2 * ATT_KV * ATT_HD + IDX_HEADS * IDX_DIM

HG_HEADS = 16
HG_DK = 128
HG_DV = 128

SSM_INNER = 2 * D_MODEL
SSM_P = 64
SSM_HEADS = SSM_INNER // SSM_P
SSM_GROUPS = 8
SSM_HPG = SSM_HEADS // SSM_GROUPS
SSM_N = 128
SSM_CONV = 4
SSM_CH = SSM_INNER + 2 * SSM_GROUPS * SSM_N
SSM_MAIN = SSM_INNER + SSM_CH

LANES = 128
SUBLANES = 8
VMEM_LIMIT_BYTES = 56 * 1024 * 1024

INT_MIN = np.int32(-2 ** 31)
NEG_BIG = -1e30


def _cp(*sem):
    return pltpu.CompilerParams(dimension_semantics=sem, vmem_limit_bytes=VMEM_LIMIT_BYTES)


def _nt(a, b):
    return lax.dot_general(a, b, (((1,), (1,)), ((), ())), preferred_element_type=F32)


def _tn(a, b):
    return lax.dot_general(a, b, (((0,), (0,)), ((), ())), preferred_element_type=F32)


def _silu(x):
    return x * jax.nn.sigmoid(x)


def _wspec(k, tn, idx, col_block):
    lead = (None,) * len(idx)
    return pl.BlockSpec(lead + (k, tn), lambda n, i: idx + (0, col_block(n)))


def _proj_body(x_ref, xs_ref, w_ref, *rest, with_bf16):
    wb_ref = rest[-1]
    outs, outs_s = (rest[0:2], rest[2:4]) if with_bf16 else (rest[0:1], rest[1:2])

    def emit(refs, acc):
        refs[0][...] = acc
        if with_bf16:
            refs[1][...] = acc.astype(BF16)

    @pl.when(pl.program_id(1) == 0)
    def _():
        wb_ref[...] = w_ref[...].astype(BF16)
        emit(outs_s, jnp.dot(xs_ref[...], wb_ref[...], preferred_element_type=F32))

    emit(outs, jnp.dot(x_ref[...], wb_ref[...], preferred_element_type=F32))


def _proj(xb, xsb, w, idx, n_cols, tm, tn, with_bf16=False, name="proj"):
    m, k = xb.shape
    ms = xsb.shape[0]
    dts = (F32, BF16) if with_bf16 else (F32,)
    res = pl.pallas_call(
        functools.partial(_proj_body, with_bf16=with_bf16),
        grid=(n_cols // tn, m // tm),
        in_specs=[pl.BlockSpec((tm, k), lambda n, i: (i, 0)),
                  pl.BlockSpec((ms, k), lambda n, i: (0, 0)),
                  _wspec(k, tn, idx, lambda n: n)],
        out_specs=[pl.BlockSpec((tm, tn), lambda n, i: (i, n)) for _ in dts]
                  + [pl.BlockSpec((ms, tn), lambda n, i: (0, n)) for _ in dts],
        out_shape=[jax.ShapeDtypeStruct((m, n_cols), dt) for dt in dts]
                  + [jax.ShapeDtypeStruct((ms, n_cols), dt) for dt in dts],
        scratch_shapes=[pltpu.VMEM((k, tn), BF16)],
        compiler_params=_cp("parallel", "arbitrary"),
        name=name,
    )(xb, xsb, w)
    nd = len(dts)
    return (res[:nd], res[nd:]) if with_bf16 else (res[0], res[1])


def _swiglu_body(x_ref, xs_ref, wg_ref, wu_ref, o_ref, os_ref, wgb_ref, wub_ref):
    def swiglu(x):
        g = jnp.dot(x, wgb_ref[...], preferred_element_type=F32)
        u = jnp.dot(x, wub_ref[...], preferred_element_type=F32)
        return (_silu(g) * u).astype(BF16)

    @pl.when(pl.program_id(1) == 0)
    def _():
        wgb_ref[...] = wg_ref[...].astype(BF16)
        wub_ref[...] = wu_ref[...].astype(BF16)
        os_ref[...] = swiglu(xs_ref[...])

    o_ref[...] = swiglu(x_ref[...])


def _mm_swiglu(xb, xsb, wgu, idx, tm, tn):
    m, k = xb.shape
    ms = xsb.shape[0]
    f = wgu.shape[-1] // 2
    nj = f // tn
    return pl.pallas_call(
        _swiglu_body,
        grid=(nj, m // tm),
        in_specs=[pl.BlockSpec((tm, k), lambda n, i: (i, 0)),
                  pl.BlockSpec((ms, k), lambda n, i: (0, 0)),
                  _wspec(k, tn, idx, lambda n: n),
                  _wspec(k, tn, idx, lambda n: n + nj)],
        out_specs=[pl.BlockSpec((tm, tn), lambda n, i: (i, n)),
                   pl.BlockSpec((ms, tn), lambda n, i: (0, n))],
        out_shape=[jax.ShapeDtypeStruct((m, f), BF16), jax.ShapeDtypeStruct((ms, f), BF16)],
        scratch_shapes=[pltpu.VMEM((k, tn), BF16), pltpu.VMEM((k, tn), BF16)],
        compiler_params=_cp("parallel", "arbitrary"),
        name="ffn_up",
    )(xb, xsb, wgu, wgu)


def _cast_body(w_ref, o_ref):
    o_ref[...] = w_ref[...].astype(BF16)


def _cast_w(w, idx, tk=512):
    k, n = w.shape[-2:]
    lead = (None,) * len(idx)
    return pl.pallas_call(
        _cast_body,
        grid=(k // tk,),
        in_specs=[pl.BlockSpec(lead + (tk, n), lambda i: idx + (i, 0))],
        out_specs=pl.BlockSpec((tk, n), lambda i: (i, 0)),
        out_shape=jax.ShapeDtypeStruct((k, n), BF16),
        compiler_params=_cp("parallel"),
        name="cast_w",
    )(w)


LN_SPLIT = 2


def _mm_ln_body(a_ref, as_ref, w_ref, r_ref, rs_ref, g_ref, b_ref, o_ref, ob_ref, os_ref, osb_ref, *, scale, tm):
    g = g_ref[...]
    b = b_ref[...]

    def ln_rows(a, r, o, ob, sl):
        y = ALPHA * r[sl, :] + scale * jnp.dot(a[sl, :], w_ref[...], preferred_element_type=F32)
        mu = jnp.mean(y, axis=-1, keepdims=True)
        yc = y - mu
        var = jnp.mean(yc * yc, axis=-1, keepdims=True)
        out = yc * lax.rsqrt(var + LN_EPS) * g + b
        o[sl, :] = out
        ob[sl, :] = out.astype(BF16)

    @pl.when(pl.program_id(0) == 0)
    def _():
        ln_rows(as_ref, rs_ref, os_ref, osb_ref, pl.ds(0, as_ref.shape[0]))

    rows = tm // LN_SPLIT
    for r in range(0, tm, rows):
        ln_rows(a_ref, r_ref, o_ref, ob_ref, pl.ds(r, rows))


def _mm_ln(ab, asb, wb, res, res_s, g, b, scale, tm, name):
    m, kdim = ab.shape
    ms = asb.shape[0]
    n = wb.shape[1]
    res = pl.pallas_call(
        functools.partial(_mm_ln_body, scale=scale, tm=tm),
        grid=(m // tm,),
        in_specs=[pl.BlockSpec((tm, kdim), lambda i: (i, 0)),
                  pl.BlockSpec((ms, kdim), lambda i: (0, 0)),
                  pl.BlockSpec((kdim, n), lambda i: (0, 0), pipeline_mode=pl.Buffered(1)),
                  pl.BlockSpec((tm, n), lambda i: (i, 0)),
                  pl.BlockSpec((ms, n), lambda i: (0, 0)),
                  pl.BlockSpec((1, n), lambda i: (0, 0)),
                  pl.BlockSpec((1, n), lambda i: (0, 0))],
        out_specs=[pl.BlockSpec((tm, n), lambda i: (i, 0)),
                   pl.BlockSpec((tm, n), lambda i: (i, 0)),
                   pl.BlockSpec((ms, n), lambda i: (0, 0)),
                   pl.BlockSpec((ms, n), lambda i: (0, 0))],
        out_shape=[jax.ShapeDtypeStruct((m, n), F32), jax.ShapeDtypeStruct((m, n), BF16),
                   jax.ShapeDtypeStruct((ms, n), F32), jax.ShapeDtypeStruct((ms, n), BF16)],
        compiler_params=_cp("arbitrary"),
        name=name,
    )(ab, asb, wb, res, res_s, g.reshape(1, n), b.reshape(1, n))
    return res[:2], res[2:]


def _ple_body(xb_ref, p_ref, x_ref, xsb_ref, ps_ref, xs_ref, wg_ref, wp_ref, o_ref, ob_ref, os_ref, osb_ref,
              wgb_ref, wpb_ref):
    def ple(xb, p, x, o, ob):
        gate = jax.nn.sigmoid(jnp.dot(xb[...], wgb_ref[...], preferred_element_type=F32))
        proj = jnp.dot(p[...], wpb_ref[...], preferred_element_type=F32)
        out = x[...] + gate * proj
        o[...] = out
        ob[...] = out.astype(BF16)

    @pl.when(pl.program_id(1) == 0)
    def _():
        wgb_ref[...] = wg_ref[...].astype(BF16)
        wpb_ref[...] = wp_ref[...].astype(BF16)
        ple(xsb_ref, ps_ref, xs_ref, os_ref, osb_ref)

    ple(xb_ref, p_ref, x_ref, o_ref, ob_ref)


def _mm_ple(x32, xb, pb, xs32, xsb, psb, wg, wp, idx, tm, tn):
    m, d = xb.shape
    ms = xsb.shape[0]
    pd = pb.shape[1]
    res = pl.pallas_call(
        _ple_body,
        grid=(d // tn, m // tm),
        in_specs=[pl.BlockSpec((tm, d), lambda n, i: (i, 0)),
                  pl.BlockSpec((tm, pd), lambda n, i: (i, 0)),
                  pl.BlockSpec((tm, tn), lambda n, i: (i, n)),
                  pl.BlockSpec((ms, d), lambda n, i: (0, 0)),
                  pl.BlockSpec((ms, pd), lambda n, i: (0, 0)),
                  pl.BlockSpec((ms, tn), lambda n, i: (0, n)),
                  _wspec(d, tn, idx, lambda n: n),
                  _wspec(pd, tn, idx, lambda n: n)],
        out_specs=[pl.BlockSpec((tm, tn), lambda n, i: (i, n)),
                   pl.BlockSpec((tm, tn), lambda n, i: (i, n)),
                   pl.BlockSpec((ms, tn), lambda n, i: (0, n)),
                   pl.BlockSpec((ms, tn), lambda n, i: (0, n))],
        out_shape=[jax.ShapeDtypeStruct((m, d), F32), jax.ShapeDtypeStruct((m, d), BF16),
                   jax.ShapeDtypeStruct((ms, d), F32), jax.ShapeDtypeStruct((ms, d), BF16)],
        scratch_shapes=[pltpu.VMEM((d, tn), BF16), pltpu.VMEM((pd, tn), BF16)],
        compiler_params=_cp("parallel", "arbitrary"),
        name="ple",
    )(xb, pb, x32, xsb, psb, xs32, wg, wp)
    return res[:2], res[2:]


def _ik_norm_body(p_ref, g_ref, o_ref, ob_ref):
    x = p_ref[...][:, :IDX_DIM]
    mu = jnp.mean(x, axis=-1, keepdims=True)
    xc = x - mu
    out = xc * lax.rsqrt(jnp.mean(xc * xc, axis=-1, keepdims=True) + LN_EPS) * g_ref[...]
    o_ref[...] = out
    ob_ref[...] = out.astype(BF16)


def _ik_norm(proj2, ik_g, tm):
    m = proj2.shape[0]
    return pl.pallas_call(
        _ik_norm_body,
        grid=(m // tm,),
        in_specs=[pl.BlockSpec((tm, LANES), lambda i: (i, 0)),
                  pl.BlockSpec((1, IDX_DIM), lambda i: (0, 0))],
        out_specs=[pl.BlockSpec((tm, IDX_DIM), lambda i: (i, 0)),
                   pl.BlockSpec((tm, IDX_DIM), lambda i: (i, 0))],
        out_shape=[jax.ShapeDtypeStruct((m, IDX_DIM), F32), jax.ShapeDtypeStruct((m, IDX_DIM), BF16)],
        compiler_params=_cp("parallel"),
        name="idx_k_norm",
    )(proj2, ik_g.reshape(1, IDX_DIM))


def _order_key(x):
    bits = pltpu.bitcast(x, jnp.int32)
    return jnp.where(bits < 0, bits ^ jnp.int32(0x7FFFFFFF), bits)


_RADIX_BITS = [INT_MIN] + [np.int32(1 << s) for s in range(30, -1, -1)]


RADIX_ROWS = 128


def _dsa_prompt_body(q_ref, iq_ref, iw_ref, k_ref, v_ref, ik_ref, o_ref, keys_ref, *, qb, kc, topk):
    i = pl.program_id(1)
    nck = ((i + 1) * qb + kc - 1) // kc
    row_pos = i * qb + lax.broadcasted_iota(jnp.int32, (qb, 1), 0)
    iq = iq_ref[...]
    iw = iw_ref[...][:, IDX_DIM:IDX_DIM + IDX_HEADS] * IDX_W_SCALE
    iq_h = [iq[:, h * IDX_DIM:(h + 1) * IDX_DIM] for h in range(IDX_HEADS)]
    iw_h = [iw[:, h:h + 1] for h in range(IDX_HEADS)]
    col0 = lax.broadcasted_iota(jnp.int32, (1, kc), 1)

    def score_chunk(c, carry):
        off = pl.multiple_of(c * kc, kc)
        ikc = ik_ref[pl.ds(off, kc), :]
        sc = jnp.zeros((qb, kc), F32)
        for h in range(IDX_HEADS):
            sc = sc + iw_h[h] * jnp.maximum(_nt(iq_h[h], ikc), 0.0)
        key = jnp.where(col0 + off <= row_pos, _order_key(sc), INT_MIN)
        keys_ref[:, pl.ds(off, kc)] = key
        return carry

    lax.fori_loop(0, nck, score_chunk, 0)

    nblk = qb // RADIX_ROWS
    assert topk <= 2 * LANES and (kc // LANES) % 2 == 0

    los, nbits = [], jnp.int32(0)
    for blk in range(nblk):
        def bucket_chunk(c, carry, r0=blk * RADIX_ROWS):
            bmax = list(carry)
            off = pl.multiple_of(c * kc, kc)
            for t in range(kc // LANES):
                tile = keys_ref[r0:r0 + RADIX_ROWS, pl.ds(off + t * LANES, LANES)]
                bmax[t % 2] = jnp.maximum(bmax[t % 2], tile)
            return tuple(bmax)

        empty = jnp.full((RADIX_ROWS, LANES), INT_MIN, jnp.int32)
        even, odd = lax.fori_loop(0, nck, bucket_chunk, (empty, empty))
        upper = jnp.max(jnp.maximum(even, odd), axis=-1, keepdims=True)
        lower = jnp.min(jnp.minimum(even, odd), axis=-1, keepdims=True)
        open_bits = 32 - lax.clz(lower ^ upper)
        nbits = jnp.maximum(nbits, jnp.max(open_bits.astype(F32)).astype(jnp.int32))
        los.append(upper)
    keep = jnp.where(nbits >= 32, jnp.int32(0), jnp.left_shift(jnp.int32(-1), jnp.minimum(nbits, 31)))
    los = tuple(((u ^ INT_MIN) & keep) ^ INT_MIN for u in los)

    def radix_pass(t, los):
        bit = jnp.left_shift(jnp.int32(1), nbits - 1 - t)
        cands = [lo + bit for lo in los]
        accs = []
        for blk in range(nblk):
            cand_b = jnp.broadcast_to(cands[blk], (RADIX_ROWS, LANES))

            def count_chunk(c, acc, cand_b=cand_b, r0=blk * RADIX_ROWS):
                off = pl.multiple_of(c * kc, kc)
                for tt in range(kc // LANES):
                    tile = keys_ref[r0:r0 + RADIX_ROWS, pl.ds(off + tt * LANES, LANES)]
                    acc = acc + jnp.where(tile >= cand_b, 1.0, 0.0)
                return acc

            accs.append(lax.fori_loop(0, nck, count_chunk, jnp.zeros((RADIX_ROWS, LANES), F32)))
        cnts = [jnp.sum(acc, axis=-1, keepdims=True) for acc in accs]
        return tuple(jnp.where(cnts[blk] >= topk, cands[blk], los[blk]) for blk in range(nblk))

    los = lax.fori_loop(0, nbits, radix_pass, los)
    thr = jnp.maximum(jnp.concatenate(los, axis=0), INT_MIN + 1)

    for g in range(ATT_KV):
        qg = jnp.concatenate(
            [q_ref[:, (g * ATT_GROUP + j) * ATT_HD:(g * ATT_GROUP + j + 1) * ATT_HD] for j in range(ATT_GROUP)], axis=0)
        qg = (qg * (ATT_SCALE * LOG2E)).astype(BF16)

        def att_chunk(c, carry, qg=qg, g=g):
            m, l, acc = carry
            off = pl.multiple_of(c * kc, kc)
            kch = k_ref[pl.ds(off, kc), g * ATT_HD:(g + 1) * ATT_HD]
            vch = v_ref[pl.ds(off, kc), g * ATT_HD:(g + 1) * ATT_HD]
            sel = (keys_ref[:, pl.ds(off, kc)] >= thr)[None]
            s = jnp.where(sel, _nt(qg, kch).reshape(ATT_GROUP, qb, kc), -jnp.inf)
            m_new = jnp.maximum(m, jnp.max(s, axis=-1, keepdims=True))
            p = jnp.exp2(s - m_new)
            a = jnp.exp2(m - m_new)
            l = a * l + jnp.sum(p, axis=-1, keepdims=True)
            pv = jnp.dot(p.reshape(ATT_GROUP * qb, kc).astype(BF16), vch, preferred_element_type=F32)
            acc = a * acc + pv.reshape(ATT_GROUP, qb, ATT_HD)
            return m_new, l, acc

        init = (jnp.full((ATT_GROUP, qb, 1), NEG_BIG, F32), jnp.zeros((ATT_GROUP, qb, 1), F32),
                jnp.zeros((ATT_GROUP, qb, ATT_HD), F32))
        _, l, acc = lax.fori_loop(0, nck, att_chunk, init)
        out = acc / l
        for j in range(ATT_GROUP):
            h = g * ATT_GROUP + j
            o_ref[:, h * ATT_HD:(h + 1) * ATT_HD] = out[j].astype(o_ref.dtype)


def _dsa_prompt(proj, projb, proj2, ikb, nb, s, qb=256, kc=512):
    nq = s // qb
    topk = min(TOPK_MAX, s // 4)
    qw = ATT_HEADS * ATT_HD
    kvw = ATT_KV * ATT_HD
    iqw = IDX_HEADS * IDX_DIM
    return pl.pallas_call(
        functools.partial(_dsa_prompt_body, qb=qb, kc=kc, topk=topk),
        grid=(nb, nq),
        in_specs=[pl.BlockSpec((qb, qw), lambda b, i: (b * nq + i, 0)),
                  pl.BlockSpec((qb, iqw), lambda b, i: (b * nq + i, (qw + 2 * kvw) // iqw)),
                  pl.BlockSpec((qb, LANES), lambda b, i: (b * nq + i, 0)),
                  pl.BlockSpec((s, kvw), lambda b, i: (b, qw // kvw)),
                  pl.BlockSpec((s, kvw), lambda b, i: (b, qw // kvw + 1)),
                  pl.BlockSpec((s, IDX_DIM), lambda b, i: (b, 0))],
        out_specs=pl.BlockSpec((qb, qw), lambda b, i: (b * nq + i, 0)),
        out_shape=jax.ShapeDtypeStruct((nb * s, qw), BF16),
        scratch_shapes=[pltpu.VMEM((qb, s), jnp.int32)],
        compiler_params=_cp("parallel", "arbitrary"),
        name="dsa_prompt",
    )(proj, projb, proj2, projb, projb, ikb)


SCORE_PAGES = 16


def _dsa_s_scores_body(pt_ref, iq_ref, iw_ref, ikn_ref, *rest, n_pages):
    page_refs, o_ref = rest[:SCORE_PAGES], rest[SCORE_PAGES]
    p = pl.program_id(1)
    iq = iq_ref[...].astype(BF16)
    iw = iw_ref[...] * IDX_W_SCALE

    @pl.when(p == 0)
    def _():
        own = jnp.broadcast_to(ikn_ref[...], (SUBLANES, IDX_DIM)).astype(BF16)
        d = jnp.maximum(_nt(iq, own), 0.0)
        sc = jnp.sum(iw * d, axis=0, keepdims=True)
        o_ref[n_pages:n_pages + SUBLANES, :] = jnp.broadcast_to(sc[:, 0:1], (SUBLANES, LANES))

    for r in range(SCORE_PAGES):
        d = jnp.maximum(_nt(iq, page_refs[r][...].astype(BF16)), 0.0)
        o_ref[pl.ds(p * SCORE_PAGES + r, 1), :] = jnp.sum(iw * d, axis=0, keepdims=True)


def _dsa_s_scores(page_table, iq, iw, ik_new, cache_ik, layer):
    nb, n_pages = page_table.shape
    page_spec = lambda r: pl.BlockSpec((None, None, PAGE, IDX_DIM),
                                       lambda b, p, pt: (layer, pt[b, p * SCORE_PAGES + r], 0, 0))
    return pl.pallas_call(
        functools.partial(_dsa_s_scores_body, n_pages=n_pages),
        grid_spec=pltpu.PrefetchScalarGridSpec(
            num_scalar_prefetch=1,
            grid=(nb, n_pages // SCORE_PAGES),
            in_specs=[pl.BlockSpec((None, IDX_HEADS, IDX_DIM), lambda b, p, pt: (b, 0, 0)),
                      pl.BlockSpec((None, IDX_HEADS, 1), lambda b, p, pt: (b, 0, 0)),
                      pl.BlockSpec((None, 1, IDX_DIM), lambda b, p, pt: (b, 0, 0))]
                     + [page_spec(r) for r in range(SCORE_PAGES)],
            out_specs=pl.BlockSpec((None, n_pages + SUBLANES, LANES), lambda b, p, pt: (b, 0, 0)),
        ),
        out_shape=jax.ShapeDtypeStruct((nb, n_pages + SUBLANES, LANES), F32),
        compiler_params=_cp("parallel", "arbitrary"),
        name="dsa_sample_scores",
    )(page_table, iq, iw, ik_new, *([cache_ik] * SCORE_PAGES))


def _dsa_s_select_body(sc_ref, ptc_ref, idx_ref, meta_ref, rank_ref, phys_ref, *, n_pages, topk):
    shape = (n_pages + SUBLANES, LANES)
    rows = lax.broadcasted_iota(jnp.int32, shape, 0)
    cols = lax.broadcasted_iota(jnp.int32, shape, 1)
    live = (rows < n_pages) | ((rows == n_pages) & (cols == 0))
    keys = jnp.where(live, _order_key(sc_ref[...]), INT_MIN)
    lo = jnp.full((1, 1), INT_MIN, jnp.int32)
    for bit in _RADIX_BITS:
        cand = lo + bit
        cnt = jnp.sum(jnp.sum(jnp.where(keys >= cand, 1.0, 0.0), axis=0, keepdims=True), axis=1, keepdims=True)
        lo = jnp.where(cnt >= topk, cand, lo)
    sel = keys >= jnp.maximum(lo, INT_MIN + 1)
    sel_c = sel[:n_pages, :]
    own = jnp.where(sel[n_pages:n_pages + 1, 0:1], 1, 0)

    r_i = lax.broadcasted_iota(jnp.int32, (LANES, LANES), 0)
    c_i = lax.broadcasted_iota(jnp.int32, (LANES, LANES), 1)
    ones_le = jnp.where(r_i <= c_i, 1.0, 0.0).astype(BF16)
    ones_gt = jnp.where(r_i > c_i, 1.0, 0.0).astype(BF16)
    m = jnp.where(sel_c, 1.0, 0.0).astype(BF16)
    within = jnp.dot(m, ones_le, preferred_element_type=F32)
    tot = jnp.broadcast_to(within[:, LANES - 1:LANES], (n_pages, LANES)).astype(BF16)
    before = jnp.dot(ones_gt, tot, preferred_element_type=F32)
    rank_ref[...] = jnp.where(sel_c, (within + before).astype(jnp.int32) - 1, -1)
    phys_ref[...] = (ptc_ref[...] * PAGE + c_i[:n_pages, :]).astype(F32)
    n_sel = (before[n_pages - 1:n_pages, 0:1] + within[n_pages - 1:n_pages, LANES - 1:LANES]).astype(jnp.int32)

    slot = lax.broadcasted_iota(jnp.int32, (topk, 1), 0)

    def place(r, acc):
        return acc + jnp.where(rank_ref[pl.ds(r, 1), :] == slot, phys_ref[pl.ds(r, 1), :], 0.0)

    acc = lax.fori_loop(0, n_pages, place, jnp.zeros((topk, LANES), F32), unroll=8)
    idx_ref[...] = jnp.sum(acc, axis=-1, keepdims=True).astype(jnp.int32)
    mrow = lax.broadcasted_iota(jnp.int32, (SUBLANES, LANES), 0)
    meta_ref[...] = jnp.where(mrow == 0, jnp.minimum(n_sel, topk), own)


def _dsa_s_select(scores, page_table):
    nb, n_pages = page_table.shape
    assert n_pages == LANES
    topk = min(TOPK_MAX, (n_pages * PAGE + 1) // 4)
    idx, meta = pl.pallas_call(
        functools.partial(_dsa_s_select_body, n_pages=n_pages, topk=topk),
        grid=(nb,),
        in_specs=[pl.BlockSpec((None, n_pages + SUBLANES, LANES), lambda b: (b, 0, 0)),
                  pl.BlockSpec((None, n_pages, 1), lambda b: (b, 0, 0))],
        out_specs=[pl.BlockSpec((None, topk, 1), lambda b: (b, 0, 0)),
                   pl.BlockSpec((None, SUBLANES, LANES), lambda b: (b, 0, 0))],
        out_shape=[jax.ShapeDtypeStruct((nb, topk, 1), jnp.int32),
                   jax.ShapeDtypeStruct((nb, SUBLANES, LANES), jnp.int32)],
        scratch_shapes=[pltpu.VMEM((n_pages, LANES), jnp.int32), pltpu.VMEM((n_pages, LANES), F32)],
        compiler_params=_cp("parallel"),
        name="dsa_sample_select",
    )(scores, page_table.reshape(nb, n_pages, 1))
    return idx.reshape(nb, topk), meta[:, 0:2, 0]


def _dsa_s_attend_body(idx_ref, meta_ref, q_ref, kn_ref, vn_ref, ck_ref, cv_ref, o_ref, kbuf, vbuf, sem,
                       *, layer, topk):
    b = pl.program_id(0)
    nb = pl.num_programs(0)
    slot = b % 2

    def row_copies(tok, buf_slot, j):
        row = idx_ref[tok, j]
        page, off = row // PAGE, row % PAGE
        dst = pl.ds(j * ATT_KV, ATT_KV)
        return (pltpu.make_async_copy(ck_ref.at[layer, page, off], kbuf.at[buf_slot, dst, :], sem.at[buf_slot, 0]),
                pltpu.make_async_copy(cv_ref.at[layer, page, off], vbuf.at[buf_slot, dst, :], sem.at[buf_slot, 1]))

    def start_gather(tok, buf_slot):
        def body(j, carry):
            for cp in row_copies(tok, buf_slot, j):
                cp.start()
            return carry
        lax.fori_loop(0, topk, body, 0)

    @pl.when(b == 0)
    def _():
        start_gather(0, 0)

    @pl.when(b + 1 < nb)
    def _():
        start_gather(b + 1, 1 - slot)

    def wait_body(j, carry):
        for cp in row_copies(b, slot, j):
            cp.wait()
        return carry

    lax.fori_loop(0, topk, wait_body, 0)

    n_sel = meta_ref[b, 0]
    own = meta_ref[b, 1] > 0
    qb = q_ref[...].astype(BF16)
    ncol = topk * ATT_KV
    col = lax.broadcasted_iota(jnp.int32, (ATT_HEADS, ncol), 1)
    head = lax.broadcasted_iota(jnp.int32, (ATT_HEADS, ncol), 0)
    valid = (col % ATT_KV == head // ATT_GROUP) & (col // ATT_KV < n_sel)
    s = jnp.where(valid, _nt(qb, kbuf[slot].astype(BF16)) * ATT_SCALE, NEG_BIG)
    kn = kn_ref[...].astype(BF16).astype(F32)
    vn = vn_ref[...].astype(BF16).astype(F32)
    expand = lambda a: jnp.concatenate(
        [jnp.broadcast_to(a[:, g * ATT_HD:(g + 1) * ATT_HD], (ATT_GROUP, ATT_HD)) for g in range(ATT_KV)], axis=0)
    s_own = jnp.where(own, jnp.sum(qb.astype(F32) * expand(kn), axis=-1, keepdims=True) * ATT_SCALE, NEG_BIG)
    m = jnp.maximum(jnp.max(s, axis=-1, keepdims=True), s_own)
    p = jnp.where(valid, jnp.exp(s - m), 0.0)
    p_own = jnp.where(own, jnp.exp(s_own - m), 0.0)
    l = jnp.sum(p, axis=-1, keepdims=True) + p_own
    acc = (jnp.dot(p.astype(BF16), vbuf[slot].astype(BF16), preferred_element_type=F32)
           + p_own.astype(BF16).astype(F32) * expand(vn))
    o_ref[...] = acc / l


def _dsa_s_attend(idx, meta, q, k_new, v_new, cache_k, cache_v, layer):
    nb, topk = idx.shape
    kvw = ATT_KV * ATT_HD
    return pl.pallas_call(
        functools.partial(_dsa_s_attend_body, layer=layer, topk=topk),
        grid_spec=pltpu.PrefetchScalarGridSpec(
            num_scalar_prefetch=2,
            grid=(nb,),
            in_specs=[pl.BlockSpec((None, ATT_HEADS, ATT_HD), lambda b, idx, meta: (b, 0, 0)),
                      pl.BlockSpec((None, 1, kvw), lambda b, idx, meta: (b, 0, 0)),
                      pl.BlockSpec((None, 1, kvw), lambda b, idx, meta: (b, 0, 0)),
                      pl.BlockSpec(memory_space=pl.ANY),
                      pl.BlockSpec(memory_space=pl.ANY)],
            out_specs=pl.BlockSpec((None, ATT_HEADS, ATT_HD), lambda b, idx, meta: (b, 0, 0)),
            scratch_shapes=[pltpu.VMEM((2, topk * ATT_KV, ATT_HD), F32),
                            pltpu.VMEM((2, topk * ATT_KV, ATT_HD), F32),
                            pltpu.SemaphoreType.DMA((2, 2))],
        ),
        out_shape=jax.ShapeDtypeStruct((nb, ATT_HEADS, ATT_HD), F32),
        compiler_params=_cp("arbitrary"),
        name="dsa_sample_attend",
    )(idx, meta, q, k_new, v_new, cache_k, cache_v)


HG_SUB = 16


HG_HB = 4
HG_EXP_LIMIT = 80.0


def _hgrn_body(lbl_ref, q_ref, f_ref, i_ref, g_ref, ng_ref, s0_ref, o_ref, so_ref, st_ref, *, layer, c, t, nc):
    ci = pl.program_id(2)
    mid = c // 2

    @pl.when(ci == 0)
    def _():
        for hh in range(HG_HB):
            st_ref[hh] = s0_ref[hh].T

    logits = lbl_ref[...]
    e = jnp.exp(logits - jnp.max(logits, axis=0, keepdims=True))
    soft = e / jnp.sum(e, axis=0, keepdims=True)
    lb_all = jnp.zeros((1, HG_HB * HG_DK), F32)
    for r in range(1, layer + 1):
        lb_all = lb_all + soft[r:r + 1, :]
    ng = ng_ref[...]

    def gates(hh, sl, row0):
        cs = slice(hh * HG_DK, (hh + 1) * HG_DK)
        lb = lb_all[:, cs]
        fg = lb + (1.0 - lb) * jax.nn.sigmoid(f_ref[sl, cs])
        lf = jnp.log(fg)
        kk = 1.0 - fg
        if t % c:
            n = lf.shape[0]
            valid = (ci * c + row0 + lax.broadcasted_iota(jnp.int32, (n, 1), 0)) < t
            lf = jnp.where(valid, lf, 0.0)
            kk = jnp.where(valid, kk, 0.0)
        return lf, kk

    def finish(hh, sl, o):
        cs = slice(hh * HG_DV, (hh + 1) * HG_DV)
        on = o * lax.rsqrt(jnp.mean(o * o, axis=-1, keepdims=True) + RMS_EPS) * ng
        o_ref[sl, cs] = (on * _silu(g_ref[sl, cs])).astype(o_ref.dtype)

    tri_c = lax.broadcasted_iota(jnp.int32, (c, c), 0) >= lax.broadcasted_iota(jnp.int32, (c, c), 1)
    full = pl.ds(0, c)
    lfs, kks, bs = [], [], []
    safe = None
    for hh in range(HG_HB):
        lf, kk = gates(hh, full, 0)
        b = jnp.dot(tri_c.astype(F32), lf, precision=lax.Precision.HIGHEST, preferred_element_type=F32)
        bm = b[mid - 1:mid, :]
        ok = jnp.min(jnp.minimum(bm, b[c - 1:c, :] - bm)) > -HG_EXP_LIMIT
        safe = ok if safe is None else jnp.logical_and(safe, ok)
        lfs.append(lf)
        kks.append(kk)
        bs.append(b)

    @pl.when(safe)
    def _():
        for hh in range(HG_HB):
            cs = slice(hh * HG_DK, (hh + 1) * HG_DK)
            b, kk = bs[hh], kks[hh]
            bm = b[mid - 1:mid, :]
            bl = b[c - 1:c, :]
            qq = _silu(q_ref[:, cs])
            vv = i_ref[:, cs].astype(BF16)
            att = jnp.where(tri_c, _nt((qq * jnp.exp(b - bm)).astype(BF16), (kk * jnp.exp(bm - b)).astype(BF16)), 0.0)
            st = st_ref[hh]
            o = (jnp.dot(att.astype(BF16), vv, preferred_element_type=F32)
                 + _nt((qq * jnp.exp(b)).astype(BF16), st.astype(BF16)))
            st_ref[hh] = st * jnp.exp(bl) + _tn(vv, (kk * jnp.exp(bl - b)).astype(BF16))
            finish(hh, full, o)

    @pl.when(jnp.logical_not(safe))
    def _():
        rows = lax.broadcasted_iota(jnp.int32, (HG_SUB, 1), 0)
        tri = (lax.broadcasted_iota(jnp.int32, (HG_SUB, HG_SUB), 0)
               >= lax.broadcasted_iota(jnp.int32, (HG_SUB, HG_SUB), 1)).astype(F32)
        for hh in range(HG_HB):
            cs = slice(hh * HG_DK, (hh + 1) * HG_DK)

            def sub_block(sb, carry, hh=hh, cs=cs):
                row0 = pl.multiple_of(sb * HG_SUB, HG_SUB)
                sl = pl.ds(row0, HG_SUB)
                lf, kk = gates(hh, sl, row0)
                qq = _silu(q_ref[sl, cs])
                vv = i_ref[sl, cs]
                b = jnp.dot(tri, lf, precision=lax.Precision.HIGHEST, preferred_element_type=F32)
                st = st_ref[hh]
                o = _nt((qq * jnp.exp(b)).astype(BF16), st.astype(BF16))
                for s in range(HG_SUB):
                    dec = jnp.exp(jnp.where(rows >= s, b - b[s:s + 1, :], -jnp.inf))
                    att = jnp.sum(qq * dec * kk[s:s + 1, :], axis=-1, keepdims=True)
                    o = o + att * vv[s:s + 1, :]
                bl = b[HG_SUB - 1:HG_SUB, :]
                st_ref[hh] = st * jnp.exp(bl) + _tn(vv.astype(BF16), (kk * jnp.exp(bl - b)).astype(BF16))
                finish(hh, sl, o)
                return carry

            lax.fori_loop(0, c // HG_SUB, sub_block, 0)

    @pl.when(ci == nc - 1)
    def _():
        for hh in range(HG_HB):
            so_ref[hh] = st_ref[hh].T


def _hgrn(proj, lb_logits, norm_g, s0, layer, t, c=128):
    nb, tpad, _ = proj.shape
    nc = tpad // c
    nhb = HG_HEADS // HG_HB
    w = HG_HB * HG_DK
    return pl.pallas_call(
        functools.partial(_hgrn_body, layer=layer, c=c, t=t, nc=nc),
        grid=(nb, nhb, nc),
        in_specs=[pl.BlockSpec((DEPTH, w), lambda b, h, ci: (0, h)),
                  pl.BlockSpec((None, c, w), lambda b, h, ci: (b, ci, h)),
                  pl.BlockSpec((None, c, w), lambda b, h, ci: (b, ci, nhb + h)),
                  pl.BlockSpec((None, c, w), lambda b, h, ci: (b, ci, 2 * nhb + h)),
                  pl.BlockSpec((None, c, w), lambda b, h, ci: (b, ci, 3 * nhb + h)),
                  pl.BlockSpec((1, HG_DV), lambda b, h, ci: (0, 0)),
                  pl.BlockSpec((None, HG_HB, HG_DK, HG_DV), lambda b, h, ci: (b, h, 0, 0))],
        out_specs=[pl.BlockSpec((None, c, w), lambda b, h, ci: (b, ci, h)),
                   pl.BlockSpec((None, HG_HB, HG_DK, HG_DV), lambda b, h, ci: (b, h, 0, 0))],
        out_shape=[jax.ShapeDtypeStruct((nb, tpad, HG_HEADS * HG_DV), BF16),
                   jax.ShapeDtypeStruct((nb, HG_HEADS, HG_DK, HG_DV), F32)],
        scratch_shapes=[pltpu.VMEM((HG_HB, HG_DV, HG_DK), F32)],
        compiler_params=_cp("parallel", "parallel", "arbitrary"),
        name="hgrn2",
    )(lb_logits, proj, proj, proj, proj, norm_g.reshape(1, HG_DV), s0)


def _conv_body(x_ref, halo_ref, cs_ref, w_ref, b_ref, o_ref, nc_ref, *, c, t, nc):
    ti = pl.program_id(2)
    halo = jnp.where(ti == 0, cs_ref[...], halo_ref[...])
    full = jnp.concatenate([halo, x_ref[...]], axis=0)
    w = w_ref[...]
    conv = b_ref[...]
    for k in range(SSM_CONV):
        lo = SUBLANES - (SSM_CONV - 1) + k
        conv = conv + full[lo:lo + c, :] * w[k:k + 1, :]
    o_ref[...] = _silu(conv)

    @pl.when(ti == nc - 1)
    def _():
        tv = t - (nc - 1) * c
        tail = full[SUBLANES + tv - (SSM_CONV - 1):SUBLANES + tv, :]
        nc_ref[...] = jnp.concatenate([tail, jnp.zeros((SUBLANES - (SSM_CONV - 1), tail.shape[1]), F32)], axis=0)


def _ssd_conv(zx, cs_pad, conv_w, conv_b, t, c, cw=512):
    nb, tpad, _ = zx.shape
    nc = tpad // c
    ncol = SSM_CH // cw
    col0 = SSM_INNER // cw
    hb = c // SUBLANES
    return pl.pallas_call(
        functools.partial(_conv_body, c=c, t=t, nc=nc),
        grid=(nb, ncol, nc),
        in_specs=[pl.BlockSpec((None, c, cw), lambda b, j, ti: (b, ti, col0 + j)),
                  pl.BlockSpec((None, SUBLANES, cw), lambda b, j, ti: (b, jnp.maximum(ti * hb - 1, 0), col0 + j)),
                  pl.BlockSpec((None, SUBLANES, cw), lambda b, j, ti: (b, 0, j)),
                  pl.BlockSpec((SSM_CONV, cw), lambda b, j, ti: (0, j)),
                  pl.BlockSpec((1, cw), lambda b, j, ti: (0, j))],
        out_specs=[pl.BlockSpec((None, c, cw), lambda b, j, ti: (b, ti, j)),
                   pl.BlockSpec((None, SUBLANES, cw), lambda b, j, ti: (b, 0, j))],
        out_shape=[jax.ShapeDtypeStruct((nb, tpad, SSM_CH), F32),
                   jax.ShapeDtypeStruct((nb, SUBLANES, SSM_CH), F32)],
        compiler_params=_cp("parallel", "parallel", "arbitrary"),
        name="ssd_conv",
    )(zx, zx, cs_pad, conv_w, conv_b.reshape(1, SSM_CH))


def _expand_heads(v, e):
    hi = v.astype(BF16)
    r1 = v - hi.astype(F32)
    mid = r1.astype(BF16)
    lo = (r1 - mid.astype(F32)).astype(BF16)
    return (jnp.dot(hi, e, preferred_element_type=F32) + jnp.dot(mid, e, preferred_element_type=F32)
            + jnp.dot(lo, e, preferred_element_type=F32))


def _ssd_body(xc_ref, z_ref, dt_ref, e_ref, dtb_ref, alog_ref, dx_ref, ng_ref, s0_ref, o_ref, so_ref, st_ref,
              *, c, t, nc):
    ci = pl.program_id(1)
    gw = SSM_HPG * SSM_P

    @pl.when(ci == 0)
    def _():
        for blk in range(SSM_INNER // LANES):
            st_ref[:, blk * LANES:(blk + 1) * LANES] = s0_ref[blk * LANES:(blk + 1) * LANES, :].T

    e = e_ref[...]
    rows = lax.broadcasted_iota(jnp.int32, (c, 1), 0)
    tri_b = lax.broadcasted_iota(jnp.int32, (c, c), 0) >= lax.broadcasted_iota(jnp.int32, (c, c), 1)
    lane_lo = lax.broadcasted_iota(jnp.int32, (1, LANES), 1) < SSM_P

    dt = jax.nn.softplus(dt_ref[...] + dtb_ref[...])
    if t % c:
        dt = jnp.where(ci * c + rows < t, dt, 0.0)
    da = dt * (-jnp.exp(alog_ref[...]))
    bcum = jnp.dot(tri_b.astype(F32), da, precision=lax.Precision.HIGHEST, preferred_element_type=F32)
    bcum_t = bcum.T
    bl = bcum[c - 1:c, :]
    dt_x = _expand_heads(dt, e)
    eb_x = _expand_heads(jnp.exp(bcum), e)
    w_x = _expand_heads(jnp.exp(bl - bcum) * dt, e)
    decay_x = _expand_heads(jnp.broadcast_to(jnp.exp(bl), (SUBLANES, LANES)), e)[0:1, :]

    xs = xc_ref[:, 0:SSM_INNER]
    xdt = (xs * dt_x).astype(BF16)
    xw = (xs * w_x).astype(BF16)
    y = xs * dx_ref[...]
    zg = _silu(z_ref[...])
    for g in range(SSM_GROUPS):
        bg = xc_ref[:, SSM_INNER + g * SSM_N:SSM_INNER + (g + 1) * SSM_N]
        cg = xc_ref[:, SSM_INNER + (SSM_GROUPS + g) * SSM_N:SSM_INNER + (SSM_GROUPS + g + 1) * SSM_N].astype(BF16)
        cb = _nt(cg, bg.astype(BF16))
        st_g = st_ref[:, g * gw:(g + 1) * gw]
        yg = jnp.dot(cg, st_g.astype(BF16), preferred_element_type=F32) * eb_x[:, g * gw:(g + 1) * gw]
        parts = []
        for jp in range(SSM_HPG // 2):
            xpair = xdt[:, g * gw + jp * LANES:g * gw + (jp + 1) * LANES]
            acc = None
            for half in range(2):
                h = g * SSM_HPG + jp * 2 + half
                dec = jnp.exp(jnp.where(tri_b, bcum[:, h:h + 1] - bcum_t[h:h + 1, :], -jnp.inf))
                w = (cb * dec).astype(BF16)
                xh = jnp.where(lane_lo if half == 0 else jnp.logical_not(lane_lo), xpair, 0.0).astype(BF16)
                r = jnp.dot(w, xh, preferred_element_type=F32)
                acc = r if acc is None else acc + r
            parts.append(acc)
        yg = yg + jnp.concatenate(parts, axis=1)
        st_ref[:, g * gw:(g + 1) * gw] = (st_g * decay_x[:, g * gw:(g + 1) * gw]
                                          + jnp.dot(bg.T.astype(BF16), xw[:, g * gw:(g + 1) * gw],
                                                    preferred_element_type=F32))
        yg = (yg + y[:, g * gw:(g + 1) * gw]) * zg[:, g * gw:(g + 1) * gw]
        yg = yg * lax.rsqrt(jnp.mean(yg * yg, axis=-1, keepdims=True) + RMS_EPS) * ng_ref[:, g * gw:(g + 1) * gw]
        o_ref[:, g * gw:(g + 1) * gw] = yg.astype(o_ref.dtype)

    @pl.when(ci == nc - 1)
    def _():
        for blk in range(SSM_INNER // LANES):
            so_ref[blk * LANES:(blk + 1) * LANES, :] = st_ref[:, blk * LANES:(blk + 1) * LANES].T


def _ssd(xc, z, dt_raw, expand, dt_bias, a_log, d_x, norm_g, s0, t, c=128):
    nb, tpad, _ = xc.shape
    nc = tpad // c
    return pl.pallas_call(
        functools.partial(_ssd_body, c=c, t=t, nc=nc),
        grid=(nb, nc),
        in_specs=[pl.BlockSpec((None, c, SSM_CH), lambda b, ci: (b, ci, 0)),
                  pl.BlockSpec((None, c, SSM_INNER), lambda b, ci: (b, ci, 0)),
                  pl.BlockSpec((None, c, LANES), lambda b, ci: (b, ci, 0)),
                  pl.BlockSpec((LANES, SSM_INNER), lambda b, ci: (0, 0)),
                  pl.BlockSpec((1, LANES), lambda b, ci: (0, 0)),
                  pl.BlockSpec((1, LANES), lambda b, ci: (0, 0)),
                  pl.BlockSpec((1, SSM_INNER), lambda b, ci: (0, 0)),
                  pl.BlockSpec((1, SSM_INNER), lambda b, ci: (0, 0)),
                  pl.BlockSpec((None, SSM_INNER, SSM_N), lambda b, ci: (b, 0, 0))],
        out_specs=[pl.BlockSpec((None, c, SSM_INNER), lambda b, ci: (b, ci, 0)),
                   pl.BlockSpec((None, SSM_INNER, SSM_N), lambda b, ci: (b, 0, 0))],
        out_shape=[jax.ShapeDtypeStruct((nb, tpad, SSM_INNER), BF16),
                   jax.ShapeDtypeStruct((nb, SSM_INNER, SSM_N), F32)],
        scratch_shapes=[pltpu.VMEM((SSM_N, SSM_INNER), F32)],
        compiler_params=_cp("parallel", "arbitrary"),
        name="ssd",
    )(xc, z, dt_raw, expand, dt_bias, a_log, d_x, norm_g.reshape(1, SSM_INNER), s0)


def _pad_cols(w, n):
    return jnp.pad(w, ((0, 0), (0, n - w.shape[1])))


def _pad_time(a, tpad):
    return jnp.pad(a, ((0, 0), (0, tpad - a.shape[1]), (0, 0)))


def kernel(x_prompt, x_sample, cache_k, cache_v, cache_idx_k, state_hgrn, state_ssm, state_conv, page_table, p_prompt, p_sample, ln_g, ln_b, ffn_w_gate_up, ffn_w_down, ple_w_proj, ple_w_gate, att_w_in, att_idx_k_norm, att_w_o, hg_w_in, hg_lb_logits, hg_norm_g, hg_w_o, ssm_w_in, ssm_conv_w, ssm_conv_b, ssm_dt_bias, ssm_a_log, ssm_d, ssm_norm_g, ssm_w_o):
    nbp, seq, d = x_prompt.shape
    nbs = x_sample.shape[0]
    mp = nbp * seq
    ms = 16
    chunk = 128

    tm, tm_ln = 1024, 512
    x_p = x_prompt.reshape(mp, d)
    x_s = jnp.pad(x_sample.reshape(nbs, d), ((0, ms - nbs), (0, 0)))
    xb_p, xb_s = x_p.astype(BF16), x_s.astype(BF16)
    pl_p = p_prompt.reshape(DEPTH, mp, PLE_DIM)
    pl_s = jnp.pad(p_sample.reshape(DEPTH, nbs, PLE_DIM), ((0, 0), (0, ms - nbs), (0, 0)))
    pad_rows = lambda a: jnp.pad(a, ((0, ms - nbs), (0, 0)))

    expand = jnp.asarray(np.kron(np.eye(LANES, SSM_HEADS, dtype=np.float32),
                                 np.ones((1, SSM_P), np.float32)), BF16)
    outs = {}

    def ffn_ln(i, which, ln_idx, x_p, xb_p, x_s, xb_s):
        wd = _cast_w(ffn_w_down, (i, which))
        h_p, h_s = _mm_swiglu(xb_p, xb_s, ffn_w_gate_up, (i, which), tm, 512)
        return _mm_ln(h_p, h_s, wd, x_p, x_s, ln_g[i, ln_idx], ln_b[i, ln_idx], 0.5, tm_ln, "ffn_down_ln")

    for i in range(DEPTH):
        j = i // N_MIXERS
        (x_p, xb_p), (x_s, xb_s) = ffn_ln(i, 0, 0, x_p, xb_p, x_s, xb_s)

        if i % N_MIXERS == 0:
            w_small = _pad_cols(att_w_in[j][:, ATT_MAIN:], LANES)
            w_o = _cast_w(att_w_o, (j,))
            qw, kvw = ATT_HEADS * ATT_HD, ATT_KV * ATT_HD
            (proj, projb), (proj_s, _) = _proj(xb_p, xb_s, att_w_in, (j,), ATT_MAIN, tm, 1024, with_bf16=True,
                                               name="att_in")
            proj2, proj2_s = _proj(xb_p, xb_s, w_small, (), LANES, tm, LANES, name="att_in_idx")
            ik, ikb = _ik_norm(proj2, att_idx_k_norm[j], tm)
            ik_s, _ = _ik_norm(proj2_s, att_idx_k_norm[j], ms)
            o_p = _dsa_prompt(proj, projb, proj2, ikb, nbp, seq)
            outs.setdefault("k_p", []).append(proj[:, qw:qw + kvw].reshape(nbp, seq, ATT_KV, ATT_HD))
            outs.setdefault("v_p", []).append(proj[:, qw + kvw:qw + 2 * kvw].reshape(nbp, seq, ATT_KV, ATT_HD))
            outs.setdefault("ik_p", []).append(ik.reshape(nbp, seq, IDX_DIM))
            pr = proj_s[:nbs]
            k_new = pr[:, qw:qw + kvw]
            v_new = pr[:, qw + kvw:qw + 2 * kvw]
            ik_new = ik_s[:nbs]
            scores = _dsa_s_scores(page_table,
                                   pr[:, qw + 2 * kvw:].reshape(nbs, IDX_HEADS, IDX_DIM),
                                   proj2_s[:nbs, IDX_DIM:IDX_DIM + IDX_HEADS].reshape(nbs, IDX_HEADS, 1),
                                   ik_new.reshape(nbs, 1, IDX_DIM), cache_idx_k, j)
            idx, meta = _dsa_s_select(scores, page_table)
            o_s = _dsa_s_attend(idx, meta, pr[:, :qw].reshape(nbs, ATT_HEADS, ATT_HD),
                                k_new.reshape(nbs, 1, kvw), v_new.reshape(nbs, 1, kvw), cache_k, cache_v, j)
            o_s = pad_rows(o_s.reshape(nbs, qw)).astype(BF16)
            outs.setdefault("k_s", []).append(k_new.reshape(nbs, 1, ATT_KV, ATT_HD))
            outs.setdefault("v_s", []).append(v_new.reshape(nbs, 1, ATT_KV, ATT_HD))
            outs.setdefault("ik_s", []).append(ik_new.reshape(nbs, 1, IDX_DIM))
        elif i % N_MIXERS == 1:
            w_o = _cast_w(hg_w_o, (j,))
            proj, proj_s = _proj(xb_p, xb_s, hg_w_in, (j,), hg_w_in.shape[-1], tm, 1024, name="hg_in")
            s0 = jnp.zeros((nbp, HG_HEADS, HG_DK, HG_DV), F32)
            o_p, s_fin = _hgrn(proj.reshape(nbp, seq, -1), hg_lb_logits, hg_norm_g[j], s0, i, seq, chunk)
            o_p = o_p.reshape(mp, -1)
            outs.setdefault("hg_p", []).append(s_fin)
            pr = _pad_time(proj_s[:nbs].reshape(nbs, 1, -1), chunk)
            o_s, s_fin = _hgrn(pr, hg_lb_logits, hg_norm_g[j], state_hgrn[j], i, 1, chunk)
            o_s = pad_rows(o_s[:, 0, :])
            outs.setdefault("hg_s", []).append(s_fin)
        else:
            w_small = _pad_cols(ssm_w_in[j][:, SSM_MAIN:], LANES)
            w_o = _cast_w(ssm_w_o, (j,))
            dt_bias = _pad_cols(ssm_dt_bias[j].reshape(1, SSM_HEADS), LANES)
            a_log = _pad_cols(ssm_a_log[j].reshape(1, SSM_HEADS), LANES)
            d_x = jnp.repeat(ssm_d[j], SSM_P).reshape(1, SSM_INNER)
            zx_p, zx_s = _proj(xb_p, xb_s, ssm_w_in, (j,), SSM_MAIN, tm, 1024, name="ssm_in")
            dtr_p, dtr_s = _proj(xb_p, xb_s, w_small, (), LANES, tm, LANES, name="ssm_in_dt")
            mix = {}
            for name, zx, dtr in (("p", zx_p, dtr_p), ("s", zx_s, dtr_s)):
                if name == "p":
                    nb_, t_, c_conv = nbp, seq, 4 * chunk
                    zx3 = zx.reshape(nbp, seq, -1)
                    dt3 = dtr.reshape(nbp, seq, LANES)
                    cs = jnp.zeros((nbp, SUBLANES, SSM_CH), F32)
                    s0 = jnp.zeros((nbp, SSM_INNER, SSM_N), F32)
                else:
                    nb_, t_, c_conv = nbs, 1, SUBLANES
                    zx3 = _pad_time(zx[:nbs].reshape(nbs, 1, -1), SUBLANES)
                    dt3 = _pad_time(dtr[:nbs].reshape(nbs, 1, LANES), chunk)
                    cs = jnp.pad(state_conv[j], ((0, 0), (SUBLANES - (SSM_CONV - 1), 0), (0, 0)))
                    s0 = state_ssm[j].reshape(nbs, SSM_INNER, SSM_N)
                xc, new_conv = _ssd_conv(zx3, cs, ssm_conv_w[j], ssm_conv_b[j], t_, c_conv, cw=1024)
                if name == "p":
                    z3 = zx3
                else:
                    xc = _pad_time(xc, chunk)
                    z3 = _pad_time(zx3[:, :, :SSM_INNER], chunk)
                y, s_fin = _ssd(xc, z3, dt3, expand, dt_bias, a_log, d_x, ssm_norm_g[j], s0, t_, chunk)
                s_fin = s_fin.reshape(nb_, SSM_HEADS, SSM_P, SSM_N)
                new_conv = new_conv[:, :SSM_CONV - 1, :]
                if name == "p":
                    mix[name] = y.reshape(mp, SSM_INNER)
                    outs.setdefault("ssm_p", []).append(s_fin)
                    outs.setdefault("conv_p", []).append(new_conv)
                else:
                    mix[name] = pad_rows(y[:, 0, :])
                    outs.setdefault("ssm_s", []).append(s_fin)
                    outs.setdefault("conv_s", []).append(new_conv)
            o_p, o_s = mix["p"], mix["s"]

        (x_p, xb_p), (x_s, xb_s) = _mm_ln(o_p, o_s, w_o, x_p, x_s, ln_g[i, 1], ln_b[i, 1], 1.0, tm_ln, "mixer_out_ln")
        (x_p, xb_p), (x_s, xb_s) = ffn_ln(i, 1, 2, x_p, xb_p, x_s, xb_s)
        (x_p, xb_p), (x_s, xb_s) = _mm_ple(x_p, xb_p, pl_p[i].astype(BF16), x_s, xb_s, pl_s[i].astype(BF16),
                                           ple_w_gate, ple_w_proj, (i,), tm, 512)

    y_prompt = x_p.reshape(nbp, seq, d)
    y_sample = x_s[:nbs].reshape(nbs, 1, d)
    stack = lambda key: jnp.stack(outs[key])
    return (y_prompt, y_sample, stack("k_p"), stack("v_p"), stack("ik_p"), stack("k_s"), stack("v_s"), stack("ik_s"),
            stack("hg_p"), stack("hg_s"), stack("ssm_p"), stack("ssm_s"), stack("conv_p"), stack("conv_s"))
```

```python
import functools
import math

import jax
import jax.numpy as jnp
import numpy as np
from jax import lax
from jax.experimental import pallas as pl
from jax.experimental.pallas import tpu as pltpu

F32 = jnp.float32
BF16 = jnp.bfloat16

D_MODEL = 2048
DEPTH = 4
N_MIXERS = 3
D_FF = 2 * D_MODEL
PLE_DIM = 256
ALPHA = (2 * DEPTH) ** 0.25
LN_EPS = 1e-5
RMS_EPS = 1e-6
PAGE = 128

ATT_HD = 128
ATT_HEADS = 16
ATT_KV = 4
ATT_GROUP = ATT_HEADS // ATT_KV
IDX_HEADS = 16
IDX_DIM = 64
IDX_W_SCALE = (IDX_HEADS ** -0.5) * (IDX_DIM ** -0.5)
TOPK_MAX = 256
ATT_SCALE = ATT_HD ** -0.5
LOG2E = math.log2(math.e)
ATT_MAIN = ATT_HEADS * ATT_HD + 2 * ATT_KV * ATT_HD + IDX_HEADS * IDX_DIM

HG_HEADS = 16
HG_DK = 128
HG_DV = 128

SSM_INNER = 2 * D_MODEL
SSM_P = 64
SSM_HEADS = SSM_INNER // SSM_P
SSM_GROUPS = 8
SSM_HPG = SSM_HEADS // SSM_GROUPS
SSM_N = 128
SSM_CONV = 4
SSM_CH = SSM_INNER + 2 * SSM_GROUPS * SSM_N
SSM_MAIN = SSM_INNER + SSM_CH

LANES = 128
SUBLANES = 8
VMEM_LIMIT_BYTES = 56 * 1024 * 1024

INT_MIN = np.int32(-2 ** 31)
NEG_BIG = -1e30


def _cp(*sem):
    return pltpu.CompilerParams(dimension_semantics=sem, vmem_limit_bytes=VMEM_LIMIT_BYTES)


def _nt(a, b):
    return lax.dot_general(a, b, (((1,), (1,)), ((), ())), preferred_element_type=F32)


def _tn(a, b):
    return lax.dot_general(a, b, (((0,), (0,)), ((), ())), preferred_element_type=F32)


def _silu(x):
    return x * jax.nn.sigmoid(x)


def _wspec(k, tn, idx, col_block):
    lead = (None,) * len(idx)
    return pl.BlockSpec(lead + (k, tn), lambda n, i: idx + (0, col_block(n)))


def _proj_body(x_ref, xs_ref, w_ref, *rest, with_bf16):
    wb_ref = rest[-1]
    outs, outs_s = (rest[0:2], rest[2:4]) if with_bf16 else (rest[0:1], rest[1:2])

    def emit(refs, acc):
        refs[0][...] = acc
        if with_bf16:
            refs[1][...] = acc.astype(BF16)

    @pl.when(pl.program_id(1) == 0)
    def _():
        wb_ref[...] = w_ref[...].astype(BF16)
        emit(outs_s, jnp.dot(xs_ref[...], wb_ref[...], preferred_element_type=F32))

    emit(outs, jnp.dot(x_ref[...], wb_ref[...], preferred_element_type=F32))


def _proj(xb, xsb, w, idx, n_cols, tm, tn, with_bf16=False, name="proj"):
    m, k = xb.shape
    ms = xsb.shape[0]
    dts = (F32, BF16) if with_bf16 else (F32,)
    res = pl.pallas_call(
        functools.partial(_proj_body, with_bf16=with_bf16),
        grid=(n_cols // tn, m // tm),
        in_specs=[pl.BlockSpec((tm, k), lambda n, i: (i, 0)),
                  pl.BlockSpec((ms, k), lambda n, i: (0, 0)),
                  _wspec(k, tn, idx, lambda n: n)],
        out_specs=[pl.BlockSpec((tm, tn), lambda n, i: (i, n)) for _ in dts]
                  + [pl.BlockSpec((ms, tn), lambda n, i: (0, n)) for _ in dts],
        out_shape=[jax.ShapeDtypeStruct((m, n_cols), dt) for dt in dts]
                  + [jax.ShapeDtypeStruct((ms, n_cols), dt) for dt in dts],
        scratch_shapes=[pltpu.VMEM((k, tn), BF16)],
        compiler_params=_cp("parallel", "arbitrary"),
        name=name,
    )(xb, xsb, w)
    nd = len(dts)
    return (res[:nd], res[nd:]) if with_bf16 else (res[0], res[1])


def _swiglu_body(x_ref, xs_ref, wg_ref, wu_ref, o_ref, os_ref, wgb_ref, wub_ref):
    def swiglu(x):
        g = jnp.dot(x, wgb_ref[...], preferred_element_type=F32)
        u = jnp.dot(x, wub_ref[...], preferred_element_type=F32)
        return (_silu(g) * u).astype(BF16)

    @pl.when(pl.program_id(1) == 0)
    def _():
        wgb_ref[...] = wg_ref[...].astype(BF16)
        wub_ref[...] = wu_ref[...].astype(BF16)
        os_ref[...] = swiglu(xs_ref[...])

    o_ref[...] = swiglu(x_ref[...])


def _mm_swiglu(xb, xsb, wgu, idx, tm, tn):
    m, k = xb.shape
    ms = xsb.shape[0]
    f = wgu.shape[-1] // 2
    nj = f // tn
    return pl.pallas_call(
        _swiglu_body,
        grid=(nj, m // tm),
        in_specs=[pl.BlockSpec((tm, k), lambda n, i: (i, 0)),
                  pl.BlockSpec((ms, k), lambda n, i: (0, 0)),
                  _wspec(k, tn, idx, lambda n: n),
                  _wspec(k, tn, idx, lambda n: n + nj)],
        out_specs=[pl.BlockSpec((tm, tn), lambda n, i: (i, n)),
                   pl.BlockSpec((ms, tn), lambda n, i: (0, n))],
        out_shape=[jax.ShapeDtypeStruct((m, f), BF16), jax.ShapeDtypeStruct((ms, f), BF16)],
        scratch_shapes=[pltpu.VMEM((k, tn), BF16), pltpu.VMEM((k, tn), BF16)],
        compiler_params=_cp("parallel", "arbitrary"),
        name="ffn_up",
    )(xb, xsb, wgu, wgu)


def _cast_body(w_ref, o_ref):
    o_ref[...] = w_ref[...].astype(BF16)


def _cast_w(w, idx, tk=512):
    k, n = w.shape[-2:]
    lead = (None,) * len(idx)
    return pl.pallas_call(
        _cast_body,
        grid=(k // tk,),
        in_specs=[pl.BlockSpec(lead + (tk, n), lambda i: idx + (i, 0))],
        out_specs=pl.BlockSpec((tk, n), lambda i: (i, 0)),
        out_shape=jax.ShapeDtypeStruct((k, n), BF16),
        compiler_params=_cp("parallel"),
        name="cast_w",
    )(w)


LN_SPLIT = 2


def _mm_ln_body(a_ref, as_ref, w_ref, r_ref, rs_ref, g_ref, b_ref, o_ref, ob_ref, os_ref, osb_ref, *, scale, tm):
    g = g_ref[...]
    b = b_ref[...]

    def ln_rows(a, r, o, ob, sl):
        y = ALPHA * r[sl, :] + scale * jnp.dot(a[sl, :], w_ref[...], preferred_element_type=F32)
        mu = jnp.mean(y, axis=-1, keepdims=True)
        yc = y - mu
        var = jnp.mean(yc * yc, axis=-1, keepdims=True)
        out = yc * lax.rsqrt(var + LN_EPS) * g + b
        o[sl, :] = out
        ob[sl, :] = out.astype(BF16)

    @pl.when(pl.program_id(0) == 0)
    def _():
        ln_rows(as_ref, rs_ref, os_ref, osb_ref, pl.ds(0, as_ref.shape[0]))

    rows = tm // LN_SPLIT
    for r in range(0, tm, rows):
        ln_rows(a_ref, r_ref, o_ref, ob_ref, pl.ds(r, rows))


def _mm_ln(ab, asb, wb, res, res_s, g, b, scale, tm, name):
    m, kdim = ab.shape
    ms = asb.shape[0]
    n = wb.shape[1]
    res = pl.pallas_call(
        functools.partial(_mm_ln_body, scale=scale, tm=tm),
        grid=(m // tm,),
        in_specs=[pl.BlockSpec((tm, kdim), lambda i: (i, 0)),
                  pl.BlockSpec((ms, kdim), lambda i: (0, 0)),
                  pl.BlockSpec((kdim, n), lambda i: (0, 0), pipeline_mode=pl.Buffered(1)),
                  pl.BlockSpec((tm, n), lambda i: (i, 0)),
                  pl.BlockSpec((ms, n), lambda i: (0, 0)),
                  pl.BlockSpec((1, n), lambda i: (0, 0)),
                  pl.BlockSpec((1, n), lambda i: (0, 0))],
        out_specs=[pl.BlockSpec((tm, n), lambda i: (i, 0)),
                   pl.BlockSpec((tm, n), lambda i: (i, 0)),
                   pl.BlockSpec((ms, n), lambda i: (0, 0)),
                   pl.BlockSpec((ms, n), lambda i: (0, 0))],
        out_shape=[jax.ShapeDtypeStruct((m, n), F32), jax.ShapeDtypeStruct((m, n), BF16),
                   jax.ShapeDtypeStruct((ms, n), F32), jax.ShapeDtypeStruct((ms, n), BF16)],
        compiler_params=_cp("arbitrary"),
        name=name,
    )(ab, asb, wb, res, res_s, g.reshape(1, n), b.reshape(1, n))
    return res[:2], res[2:]


def _ple_body(xb_ref, p_ref, x_ref, xsb_ref, ps_ref, xs_ref, wg_ref, wp_ref, o_ref, ob_ref, os_ref, osb_ref,
              wgb_ref, wpb_ref):
    def ple(xb, p, x, o, ob):
        gate = jax.nn.sigmoid(jnp.dot(xb[...], wgb_ref[...], preferred_element_type=F32))
        proj = jnp.dot(p[...], wpb_ref[...], preferred_element_type=F32)
        out = x[...] + gate * proj
        o[...] = out
        ob[...] = out.astype(BF16)

    @pl.when(pl.program_id(1) == 0)
    def _():
        wgb_ref[...] = wg_ref[...].astype(BF16)
        wpb_ref[...] = wp_ref[...].astype(BF16)
        ple(xsb_ref, ps_ref, xs_ref, os_ref, osb_ref)

    ple(xb_ref, p_ref, x_ref, o_ref, ob_ref)


def _mm_ple(x32, xb, pb, xs32, xsb, psb, wg, wp, idx, tm, tn):
    m, d = xb.shape
    ms = xsb.shape[0]
    pd = pb.shape[1]
    res = pl.pallas_call(
        _ple_body,
        grid=(d // tn, m // tm),
        in_specs=[pl.BlockSpec((tm, d), lambda n, i: (i, 0)),
                  pl.BlockSpec((tm, pd), lambda n, i: (i, 0)),
                  pl.BlockSpec((tm, tn), lambda n, i: (i, n)),
                  pl.BlockSpec((ms, d), lambda n, i: (0, 0)),
                  pl.BlockSpec((ms, pd), lambda n, i: (0, 0)),
                  pl.BlockSpec((ms, tn), lambda n, i: (0, n)),
                  _wspec(d, tn, idx, lambda n: n),
                  _wspec(pd, tn, idx, lambda n: n)],
        out_specs=[pl.BlockSpec((tm, tn), lambda n, i: (i, n)),
                   pl.BlockSpec((tm, tn), lambda n, i: (i, n)),
                   pl.BlockSpec((ms, tn), lambda n, i: (0, n)),
                   pl.BlockSpec((ms, tn), lambda n, i: (0, n))],
        out_shape=[jax.ShapeDtypeStruct((m, d), F32), jax.ShapeDtypeStruct((m, d), BF16),
                   jax.ShapeDtypeStruct((ms, d), F32), jax.ShapeDtypeStruct((ms, d), BF16)],
        scratch_shapes=[pltpu.VMEM((d, tn), BF16), pltpu.VMEM((pd, tn), BF16)],
        compiler_params=_cp("parallel", "arbitrary"),
        name="ple",
    )(xb, pb, x32, xsb, psb, xs32, wg, wp)
    return res[:2], res[2:]


def _ik_norm_body(p_ref, g_ref, o_ref, ob_ref):
    x = p_ref[...][:, :IDX_DIM]
    mu = jnp.mean(x, axis=-1, keepdims=True)
    xc = x - mu
    out = xc * lax.rsqrt(jnp.mean(xc * xc, axis=-1, keepdims=True) + LN_EPS) * g_ref[...]
    o_ref[...] = out
    ob_ref[...] = out.astype(BF16)


def _ik_norm(proj2, ik_g, tm):
    m = proj2.shape[0]
    return pl.pallas_call(
        _ik_norm_body,
        grid=(m // tm,),
        in_specs=[pl.BlockSpec((tm, LANES), lambda i: (i, 0)),
                  pl.BlockSpec((1, IDX_DIM), lambda i: (0, 0))],
        out_specs=[pl.BlockSpec((tm, IDX_DIM), lambda i: (i, 0)),
                   pl.BlockSpec((tm, IDX_DIM), lambda i: (i, 0))],
        out_shape=[jax.ShapeDtypeStruct((m, IDX_DIM), F32), jax.ShapeDtypeStruct((m, IDX_DIM), BF16)],
        compiler_params=_cp("parallel"),
        name="idx_k_norm",
    )(proj2, ik_g.reshape(1, IDX_DIM))


def _order_key(x):
    bits = pltpu.bitcast(x, jnp.int32)
    return jnp.where(bits < 0, bits ^ jnp.int32(0x7FFFFFFF), bits)


_RADIX_BITS = [INT_MIN] + [np.int32(1 << s) for s in range(30, -1, -1)]


RADIX_ROWS = 128
KNORM_ROWS = 512
SUM_FLOOR = 2.0 ** -100


def _dsa_prompt_body(q_ref, iq_ref, iw_ref, k_ref, v_ref, ik_ref, o_ref, keys_ref, knorm_ref, *, qb, kc, topk):
    i = pl.program_id(1)
    nck = ((i + 1) * qb + kc - 1) // kc
    row_pos = i * qb + lax.broadcasted_iota(jnp.int32, (qb, 1), 0)
    iq = iq_ref[...]
    iw = iw_ref[...][:, IDX_DIM:IDX_DIM + IDX_HEADS] * IDX_W_SCALE
    iq_h = [iq[:, h * IDX_DIM:(h + 1) * IDX_DIM] for h in range(IDX_HEADS)]
    iw_h = [iw[:, h:h + 1] for h in range(IDX_HEADS)]
    col0 = lax.broadcasted_iota(jnp.int32, (1, kc), 1)

    def score_chunk(c, carry):
        off = pl.multiple_of(c * kc, kc)
        ikc = ik_ref[pl.ds(off, kc), :]
        sc = jnp.zeros((qb, kc), F32)
        for h in range(IDX_HEADS):
            sc = sc + iw_h[h] * jnp.maximum(_nt(iq_h[h], ikc), 0.0)
        key = jnp.where(col0 + off <= row_pos, _order_key(sc), INT_MIN)
        keys_ref[:, pl.ds(off, kc)] = key
        return carry

    lax.fori_loop(0, nck, score_chunk, 0)

    nblk = qb // RADIX_ROWS
    los = [jnp.full((RADIX_ROWS, 1), INT_MIN, jnp.int32) for _ in range(nblk)]
    for bit in _RADIX_BITS:
        cands = [lo + bit for lo in los]
        accs = []
        for blk in range(nblk):
            cand_b = jnp.broadcast_to(cands[blk], (RADIX_ROWS, LANES))

            def count_chunk(c, acc, cand_b=cand_b, r0=blk * RADIX_ROWS):
                off = pl.multiple_of(c * kc, kc)
                for t in range(kc // LANES):
                    tile = keys_ref[r0:r0 + RADIX_ROWS, pl.ds(off + t * LANES, LANES)]
                    acc = acc + jnp.where(tile >= cand_b, 1.0, 0.0)
                return acc

            accs.append(lax.fori_loop(0, nck, count_chunk, jnp.zeros((RADIX_ROWS, LANES), F32)))
        cnts = [jnp.sum(acc, axis=-1, keepdims=True) for acc in accs]
        los = [jnp.where(cnts[blk] >= topk, cands[blk], los[blk]) for blk in range(nblk)]
    thr = jnp.maximum(jnp.concatenate(los, axis=0), INT_MIN + 1)

    @pl.when(i == 0)
    def _():
        for g in range(ATT_KV):
            best = jnp.zeros((1, 1), F32)
            for r0 in range(0, k_ref.shape[0], KNORM_ROWS):
                kk = k_ref[r0:r0 + KNORM_ROWS, g * ATT_HD:(g + 1) * ATT_HD].astype(F32)
                best = jnp.maximum(best, jnp.max(jnp.sum(kk * kk, axis=-1, keepdims=True), axis=0, keepdims=True))
            knorm_ref[g:g + 1, :] = jnp.broadcast_to(jnp.sqrt(best), (1, LANES))

    def store(g, out):
        for j in range(ATT_GROUP):
            h = g * ATT_GROUP + j
            o_ref[:, h * ATT_HD:(h + 1) * ATT_HD] = out[j].astype(o_ref.dtype)

    for g in range(ATT_KV):
        qg = jnp.concatenate(
            [q_ref[:, (g * ATT_GROUP + j) * ATT_HD:(g * ATT_GROUP + j + 1) * ATT_HD] for j in range(ATT_GROUP)], axis=0)
        qg = (qg * (ATT_SCALE * LOG2E)).astype(BF16)
        q32 = qg.astype(F32)
        shift = (jnp.sqrt(jnp.sum(q32 * q32, axis=-1, keepdims=True)) * knorm_ref[g:g + 1, 0:1]).reshape(
            ATT_GROUP, qb, 1)

        def chunk_operands(c, g=g):
            off = pl.multiple_of(c * kc, kc)
            kch = k_ref[pl.ds(off, kc), g * ATT_HD:(g + 1) * ATT_HD]
            vch = v_ref[pl.ds(off, kc), g * ATT_HD:(g + 1) * ATT_HD]
            sel = (keys_ref[:, pl.ds(off, kc)] >= thr)[None]
            return kch, vch, sel

        def fixed_chunk(c, carry, qg=qg, shift=shift):
            l, acc = carry
            kch, vch, sel = chunk_operands(c)
            p = jnp.exp2(jnp.where(sel, _nt(qg, kch).reshape(ATT_GROUP, qb, kc) - shift, -jnp.inf))
            l = l + jnp.sum(p, axis=-1, keepdims=True)
            pv = jnp.dot(p.reshape(ATT_GROUP * qb, kc).astype(BF16), vch, preferred_element_type=F32)
            return l, acc + pv.reshape(ATT_GROUP, qb, ATT_HD)

        l, acc = lax.fori_loop(0, nck, fixed_chunk, (jnp.zeros((ATT_GROUP, qb, 1), F32),
                                                     jnp.zeros((ATT_GROUP, qb, ATT_HD), F32)))
        healthy = jnp.min(l) > SUM_FLOOR

        @pl.when(healthy)
        def _(g=g, l=l, acc=acc):
            store(g, acc / l)

        @pl.when(jnp.logical_not(healthy))
        def _(g=g, qg=qg):
            def running_chunk(c, carry):
                m, l, acc = carry
                kch, vch, sel = chunk_operands(c)
                s = jnp.where(sel, _nt(qg, kch).reshape(ATT_GROUP, qb, kc), -jnp.inf)
                m_new = jnp.maximum(m, jnp.max(s, axis=-1, keepdims=True))
                p = jnp.exp2(s - m_new)
                a = jnp.exp2(m - m_new)
                l = a * l + jnp.sum(p, axis=-1, keepdims=True)
                pv = jnp.dot(p.reshape(ATT_GROUP * qb, kc).astype(BF16), vch, preferred_element_type=F32)
                return m_new, l, a * acc + pv.reshape(ATT_GROUP, qb, ATT_HD)

            init = (jnp.full((ATT_GROUP, qb, 1), NEG_BIG, F32), jnp.zeros((ATT_GROUP, qb, 1), F32),
                    jnp.zeros((ATT_GROUP, qb, ATT_HD), F32))
            _, l, acc = lax.fori_loop(0, nck, running_chunk, init)
            store(g, acc / l)


def _dsa_prompt(proj, projb, proj2, ikb, nb, s, qb=256, kc=512):
    nq = s // qb
    topk = min(TOPK_MAX, s // 4)
    qw = ATT_HEADS * ATT_HD
    kvw = ATT_KV * ATT_HD
    iqw = IDX_HEADS * IDX_DIM
    return pl.pallas_call(
        functools.partial(_dsa_prompt_body, qb=qb, kc=kc, topk=topk),
        grid=(nb, nq),
        in_specs=[pl.BlockSpec((qb, qw), lambda b, i: (b * nq + i, 0)),
                  pl.BlockSpec((qb, iqw), lambda b, i: (b * nq + i, (qw + 2 * kvw) // iqw)),
                  pl.BlockSpec((qb, LANES), lambda b, i: (b * nq + i, 0)),
                  pl.BlockSpec((s, kvw), lambda b, i: (b, qw // kvw)),
                  pl.BlockSpec((s, kvw), lambda b, i: (b, qw // kvw + 1)),
                  pl.BlockSpec((s, IDX_DIM), lambda b, i: (b, 0))],
        out_specs=pl.BlockSpec((qb, qw), lambda b, i: (b * nq + i, 0)),
        out_shape=jax.ShapeDtypeStruct((nb * s, qw), BF16),
        scratch_shapes=[pltpu.VMEM((qb, s), jnp.int32), pltpu.VMEM((SUBLANES, LANES), F32)],
        compiler_params=_cp("parallel", "arbitrary"),
        name="dsa_prompt",
    )(proj, projb, proj2, projb, projb, ikb)


SCORE_PAGES = 16


def _dsa_s_scores_body(pt_ref, iq_ref, iw_ref, ikn_ref, *rest, n_pages):
    page_refs, o_ref = rest[:SCORE_PAGES], rest[SCORE_PAGES]
    p = pl.program_id(1)
    iq = iq_ref[...].astype(BF16)
    iw = iw_ref[...] * IDX_W_SCALE

    @pl.when(p == 0)
    def _():
        own = jnp.broadcast_to(ikn_ref[...], (SUBLANES, IDX_DIM)).astype(BF16)
        d = jnp.maximum(_nt(iq, own), 0.0)
        sc = jnp.sum(iw * d, axis=0, keepdims=True)
        o_ref[n_pages:n_pages + SUBLANES, :] = jnp.broadcast_to(sc[:, 0:1], (SUBLANES, LANES))

    for r in range(SCORE_PAGES):
        d = jnp.maximum(_nt(iq, page_refs[r][...].astype(BF16)), 0.0)
        o_ref[pl.ds(p * SCORE_PAGES + r, 1), :] = jnp.sum(iw * d, axis=0, keepdims=True)


def _dsa_s_scores(page_table, iq, iw, ik_new, cache_ik, layer):
    nb, n_pages = page_table.shape
    page_spec = lambda r: pl.BlockSpec((None, None, PAGE, IDX_DIM),
                                       lambda b, p, pt: (layer, pt[b, p * SCORE_PAGES + r], 0, 0))
    return pl.pallas_call(
        functools.partial(_dsa_s_scores_body, n_pages=n_pages),
        grid_spec=pltpu.PrefetchScalarGridSpec(
            num_scalar_prefetch=1,
            grid=(nb, n_pages // SCORE_PAGES),
            in_specs=[pl.BlockSpec((None, IDX_HEADS, IDX_DIM), lambda b, p, pt: (b, 0, 0)),
                      pl.BlockSpec((None, IDX_HEADS, 1), lambda b, p, pt: (b, 0, 0)),
                      pl.BlockSpec((None, 1, IDX_DIM), lambda b, p, pt: (b, 0, 0))]
                     + [page_spec(r) for r in range(SCORE_PAGES)],
            out_specs=pl.BlockSpec((None, n_pages + SUBLANES, LANES), lambda b, p, pt: (b, 0, 0)),
        ),
        out_shape=jax.ShapeDtypeStruct((nb, n_pages + SUBLANES, LANES), F32),
        compiler_params=_cp("parallel", "arbitrary"),
        name="dsa_sample_scores",
    )(page_table, iq, iw, ik_new, *([cache_ik] * SCORE_PAGES))


def _dsa_s_select_body(sc_ref, ptc_ref, idx_ref, meta_ref, rank_ref, phys_ref, *, n_pages, topk):
    shape = (n_pages + SUBLANES, LANES)
    rows = lax.broadcasted_iota(jnp.int32, shape, 0)
    cols = lax.broadcasted_iota(jnp.int32, shape, 1)
    live = (rows < n_pages) | ((rows == n_pages) & (cols == 0))
    keys = jnp.where(live, _order_key(sc_ref[...]), INT_MIN)
    lo = jnp.full((1, 1), INT_MIN, jnp.int32)
    for bit in _RADIX_BITS:
        cand = lo + bit
        cnt = jnp.sum(jnp.sum(jnp.where(keys >= cand, 1.0, 0.0), axis=0, keepdims=True), axis=1, keepdims=True)
        lo = jnp.where(cnt >= topk, cand, lo)
    sel = keys >= jnp.maximum(lo, INT_MIN + 1)
    sel_c = sel[:n_pages, :]
    own = jnp.where(sel[n_pages:n_pages + 1, 0:1], 1, 0)

    r_i = lax.broadcasted_iota(jnp.int32, (LANES, LANES), 0)
    c_i = lax.broadcasted_iota(jnp.int32, (LANES, LANES), 1)
    ones_le = jnp.where(r_i <= c_i, 1.0, 0.0).astype(BF16)
    ones_gt = jnp.where(r_i > c_i, 1.0, 0.0).astype(BF16)
    m = jnp.where(sel_c, 1.0, 0.0).astype(BF16)
    within = jnp.dot(m, ones_le, preferred_element_type=F32)
    tot = jnp.broadcast_to(within[:, LANES - 1:LANES], (n_pages, LANES)).astype(BF16)
    before = jnp.dot(ones_gt, tot, preferred_element_type=F32)
    rank_ref[...] = jnp.where(sel_c, (within + before).astype(jnp.int32) - 1, -1)
    phys_ref[...] = (ptc_ref[...] * PAGE + c_i[:n_pages, :]).astype(F32)
    n_sel = (before[n_pages - 1:n_pages, 0:1] + within[n_pages - 1:n_pages, LANES - 1:LANES]).astype(jnp.int32)

    slot = lax.broadcasted_iota(jnp.int32, (topk, 1), 0)

    def place(r, acc):
        return acc + jnp.where(rank_ref[pl.ds(r, 1), :] == slot, phys_ref[pl.ds(r, 1), :], 0.0)

    acc = lax.fori_loop(0, n_pages, place, jnp.zeros((topk, LANES), F32), unroll=8)
    idx_ref[...] = jnp.sum(acc, axis=-1, keepdims=True).astype(jnp.int32)
    mrow = lax.broadcasted_iota(jnp.int32, (SUBLANES, LANES), 0)
    meta_ref[...] = jnp.where(mrow == 0, jnp.minimum(n_sel, topk), own)


def _dsa_s_select(scores, page_table):
    nb, n_pages = page_table.shape
    assert n_pages == LANES
    topk = min(TOPK_MAX, (n_pages * PAGE + 1) // 4)
    idx, meta = pl.pallas_call(
        functools.partial(_dsa_s_select_body, n_pages=n_pages, topk=topk),
        grid=(nb,),
        in_specs=[pl.BlockSpec((None, n_pages + SUBLANES, LANES), lambda b: (b, 0, 0)),
                  pl.BlockSpec((None, n_pages, 1), lambda b: (b, 0, 0))],
        out_specs=[pl.BlockSpec((None, topk, 1), lambda b: (b, 0, 0)),
                   pl.BlockSpec((None, SUBLANES, LANES), lambda b: (b, 0, 0))],
        out_shape=[jax.ShapeDtypeStruct((nb, topk, 1), jnp.int32),
                   jax.ShapeDtypeStruct((nb, SUBLANES, LANES), jnp.int32)],
        scratch_shapes=[pltpu.VMEM((n_pages, LANES), jnp.int32), pltpu.VMEM((n_pages, LANES), F32)],
        compiler_params=_cp("parallel"),
        name="dsa_sample_select",
    )(scores, page_table.reshape(nb, n_pages, 1))
    return idx.reshape(nb, topk), meta[:, 0:2, 0]


def _dsa_s_attend_body(idx_ref, meta_ref, q_ref, kn_ref, vn_ref, ck_ref, cv_ref, o_ref, kbuf, vbuf, sem,
                       *, layer, topk):
    b = pl.program_id(0)
    nb = pl.num_programs(0)
    slot = b % 2

    def row_copies(tok, buf_slot, j):
        row = idx_ref[tok, j]
        page, off = row // PAGE, row % PAGE
        dst = pl.ds(j * ATT_KV, ATT_KV)
        return (pltpu.make_async_copy(ck_ref.at[layer, page, off], kbuf.at[buf_slot, dst, :], sem.at[buf_slot, 0]),
                pltpu.make_async_copy(cv_ref.at[layer, page, off], vbuf.at[buf_slot, dst, :], sem.at[buf_slot, 1]))

    def start_gather(tok, buf_slot):
        def body(j, carry):
            for cp in row_copies(tok, buf_slot, j):
                cp.start()
            return carry
        lax.fori_loop(0, topk, body, 0)

    @pl.when(b == 0)
    def _():
        start_gather(0, 0)

    @pl.when(b + 1 < nb)
    def _():
        start_gather(b + 1, 1 - slot)

    def wait_body(j, carry):
        for cp in row_copies(b, slot, j):
            cp.wait()
        return carry

    lax.fori_loop(0, topk, wait_body, 0)

    n_sel = meta_ref[b, 0]
    own = meta_ref[b, 1] > 0
    qb = q_ref[...].astype(BF16)
    ncol = topk * ATT_KV
    col = lax.broadcasted_iota(jnp.int32, (ATT_HEADS, ncol), 1)
    head = lax.broadcasted_iota(jnp.int32, (ATT_HEADS, ncol), 0)
    valid = (col % ATT_KV == head // ATT_GROUP) & (col // ATT_KV < n_sel)
    s = jnp.where(valid, _nt(qb, kbuf[slot].astype(BF16)) * ATT_SCALE, NEG_BIG)
    kn = kn_ref[...].astype(BF16).astype(F32)
    vn = vn_ref[...].astype(BF16).astype(F32)
    expand = lambda a: jnp.concatenate(
        [jnp.broadcast_to(a[:, g * ATT_HD:(g + 1) * ATT_HD], (ATT_GROUP, ATT_HD)) for g in range(ATT_KV)], axis=0)
    s_own = jnp.where(own, jnp.sum(qb.astype(F32) * expand(kn), axis=-1, keepdims=True) * ATT_SCALE, NEG_BIG)
    m = jnp.maximum(jnp.max(s, axis=-1, keepdims=True), s_own)
    p = jnp.where(valid, jnp.exp(s - m), 0.0)
    p_own = jnp.where(own, jnp.exp(s_own - m), 0.0)
    l = jnp.sum(p, axis=-1, keepdims=True) + p_own
    acc = (jnp.dot(p.astype(BF16), vbuf[slot].astype(BF16), preferred_element_type=F32)
           + p_own.astype(BF16).astype(F32) * expand(vn))
    o_ref[...] = acc / l


def _dsa_s_attend(idx, meta, q, k_new, v_new, cache_k, cache_v, layer):
    nb, topk = idx.shape
    kvw = ATT_KV * ATT_HD
    return pl.pallas_call(
        functools.partial(_dsa_s_attend_body, layer=layer, topk=topk),
        grid_spec=pltpu.PrefetchScalarGridSpec(
            num_scalar_prefetch=2,
            grid=(nb,),
            in_specs=[pl.BlockSpec((None, ATT_HEADS, ATT_HD), lambda b, idx, meta: (b, 0, 0)),
                      pl.BlockSpec((None, 1, kvw), lambda b, idx, meta: (b, 0, 0)),
                      pl.BlockSpec((None, 1, kvw), lambda b, idx, meta: (b, 0, 0)),
                      pl.BlockSpec(memory_space=pl.ANY),
                      pl.BlockSpec(memory_space=pl.ANY)],
            out_specs=pl.BlockSpec((None, ATT_HEADS, ATT_HD), lambda b, idx, meta: (b, 0, 0)),
            scratch_shapes=[pltpu.VMEM((2, topk * ATT_KV, ATT_HD), F32),
                            pltpu.VMEM((2, topk * ATT_KV, ATT_HD), F32),
                            pltpu.SemaphoreType.DMA((2, 2))],
        ),
        out_shape=jax.ShapeDtypeStruct((nb, ATT_HEADS, ATT_HD), F32),
        compiler_params=_cp("arbitrary"),
        name="dsa_sample_attend",
    )(idx, meta, q, k_new, v_new, cache_k, cache_v)


HG_SUB = 16


HG_HB = 4
HG_EXP_LIMIT = 80.0


def _hgrn_body(lbl_ref, q_ref, f_ref, i_ref, g_ref, ng_ref, s0_ref, o_ref, so_ref, st_ref, *, layer, c, t, nc):
    ci = pl.program_id(2)
    mid = c // 2

    @pl.when(ci == 0)
    def _():
        for hh in range(HG_HB):
            st_ref[hh] = s0_ref[hh].T

    logits = lbl_ref[...]
    e = jnp.exp(logits - jnp.max(logits, axis=0, keepdims=True))
    soft = e / jnp.sum(e, axis=0, keepdims=True)
    lb_all = jnp.zeros((1, HG_HB * HG_DK), F32)
    for r in range(1, layer + 1):
        lb_all = lb_all + soft[r:r + 1, :]
    ng = ng_ref[...]

    def gates(hh, sl, row0):
        cs = slice(hh * HG_DK, (hh + 1) * HG_DK)
        lb = lb_all[:, cs]
        fg = lb + (1.0 - lb) * jax.nn.sigmoid(f_ref[sl, cs])
        lf = jnp.log(fg)
        kk = 1.0 - fg
        if t % c:
            n = lf.shape[0]
            valid = (ci * c + row0 + lax.broadcasted_iota(jnp.int32, (n, 1), 0)) < t
            lf = jnp.where(valid, lf, 0.0)
            kk = jnp.where(valid, kk, 0.0)
        return lf, kk

    def finish(hh, sl, o):
        cs = slice(hh * HG_DV, (hh + 1) * HG_DV)
        on = o * lax.rsqrt(jnp.mean(o * o, axis=-1, keepdims=True) + RMS_EPS) * ng
        o_ref[sl, cs] = (on * _silu(g_ref[sl, cs])).astype(o_ref.dtype)

    tri_c = lax.broadcasted_iota(jnp.int32, (c, c), 0) >= lax.broadcasted_iota(jnp.int32, (c, c), 1)
    full = pl.ds(0, c)
    lfs, kks, bs = [], [], []
    safe = None
    for hh in range(HG_HB):
        lf, kk = gates(hh, full, 0)
        b = jnp.dot(tri_c.astype(F32), lf, precision=lax.Precision.HIGHEST, preferred_element_type=F32)
        bm = b[mid - 1:mid, :]
        ok = jnp.min(jnp.minimum(bm, b[c - 1:c, :] - bm)) > -HG_EXP_LIMIT
        safe = ok if safe is None else jnp.logical_and(safe, ok)
        lfs.append(lf)
        kks.append(kk)
        bs.append(b)

    @pl.when(safe)
    def _():
        for hh in range(HG_HB):
            cs = slice(hh * HG_DK, (hh + 1) * HG_DK)
            b, kk = bs[hh], kks[hh]
            bm = b[mid - 1:mid, :]
            bl = b[c - 1:c, :]
            qq = _silu(q_ref[:, cs])
            vv = i_ref[:, cs].astype(BF16)
            att = jnp.where(tri_c, _nt((qq * jnp.exp(b - bm)).astype(BF16), (kk * jnp.exp(bm - b)).astype(BF16)), 0.0)
            st = st_ref[hh]
            o = (jnp.dot(att.astype(BF16), vv, preferred_element_type=F32)
                 + _nt((qq * jnp.exp(b)).astype(BF16), st.astype(BF16)))
            st_ref[hh] = st * jnp.exp(bl) + _tn(vv, (kk * jnp.exp(bl - b)).astype(BF16))
            finish(hh, full, o)

    @pl.when(jnp.logical_not(safe))
    def _():
        rows = lax.broadcasted_iota(jnp.int32, (HG_SUB, 1), 0)
        tri = (lax.broadcasted_iota(jnp.int32, (HG_SUB, HG_SUB), 0)
               >= lax.broadcasted_iota(jnp.int32, (HG_SUB, HG_SUB), 1)).astype(F32)
        for hh in range(HG_HB):
            cs = slice(hh * HG_DK, (hh + 1) * HG_DK)

            def sub_block(sb, carry, hh=hh, cs=cs):
                row0 = pl.multiple_of(sb * HG_SUB, HG_SUB)
                sl = pl.ds(row0, HG_SUB)
                lf, kk = gates(hh, sl, row0)
                qq = _silu(q_ref[sl, cs])
                vv = i_ref[sl, cs]
                b = jnp.dot(tri, lf, precision=lax.Precision.HIGHEST, preferred_element_type=F32)
                st = st_ref[hh]
                o = _nt((qq * jnp.exp(b)).astype(BF16), st.astype(BF16))
                for s in range(HG_SUB):
                    dec = jnp.exp(jnp.where(rows >= s, b - b[s:s + 1, :], -jnp.inf))
                    att = jnp.sum(qq * dec * kk[s:s + 1, :], axis=-1, keepdims=True)
                    o = o + att * vv[s:s + 1, :]
                bl = b[HG_SUB - 1:HG_SUB, :]
                st_ref[hh] = st * jnp.exp(bl) + _tn(vv.astype(BF16), (kk * jnp.exp(bl - b)).astype(BF16))
                finish(hh, sl, o)
                return carry

            lax.fori_loop(0, c // HG_SUB, sub_block, 0)

    @pl.when(ci == nc - 1)
    def _():
        for hh in range(HG_HB):
            so_ref[hh] = st_ref[hh].T


def _hgrn(proj, lb_logits, norm_g, s0, layer, t, c=128):
    nb, tpad, _ = proj.shape
    nc = tpad // c
    nhb = HG_HEADS // HG_HB
    w = HG_HB * HG_DK
    return pl.pallas_call(
        functools.partial(_hgrn_body, layer=layer, c=c, t=t, nc=nc),
        grid=(nb, nhb, nc),
        in_specs=[pl.BlockSpec((DEPTH, w), lambda b, h, ci: (0, h)),
                  pl.BlockSpec((None, c, w), lambda b, h, ci: (b, ci, h)),
                  pl.BlockSpec((None, c, w), lambda b, h, ci: (b, ci, nhb + h)),
                  pl.BlockSpec((None, c, w), lambda b, h, ci: (b, ci, 2 * nhb + h)),
                  pl.BlockSpec((None, c, w), lambda b, h, ci: (b, ci, 3 * nhb + h)),
                  pl.BlockSpec((1, HG_DV), lambda b, h, ci: (0, 0)),
                  pl.BlockSpec((None, HG_HB, HG_DK, HG_DV), lambda b, h, ci: (b, h, 0, 0))],
        out_specs=[pl.BlockSpec((None, c, w), lambda b, h, ci: (b, ci, h)),
                   pl.BlockSpec((None, HG_HB, HG_DK, HG_DV), lambda b, h, ci: (b, h, 0, 0))],
        out_shape=[jax.ShapeDtypeStruct((nb, tpad, HG_HEADS * HG_DV), BF16),
                   jax.ShapeDtypeStruct((nb, HG_HEADS, HG_DK, HG_DV), F32)],
        scratch_shapes=[pltpu.VMEM((HG_HB, HG_DV, HG_DK), F32)],
        compiler_params=_cp("parallel", "parallel", "arbitrary"),
        name="hgrn2",
    )(lb_logits, proj, proj, proj, proj, norm_g.reshape(1, HG_DV), s0)


def _conv_body(x_ref, halo_ref, cs_ref, w_ref, b_ref, o_ref, nc_ref, *, c, t, nc):
    ti = pl.program_id(2)
    halo = jnp.where(ti == 0, cs_ref[...], halo_ref[...])
    full = jnp.concatenate([halo, x_ref[...]], axis=0)
    w = w_ref[...]
    conv = b_ref[...]
    for k in range(SSM_CONV):
        lo = SUBLANES - (SSM_CONV - 1) + k
        conv = conv + full[lo:lo + c, :] * w[k:k + 1, :]
    o_ref[...] = _silu(conv)

    @pl.when(ti == nc - 1)
    def _():
        tv = t - (nc - 1) * c
        tail = full[SUBLANES + tv - (SSM_CONV - 1):SUBLANES + tv, :]
        nc_ref[...] = jnp.concatenate([tail, jnp.zeros((SUBLANES - (SSM_CONV - 1), tail.shape[1]), F32)], axis=0)


def _ssd_conv(zx, cs_pad, conv_w, conv_b, t, c, cw=512):
    nb, tpad, _ = zx.shape
    nc = tpad // c
    ncol = SSM_CH // cw
    col0 = SSM_INNER // cw
    hb = c // SUBLANES
    return pl.pallas_call(
        functools.partial(_conv_body, c=c, t=t, nc=nc),
        grid=(nb, ncol, nc),
        in_specs=[pl.BlockSpec((None, c, cw), lambda b, j, ti: (b, ti, col0 + j)),
                  pl.BlockSpec((None, SUBLANES, cw), lambda b, j, ti: (b, jnp.maximum(ti * hb - 1, 0), col0 + j)),
                  pl.BlockSpec((None, SUBLANES, cw), lambda b, j, ti: (b, 0, j)),
                  pl.BlockSpec((SSM_CONV, cw), lambda b, j, ti: (0, j)),
                  pl.BlockSpec((1, cw), lambda b, j, ti: (0, j))],
        out_specs=[pl.BlockSpec((None, c, cw), lambda b, j, ti: (b, ti, j)),
                   pl.BlockSpec((None, SUBLANES, cw), lambda b, j, ti: (b, 0, j))],
        out_shape=[jax.ShapeDtypeStruct((nb, tpad, SSM_CH), F32),
                   jax.ShapeDtypeStruct((nb, SUBLANES, SSM_CH), F32)],
        compiler_params=_cp("parallel", "parallel", "arbitrary"),
        name="ssd_conv",
    )(zx, zx, cs_pad, conv_w, conv_b.reshape(1, SSM_CH))


def _expand_heads(v, e):
    hi = v.astype(BF16)
    r1 = v - hi.astype(F32)
    mid = r1.astype(BF16)
    lo = (r1 - mid.astype(F32)).astype(BF16)
    return (jnp.dot(hi, e, preferred_element_type=F32) + jnp.dot(mid, e, preferred_element_type=F32)
            + jnp.dot(lo, e, preferred_element_type=F32))


def _ssd_body(xc_ref, z_ref, dt_ref, e_ref, dtb_ref, alog_ref, dx_ref, ng_ref, s0_ref, o_ref, so_ref, st_ref,
              *, c, t, nc):
    ci = pl.program_id(1)
    gw = SSM_HPG * SSM_P

    @pl.when(ci == 0)
    def _():
        for blk in range(SSM_INNER // LANES):
            st_ref[:, blk * LANES:(blk + 1) * LANES] = s0_ref[blk * LANES:(blk + 1) * LANES, :].T

    e = e_ref[...]
    rows = lax.broadcasted_iota(jnp.int32, (c, 1), 0)
    tri_b = lax.broadcasted_iota(jnp.int32, (c, c), 0) >= lax.broadcasted_iota(jnp.int32, (c, c), 1)
    lane_lo = lax.broadcasted_iota(jnp.int32, (1, LANES), 1) < SSM_P

    dt = jax.nn.softplus(dt_ref[...] + dtb_ref[...])
    if t % c:
        dt = jnp.where(ci * c + rows < t, dt, 0.0)
    da = dt * (-jnp.exp(alog_ref[...]))
    bcum = jnp.dot(tri_b.astype(F32), da, precision=lax.Precision.HIGHEST, preferred_element_type=F32)
    bcum_t = bcum.T
    bl = bcum[c - 1:c, :]
    dt_x = _expand_heads(dt, e)
    eb_x = _expand_heads(jnp.exp(bcum), e)
    w_x = _expand_heads(jnp.exp(bl - bcum) * dt, e)
    decay_x = _expand_heads(jnp.broadcast_to(jnp.exp(bl), (SUBLANES, LANES)), e)[0:1, :]

    xs = xc_ref[:, 0:SSM_INNER]
    xdt = (xs * dt_x).astype(BF16)
    xw = (xs * w_x).astype(BF16)
    y = xs * dx_ref[...]
    zg = _silu(z_ref[...])
    for g in range(SSM_GROUPS):
        bg = xc_ref[:, SSM_INNER + g * SSM_N:SSM_INNER + (g + 1) * SSM_N]
        cg = xc_ref[:, SSM_INNER + (SSM_GROUPS + g) * SSM_N:SSM_INNER + (SSM_GROUPS + g + 1) * SSM_N].astype(BF16)
        cb = _nt(cg, bg.astype(BF16))
        st_g = st_ref[:, g * gw:(g + 1) * gw]
        yg = jnp.dot(cg, st_g.astype(BF16), preferred_element_type=F32) * eb_x[:, g * gw:(g + 1) * gw]
        parts = []
        for jp in range(SSM_HPG // 2):
            xpair = xdt[:, g * gw + jp * LANES:g * gw + (jp + 1) * LANES]
            acc = None
            for half in range(2):
                h = g * SSM_HPG + jp * 2 + half
                dec = jnp.exp(jnp.where(tri_b, bcum[:, h:h + 1] - bcum_t[h:h + 1, :], -jnp.inf))
                w = (cb * dec).astype(BF16)
                xh = jnp.where(lane_lo if half == 0 else jnp.logical_not(lane_lo), xpair, 0.0).astype(BF16)
                r = jnp.dot(w, xh, preferred_element_type=F32)
                acc = r if acc is None else acc + r
            parts.append(acc)
        yg = yg + jnp.concatenate(parts, axis=1)
        st_ref[:, g * gw:(g + 1) * gw] = (st_g * decay_x[:, g * gw:(g + 1) * gw]
                                          + jnp.dot(bg.T.astype(BF16), xw[:, g * gw:(g + 1) * gw],
                                                    preferred_element_type=F32))
        yg = (yg + y[:, g * gw:(g + 1) * gw]) * zg[:, g * gw:(g + 1) * gw]
        yg = yg * lax.rsqrt(jnp.mean(yg * yg, axis=-1, keepdims=True) + RMS_EPS) * ng_ref[:, g * gw:(g + 1) * gw]
        o_ref[:, g * gw:(g + 1) * gw] = yg.astype(o_ref.dtype)

    @pl.when(ci == nc - 1)
    def _():
        for blk in range(SSM_INNER // LANES):
            so_ref[blk * LANES:(blk + 1) * LANES, :] = st_ref[:, blk * LANES:(blk + 1) * LANES].T


def _ssd(xc, z, dt_raw, expand, dt_bias, a_log, d_x, norm_g, s0, t, c=128):
    nb, tpad, _ = xc.shape
    nc = tpad // c
    return pl.pallas_call(
        functools.partial(_ssd_body, c=c, t=t, nc=nc),
        grid=(nb, nc),
        in_specs=[pl.BlockSpec((None, c, SSM_CH), lambda b, ci: (b, ci, 0)),
                  pl.BlockSpec((None, c, SSM_INNER), lambda b, ci: (b, ci, 0)),
                  pl.BlockSpec((None, c, LANES), lambda b, ci: (b, ci, 0)),
                  pl.BlockSpec((LANES, SSM_INNER), lambda b, ci: (0, 0)),
                  pl.BlockSpec((1, LANES), lambda b, ci: (0, 0)),
                  pl.BlockSpec((1, LANES), lambda b, ci: (0, 0)),
                  pl.BlockSpec((1, SSM_INNER), lambda b, ci: (0, 0)),
                  pl.BlockSpec((1, SSM_INNER), lambda b, ci: (0, 0)),
                  pl.BlockSpec((None, SSM_INNER, SSM_N), lambda b, ci: (b, 0, 0))],
        out_specs=[pl.BlockSpec((None, c, SSM_INNER), lambda b, ci: (b, ci, 0)),
                   pl.BlockSpec((None, SSM_INNER, SSM_N), lambda b, ci: (b, 0, 0))],
        out_shape=[jax.ShapeDtypeStruct((nb, tpad, SSM_INNER), BF16),
                   jax.ShapeDtypeStruct((nb, SSM_INNER, SSM_N), F32)],
        scratch_shapes=[pltpu.VMEM((SSM_N, SSM_INNER), F32)],
        compiler_params=_cp("parallel", "arbitrary"),
        name="ssd",
    )(xc, z, dt_raw, expand, dt_bias, a_log, d_x, norm_g.reshape(1, SSM_INNER), s0)


def _pad_cols(w, n):
    return jnp.pad(w, ((0, 0), (0, n - w.shape[1])))


def _pad_time(a, tpad):
    return jnp.pad(a, ((0, 0), (0, tpad - a.shape[1]), (0, 0)))


def kernel(x_prompt, x_sample, cache_k, cache_v, cache_idx_k, state_hgrn, state_ssm, state_conv, page_table, p_prompt, p_sample, ln_g, ln_b, ffn_w_gate_up, ffn_w_down, ple_w_proj, ple_w_gate, att_w_in, att_idx_k_norm, att_w_o, hg_w_in, hg_lb_logits, hg_norm_g, hg_w_o, ssm_w_in, ssm_conv_w, ssm_conv_b, ssm_dt_bias, ssm_a_log, ssm_d, ssm_norm_g, ssm_w_o):
    nbp, seq, d = x_prompt.shape
    nbs = x_sample.shape[0]
    mp = nbp * seq
    ms = 16
    chunk = 128

    tm, tm_ln = 1024, 512
    x_p = x_prompt.reshape(mp, d)
    x_s = jnp.pad(x_sample.reshape(nbs, d), ((0, ms - nbs), (0, 0)))
    xb_p, xb_s = x_p.astype(BF16), x_s.astype(BF16)
    pl_p = p_prompt.reshape(DEPTH, mp, PLE_DIM)
    pl_s = jnp.pad(p_sample.reshape(DEPTH, nbs, PLE_DIM), ((0, 0), (0, ms - nbs), (0, 0)))
    pad_rows = lambda a: jnp.pad(a, ((0, ms - nbs), (0, 0)))

    expand = jnp.asarray(np.kron(np.eye(LANES, SSM_HEADS, dtype=np.float32),
                                 np.ones((1, SSM_P), np.float32)), BF16)
    outs = {}

    def ffn_ln(i, which, ln_idx, x_p, xb_p, x_s, xb_s):
        wd = _cast_w(ffn_w_down, (i, which))
        h_p, h_s = _mm_swiglu(xb_p, xb_s, ffn_w_gate_up, (i, which), tm, 512)
        return _mm_ln(h_p, h_s, wd, x_p, x_s, ln_g[i, ln_idx], ln_b[i, ln_idx], 0.5, tm_ln, "ffn_down_ln")

    for i in range(DEPTH):
        j = i // N_MIXERS
        (x_p, xb_p), (x_s, xb_s) = ffn_ln(i, 0, 0, x_p, xb_p, x_s, xb_s)

        if i % N_MIXERS == 0:
            w_small = _pad_cols(att_w_in[j][:, ATT_MAIN:], LANES)
            w_o = _cast_w(att_w_o, (j,))
            qw, kvw = ATT_HEADS * ATT_HD, ATT_KV * ATT_HD
            (proj, projb), (proj_s, _) = _proj(xb_p, xb_s, att_w_in, (j,), ATT_MAIN, tm, 1024, with_bf16=True,
                                               name="att_in")
            proj2, proj2_s = _proj(xb_p, xb_s, w_small, (), LANES, tm, LANES, name="att_in_idx")
            ik, ikb = _ik_norm(proj2, att_idx_k_norm[j], tm)
            ik_s, _ = _ik_norm(proj2_s, att_idx_k_norm[j], ms)
            o_p = _dsa_prompt(proj, projb, proj2, ikb, nbp, seq)
            outs.setdefault("k_p", []).append(proj[:, qw:qw + kvw].reshape(nbp, seq, ATT_KV, ATT_HD))
            outs.setdefault("v_p", []).append(proj[:, qw + kvw:qw + 2 * kvw].reshape(nbp, seq, ATT_KV, ATT_HD))
            outs.setdefault("ik_p", []).append(ik.reshape(nbp, seq, IDX_DIM))
            pr = proj_s[:nbs]
            k_new = pr[:, qw:qw + kvw]
            v_new = pr[:, qw + kvw:qw + 2 * kvw]
            ik_new = ik_s[:nbs]
            scores = _dsa_s_scores(page_table,
                                   pr[:, qw + 2 * kvw:].reshape(nbs, IDX_HEADS, IDX_DIM),
                                   proj2_s[:nbs, IDX_DIM:IDX_DIM + IDX_HEADS].reshape(nbs, IDX_HEADS, 1),
                                   ik_new.reshape(nbs, 1, IDX_DIM), cache_idx_k, j)
            idx, meta = _dsa_s_select(scores, page_table)
            o_s = _dsa_s_attend(idx, meta, pr[:, :qw].reshape(nbs, ATT_HEADS, ATT_HD),
                                k_new.reshape(nbs, 1, kvw), v_new.reshape(nbs, 1, kvw), cache_k, cache_v, j)
            o_s = pad_rows(o_s.reshape(nbs, qw)).astype(BF16)
            outs.setdefault("k_s", []).append(k_new.reshape(nbs, 1, ATT_KV, ATT_HD))
            outs.setdefault("v_s", []).append(v_new.reshape(nbs, 1, ATT_KV, ATT_HD))
            outs.setdefault("ik_s", []).append(ik_new.reshape(nbs, 1, IDX_DIM))
        elif i % N_MIXERS == 1:
            w_o = _cast_w(hg_w_o, (j,))
            proj, proj_s = _proj(xb_p, xb_s, hg_w_in, (j,), hg_w_in.shape[-1], tm, 1024, name="hg_in")
            s0 = jnp.zeros((nbp, HG_HEADS, HG_DK, HG_DV), F32)
            o_p, s_fin = _hgrn(proj.reshape(nbp, seq, -1), hg_lb_logits, hg_norm_g[j], s0, i, seq, chunk)
            o_p = o_p.reshape(mp, -1)
            outs.setdefault("hg_p", []).append(s_fin)
            pr = _pad_time(proj_s[:nbs].reshape(nbs, 1, -1), chunk)
            o_s, s_fin = _hgrn(pr, hg_lb_logits, hg_norm_g[j], state_hgrn[j], i, 1, chunk)
            o_s = pad_rows(o_s[:, 0, :])
            outs.setdefault("hg_s", []).append(s_fin)
        else:
            w_small = _pad_cols(ssm_w_in[j][:, SSM_MAIN:], LANES)
            w_o = _cast_w(ssm_w_o, (j,))
            dt_bias = _pad_cols(ssm_dt_bias[j].reshape(1, SSM_HEADS), LANES)
            a_log = _pad_cols(ssm_a_log[j].reshape(1, SSM_HEADS), LANES)
            d_x = jnp.repeat(ssm_d[j], SSM_P).reshape(1, SSM_INNER)
            zx_p, zx_s = _proj(xb_p, xb_s, ssm_w_in, (j,), SSM_MAIN, tm, 1024, name="ssm_in")
            dtr_p, dtr_s = _proj(xb_p, xb_s, w_small, (), LANES, tm, LANES, name="ssm_in_dt")
            mix = {}
            for name, zx, dtr in (("p", zx_p, dtr_p), ("s", zx_s, dtr_s)):
                if name == "p":
                    nb_, t_, c_conv = nbp, seq, 4 * chunk
                    zx3 = zx.reshape(nbp, seq, -1)
                    dt3 = dtr.reshape(nbp, seq, LANES)
                    cs = jnp.zeros((nbp, SUBLANES, SSM_CH), F32)
                    s0 = jnp.zeros((nbp, SSM_INNER, SSM_N), F32)
                else:
                    nb_, t_, c_conv = nbs, 1, SUBLANES
                    zx3 = _pad_time(zx[:nbs].reshape(nbs, 1, -1), SUBLANES)
                    dt3 = _pad_time(dtr[:nbs].reshape(nbs, 1, LANES), chunk)
                    cs = jnp.pad(state_conv[j], ((0, 0), (SUBLANES - (SSM_CONV - 1), 0), (0, 0)))
                    s0 = state_ssm[j].reshape(nbs, SSM_INNER, SSM_N)
                xc, new_conv = _ssd_conv(zx3, cs, ssm_conv_w[j], ssm_conv_b[j], t_, c_conv, cw=1024)
                if name == "p":
                    z3 = zx3
                else:
                    xc = _pad_time(xc, chunk)
                    z3 = _pad_time(zx3[:, :, :SSM_INNER], chunk)
                y, s_fin = _ssd(xc, z3, dt3, expand, dt_bias, a_log, d_x, ssm_norm_g[j], s0, t_, chunk)
                s_fin = s_fin.reshape(nb_, SSM_HEADS, SSM_P, SSM_N)
                new_conv = new_conv[:, :SSM_CONV - 1, :]
                if name == "p":
                    mix[name] = y.reshape(mp, SSM_INNER)
                    outs.setdefault("ssm_p", []).append(s_fin)
                    outs.setdefault("conv_p", []).append(new_conv)
                else:
                    mix[name] = pad_rows(y[:, 0, :])
                    outs.setdefault("ssm_s", []).append(s_fin)
                    outs.setdefault("conv_s", []).append(new_conv)
            o_p, o_s = mix["p"], mix["s"]

        (x_p, xb_p), (x_s, xb_s) = _mm_ln(o_p, o_s, w_o, x_p, x_s, ln_g[i, 1], ln_b[i, 1], 1.0, tm_ln, "mixer_out_ln")
        (x_p, xb_p), (x_s, xb_s) = ffn_ln(i, 1, 2, x_p, xb_p, x_s, xb_s)
        (x_p, xb_p), (x_s, xb_s) = _mm_ple(x_p, xb_p, pl_p[i].astype(BF16), x_s, xb_s, pl_s[i].astype(BF16),
                                           ple_w_gate, ple_w_proj, (i,), tm, 512)

    y_prompt = x_p.reshape(nbp, seq, d)
    y_sample = x_s[:nbs].reshape(nbs, 1, d)
    stack = lambda key: jnp.stack(outs[key])
    return (y_prompt, y_sample, stack("k_p"), stack("v_p"), stack("ik_p"), stack("k_s"), stack("v_s"), stack("ik_s"),
            stack("hg_p"), stack("hg_s"), stack("ssm_p"), stack("ssm_s"), stack("conv_p"), stack("conv_s"))
```

```python
import functools
import math

import jax
import jax.numpy as jnp
import numpy as np
from jax import lax
from jax.experimental import pallas as pl
from jax.experimental.pallas import tpu as pltpu

F32 = jnp.float32
BF16 = jnp.bfloat16

D_MODEL = 2048
DEPTH = 4
N_MIXERS = 3
D_FF = 2 * D_MODEL
PLE_DIM = 256
ALPHA = (2 * DEPTH) ** 0.25
LN_EPS = 1e-5
RMS_EPS = 1e-6
PAGE = 128

ATT_HD = 128
ATT_HEADS = 16
ATT_KV = 4
ATT_GROUP = ATT_HEADS // ATT_KV
IDX_HEADS = 16
IDX_DIM = 64
IDX_W_SCALE = (IDX_HEADS ** -0.5) * (IDX_DIM ** -0.5)
TOPK_MAX = 256
ATT_SCALE = ATT_HD ** -0.5
LOG2E = math.log2(math.e)
ATT_MAIN = ATT_HEADS * ATT_HD + 2 * ATT_KV * ATT_HD + IDX_HEADS * IDX_DIM

HG_HEADS = 16
HG_DK = 128
HG_DV = 128

SSM_INNER = 2 * D_MODEL
SSM_P = 64
SSM_HEADS = SSM_INNER // SSM_P
SSM_GROUPS = 8
SSM_HPG = SSM_HEADS // SSM_GROUPS
SSM_N = 128
SSM_CONV = 4
SSM_CH = SSM_INNER + 2 * SSM_GROUPS * SSM_N
SSM_MAIN = SSM_INNER + SSM_CH

LANES = 128
SUBLANES = 8
VMEM_LIMIT_BYTES = 56 * 1024 * 1024

INT_MIN = np.int32(-2 ** 31)
NEG_BIG = -1e30


def _cp(*sem):
    return pltpu.CompilerParams(dimension_semantics=sem, vmem_limit_bytes=VMEM_LIMIT_BYTES)


def _nt(a, b):
    return lax.dot_general(a, b, (((1,), (1,)), ((), ())), preferred_element_type=F32)


def _tn(a, b):
    return lax.dot_general(a, b, (((0,), (0,)), ((), ())), preferred_element_type=F32)


def _silu(x):
    return x * jax.nn.sigmoid(x)


def _wspec(k, tn, idx, col_block):
    lead = (None,) * len(idx)
    return pl.BlockSpec(lead + (k, tn), lambda n, i: idx + (0, col_block(n)))


def _proj_body(x_ref, xs_ref, w_ref, *rest, with_bf16):
    wb_ref = rest[-1]
    outs, outs_s = (rest[0:2], rest[2:4]) if with_bf16 else (rest[0:1], rest[1:2])

    def emit(refs, acc):
        refs[0][...] = acc
        if with_bf16:
            refs[1][...] = acc.astype(BF16)

    @pl.when(pl.program_id(1) == 0)
    def _():
        wb_ref[...] = w_ref[...].astype(BF16)
        emit(outs_s, jnp.dot(xs_ref[...], wb_ref[...], preferred_element_type=F32))

    emit(outs, jnp.dot(x_ref[...], wb_ref[...], preferred_element_type=F32))


def _proj(xb, xsb, w, idx, n_cols, tm, tn, with_bf16=False, name="proj"):
    m, k = xb.shape
    ms = xsb.shape[0]
    dts = (F32, BF16) if with_bf16 else (F32,)
    res = pl.pallas_call(
        functools.partial(_proj_body, with_bf16=with_bf16),
        grid=(n_cols // tn, m // tm),
        in_specs=[pl.BlockSpec((tm, k), lambda n, i: (i, 0)),
                  pl.BlockSpec((ms, k), lambda n, i: (0, 0)),
                  _wspec(k, tn, idx, lambda n: n)],
        out_specs=[pl.BlockSpec((tm, tn), lambda n, i: (i, n)) for _ in dts]
                  + [pl.BlockSpec((ms, tn), lambda n, i: (0, n)) for _ in dts],
        out_shape=[jax.ShapeDtypeStruct((m, n_cols), dt) for dt in dts]
                  + [jax.ShapeDtypeStruct((ms, n_cols), dt) for dt in dts],
        scratch_shapes=[pltpu.VMEM((k, tn), BF16)],
        compiler_params=_cp("parallel", "arbitrary"),
        name=name,
    )(xb, xsb, w)
    nd = len(dts)
    return (res[:nd], res[nd:]) if with_bf16 else (res[0], res[1])


def _swiglu_body(x_ref, xs_ref, wg_ref, wu_ref, o_ref, os_ref, wgb_ref, wub_ref):
    def swiglu(x):
        g = jnp.dot(x, wgb_ref[...], preferred_element_type=F32)
        u = jnp.dot(x, wub_ref[...], preferred_element_type=F32)
        return (_silu(g) * u).astype(BF16)

    @pl.when(pl.program_id(1) == 0)
    def _():
        wgb_ref[...] = wg_ref[...].astype(BF16)
        wub_ref[...] = wu_ref[...].astype(BF16)
        os_ref[...] = swiglu(xs_ref[...])

    o_ref[...] = swiglu(x_ref[...])


def _mm_swiglu(xb, xsb, wgu, idx, tm, tn):
    m, k = xb.shape
    ms = xsb.shape[0]
    f = wgu.shape[-1] // 2
    nj = f // tn
    return pl.pallas_call(
        _swiglu_body,
        grid=(nj, m // tm),
        in_specs=[pl.BlockSpec((tm, k), lambda n, i: (i, 0)),
                  pl.BlockSpec((ms, k), lambda n, i: (0, 0)),
                  _wspec(k, tn, idx, lambda n: n),
                  _wspec(k, tn, idx, lambda n: n + nj)],
        out_specs=[pl.BlockSpec((tm, tn), lambda n, i: (i, n)),
                   pl.BlockSpec((ms, tn), lambda n, i: (0, n))],
        out_shape=[jax.ShapeDtypeStruct((m, f), BF16), jax.ShapeDtypeStruct((ms, f), BF16)],
        scratch_shapes=[pltpu.VMEM((k, tn), BF16), pltpu.VMEM((k, tn), BF16)],
        compiler_params=_cp("parallel", "arbitrary"),
        name="ffn_up",
    )(xb, xsb, wgu, wgu)


def _cast_body(w_ref, o_ref):
    o_ref[...] = w_ref[...].astype(BF16)


def _cast_w(w, idx, tk=512):
    k, n = w.shape[-2:]
    lead = (None,) * len(idx)
    return pl.pallas_call(
        _cast_body,
        grid=(k // tk,),
        in_specs=[pl.BlockSpec(lead + (tk, n), lambda i: idx + (i, 0))],
        out_specs=pl.BlockSpec((tk, n), lambda i: (i, 0)),
        out_shape=jax.ShapeDtypeStruct((k, n), BF16),
        compiler_params=_cp("parallel"),
        name="cast_w",
    )(w)


LN_SPLIT = 2


def _mm_ln_body(a_ref, as_ref, w_ref, r_ref, rs_ref, g_ref, b_ref, o_ref, ob_ref, os_ref, osb_ref, *, scale, tm):
    g = g_ref[...]
    b = b_ref[...]

    def ln_rows(a, r, o, ob, sl):
        y = ALPHA * r[sl, :] + scale * jnp.dot(a[sl, :], w_ref[...], preferred_element_type=F32)
        mu = jnp.mean(y, axis=-1, keepdims=True)
        yc = y - mu
        var = jnp.mean(yc * yc, axis=-1, keepdims=True)
        out = yc * lax.rsqrt(var + LN_EPS) * g + b
        o[sl, :] = out
        ob[sl, :] = out.astype(BF16)

    @pl.when(pl.program_id(0) == 0)
    def _():
        ln_rows(as_ref, rs_ref, os_ref, osb_ref, pl.ds(0, as_ref.shape[0]))

    rows = tm // LN_SPLIT
    for r in range(0, tm, rows):
        ln_rows(a_ref, r_ref, o_ref, ob_ref, pl.ds(r, rows))


def _mm_ln(ab, asb, wb, res, res_s, g, b, scale, tm, name):
    m, kdim = ab.shape
    ms = asb.shape[0]
    n = wb.shape[1]
    res = pl.pallas_call(
        functools.partial(_mm_ln_body, scale=scale, tm=tm),
        grid=(m // tm,),
        in_specs=[pl.BlockSpec((tm, kdim), lambda i: (i, 0)),
                  pl.BlockSpec((ms, kdim), lambda i: (0, 0)),
                  pl.BlockSpec((kdim, n), lambda i: (0, 0), pipeline_mode=pl.Buffered(1)),
                  pl.BlockSpec((tm, n), lambda i: (i, 0)),
                  pl.BlockSpec((ms, n), lambda i: (0, 0)),
                  pl.BlockSpec((1, n), lambda i: (0, 0)),
                  pl.BlockSpec((1, n), lambda i: (0, 0))],
        out_specs=[pl.BlockSpec((tm, n), lambda i: (i, 0)),
                   pl.BlockSpec((tm, n), lambda i: (i, 0)),
                   pl.BlockSpec((ms, n), lambda i: (0, 0)),
                   pl.BlockSpec((ms, n), lambda i: (0, 0))],
        out_shape=[jax.ShapeDtypeStruct((m, n), F32), jax.ShapeDtypeStruct((m, n), BF16),
                   jax.ShapeDtypeStruct((ms, n), F32), jax.ShapeDtypeStruct((ms, n), BF16)],
        compiler_params=_cp("arbitrary"),
        name=name,
    )(ab, asb, wb, res, res_s, g.reshape(1, n), b.reshape(1, n))
    return res[:2], res[2:]


def _ple_body(xb_ref, p_ref, x_ref, xsb_ref, ps_ref, xs_ref, wg_ref, wp_ref, o_ref, ob_ref, os_ref, osb_ref,
              wgb_ref, wpb_ref):
    def ple(xb, p, x, o, ob):
        gate = jax.nn.sigmoid(jnp.dot(xb[...], wgb_ref[...], preferred_element_type=F32))
        proj = jnp.dot(p[...], wpb_ref[...], preferred_element_type=F32)
        out = x[...] + gate * proj
        o[...] = out
        ob[...] = out.astype(BF16)

    @pl.when(pl.program_id(1) == 0)
    def _():
        wgb_ref[...] = wg_ref[...].astype(BF16)
        wpb_ref[...] = wp_ref[...].astype(BF16)
        ple(xsb_ref, ps_ref, xs_ref, os_ref, osb_ref)

    ple(xb_ref, p_ref, x_ref, o_ref, ob_ref)


def _mm_ple(x32, xb, pb, xs32, xsb, psb, wg, wp, idx, tm, tn):
    m, d = xb.shape
    ms = xsb.shape[0]
    pd = pb.shape[1]
    res = pl.pallas_call(
        _ple_body,
        grid=(d // tn, m // tm),
        in_specs=[pl.BlockSpec((tm, d), lambda n, i: (i, 0)),
                  pl.BlockSpec((tm, pd), lambda n, i: (i, 0)),
                  pl.BlockSpec((tm, tn), lambda n, i: (i, n)),
                  pl.BlockSpec((ms, d), lambda n, i: (0, 0)),
                  pl.BlockSpec((ms, pd), lambda n, i: (0, 0)),
                  pl.BlockSpec((ms, tn), lambda n, i: (0, n)),
                  _wspec(d, tn, idx, lambda n: n),
                  _wspec(pd, tn, idx, lambda n: n)],
        out_specs=[pl.BlockSpec((tm, tn), lambda n, i: (i, n)),
                   pl.BlockSpec((tm, tn), lambda n, i: (i, n)),
                   pl.BlockSpec((ms, tn), lambda n, i: (0, n)),
                   pl.BlockSpec((ms, tn), lambda n, i: (0, n))],
        out_shape=[jax.ShapeDtypeStruct((m, d), F32), jax.ShapeDtypeStruct((m, d), BF16),
                   jax.ShapeDtypeStruct((ms, d), F32), jax.ShapeDtypeStruct((ms, d), BF16)],
        scratch_shapes=[pltpu.VMEM((d, tn), BF16), pltpu.VMEM((pd, tn), BF16)],
        compiler_params=_cp("parallel", "arbitrary"),
        name="ple",
    )(xb, pb, x32, xsb, psb, xs32, wg, wp)
    return res[:2], res[2:]


def _ik_norm_body(p_ref, g_ref, o_ref, ob_ref):
    x = p_ref[...][:, :IDX_DIM]
    mu = jnp.mean(x, axis=-1, keepdims=True)
    xc = x - mu
    out = xc * lax.rsqrt(jnp.mean(xc * xc, axis=-1, keepdims=True) + LN_EPS) * g_ref[...]
    o_ref[...] = out
    ob_ref[...] = out.astype(BF16)


def _ik_norm(proj2, ik_g, tm):
    m = proj2.shape[0]
    return pl.pallas_call(
        _ik_norm_body,
        grid=(m // tm,),
        in_specs=[pl.BlockSpec((tm, LANES), lambda i: (i, 0)),
                  pl.BlockSpec((1, IDX_DIM), lambda i: (0, 0))],
        out_specs=[pl.BlockSpec((tm, IDX_DIM), lambda i: (i, 0)),
                   pl.BlockSpec((tm, IDX_DIM), lambda i: (i, 0))],
        out_shape=[jax.ShapeDtypeStruct((m, IDX_DIM), F32), jax.ShapeDtypeStruct((m, IDX_DIM), BF16)],
        compiler_params=_cp("parallel"),
        name="idx_k_norm",
    )(proj2, ik_g.reshape(1, IDX_DIM))


def _order_key(x):
    bits = pltpu.bitcast(x, jnp.int32)
    return jnp.where(bits < 0, bits ^ jnp.int32(0x7FFFFFFF), bits)


_RADIX_BITS = [INT_MIN] + [np.int32(1 << s) for s in range(30, -1, -1)]


RADIX_ROWS = 128
KNORM_ROWS = 512
SUM_FLOOR = 2.0 ** -100


def _dsa_prompt_body(q_ref, iq_ref, iw_ref, k_ref, v_ref, ik_ref, o_ref, keys_ref, knorm_ref, *, qb, kc, topk):
    i = pl.program_id(1)
    nck = ((i + 1) * qb + kc - 1) // kc
    row_pos = i * qb + lax.broadcasted_iota(jnp.int32, (qb, 1), 0)
    iq = iq_ref[...]
    iw = iw_ref[...][:, IDX_DIM:IDX_DIM + IDX_HEADS] * IDX_W_SCALE
    iq_h = [iq[:, h * IDX_DIM:(h + 1) * IDX_DIM] for h in range(IDX_HEADS)]
    iw_h = [iw[:, h:h + 1] for h in range(IDX_HEADS)]
    col0 = lax.broadcasted_iota(jnp.int32, (1, kc), 1)

    def score_chunk(c, carry):
        off = pl.multiple_of(c * kc, kc)
        ikc = ik_ref[pl.ds(off, kc), :]
        sc = jnp.zeros((qb, kc), F32)
        for h in range(IDX_HEADS):
            sc = sc + iw_h[h] * jnp.maximum(_nt(iq_h[h], ikc), 0.0)
        key = jnp.where(col0 + off <= row_pos, _order_key(sc), INT_MIN)
        keys_ref[:, pl.ds(off, kc)] = key
        return carry

    lax.fori_loop(0, nck, score_chunk, 0)

    nblk = qb // RADIX_ROWS
    los = [jnp.full((RADIX_ROWS, 1), INT_MIN, jnp.int32) for _ in range(nblk)]
    for bit in _RADIX_BITS:
        cands = [lo + bit for lo in los]
        accs = []
        for blk in range(nblk):
            cand_b = jnp.broadcast_to(cands[blk], (RADIX_ROWS, LANES))

            def count_chunk(c, acc, cand_b=cand_b, r0=blk * RADIX_ROWS):
                off = pl.multiple_of(c * kc, kc)
                for t in range(kc // LANES):
                    tile = keys_ref[r0:r0 + RADIX_ROWS, pl.ds(off + t * LANES, LANES)]
                    acc = acc + jnp.where(tile >= cand_b, 1.0, 0.0)
                return acc

            accs.append(lax.fori_loop(0, nck, count_chunk, jnp.zeros((RADIX_ROWS, LANES), F32)))
        cnts = [jnp.sum(acc, axis=-1, keepdims=True) for acc in accs]
        los = [jnp.where(cnts[blk] >= topk, cands[blk], los[blk]) for blk in range(nblk)]
    thr = jnp.maximum(jnp.concatenate(los, axis=0), INT_MIN + 1)

    @pl.when(i == 0)
    def _():
        for g in range(ATT_KV):
            best = jnp.zeros((1, 1), F32)
            for r0 in range(0, k_ref.shape[0], KNORM_ROWS):
                kk = k_ref[r0:r0 + KNORM_ROWS, g * ATT_HD:(g + 1) * ATT_HD].astype(F32)
                best = jnp.maximum(best, jnp.max(jnp.sum(kk * kk, axis=-1, keepdims=True), axis=0, keepdims=True))
            knorm_ref[g:g + 1, :] = jnp.broadcast_to(jnp.sqrt(best), (1, LANES))

    def store(g, out):
        for j in range(ATT_GROUP):
            h = g * ATT_GROUP + j
            o_ref[:, h * ATT_HD:(h + 1) * ATT_HD] = out[j].astype(o_ref.dtype)

    for g in range(ATT_KV):
        qg = jnp.concatenate(
            [q_ref[:, (g * ATT_GROUP + j) * ATT_HD:(g * ATT_GROUP + j + 1) * ATT_HD] for j in range(ATT_GROUP)], axis=0)
        qg = (qg * (ATT_SCALE * LOG2E)).astype(BF16)
        q32 = qg.astype(F32)
        shift = (jnp.sqrt(jnp.sum(q32 * q32, axis=-1, keepdims=True)) * knorm_ref[g:g + 1, 0:1]).reshape(
            ATT_GROUP, qb, 1)

        def chunk_operands(c, g=g):
            off = pl.multiple_of(c * kc, kc)
            kch = k_ref[pl.ds(off, kc), g * ATT_HD:(g + 1) * ATT_HD]
            vch = v_ref[pl.ds(off, kc), g * ATT_HD:(g + 1) * ATT_HD]
            sel = (keys_ref[:, pl.ds(off, kc)] >= thr)[None]
            return kch, vch, sel

        def fixed_chunk(c, carry, qg=qg, shift=shift):
            l, acc = carry
            kch, vch, sel = chunk_operands(c)
            p = jnp.exp2(jnp.where(sel, _nt(qg, kch).reshape(ATT_GROUP, qb, kc) - shift, -jnp.inf))
            l = l + jnp.sum(p, axis=-1, keepdims=True)
            pv = jnp.dot(p.reshape(ATT_GROUP * qb, kc).astype(BF16), vch, preferred_element_type=F32)
            return l, acc + pv.reshape(ATT_GROUP, qb, ATT_HD)

        l, acc = lax.fori_loop(0, nck, fixed_chunk, (jnp.zeros((ATT_GROUP, qb, 1), F32),
                                                     jnp.zeros((ATT_GROUP, qb, ATT_HD), F32)))
        healthy = jnp.min(l) > SUM_FLOOR

        @pl.when(healthy)
        def _(g=g, l=l, acc=acc):
            store(g, acc / l)

        @pl.when(jnp.logical_not(healthy))
        def _(g=g, qg=qg):
            def running_chunk(c, carry):
                m, l, acc = carry
                kch, vch, sel = chunk_operands(c)
                s = jnp.where(sel, _nt(qg, kch).reshape(ATT_GROUP, qb, kc), -jnp.inf)
                m_new = jnp.maximum(m, jnp.max(s, axis=-1, keepdims=True))
                p = jnp.exp2(s - m_new)
                a = jnp.exp2(m - m_new)
                l = a * l + jnp.sum(p, axis=-1, keepdims=True)
                pv = jnp.dot(p.reshape(ATT_GROUP * qb, kc).astype(BF16), vch, preferred_element_type=F32)
                return m_new, l, a * acc + pv.reshape(ATT_GROUP, qb, ATT_HD)

            init = (jnp.full((ATT_GROUP, qb, 1), NEG_BIG, F32), jnp.zeros((ATT_GROUP, qb, 1), F32),
                    jnp.zeros((ATT_GROUP, qb, ATT_HD), F32))
            _, l, acc = lax.fori_loop(0, nck, running_chunk, init)
            store(g, acc / l)


def _dsa_prompt(proj, projb, proj2, ikb, nb, s, qb=256, kc=512):
    nq = s // qb
    topk = min(TOPK_MAX, s // 4)
    qw = ATT_HEADS * ATT_HD
    kvw = ATT_KV * ATT_HD
    iqw = IDX_HEADS * IDX_DIM
    return pl.pallas_call(
        functools.partial(_dsa_prompt_body, qb=qb, kc=kc, topk=topk),
        grid=(nb, nq),
        in_specs=[pl.BlockSpec((qb, qw), lambda b, i: (b * nq + i, 0)),
                  pl.BlockSpec((qb, iqw), lambda b, i: (b * nq + i, (qw + 2 * kvw) // iqw)),
                  pl.BlockSpec((qb, LANES), lambda b, i: (b * nq + i, 0)),
                  pl.BlockSpec((s, kvw), lambda b, i: (b, qw // kvw)),
                  pl.BlockSpec((s, kvw), lambda b, i: (b, qw // kvw + 1)),
                  pl.BlockSpec((s, IDX_DIM), lambda b, i: (b, 0))],
        out_specs=pl.BlockSpec((qb, qw), lambda b, i: (b * nq + i, 0)),
        out_shape=jax.ShapeDtypeStruct((nb * s, qw), BF16),
        scratch_shapes=[pltpu.VMEM((qb, s), jnp.int32), pltpu.VMEM((SUBLANES, LANES), F32)],
        compiler_params=_cp("parallel", "arbitrary"),
        name="dsa_prompt",
    )(proj, projb, proj2, projb, projb, ikb)


SCORE_PAGES = 16


def _dsa_s_scores_body(pt_ref, iq_ref, iw_ref, ikn_ref, *rest, n_pages):
    page_refs, o_ref = rest[:SCORE_PAGES], rest[SCORE_PAGES]
    p = pl.program_id(1)
    iq = iq_ref[...].astype(BF16)
    iw = iw_ref[...] * IDX_W_SCALE

    @pl.when(p == 0)
    def _():
        own = jnp.broadcast_to(ikn_ref[...], (SUBLANES, IDX_DIM)).astype(BF16)
        d = jnp.maximum(_nt(iq, own), 0.0)
        sc = jnp.sum(iw * d, axis=0, keepdims=True)
        o_ref[n_pages:n_pages + SUBLANES, :] = jnp.broadcast_to(sc[:, 0:1], (SUBLANES, LANES))

    for r in range(SCORE_PAGES):
        d = jnp.maximum(_nt(iq, page_refs[r][...].astype(BF16)), 0.0)
        o_ref[pl.ds(p * SCORE_PAGES + r, 1), :] = jnp.sum(iw * d, axis=0, keepdims=True)


def _dsa_s_scores(page_table, iq, iw, ik_new, cache_ik, layer):
    nb, n_pages = page_table.shape
    page_spec = lambda r: pl.BlockSpec((None, None, PAGE, IDX_DIM),
                                       lambda b, p, pt: (layer, pt[b, p * SCORE_PAGES + r], 0, 0))
    return pl.pallas_call(
        functools.partial(_dsa_s_scores_body, n_pages=n_pages),
        grid_spec=pltpu.PrefetchScalarGridSpec(
            num_scalar_prefetch=1,
            grid=(nb, n_pages // SCORE_PAGES),
            in_specs=[pl.BlockSpec((None, IDX_HEADS, IDX_DIM), lambda b, p, pt: (b, 0, 0)),
                      pl.BlockSpec((None, IDX_HEADS, 1), lambda b, p, pt: (b, 0, 0)),
                      pl.BlockSpec((None, 1, IDX_DIM), lambda b, p, pt: (b, 0, 0))]
                     + [page_spec(r) for r in range(SCORE_PAGES)],
            out_specs=pl.BlockSpec((None, n_pages + SUBLANES, LANES), lambda b, p, pt: (b, 0, 0)),
        ),
        out_shape=jax.ShapeDtypeStruct((nb, n_pages + SUBLANES, LANES), F32),
        compiler_params=_cp("parallel", "arbitrary"),
        name="dsa_sample_scores",
    )(page_table, iq, iw, ik_new, *([cache_ik] * SCORE_PAGES))


def _dsa_s_select_body(sc_ref, ptc_ref, idx_ref, meta_ref, rank_ref, phys_ref, *, n_pages, topk):
    shape = (n_pages + SUBLANES, LANES)
    rows = lax.broadcasted_iota(jnp.int32, shape, 0)
    cols = lax.broadcasted_iota(jnp.int32, shape, 1)
    live = (rows < n_pages) | ((rows == n_pages) & (cols == 0))
    keys = jnp.where(live, _order_key(sc_ref[...]), INT_MIN)
    lo = jnp.full((1, 1), INT_MIN, jnp.int32)
    for bit in _RADIX_BITS:
        cand = lo + bit
        cnt = jnp.sum(jnp.sum(jnp.where(keys >= cand, 1.0, 0.0), axis=0, keepdims=True), axis=1, keepdims=True)
        lo = jnp.where(cnt >= topk, cand, lo)
    sel = keys >= jnp.maximum(lo, INT_MIN + 1)
    sel_c = sel[:n_pages, :]
    own = jnp.where(sel[n_pages:n_pages + 1, 0:1], 1, 0)

    r_i = lax.broadcasted_iota(jnp.int32, (LANES, LANES), 0)
    c_i = lax.broadcasted_iota(jnp.int32, (LANES, LANES), 1)
    ones_le = jnp.where(r_i <= c_i, 1.0, 0.0).astype(BF16)
    ones_gt = jnp.where(r_i > c_i, 1.0, 0.0).astype(BF16)
    m = jnp.where(sel_c, 1.0, 0.0).astype(BF16)
    within = jnp.dot(m, ones_le, preferred_element_type=F32)
    tot = jnp.broadcast_to(within[:, LANES - 1:LANES], (n_pages, LANES)).astype(BF16)
    before = jnp.dot(ones_gt, tot, preferred_element_type=F32)
    rank_ref[...] = jnp.where(sel_c, (within + before).astype(jnp.int32) - 1, -1)
    phys_ref[...] = (ptc_ref[...] * PAGE + c_i[:n_pages, :]).astype(F32)
    n_sel = (before[n_pages - 1:n_pages, 0:1] + within[n_pages - 1:n_pages, LANES - 1:LANES]).astype(jnp.int32)

    slot = lax.broadcasted_iota(jnp.int32, (topk, 1), 0)

    def place(r, acc):
        return acc + jnp.where(rank_ref[pl.ds(r, 1), :] == slot, phys_ref[pl.ds(r, 1), :], 0.0)

    acc = lax.fori_loop(0, n_pages, place, jnp.zeros((topk, LANES), F32), unroll=8)
    idx_ref[...] = jnp.sum(acc, axis=-1, keepdims=True).astype(jnp.int32)
    mrow = lax.broadcasted_iota(jnp.int32, (SUBLANES, LANES), 0)
    meta_ref[...] = jnp.where(mrow == 0, jnp.minimum(n_sel, topk), own)


def _dsa_s_select(scores, page_table):
    nb, n_pages = page_table.shape
    assert n_pages == LANES
    topk = min(TOPK_MAX, (n_pages * PAGE + 1) // 4)
    idx, meta = pl.pallas_call(
        functools.partial(_dsa_s_select_body, n_pages=n_pages, topk=topk),
        grid=(nb,),
        in_specs=[pl.BlockSpec((None, n_pages + SUBLANES, LANES), lambda b: (b, 0, 0)),
                  pl.BlockSpec((None, n_pages, 1), lambda b: (b, 0, 0))],
        out_specs=[pl.BlockSpec((None, topk, 1), lambda b: (b, 0, 0)),
                   pl.BlockSpec((None, SUBLANES, LANES), lambda b: (b, 0, 0))],
        out_shape=[jax.ShapeDtypeStruct((nb, topk, 1), jnp.int32),
                   jax.ShapeDtypeStruct((nb, SUBLANES, LANES), jnp.int32)],
        scratch_shapes=[pltpu.VMEM((n_pages, LANES), jnp.int32), pltpu.VMEM((n_pages, LANES), F32)],
        compiler_params=_cp("parallel"),
        name="dsa_sample_select",
    )(scores, page_table.reshape(nb, n_pages, 1))
    return idx.reshape(nb, topk), meta[:, 0:2, 0]


def _dsa_s_attend_body(idx_ref, meta_ref, q_ref, kn_ref, vn_ref, ck_ref, cv_ref, o_ref, kbuf, vbuf, sem,
                       *, layer, topk):
    b = pl.program_id(0)
    nb = pl.num_programs(0)
    slot = b % 2

    def row_copies(tok, buf_slot, j):
        row = idx_ref[tok, j]
        page, off = row // PAGE, row % PAGE
        dst = pl.ds(j * ATT_KV, ATT_KV)
        return (pltpu.make_async_copy(ck_ref.at[layer, page, off], kbuf.at[buf_slot, dst, :], sem.at[buf_slot, 0]),
                pltpu.make_async_copy(cv_ref.at[layer, page, off], vbuf.at[buf_slot, dst, :], sem.at[buf_slot, 1]))

    def start_gather(tok, buf_slot):
        def body(j, carry):
            for cp in row_copies(tok, buf_slot, j):
                cp.start()
            return carry
        lax.fori_loop(0, topk, body, 0)

    @pl.when(b == 0)
    def _():
        start_gather(0, 0)

    @pl.when(b + 1 < nb)
    def _():
        start_gather(b + 1, 1 - slot)

    def wait_body(j, carry):
        for cp in row_copies(b, slot, j):
            cp.wait()
        return carry

    lax.fori_loop(0, topk, wait_body, 0)

    n_sel = meta_ref[b, 0]
    own = meta_ref[b, 1] > 0
    qb = q_ref[...].astype(BF16)
    ncol = topk * ATT_KV
    col = lax.broadcasted_iota(jnp.int32, (ATT_HEADS, ncol), 1)
    head = lax.broadcasted_iota(jnp.int32, (ATT_HEADS, ncol), 0)
    valid = (col % ATT_KV == head // ATT_GROUP) & (col // ATT_KV < n_sel)
    s = jnp.where(valid, _nt(qb, kbuf[slot].astype(BF16)) * ATT_SCALE, NEG_BIG)
    kn = kn_ref[...].astype(BF16).astype(F32)
    vn = vn_ref[...].astype(BF16).astype(F32)
    expand = lambda a: jnp.concatenate(
        [jnp.broadcast_to(a[:, g * ATT_HD:(g + 1) * ATT_HD], (ATT_GROUP, ATT_HD)) for g in range(ATT_KV)], axis=0)
    s_own = jnp.where(own, jnp.sum(qb.astype(F32) * expand(kn), axis=-1, keepdims=True) * ATT_SCALE, NEG_BIG)
    m = jnp.maximum(jnp.max(s, axis=-1, keepdims=True), s_own)
    p = jnp.where(valid, jnp.exp(s - m), 0.0)
    p_own = jnp.where(own, jnp.exp(s_own - m), 0.0)
    l = jnp.sum(p, axis=-1, keepdims=True) + p_own
    acc = (jnp.dot(p.astype(BF16), vbuf[slot].astype(BF16), preferred_element_type=F32)
           + p_own.astype(BF16).astype(F32) * expand(vn))
    o_ref[...] = acc / l


def _dsa_s_attend(idx, meta, q, k_new, v_new, cache_k, cache_v, layer):
    nb, topk = idx.shape
    kvw = ATT_KV * ATT_HD
    return pl.pallas_call(
        functools.partial(_dsa_s_attend_body, layer=layer, topk=topk),
        grid_spec=pltpu.PrefetchScalarGridSpec(
            num_scalar_prefetch=2,
            grid=(nb,),
            in_specs=[pl.BlockSpec((None, ATT_HEADS, ATT_HD), lambda b, idx, meta: (b, 0, 0)),
                      pl.BlockSpec((None, 1, kvw), lambda b, idx, meta: (b, 0, 0)),
                      pl.BlockSpec((None, 1, kvw), lambda b, idx, meta: (b, 0, 0)),
                      pl.BlockSpec(memory_space=pl.ANY),
                      pl.BlockSpec(memory_space=pl.ANY)],
            out_specs=pl.BlockSpec((None, ATT_HEADS, ATT_HD), lambda b, idx, meta: (b, 0, 0)),
            scratch_shapes=[pltpu.VMEM((2, topk * ATT_KV, ATT_HD), F32),
                            pltpu.VMEM((2, topk * ATT_KV, ATT_HD), F32),
                            pltpu.SemaphoreType.DMA((2, 2))],
        ),
        out_shape=jax.ShapeDtypeStruct((nb, ATT_HEADS, ATT_HD), F32),
        compiler_params=_cp("arbitrary"),
        name="dsa_sample_attend",
    )(idx, meta, q, k_new, v_new, cache_k, cache_v)


HG_SUB = 16


HG_HB = 4
HG_EXP_LIMIT = 80.0


def _hgrn_body(lbl_ref, q_ref, f_ref, i_ref, g_ref, ng_ref, s0_ref, o_ref, so_ref, st_ref, *, layer, c, t, nc):
    ci = pl.program_id(2)
    mid = c // 2

    @pl.when(ci == 0)
    def _():
        for hh in range(HG_HB):
            st_ref[hh] = s0_ref[hh].T

    logits = lbl_ref[...]
    e = jnp.exp(logits - jnp.max(logits, axis=0, keepdims=True))
    soft = e / jnp.sum(e, axis=0, keepdims=True)
    lb_all = jnp.zeros((1, HG_HB * HG_DK), F32)
    for r in range(1, layer + 1):
        lb_all = lb_all + soft[r:r + 1, :]
    ng = ng_ref[...]

    def gates(hh, sl, row0):
        cs = slice(hh * HG_DK, (hh + 1) * HG_DK)
        lb = lb_all[:, cs]
        fg = lb + (1.0 - lb) * jax.nn.sigmoid(f_ref[sl, cs])
        lf = jnp.log(fg)
        kk = 1.0 - fg
        if t % c:
            n = lf.shape[0]
            valid = (ci * c + row0 + lax.broadcasted_iota(jnp.int32, (n, 1), 0)) < t
            lf = jnp.where(valid, lf, 0.0)
            kk = jnp.where(valid, kk, 0.0)
        return lf, kk

    def finish(hh, sl, o):
        cs = slice(hh * HG_DV, (hh + 1) * HG_DV)
        on = o * lax.rsqrt(jnp.mean(o * o, axis=-1, keepdims=True) + RMS_EPS) * ng
        o_ref[sl, cs] = (on * _silu(g_ref[sl, cs])).astype(o_ref.dtype)

    tri_c = lax.broadcasted_iota(jnp.int32, (c, c), 0) >= lax.broadcasted_iota(jnp.int32, (c, c), 1)
    full = pl.ds(0, c)
    lfs, kks, bs = [], [], []
    safe = None
    for hh in range(HG_HB):
        lf, kk = gates(hh, full, 0)
        b = jnp.dot(tri_c.astype(F32), lf, precision=lax.Precision.HIGHEST, preferred_element_type=F32)
        bm = b[mid - 1:mid, :]
        ok = jnp.min(jnp.minimum(bm, b[c - 1:c, :] - bm)) > -HG_EXP_LIMIT
        safe = ok if safe is None else jnp.logical_and(safe, ok)
        lfs.append(lf)
        kks.append(kk)
        bs.append(b)

    @pl.when(safe)
    def _():
        for hh in range(HG_HB):
            cs = slice(hh * HG_DK, (hh + 1) * HG_DK)
            b, kk = bs[hh], kks[hh]
            bm = b[mid - 1:mid, :]
            bl = b[c - 1:c, :]
            qq = _silu(q_ref[:, cs])
            vv = i_ref[:, cs].astype(BF16)
            att = jnp.where(tri_c, _nt((qq * jnp.exp(b - bm)).astype(BF16), (kk * jnp.exp(bm - b)).astype(BF16)), 0.0)
            st = st_ref[hh]
            o = (jnp.dot(att.astype(BF16), vv, preferred_element_type=F32)
                 + _nt((qq * jnp.exp(b)).astype(BF16), st.astype(BF16)))
            st_ref[hh] = st * jnp.exp(bl) + _tn(vv, (kk * jnp.exp(bl - b)).astype(BF16))
            finish(hh, full, o)

    @pl.when(jnp.logical_not(safe))
    def _():
        rows = lax.broadcasted_iota(jnp.int32, (HG_SUB, 1), 0)
        tri = (lax.broadcasted_iota(jnp.int32, (HG_SUB, HG_SUB), 0)
               >= lax.broadcasted_iota(jnp.int32, (HG_SUB, HG_SUB), 1)).astype(F32)
        for hh in range(HG_HB):
            cs = slice(hh * HG_DK, (hh + 1) * HG_DK)

            def sub_block(sb, carry, hh=hh, cs=cs):
                row0 = pl.multiple_of(sb * HG_SUB, HG_SUB)
                sl = pl.ds(row0, HG_SUB)
                lf, kk = gates(hh, sl, row0)
                qq = _silu(q_ref[sl, cs])
                vv = i_ref[sl, cs]
                b = jnp.dot(tri, lf, precision=lax.Precision.HIGHEST, preferred_element_type=F32)
                st = st_ref[hh]
                o = _nt((qq * jnp.exp(b)).astype(BF16), st.astype(BF16))
                for s in range(HG_SUB):
                    dec = jnp.exp(jnp.where(rows >= s, b - b[s:s + 1, :], -jnp.inf))
                    att = jnp.sum(qq * dec * kk[s:s + 1, :], axis=-1, keepdims=True)
                    o = o + att * vv[s:s + 1, :]
                bl = b[HG_SUB - 1:HG_SUB, :]
                st_ref[hh] = st * jnp.exp(bl) + _tn(vv.astype(BF16), (kk * jnp.exp(bl - b)).astype(BF16))
                finish(hh, sl, o)
                return carry

            lax.fori_loop(0, c // HG_SUB, sub_block, 0)

    @pl.when(ci == nc - 1)
    def _():
        for hh in range(HG_HB):
            so_ref[hh] = st_ref[hh].T


def _hgrn(proj, lb_logits, norm_g, s0, layer, t, c=128):
    nb, tpad, _ = proj.shape
    nc = tpad // c
    nhb = HG_HEADS // HG_HB
    w = HG_HB * HG_DK
    return pl.pallas_call(
        functools.partial(_hgrn_body, layer=layer, c=c, t=t, nc=nc),
        grid=(nb, nhb, nc),
        in_specs=[pl.BlockSpec((DEPTH, w), lambda b, h, ci: (0, h)),
                  pl.BlockSpec((None, c, w), lambda b, h, ci: (b, ci, h)),
                  pl.BlockSpec((None, c, w), lambda b, h, ci: (b, ci, nhb + h)),
                  pl.BlockSpec((None, c, w), lambda b, h, ci: (b, ci, 2 * nhb + h)),
                  pl.BlockSpec((None, c, w), lambda b, h, ci: (b, ci, 3 * nhb + h)),
                  pl.BlockSpec((1, HG_DV), lambda b, h, ci: (0, 0)),
                  pl.BlockSpec((None, HG_HB, HG_DK, HG_DV), lambda b, h, ci: (b, h, 0, 0))],
        out_specs=[pl.BlockSpec((None, c, w), lambda b, h, ci: (b, ci, h)),
                   pl.BlockSpec((None, HG_HB, HG_DK, HG_DV), lambda b, h, ci: (b, h, 0, 0))],
        out_shape=[jax.ShapeDtypeStruct((nb, tpad, HG_HEADS * HG_DV), BF16),
                   jax.ShapeDtypeStruct((nb, HG_HEADS, HG_DK, HG_DV), F32)],
        scratch_shapes=[pltpu.VMEM((HG_HB, HG_DV, HG_DK), F32)],
        compiler_params=_cp("parallel", "parallel", "arbitrary"),
        name="hgrn2",
    )(lb_logits, proj, proj, proj, proj, norm_g.reshape(1, HG_DV), s0)


def _expand_heads(v, e, terms):
    out = None
    rest = v
    for _ in range(terms):
        part = rest.astype(BF16)
        rest = rest - part.astype(F32)
        d = jnp.dot(part, e, preferred_element_type=F32)
        out = d if out is None else out + d
    return out


def _ssd_body(zx_ref, dt_ref, cs_ref, cw_ref, cbias_ref, e_ref, dtb_ref, alog_ref, dx_ref, ng_ref, s0_ref,
              o_ref, so_ref, nc_ref, st_ref, xc_ref, halo_ref, *, c, t, nc):
    ci = pl.program_id(1)
    gw = SSM_HPG * SSM_P

    @pl.when(ci == 0)
    def _():
        for blk in range(SSM_INNER // LANES):
            st_ref[:, blk * LANES:(blk + 1) * LANES] = s0_ref[blk * LANES:(blk + 1) * LANES, :].T
        halo_ref[...] = cs_ref[...]

    full = jnp.concatenate([halo_ref[...], zx_ref[:, SSM_INNER:SSM_INNER + SSM_CH]], axis=0)
    conv = cbias_ref[...]
    for k in range(SSM_CONV):
        lo = SUBLANES - (SSM_CONV - 1) + k
        conv = conv + full[lo:lo + c, :] * cw_ref[k:k + 1, :]
    xc_ref[...] = _silu(conv)
    halo_ref[...] = full[c:c + SUBLANES, :]

    @pl.when(ci == nc - 1)
    def _():
        tv = t - (nc - 1) * c
        tail = full[SUBLANES + tv - (SSM_CONV - 1):SUBLANES + tv, :]
        nc_ref[...] = jnp.concatenate([tail, jnp.zeros((SUBLANES - (SSM_CONV - 1), SSM_CH), F32)], axis=0)

    e = e_ref[...]
    rows = lax.broadcasted_iota(jnp.int32, (c, 1), 0)
    tri_b = lax.broadcasted_iota(jnp.int32, (c, c), 0) >= lax.broadcasted_iota(jnp.int32, (c, c), 1)
    lane_lo = lax.broadcasted_iota(jnp.int32, (1, LANES), 1) < SSM_P

    dt = jax.nn.softplus(dt_ref[...] + dtb_ref[...])
    if t % c:
        dt = jnp.where(ci * c + rows < t, dt, 0.0)
    da = dt * (-jnp.exp(alog_ref[...]))
    bcum = jnp.dot(tri_b.astype(F32), da, precision=lax.Precision.HIGHEST, preferred_element_type=F32)
    bcum_t = bcum.T
    dt_t = dt.T
    bl = bcum[c - 1:c, :]
    eb_x = _expand_heads(jnp.exp(bcum), e, 2)
    w_x = _expand_heads(jnp.exp(bl - bcum) * dt, e, 2)
    decay_x = _expand_heads(jnp.broadcast_to(jnp.exp(bl), (SUBLANES, LANES)), e, 3)[0:1, :]

    xs = xc_ref[:, 0:SSM_INNER]
    xdt = xs.astype(BF16)
    xw = (xs * w_x).astype(BF16)
    y = xs * dx_ref[...]
    zg = _silu(zx_ref[:, 0:SSM_INNER])
    for g in range(SSM_GROUPS):
        bg = xc_ref[:, SSM_INNER + g * SSM_N:SSM_INNER + (g + 1) * SSM_N]
        cg = xc_ref[:, SSM_INNER + (SSM_GROUPS + g) * SSM_N:SSM_INNER + (SSM_GROUPS + g + 1) * SSM_N].astype(BF16)
        cb = _nt(cg, bg.astype(BF16))
        st_g = st_ref[:, g * gw:(g + 1) * gw]
        yg = jnp.dot(cg, st_g.astype(BF16), preferred_element_type=F32) * eb_x[:, g * gw:(g + 1) * gw]
        parts = []
        for jp in range(SSM_HPG // 2):
            xpair = xdt[:, g * gw + jp * LANES:g * gw + (jp + 1) * LANES]
            acc = None
            for half in range(2):
                h = g * SSM_HPG + jp * 2 + half
                dec = jnp.exp(jnp.where(tri_b, bcum[:, h:h + 1] - bcum_t[h:h + 1, :], -jnp.inf))
                w = (cb * dec * dt_t[h:h + 1, :]).astype(BF16)
                xh = jnp.where(lane_lo if half == 0 else jnp.logical_not(lane_lo), xpair, 0.0).astype(BF16)
                r = jnp.dot(w, xh, preferred_element_type=F32)
                acc = r if acc is None else acc + r
            parts.append(acc)
        yg = yg + jnp.concatenate(parts, axis=1)
        st_ref[:, g * gw:(g + 1) * gw] = (st_g * decay_x[:, g * gw:(g + 1) * gw]
                                          + jnp.dot(bg.T.astype(BF16), xw[:, g * gw:(g + 1) * gw],
                                                    preferred_element_type=F32))
        yg = (yg + y[:, g * gw:(g + 1) * gw]) * zg[:, g * gw:(g + 1) * gw]
        yg = yg * lax.rsqrt(jnp.mean(yg * yg, axis=-1, keepdims=True) + RMS_EPS) * ng_ref[:, g * gw:(g + 1) * gw]
        o_ref[:, g * gw:(g + 1) * gw] = yg.astype(o_ref.dtype)

    @pl.when(ci == nc - 1)
    def _():
        for blk in range(SSM_INNER // LANES):
            so_ref[blk * LANES:(blk + 1) * LANES, :] = st_ref[:, blk * LANES:(blk + 1) * LANES].T


def _ssd(zx, dt_raw, cs_pad, conv_w, conv_b, expand, dt_bias, a_log, d_x, norm_g, s0, t, c=128):
    nb, tpad, _ = zx.shape
    nc = tpad // c
    return pl.pallas_call(
        functools.partial(_ssd_body, c=c, t=t, nc=nc),
        grid=(nb, nc),
        in_specs=[pl.BlockSpec((None, c, SSM_MAIN), lambda b, ci: (b, ci, 0)),
                  pl.BlockSpec((None, c, LANES), lambda b, ci: (b, ci, 0)),
                  pl.BlockSpec((None, SUBLANES, SSM_CH), lambda b, ci: (b, 0, 0)),
                  pl.BlockSpec((SSM_CONV, SSM_CH), lambda b, ci: (0, 0)),
                  pl.BlockSpec((1, SSM_CH), lambda b, ci: (0, 0)),
                  pl.BlockSpec((LANES, SSM_INNER), lambda b, ci: (0, 0)),
                  pl.BlockSpec((1, LANES), lambda b, ci: (0, 0)),
                  pl.BlockSpec((1, LANES), lambda b, ci: (0, 0)),
                  pl.BlockSpec((1, SSM_INNER), lambda b, ci: (0, 0)),
                  pl.BlockSpec((1, SSM_INNER), lambda b, ci: (0, 0)),
                  pl.BlockSpec((None, SSM_INNER, SSM_N), lambda b, ci: (b, 0, 0))],
        out_specs=[pl.BlockSpec((None, c, SSM_INNER), lambda b, ci: (b, ci, 0)),
                   pl.BlockSpec((None, SSM_INNER, SSM_N), lambda b, ci: (b, 0, 0)),
                   pl.BlockSpec((None, SUBLANES, SSM_CH), lambda b, ci: (b, 0, 0))],
        out_shape=[jax.ShapeDtypeStruct((nb, tpad, SSM_INNER), BF16),
                   jax.ShapeDtypeStruct((nb, SSM_INNER, SSM_N), F32),
                   jax.ShapeDtypeStruct((nb, SUBLANES, SSM_CH), F32)],
        scratch_shapes=[pltpu.VMEM((SSM_N, SSM_INNER), F32),
                        pltpu.VMEM((c, SSM_CH), F32),
                        pltpu.VMEM((SUBLANES, SSM_CH), F32)],
        compiler_params=_cp("parallel", "arbitrary"),
        name="ssd",
    )(zx, dt_raw, cs_pad, conv_w, conv_b.reshape(1, SSM_CH), expand, dt_bias, a_log, d_x,
      norm_g.reshape(1, SSM_INNER), s0)


def _pad_cols(w, n):
    return jnp.pad(w, ((0, 0), (0, n - w.shape[1])))


def _pad_time(a, tpad):
    return jnp.pad(a, ((0, 0), (0, tpad - a.shape[1]), (0, 0)))


def kernel(x_prompt, x_sample, cache_k, cache_v, cache_idx_k, state_hgrn, state_ssm, state_conv, page_table, p_prompt, p_sample, ln_g, ln_b, ffn_w_gate_up, ffn_w_down, ple_w_proj, ple_w_gate, att_w_in, att_idx_k_norm, att_w_o, hg_w_in, hg_lb_logits, hg_norm_g, hg_w_o, ssm_w_in, ssm_conv_w, ssm_conv_b, ssm_dt_bias, ssm_a_log, ssm_d, ssm_norm_g, ssm_w_o):
    nbp, seq, d = x_prompt.shape
    nbs = x_sample.shape[0]
    mp = nbp * seq
    ms = 16
    chunk = 128

    tm, tm_ln = 1024, 512
    x_p = x_prompt.reshape(mp, d)
    x_s = jnp.pad(x_sample.reshape(nbs, d), ((0, ms - nbs), (0, 0)))
    xb_p, xb_s = x_p.astype(BF16), x_s.astype(BF16)
    pl_p = p_prompt.reshape(DEPTH, mp, PLE_DIM)
    pl_s = jnp.pad(p_sample.reshape(DEPTH, nbs, PLE_DIM), ((0, 0), (0, ms - nbs), (0, 0)))
    pad_rows = lambda a: jnp.pad(a, ((0, ms - nbs), (0, 0)))

    expand = jnp.asarray(np.kron(np.eye(LANES, SSM_HEADS, dtype=np.float32),
                                 np.ones((1, SSM_P), np.float32)), BF16)
    outs = {}

    def ffn_ln(i, which, ln_idx, x_p, xb_p, x_s, xb_s):
        wd = _cast_w(ffn_w_down, (i, which))
        h_p, h_s = _mm_swiglu(xb_p, xb_s, ffn_w_gate_up, (i, which), tm, 512)
        return _mm_ln(h_p, h_s, wd, x_p, x_s, ln_g[i, ln_idx], ln_b[i, ln_idx], 0.5, tm_ln, "ffn_down_ln")

    for i in range(DEPTH):
        j = i // N_MIXERS
        (x_p, xb_p), (x_s, xb_s) = ffn_ln(i, 0, 0, x_p, xb_p, x_s, xb_s)

        if i % N_MIXERS == 0:
            w_small = _pad_cols(att_w_in[j][:, ATT_MAIN:], LANES)
            w_o = _cast_w(att_w_o, (j,))
            qw, kvw = ATT_HEADS * ATT_HD, ATT_KV * ATT_HD
            (proj, projb), (proj_s, _) = _proj(xb_p, xb_s, att_w_in, (j,), ATT_MAIN, tm, 1024, with_bf16=True,
                                               name="att_in")
            proj2, proj2_s = _proj(xb_p, xb_s, w_small, (), LANES, tm, LANES, name="att_in_idx")
            ik, ikb = _ik_norm(proj2, att_idx_k_norm[j], tm)
            ik_s, _ = _ik_norm(proj2_s, att_idx_k_norm[j], ms)
            o_p = _dsa_prompt(proj, projb, proj2, ikb, nbp, seq)
            outs.setdefault("k_p", []).append(proj[:, qw:qw + kvw].reshape(nbp, seq, ATT_KV, ATT_HD))
            outs.setdefault("v_p", []).append(proj[:, qw + kvw:qw + 2 * kvw].reshape(nbp, seq, ATT_KV, ATT_HD))
            outs.setdefault("ik_p", []).append(ik.reshape(nbp, seq, IDX_DIM))
            pr = proj_s[:nbs]
            k_new = pr[:, qw:qw + kvw]
            v_new = pr[:, qw + kvw:qw + 2 * kvw]
            ik_new = ik_s[:nbs]
            scores = _dsa_s_scores(page_table,
                                   pr[:, qw + 2 * kvw:].reshape(nbs, IDX_HEADS, IDX_DIM),
                                   proj2_s[:nbs, IDX_DIM:IDX_DIM + IDX_HEADS].reshape(nbs, IDX_HEADS, 1),
                                   ik_new.reshape(nbs, 1, IDX_DIM), cache_idx_k, j)
            idx, meta = _dsa_s_select(scores, page_table)
            o_s = _dsa_s_attend(idx, meta, pr[:, :qw].reshape(nbs, ATT_HEADS, ATT_HD),
                                k_new.reshape(nbs, 1, kvw), v_new.reshape(nbs, 1, kvw), cache_k, cache_v, j)
            o_s = pad_rows(o_s.reshape(nbs, qw)).astype(BF16)
            outs.setdefault("k_s", []).append(k_new.reshape(nbs, 1, ATT_KV, ATT_HD))
            outs.setdefault("v_s", []).append(v_new.reshape(nbs, 1, ATT_KV, ATT_HD))
            outs.setdefault("ik_s", []).append(ik_new.reshape(nbs, 1, IDX_DIM))
        elif i % N_MIXERS == 1:
            w_o = _cast_w(hg_w_o, (j,))
            proj, proj_s = _proj(xb_p, xb_s, hg_w_in, (j,), hg_w_in.shape[-1], tm, 1024, name="hg_in")
            s0 = jnp.zeros((nbp, HG_HEADS, HG_DK, HG_DV), F32)
            o_p, s_fin = _hgrn(proj.reshape(nbp, seq, -1), hg_lb_logits, hg_norm_g[j], s0, i, seq, chunk)
            o_p = o_p.reshape(mp, -1)
            outs.setdefault("hg_p", []).append(s_fin)
            pr = _pad_time(proj_s[:nbs].reshape(nbs, 1, -1), chunk)
            o_s, s_fin = _hgrn(pr, hg_lb_logits, hg_norm_g[j], state_hgrn[j], i, 1, chunk)
            o_s = pad_rows(o_s[:, 0, :])
            outs.setdefault("hg_s", []).append(s_fin)
        else:
            w_small = _pad_cols(ssm_w_in[j][:, SSM_MAIN:], LANES)
            w_o = _cast_w(ssm_w_o, (j,))
            dt_bias = _pad_cols(ssm_dt_bias[j].reshape(1, SSM_HEADS), LANES)
            a_log = _pad_cols(ssm_a_log[j].reshape(1, SSM_HEADS), LANES)
            d_x = jnp.repeat(ssm_d[j], SSM_P).reshape(1, SSM_INNER)
            zx_p, zx_s = _proj(xb_p, xb_s, ssm_w_in, (j,), SSM_MAIN, tm, 1024, name="ssm_in")
            dtr_p, dtr_s = _proj(xb_p, xb_s, w_small, (), LANES, tm, LANES, name="ssm_in_dt")
            mix = {}
            for name, zx, dtr in (("p", zx_p, dtr_p), ("s", zx_s, dtr_s)):
                if name == "p":
                    nb_, t_ = nbp, seq
                    zx3 = zx.reshape(nbp, seq, -1)
                    dt3 = dtr.reshape(nbp, seq, LANES)
                    cs = jnp.zeros((nbp, SUBLANES, SSM_CH), F32)
                    s0 = jnp.zeros((nbp, SSM_INNER, SSM_N), F32)
                else:
                    nb_, t_ = nbs, 1
                    zx3 = _pad_time(zx[:nbs].reshape(nbs, 1, -1), chunk)
                    dt3 = _pad_time(dtr[:nbs].reshape(nbs, 1, LANES), chunk)
                    cs = jnp.pad(state_conv[j], ((0, 0), (SUBLANES - (SSM_CONV - 1), 0), (0, 0)))
                    s0 = state_ssm[j].reshape(nbs, SSM_INNER, SSM_N)
                y, s_fin, new_conv = _ssd(zx3, dt3, cs, ssm_conv_w[j], ssm_conv_b[j], expand, dt_bias, a_log, d_x,
                                          ssm_norm_g[j], s0, t_, chunk)
                s_fin = s_fin.reshape(nb_, SSM_HEADS, SSM_P, SSM_N)
                new_conv = new_conv[:, :SSM_CONV - 1, :]
                if name == "p":
                    mix[name] = y.reshape(mp, SSM_INNER)
                    outs.setdefault("ssm_p", []).append(s_fin)
                    outs.setdefault("conv_p", []).append(new_conv)
                else:
                    mix[name] = pad_rows(y[:, 0, :])
                    outs.setdefault("ssm_s", []).append(s_fin)
                    outs.setdefault("conv_s", []).append(new_conv)
            o_p, o_s = mix["p"], mix["s"]

        (x_p, xb_p), (x_s, xb_s) = _mm_ln(o_p, o_s, w_o, x_p, x_s, ln_g[i, 1], ln_b[i, 1], 1.0, tm_ln, "mixer_out_ln")
        (x_p, xb_p), (x_s, xb_s) = ffn_ln(i, 1, 2, x_p, xb_p, x_s, xb_s)
        (x_p, xb_p), (x_s, xb_s) = _mm_ple(x_p, xb_p, pl_p[i].astype(BF16), x_s, xb_s, pl_s[i].astype(BF16),
                                           ple_w_gate, ple_w_proj, (i,), tm, 512)

    y_prompt = x_p.reshape(nbp, seq, d)
    y_sample = x_s[:nbs].reshape(nbs, 1, d)
    stack = lambda key: jnp.stack(outs[key])
    return (y_prompt, y_sample, stack("k_p"), stack("v_p"), stack("ik_p"), stack("k_s"), stack("v_s"), stack("ik_s"),
            stack("hg_p"), stack("hg_s"), stack("ssm_p"), stack("ssm_s"), stack("conv_p"), stack("conv_s"))
```

```python
import functools
import math

import jax
import jax.numpy as jnp
import numpy as np
from jax import lax
from jax.experimental import pallas as pl
from jax.experimental.pallas import tpu as pltpu

F32 = jnp.float32
BF16 = jnp.bfloat16

D_MODEL = 2048
DEPTH = 4
N_MIXERS = 3
D_FF = 2 * D_MODEL
PLE_DIM = 256
ALPHA = (2 * DEPTH) ** 0.25
LN_EPS = 1e-5
RMS_EPS = 1e-6
PAGE = 128

ATT_HD = 128
ATT_HEADS = 16
ATT_KV = 4
ATT_GROUP = ATT_HEADS // ATT_KV
IDX_HEADS = 16
IDX_DIM = 64
IDX_W_SCALE = (IDX_HEADS ** -0.5) * (IDX_DIM ** -0.5)
TOPK_MAX = 256
ATT_SCALE = ATT_HD ** -0.5
LOG2E = math.log2(math.e)
ATT_MAIN = ATT_HEADS * ATT_HD + 2 * ATT_KV * ATT_HD + IDX_HEADS * IDX_DIM

HG_HEADS = 16
HG_DK = 128
HG_DV = 128

SSM_INNER = 2 * D_MODEL
SSM_P = 64
SSM_HEADS = SSM_INNER // SSM_P
SSM_GROUPS = 8
SSM_HPG = SSM_HEADS // SSM_GROUPS
SSM_N = 128
SSM_CONV = 4
SSM_CH = SSM_INNER + 2 * SSM_GROUPS * SSM_N
SSM_MAIN = SSM_INNER + SSM_CH

LANES = 128
SUBLANES = 8
VMEM_LIMIT_BYTES = 56 * 1024 * 1024

INT_MIN = np.int32(-2 ** 31)
NEG_BIG = -1e30


def _cp(*sem):
    return pltpu.CompilerParams(dimension_semantics=sem, vmem_limit_bytes=VMEM_LIMIT_BYTES)


def _nt(a, b):
    return lax.dot_general(a, b, (((1,), (1,)), ((), ())), preferred_element_type=F32)


def _tn(a, b):
    return lax.dot_general(a, b, (((0,), (0,)), ((), ())), preferred_element_type=F32)


def _silu(x):
    return x * jax.nn.sigmoid(x)


def _prefix_sums(tri, v):
    out = None
    rest = v
    for _ in range(3):
        part = rest.astype(BF16)
        rest = rest - part.astype(F32)
        d = jnp.dot(tri, part, preferred_element_type=F32)
        out = d if out is None else out + d
    return out


def _wspec(k, tn, idx, col_block):
    lead = (None,) * len(idx)
    return pl.BlockSpec(lead + (k, tn), lambda n, i: idx + (0, col_block(n)))


def _proj_body(x_ref, xs_ref, w_ref, *rest, with_bf16):
    wb_ref = rest[-1]
    outs, outs_s = (rest[0:2], rest[2:4]) if with_bf16 else (rest[0:1], rest[1:2])

    def emit(refs, acc):
        refs[0][...] = acc
        if with_bf16:
            refs[1][...] = acc.astype(BF16)

    @pl.when(pl.program_id(1) == 0)
    def _():
        wb_ref[...] = w_ref[...].astype(BF16)
        emit(outs_s, jnp.dot(xs_ref[...], wb_ref[...], preferred_element_type=F32))

    emit(outs, jnp.dot(x_ref[...], wb_ref[...], preferred_element_type=F32))


def _proj(xb, xsb, w, idx, n_cols, tm, tn, with_bf16=False, name="proj"):
    m, k = xb.shape
    ms = xsb.shape[0]
    dts = (F32, BF16) if with_bf16 else (F32,)
    res = pl.pallas_call(
        functools.partial(_proj_body, with_bf16=with_bf16),
        grid=(n_cols // tn, m // tm),
        in_specs=[pl.BlockSpec((tm, k), lambda n, i: (i, 0)),
                  pl.BlockSpec((ms, k), lambda n, i: (0, 0)),
                  _wspec(k, tn, idx, lambda n: n)],
        out_specs=[pl.BlockSpec((tm, tn), lambda n, i: (i, n)) for _ in dts]
                  + [pl.BlockSpec((ms, tn), lambda n, i: (0, n)) for _ in dts],
        out_shape=[jax.ShapeDtypeStruct((m, n_cols), dt) for dt in dts]
                  + [jax.ShapeDtypeStruct((ms, n_cols), dt) for dt in dts],
        scratch_shapes=[pltpu.VMEM((k, tn), BF16)],
        compiler_params=_cp("parallel", "arbitrary"),
        name=name,
    )(xb, xsb, w)
    nd = len(dts)
    return (res[:nd], res[nd:]) if with_bf16 else (res[0], res[1])


def _swiglu_body(x_ref, xs_ref, wg_ref, wu_ref, wd_ref, o_ref, os_ref, wdb_ref, wgb_ref, wub_ref):
    def swiglu(x):
        g = jnp.dot(x, wgb_ref[...], preferred_element_type=F32)
        u = jnp.dot(x, wub_ref[...], preferred_element_type=F32)
        return (_silu(g) * u).astype(BF16)

    @pl.when(pl.program_id(1) == 0)
    def _():
        wgb_ref[...] = wg_ref[...].astype(BF16)
        wub_ref[...] = wu_ref[...].astype(BF16)
        wdb_ref[...] = wd_ref[...].astype(BF16)
        os_ref[...] = swiglu(xs_ref[...])

    o_ref[...] = swiglu(x_ref[...])


def _mm_swiglu(xb, xsb, wgu, wd, idx, tm, tn):
    m, k = xb.shape
    ms = xsb.shape[0]
    f = wgu.shape[-1] // 2
    nj = f // tn
    n_out = wd.shape[-1]
    lead = (None,) * len(idx)
    return pl.pallas_call(
        _swiglu_body,
        grid=(nj, m // tm),
        in_specs=[pl.BlockSpec((tm, k), lambda n, i: (i, 0)),
                  pl.BlockSpec((ms, k), lambda n, i: (0, 0)),
                  _wspec(k, tn, idx, lambda n: n),
                  _wspec(k, tn, idx, lambda n: n + nj),
                  pl.BlockSpec(lead + (f // nj, n_out), lambda n, i: idx + (n, 0))],
        out_specs=[pl.BlockSpec((tm, tn), lambda n, i: (i, n)),
                   pl.BlockSpec((ms, tn), lambda n, i: (0, n)),
                   pl.BlockSpec((f // nj, n_out), lambda n, i: (n, 0))],
        out_shape=[jax.ShapeDtypeStruct((m, f), BF16), jax.ShapeDtypeStruct((ms, f), BF16),
                   jax.ShapeDtypeStruct((f, n_out), BF16)],
        scratch_shapes=[pltpu.VMEM((k, tn), BF16), pltpu.VMEM((k, tn), BF16)],
        compiler_params=_cp("parallel", "arbitrary"),
        name="ffn_up",
    )(xb, xsb, wgu, wgu, wd)


def _cast_body(w_ref, o_ref):
    o_ref[...] = w_ref[...].astype(BF16)


def _cast_w(w, idx, tk=512):
    k, n = w.shape[-2:]
    lead = (None,) * len(idx)
    return pl.pallas_call(
        _cast_body,
        grid=(k // tk,),
        in_specs=[pl.BlockSpec(lead + (tk, n), lambda i: idx + (i, 0))],
        out_specs=pl.BlockSpec((tk, n), lambda i: (i, 0)),
        out_shape=jax.ShapeDtypeStruct((k, n), BF16),
        compiler_params=_cp("parallel"),
        name="cast_w",
    )(w)


LN_SPLIT = 2


def _mm_ln_body(a_ref, as_ref, w_ref, r_ref, rs_ref, g_ref, b_ref, o_ref, ob_ref, os_ref, osb_ref, *, scale, tm):
    g = g_ref[...]
    b = b_ref[...]

    def ln_rows(a, r, o, ob, sl):
        y = ALPHA * r[sl, :] + scale * jnp.dot(a[sl, :], w_ref[...], preferred_element_type=F32)
        mu = jnp.mean(y, axis=-1, keepdims=True)
        yc = y - mu
        var = jnp.mean(yc * yc, axis=-1, keepdims=True)
        out = yc * lax.rsqrt(var + LN_EPS) * g + b
        o[sl, :] = out
        ob[sl, :] = out.astype(BF16)

    @pl.when(pl.program_id(0) == 0)
    def _():
        ln_rows(as_ref, rs_ref, os_ref, osb_ref, pl.ds(0, as_ref.shape[0]))

    rows = tm // LN_SPLIT
    for r in range(0, tm, rows):
        ln_rows(a_ref, r_ref, o_ref, ob_ref, pl.ds(r, rows))


def _mm_ln(ab, asb, wb, res, res_s, g, b, scale, tm, name):
    m, kdim = ab.shape
    ms = asb.shape[0]
    n = wb.shape[1]
    res = pl.pallas_call(
        functools.partial(_mm_ln_body, scale=scale, tm=tm),
        grid=(m // tm,),
        in_specs=[pl.BlockSpec((tm, kdim), lambda i: (i, 0)),
                  pl.BlockSpec((ms, kdim), lambda i: (0, 0)),
                  pl.BlockSpec((kdim, n), lambda i: (0, 0), pipeline_mode=pl.Buffered(1)),
                  pl.BlockSpec((tm, n), lambda i: (i, 0)),
                  pl.BlockSpec((ms, n), lambda i: (0, 0)),
                  pl.BlockSpec((1, n), lambda i: (0, 0)),
                  pl.BlockSpec((1, n), lambda i: (0, 0))],
        out_specs=[pl.BlockSpec((tm, n), lambda i: (i, 0)),
                   pl.BlockSpec((tm, n), lambda i: (i, 0)),
                   pl.BlockSpec((ms, n), lambda i: (0, 0)),
                   pl.BlockSpec((ms, n), lambda i: (0, 0))],
        out_shape=[jax.ShapeDtypeStruct((m, n), F32), jax.ShapeDtypeStruct((m, n), BF16),
                   jax.ShapeDtypeStruct((ms, n), F32), jax.ShapeDtypeStruct((ms, n), BF16)],
        compiler_params=_cp("arbitrary"),
        name=name,
    )(ab, asb, wb, res, res_s, g.reshape(1, n), b.reshape(1, n))
    return res[:2], res[2:]


def _ple_body(xb_ref, p_ref, x_ref, xsb_ref, ps_ref, xs_ref, wg_ref, wp_ref, o_ref, ob_ref, os_ref, osb_ref,
              wgb_ref, wpb_ref):
    def ple(xb, p, x, o, ob):
        gate = jax.nn.sigmoid(jnp.dot(xb[...], wgb_ref[...], preferred_element_type=F32))
        proj = jnp.dot(p[...], wpb_ref[...], preferred_element_type=F32)
        out = x[...] + gate * proj
        o[...] = out
        ob[...] = out.astype(BF16)

    @pl.when(pl.program_id(1) == 0)
    def _():
        wgb_ref[...] = wg_ref[...].astype(BF16)
        wpb_ref[...] = wp_ref[...].astype(BF16)
        ple(xsb_ref, ps_ref, xs_ref, os_ref, osb_ref)

    ple(xb_ref, p_ref, x_ref, o_ref, ob_ref)


def _mm_ple(x32, xb, pb, xs32, xsb, psb, wg, wp, idx, tm, tn):
    m, d = xb.shape
    ms = xsb.shape[0]
    pd = pb.shape[1]
    res = pl.pallas_call(
        _ple_body,
        grid=(d // tn, m // tm),
        in_specs=[pl.BlockSpec((tm, d), lambda n, i: (i, 0)),
                  pl.BlockSpec((tm, pd), lambda n, i: (i, 0)),
                  pl.BlockSpec((tm, tn), lambda n, i: (i, n)),
                  pl.BlockSpec((ms, d), lambda n, i: (0, 0)),
                  pl.BlockSpec((ms, pd), lambda n, i: (0, 0)),
                  pl.BlockSpec((ms, tn), lambda n, i: (0, n)),
                  _wspec(d, tn, idx, lambda n: n),
                  _wspec(pd, tn, idx, lambda n: n)],
        out_specs=[pl.BlockSpec((tm, tn), lambda n, i: (i, n)),
                   pl.BlockSpec((tm, tn), lambda n, i: (i, n)),
                   pl.BlockSpec((ms, tn), lambda n, i: (0, n)),
                   pl.BlockSpec((ms, tn), lambda n, i: (0, n))],
        out_shape=[jax.ShapeDtypeStruct((m, d), F32), jax.ShapeDtypeStruct((m, d), BF16),
                   jax.ShapeDtypeStruct((ms, d), F32), jax.ShapeDtypeStruct((ms, d), BF16)],
        scratch_shapes=[pltpu.VMEM((d, tn), BF16), pltpu.VMEM((pd, tn), BF16)],
        compiler_params=_cp("parallel", "arbitrary"),
        name="ple",
    )(xb, pb, x32, xsb, psb, xs32, wg, wp)
    return res[:2], res[2:]


def _ik_norm_body(p_ref, g_ref, o_ref, ob_ref):
    x = p_ref[...][:, :IDX_DIM]
    mu = jnp.mean(x, axis=-1, keepdims=True)
    xc = x - mu
    out = xc * lax.rsqrt(jnp.mean(xc * xc, axis=-1, keepdims=True) + LN_EPS) * g_ref[...]
    o_ref[...] = out
    ob_ref[...] = out.astype(BF16)


def _ik_norm(proj2, ik_g, tm):
    m = proj2.shape[0]
    return pl.pallas_call(
        _ik_norm_body,
        grid=(m // tm,),
        in_specs=[pl.BlockSpec((tm, LANES), lambda i: (i, 0)),
                  pl.BlockSpec((1, IDX_DIM), lambda i: (0, 0))],
        out_specs=[pl.BlockSpec((tm, IDX_DIM), lambda i: (i, 0)),
                   pl.BlockSpec((tm, IDX_DIM), lambda i: (i, 0))],
        out_shape=[jax.ShapeDtypeStruct((m, IDX_DIM), F32), jax.ShapeDtypeStruct((m, IDX_DIM), BF16)],
        compiler_params=_cp("parallel"),
        name="idx_k_norm",
    )(proj2, ik_g.reshape(1, IDX_DIM))


def _order_key(x):
    bits = pltpu.bitcast(x, jnp.int32)
    return jnp.where(bits < 0, bits ^ jnp.int32(0x7FFFFFFF), bits)


_RADIX_BITS = [INT_MIN] + [np.int32(1 << s) for s in range(30, -1, -1)]


RADIX_ROWS = 128
KNORM_ROWS = 512
SUM_FLOOR = 2.0 ** -100


def _dsa_prompt_body(q_ref, iq_ref, iw_ref, k_ref, v_ref, ik_ref, o_ref, keys_ref, knorm_ref, *, qb, kc, topk):
    i = pl.program_id(1)
    nck = ((i + 1) * qb + kc - 1) // kc
    row_pos = i * qb + lax.broadcasted_iota(jnp.int32, (qb, 1), 0)
    iq = iq_ref[...]
    iw = iw_ref[...][:, IDX_DIM:IDX_DIM + IDX_HEADS] * IDX_W_SCALE
    iq_h = [iq[:, h * IDX_DIM:(h + 1) * IDX_DIM] for h in range(IDX_HEADS)]
    iw_h = [iw[:, h:h + 1] for h in range(IDX_HEADS)]
    col0 = lax.broadcasted_iota(jnp.int32, (1, kc), 1)

    def score_chunk(c, carry):
        off = pl.multiple_of(c * kc, kc)
        ikc = ik_ref[pl.ds(off, kc), :]
        sc = jnp.zeros((qb, kc), F32)
        for h in range(IDX_HEADS):
            sc = sc + iw_h[h] * jnp.maximum(_nt(iq_h[h], ikc), 0.0)
        key = jnp.where(col0 + off <= row_pos, _order_key(sc), INT_MIN)
        keys_ref[:, pl.ds(off, kc)] = key
        return carry

    lax.fori_loop(0, nck, score_chunk, 0)

    count_kc = 2 * kc
    n_count = (nck + 1) // 2

    @pl.when(nck % 2 == 1)
    def _():
        keys_ref[:, pl.ds(pl.multiple_of(nck * kc, kc), kc)] = jnp.full((qb, kc), INT_MIN, jnp.int32)

    nblk = qb // RADIX_ROWS
    los = [jnp.full((RADIX_ROWS, 1), INT_MIN, jnp.int32) for _ in range(nblk)]
    for bit in _RADIX_BITS:
        cands = [lo + bit for lo in los]
        accs = []
        for blk in range(nblk):
            cand_b = jnp.broadcast_to(cands[blk], (RADIX_ROWS, LANES))

            def count_chunk(c, acc, cand_b=cand_b, r0=blk * RADIX_ROWS):
                off = pl.multiple_of(c * count_kc, count_kc)
                for t in range(count_kc // LANES):
                    tile = keys_ref[r0:r0 + RADIX_ROWS, pl.ds(off + t * LANES, LANES)]
                    acc = acc + jnp.where(tile >= cand_b, 1.0, 0.0)
                return acc

            accs.append(lax.fori_loop(0, n_count, count_chunk, jnp.zeros((RADIX_ROWS, LANES), F32)))
        cnts = [jnp.sum(acc, axis=-1, keepdims=True) for acc in accs]
        los = [jnp.where(cnts[blk] >= topk, cands[blk], los[blk]) for blk in range(nblk)]
    thr = jnp.maximum(jnp.concatenate(los, axis=0), INT_MIN + 1)

    @pl.when(i == 0)
    def _():
        for g in range(ATT_KV):
            best = jnp.zeros((1, 1), F32)
            for r0 in range(0, k_ref.shape[0], KNORM_ROWS):
                kk = k_ref[r0:r0 + KNORM_ROWS, g * ATT_HD:(g + 1) * ATT_HD].astype(F32)
                best = jnp.maximum(best, jnp.max(jnp.sum(kk * kk, axis=-1, keepdims=True), axis=0, keepdims=True))
            knorm_ref[g:g + 1, :] = jnp.broadcast_to(jnp.sqrt(best), (1, LANES))

    def store(g, out):
        for j in range(ATT_GROUP):
            h = g * ATT_GROUP + j
            o_ref[:, h * ATT_HD:(h + 1) * ATT_HD] = out[j].astype(o_ref.dtype)

    for g in range(ATT_KV):
        qg = jnp.concatenate(
            [q_ref[:, (g * ATT_GROUP + j) * ATT_HD:(g * ATT_GROUP + j + 1) * ATT_HD] for j in range(ATT_GROUP)], axis=0)
        qg = (qg * (ATT_SCALE * LOG2E)).astype(BF16)
        q32 = qg.astype(F32)
        shift = (jnp.sqrt(jnp.sum(q32 * q32, axis=-1, keepdims=True)) * knorm_ref[g:g + 1, 0:1]).reshape(
            ATT_GROUP, qb, 1)

        def chunk_operands(c, g=g):
            off = pl.multiple_of(c * kc, kc)
            kch = k_ref[pl.ds(off, kc), g * ATT_HD:(g + 1) * ATT_HD]
            vch = v_ref[pl.ds(off, kc), g * ATT_HD:(g + 1) * ATT_HD]
            sel = (keys_ref[:, pl.ds(off, kc)] >= thr)[None]
            return kch, vch, sel

        def fixed_chunk(c, carry, qg=qg, shift=shift):
            l, acc = carry
            kch, vch, sel = chunk_operands(c)
            p = jnp.exp2(jnp.where(sel, _nt(qg, kch).reshape(ATT_GROUP, qb, kc) - shift, -jnp.inf))
            l = l + jnp.sum(p, axis=-1, keepdims=True)
            pv = jnp.dot(p.reshape(ATT_GROUP * qb, kc).astype(BF16), vch, preferred_element_type=F32)
            return l, acc + pv.reshape(ATT_GROUP, qb, ATT_HD)

        l, acc = lax.fori_loop(0, nck, fixed_chunk, (jnp.zeros((ATT_GROUP, qb, 1), F32),
                                                     jnp.zeros((ATT_GROUP, qb, ATT_HD), F32)))
        healthy = jnp.min(l) > SUM_FLOOR

        @pl.when(healthy)
        def _(g=g, l=l, acc=acc):
            store(g, acc / l)

        @pl.when(jnp.logical_not(healthy))
        def _(g=g, qg=qg):
            def running_chunk(c, carry):
                m, l, acc = carry
                kch, vch, sel = chunk_operands(c)
                s = jnp.where(sel, _nt(qg, kch).reshape(ATT_GROUP, qb, kc), -jnp.inf)
                m_new = jnp.maximum(m, jnp.max(s, axis=-1, keepdims=True))
                p = jnp.exp2(s - m_new)
                a = jnp.exp2(m - m_new)
                l = a * l + jnp.sum(p, axis=-1, keepdims=True)
                pv = jnp.dot(p.reshape(ATT_GROUP * qb, kc).astype(BF16), vch, preferred_element_type=F32)
                return m_new, l, a * acc + pv.reshape(ATT_GROUP, qb, ATT_HD)

            init = (jnp.full((ATT_GROUP, qb, 1), NEG_BIG, F32), jnp.zeros((ATT_GROUP, qb, 1), F32),
                    jnp.zeros((ATT_GROUP, qb, ATT_HD), F32))
            _, l, acc = lax.fori_loop(0, nck, running_chunk, init)
            store(g, acc / l)


def _dsa_prompt(proj, projb, proj2, ikb, nb, s, qb=256, kc=512):
    nq = s // qb
    topk = min(TOPK_MAX, s // 4)
    assert (s // kc) % 2 == 0
    qw = ATT_HEADS * ATT_HD
    kvw = ATT_KV * ATT_HD
    iqw = IDX_HEADS * IDX_DIM
    return pl.pallas_call(
        functools.partial(_dsa_prompt_body, qb=qb, kc=kc, topk=topk),
        grid=(nb, nq),
        in_specs=[pl.BlockSpec((qb, qw), lambda b, i: (b * nq + i, 0)),
                  pl.BlockSpec((qb, iqw), lambda b, i: (b * nq + i, (qw + 2 * kvw) // iqw)),
                  pl.BlockSpec((qb, LANES), lambda b, i: (b * nq + i, 0)),
                  pl.BlockSpec((s, kvw), lambda b, i: (b, qw // kvw)),
                  pl.BlockSpec((s, kvw), lambda b, i: (b, qw // kvw + 1)),
                  pl.BlockSpec((s, IDX_DIM), lambda b, i: (b, 0))],
        out_specs=pl.BlockSpec((qb, qw), lambda b, i: (b * nq + i, 0)),
        out_shape=jax.ShapeDtypeStruct((nb * s, qw), BF16),
        scratch_shapes=[pltpu.VMEM((qb, s), jnp.int32), pltpu.VMEM((SUBLANES, LANES), F32)],
        compiler_params=_cp("parallel", "arbitrary"),
        name="dsa_prompt",
    )(proj, projb, proj2, projb, projb, ikb)


SCORE_PAGES = 16


def _dsa_s_scores_body(pt_ref, iq_ref, iw_ref, ikn_ref, *rest, n_pages):
    page_refs, o_ref = rest[:SCORE_PAGES], rest[SCORE_PAGES]
    p = pl.program_id(1)
    iq = iq_ref[...].astype(BF16)
    iw = iw_ref[...] * IDX_W_SCALE

    @pl.when(p == 0)
    def _():
        own = jnp.broadcast_to(ikn_ref[...], (SUBLANES, IDX_DIM)).astype(BF16)
        d = jnp.maximum(_nt(iq, own), 0.0)
        sc = jnp.sum(iw * d, axis=0, keepdims=True)
        o_ref[n_pages:n_pages + SUBLANES, :] = jnp.broadcast_to(sc[:, 0:1], (SUBLANES, LANES))

    for r in range(SCORE_PAGES):
        d = jnp.maximum(_nt(iq, page_refs[r][...].astype(BF16)), 0.0)
        o_ref[pl.ds(p * SCORE_PAGES + r, 1), :] = jnp.sum(iw * d, axis=0, keepdims=True)


def _dsa_s_scores(page_table, iq, iw, ik_new, cache_ik, layer):
    nb, n_pages = page_table.shape
    page_spec = lambda r: pl.BlockSpec((None, None, PAGE, IDX_DIM),
                                       lambda b, p, pt: (layer, pt[b, p * SCORE_PAGES + r], 0, 0))
    return pl.pallas_call(
        functools.partial(_dsa_s_scores_body, n_pages=n_pages),
        grid_spec=pltpu.PrefetchScalarGridSpec(
            num_scalar_prefetch=1,
            grid=(nb, n_pages // SCORE_PAGES),
            in_specs=[pl.BlockSpec((None, IDX_HEADS, IDX_DIM), lambda b, p, pt: (b, 0, 0)),
                      pl.BlockSpec((None, IDX_HEADS, 1), lambda b, p, pt: (b, 0, 0)),
                      pl.BlockSpec((None, 1, IDX_DIM), lambda b, p, pt: (b, 0, 0))]
                     + [page_spec(r) for r in range(SCORE_PAGES)],
            out_specs=pl.BlockSpec((None, n_pages + SUBLANES, LANES), lambda b, p, pt: (b, 0, 0)),
        ),
        out_shape=jax.ShapeDtypeStruct((nb, n_pages + SUBLANES, LANES), F32),
        compiler_params=_cp("parallel", "arbitrary"),
        name="dsa_sample_scores",
    )(page_table, iq, iw, ik_new, *([cache_ik] * SCORE_PAGES))


def _dsa_s_select_body(sc_ref, ptc_ref, idx_ref, meta_ref, rank_ref, phys_ref, *, n_pages, topk):
    shape = (n_pages + SUBLANES, LANES)
    rows = lax.broadcasted_iota(jnp.int32, shape, 0)
    cols = lax.broadcasted_iota(jnp.int32, shape, 1)
    live = (rows < n_pages) | ((rows == n_pages) & (cols == 0))
    keys = jnp.where(live, _order_key(sc_ref[...]), INT_MIN)
    lo = jnp.full((1, 1), INT_MIN, jnp.int32)
    for bit in _RADIX_BITS:
        cand = lo + bit
        cnt = jnp.sum(jnp.sum(jnp.where(keys >= cand, 1.0, 0.0), axis=0, keepdims=True), axis=1, keepdims=True)
        lo = jnp.where(cnt >= topk, cand, lo)
    sel = keys >= jnp.maximum(lo, INT_MIN + 1)
    sel_c = sel[:n_pages, :]
    own = jnp.where(sel[n_pages:n_pages + 1, 0:1], 1, 0)

    r_i = lax.broadcasted_iota(jnp.int32, (LANES, LANES), 0)
    c_i = lax.broadcasted_iota(jnp.int32, (LANES, LANES), 1)
    ones_le = jnp.where(r_i <= c_i, 1.0, 0.0).astype(BF16)
    ones_gt = jnp.where(r_i > c_i, 1.0, 0.0).astype(BF16)
    m = jnp.where(sel_c, 1.0, 0.0).astype(BF16)
    within = jnp.dot(m, ones_le, preferred_element_type=F32)
    tot = jnp.broadcast_to(within[:, LANES - 1:LANES], (n_pages, LANES)).astype(BF16)
    before = jnp.dot(ones_gt, tot, preferred_element_type=F32)
    rank_ref[...] = jnp.where(sel_c, (within + before).astype(jnp.int32) - 1, -1)
    phys_ref[...] = (ptc_ref[...] * PAGE + c_i[:n_pages, :]).astype(F32)
    n_sel = (before[n_pages - 1:n_pages, 0:1] + within[n_pages - 1:n_pages, LANES - 1:LANES]).astype(jnp.int32)

    slot = lax.broadcasted_iota(jnp.int32, (topk, 1), 0)

    def place(r, acc):
        return acc + jnp.where(rank_ref[pl.ds(r, 1), :] == slot, phys_ref[pl.ds(r, 1), :], 0.0)

    acc = lax.fori_loop(0, n_pages, place, jnp.zeros((topk, LANES), F32), unroll=8)
    idx_ref[...] = jnp.sum(acc, axis=-1, keepdims=True).astype(jnp.int32)
    mrow = lax.broadcasted_iota(jnp.int32, (SUBLANES, LANES), 0)
    meta_ref[...] = jnp.where(mrow == 0, jnp.minimum(n_sel, topk), own)


def _dsa_s_select(scores, page_table):
    nb, n_pages = page_table.shape
    assert n_pages == LANES
    topk = min(TOPK_MAX, (n_pages * PAGE + 1) // 4)
    idx, meta = pl.pallas_call(
        functools.partial(_dsa_s_select_body, n_pages=n_pages, topk=topk),
        grid=(nb,),
        in_specs=[pl.BlockSpec((None, n_pages + SUBLANES, LANES), lambda b: (b, 0, 0)),
                  pl.BlockSpec((None, n_pages, 1), lambda b: (b, 0, 0))],
        out_specs=[pl.BlockSpec((None, topk, 1), lambda b: (b, 0, 0)),
                   pl.BlockSpec((None, SUBLANES, LANES), lambda b: (b, 0, 0))],
        out_shape=[jax.ShapeDtypeStruct((nb, topk, 1), jnp.int32),
                   jax.ShapeDtypeStruct((nb, SUBLANES, LANES), jnp.int32)],
        scratch_shapes=[pltpu.VMEM((n_pages, LANES), jnp.int32), pltpu.VMEM((n_pages, LANES), F32)],
        compiler_params=_cp("parallel"),
        name="dsa_sample_select",
    )(scores, page_table.reshape(nb, n_pages, 1))
    return idx.reshape(nb, topk), meta[:, 0:2, 0]


def _dsa_s_attend_body(idx_ref, meta_ref, q_ref, kn_ref, vn_ref, ck_ref, cv_ref, o_ref, kbuf, vbuf, sem,
                       *, layer, topk):
    b = pl.program_id(0)
    nb = pl.num_programs(0)
    slot = b % 2

    def row_copies(tok, buf_slot, j):
        row = idx_ref[tok, j]
        page, off = row // PAGE, row % PAGE
        dst = pl.ds(j * ATT_KV, ATT_KV)
        return (pltpu.make_async_copy(ck_ref.at[layer, page, off], kbuf.at[buf_slot, dst, :], sem.at[buf_slot, 0]),
                pltpu.make_async_copy(cv_ref.at[layer, page, off], vbuf.at[buf_slot, dst, :], sem.at[buf_slot, 1]))

    def start_gather(tok, buf_slot):
        def body(j, carry):
            for cp in row_copies(tok, buf_slot, j):
                cp.start()
            return carry
        lax.fori_loop(0, topk, body, 0)

    @pl.when(b == 0)
    def _():
        start_gather(0, 0)

    @pl.when(b + 1 < nb)
    def _():
        start_gather(b + 1, 1 - slot)

    def wait_body(j, carry):
        for cp in row_copies(b, slot, j):
            cp.wait()
        return carry

    lax.fori_loop(0, topk, wait_body, 0)

    n_sel = meta_ref[b, 0]
    own = meta_ref[b, 1] > 0
    qb = q_ref[...].astype(BF16)
    ncol = topk * ATT_KV
    col = lax.broadcasted_iota(jnp.int32, (ATT_HEADS, ncol), 1)
    head = lax.broadcasted_iota(jnp.int32, (ATT_HEADS, ncol), 0)
    valid = (col % ATT_KV == head // ATT_GROUP) & (col // ATT_KV < n_sel)
    s = jnp.where(valid, _nt(qb, kbuf[slot].astype(BF16)) * ATT_SCALE, NEG_BIG)
    kn = kn_ref[...].astype(BF16).astype(F32)
    vn = vn_ref[...].astype(BF16).astype(F32)
    expand = lambda a: jnp.concatenate(
        [jnp.broadcast_to(a[:, g * ATT_HD:(g + 1) * ATT_HD], (ATT_GROUP, ATT_HD)) for g in range(ATT_KV)], axis=0)
    s_own = jnp.where(own, jnp.sum(qb.astype(F32) * expand(kn), axis=-1, keepdims=True) * ATT_SCALE, NEG_BIG)
    m = jnp.maximum(jnp.max(s, axis=-1, keepdims=True), s_own)
    p = jnp.where(valid, jnp.exp(s - m), 0.0)
    p_own = jnp.where(own, jnp.exp(s_own - m), 0.0)
    l = jnp.sum(p, axis=-1, keepdims=True) + p_own
    acc = (jnp.dot(p.astype(BF16), vbuf[slot].astype(BF16), preferred_element_type=F32)
           + p_own.astype(BF16).astype(F32) * expand(vn))
    o_ref[...] = acc / l


def _dsa_s_attend(idx, meta, q, k_new, v_new, cache_k, cache_v, layer):
    nb, topk = idx.shape
    kvw = ATT_KV * ATT_HD
    return pl.pallas_call(
        functools.partial(_dsa_s_attend_body, layer=layer, topk=topk),
        grid_spec=pltpu.PrefetchScalarGridSpec(
            num_scalar_prefetch=2,
            grid=(nb,),
            in_specs=[pl.BlockSpec((None, ATT_HEADS, ATT_HD), lambda b, idx, meta: (b, 0, 0)),
                      pl.BlockSpec((None, 1, kvw), lambda b, idx, meta: (b, 0, 0)),
                      pl.BlockSpec((None, 1, kvw), lambda b, idx, meta: (b, 0, 0)),
                      pl.BlockSpec(memory_space=pl.ANY),
                      pl.BlockSpec(memory_space=pl.ANY)],
            out_specs=pl.BlockSpec((None, ATT_HEADS, ATT_HD), lambda b, idx, meta: (b, 0, 0)),
            scratch_shapes=[pltpu.VMEM((2, topk * ATT_KV, ATT_HD), F32),
                            pltpu.VMEM((2, topk * ATT_KV, ATT_HD), F32),
                            pltpu.SemaphoreType.DMA((2, 2))],
        ),
        out_shape=jax.ShapeDtypeStruct((nb, ATT_HEADS, ATT_HD), F32),
        compiler_params=_cp("arbitrary"),
        name="dsa_sample_attend",
    )(idx, meta, q, k_new, v_new, cache_k, cache_v)


HG_SUB = 16


HG_HB = 4
HG_EXP_LIMIT = 80.0


def _hgrn_body(lbl_ref, q_ref, f_ref, i_ref, g_ref, ng_ref, s0_ref, o_ref, so_ref, st_ref, *, layer, c, t, nc):
    ci = pl.program_id(2)
    mid = c // 2

    @pl.when(ci == 0)
    def _():
        for hh in range(HG_HB):
            st_ref[hh] = s0_ref[hh].T

    logits = lbl_ref[...]
    e = jnp.exp(logits - jnp.max(logits, axis=0, keepdims=True))
    soft = e / jnp.sum(e, axis=0, keepdims=True)
    lb_all = jnp.zeros((1, HG_HB * HG_DK), F32)
    for r in range(1, layer + 1):
        lb_all = lb_all + soft[r:r + 1, :]
    ng = ng_ref[...]

    def gates(hh, sl, row0):
        cs = slice(hh * HG_DK, (hh + 1) * HG_DK)
        lb = lb_all[:, cs]
        fg = lb + (1.0 - lb) * jax.nn.sigmoid(f_ref[sl, cs])
        lf = jnp.log(fg)
        kk = 1.0 - fg
        if t % c:
            n = lf.shape[0]
            valid = (ci * c + row0 + lax.broadcasted_iota(jnp.int32, (n, 1), 0)) < t
            lf = jnp.where(valid, lf, 0.0)
            kk = jnp.where(valid, kk, 0.0)
        return lf, kk

    def finish(hh, sl, o):
        cs = slice(hh * HG_DV, (hh + 1) * HG_DV)
        on = o * lax.rsqrt(jnp.mean(o * o, axis=-1, keepdims=True) + RMS_EPS) * ng
        o_ref[sl, cs] = (on * _silu(g_ref[sl, cs])).astype(o_ref.dtype)

    tri_c = lax.broadcasted_iota(jnp.int32, (c, c), 0) >= lax.broadcasted_iota(jnp.int32, (c, c), 1)
    full = pl.ds(0, c)
    lfs, kks, bs = [], [], []
    safe = None
    for hh in range(HG_HB):
        lf, kk = gates(hh, full, 0)
        b = _prefix_sums(tri_c.astype(BF16), lf)
        bm = b[mid - 1:mid, :]
        ok = jnp.min(jnp.minimum(bm, b[c - 1:c, :] - bm)) > -HG_EXP_LIMIT
        safe = ok if safe is None else jnp.logical_and(safe, ok)
        lfs.append(lf)
        kks.append(kk)
        bs.append(b)

    @pl.when(safe)
    def _():
        for hh in range(HG_HB):
            cs = slice(hh * HG_DK, (hh + 1) * HG_DK)
            b, kk = bs[hh], kks[hh]
            bm = b[mid - 1:mid, :]
            bl = b[c - 1:c, :]
            qq = _silu(q_ref[:, cs])
            vv = i_ref[:, cs].astype(BF16)
            att = jnp.where(tri_c, _nt((qq * jnp.exp(b - bm)).astype(BF16), (kk * jnp.exp(bm - b)).astype(BF16)), 0.0)
            st = st_ref[hh]
            o = (jnp.dot(att.astype(BF16), vv, preferred_element_type=F32)
                 + _nt((qq * jnp.exp(b)).astype(BF16), st.astype(BF16)))
            st_ref[hh] = st * jnp.exp(bl) + _tn(vv, (kk * jnp.exp(bl - b)).astype(BF16))
            finish(hh, full, o)

    @pl.when(jnp.logical_not(safe))
    def _():
        rows = lax.broadcasted_iota(jnp.int32, (HG_SUB, 1), 0)
        tri = (lax.broadcasted_iota(jnp.int32, (HG_SUB, HG_SUB), 0)
               >= lax.broadcasted_iota(jnp.int32, (HG_SUB, HG_SUB), 1)).astype(F32)
        for hh in range(HG_HB):
            cs = slice(hh * HG_DK, (hh + 1) * HG_DK)

            def sub_block(sb, carry, hh=hh, cs=cs):
                row0 = pl.multiple_of(sb * HG_SUB, HG_SUB)
                sl = pl.ds(row0, HG_SUB)
                lf, kk = gates(hh, sl, row0)
                qq = _silu(q_ref[sl, cs])
                vv = i_ref[sl, cs]
                b = _prefix_sums(tri.astype(BF16), lf)
                st = st_ref[hh]
                o = _nt((qq * jnp.exp(b)).astype(BF16), st.astype(BF16))
                for s in range(HG_SUB):
                    dec = jnp.exp(jnp.where(rows >= s, b - b[s:s + 1, :], -jnp.inf))
                    att = jnp.sum(qq * dec * kk[s:s + 1, :], axis=-1, keepdims=True)
                    o = o + att * vv[s:s + 1, :]
                bl = b[HG_SUB - 1:HG_SUB, :]
                st_ref[hh] = st * jnp.exp(bl) + _tn(vv.astype(BF16), (kk * jnp.exp(bl - b)).astype(BF16))
                finish(hh, sl, o)
                return carry

            lax.fori_loop(0, c // HG_SUB, sub_block, 0)

    @pl.when(ci == nc - 1)
    def _():
        for hh in range(HG_HB):
            so_ref[hh] = st_ref[hh].T


def _hgrn(proj, lb_logits, norm_g, s0, layer, t, c=128):
    nb, tpad, _ = proj.shape
    nc = tpad // c
    nhb = HG_HEADS // HG_HB
    w = HG_HB * HG_DK
    return pl.pallas_call(
        functools.partial(_hgrn_body, layer=layer, c=c, t=t, nc=nc),
        grid=(nb, nhb, nc),
        in_specs=[pl.BlockSpec((DEPTH, w), lambda b, h, ci: (0, h)),
                  pl.BlockSpec((None, c, w), lambda b, h, ci: (b, ci, h)),
                  pl.BlockSpec((None, c, w), lambda b, h, ci: (b, ci, nhb + h)),
                  pl.BlockSpec((None, c, w), lambda b, h, ci: (b, ci, 2 * nhb + h)),
                  pl.BlockSpec((None, c, w), lambda b, h, ci: (b, ci, 3 * nhb + h)),
                  pl.BlockSpec((1, HG_DV), lambda b, h, ci: (0, 0)),
                  pl.BlockSpec((None, HG_HB, HG_DK, HG_DV), lambda b, h, ci: (b, h, 0, 0))],
        out_specs=[pl.BlockSpec((None, c, w), lambda b, h, ci: (b, ci, h)),
                   pl.BlockSpec((None, HG_HB, HG_DK, HG_DV), lambda b, h, ci: (b, h, 0, 0))],
        out_shape=[jax.ShapeDtypeStruct((nb, tpad, HG_HEADS * HG_DV), BF16),
                   jax.ShapeDtypeStruct((nb, HG_HEADS, HG_DK, HG_DV), F32)],
        scratch_shapes=[pltpu.VMEM((HG_HB, HG_DV, HG_DK), F32)],
        compiler_params=_cp("parallel", "parallel", "arbitrary"),
        name="hgrn2",
    )(lb_logits, proj, proj, proj, proj, norm_g.reshape(1, HG_DV), s0)


def _expand_heads(v, e, terms):
    out = None
    rest = v
    for _ in range(terms):
        part = rest.astype(BF16)
        rest = rest - part.astype(F32)
        d = jnp.dot(part, e, preferred_element_type=F32)
        out = d if out is None else out + d
    return out


def _ssd_body(zx_ref, dt_ref, cs_ref, cw_ref, cbias_ref, e_ref, dtb_ref, alog_ref, dx_ref, ng_ref, s0_ref,
              o_ref, so_ref, nc_ref, st_ref, xc_ref, halo_ref, *, c, t, nc):
    ci = pl.program_id(1)
    gw = SSM_HPG * SSM_P

    @pl.when(ci == 0)
    def _():
        for blk in range(SSM_INNER // LANES):
            st_ref[:, blk * LANES:(blk + 1) * LANES] = s0_ref[blk * LANES:(blk + 1) * LANES, :].T
        halo_ref[...] = cs_ref[...]

    full = jnp.concatenate([halo_ref[...], zx_ref[:, SSM_INNER:SSM_INNER + SSM_CH]], axis=0)
    conv = cbias_ref[...]
    for k in range(SSM_CONV):
        lo = SUBLANES - (SSM_CONV - 1) + k
        conv = conv + full[lo:lo + c, :] * cw_ref[k:k + 1, :]
    xc_ref[...] = _silu(conv)
    halo_ref[...] = full[c:c + SUBLANES, :]

    @pl.when(ci == nc - 1)
    def _():
        tv = t - (nc - 1) * c
        tail = full[SUBLANES + tv - (SSM_CONV - 1):SUBLANES + tv, :]
        nc_ref[...] = jnp.concatenate([tail, jnp.zeros((SUBLANES - (SSM_CONV - 1), SSM_CH), F32)], axis=0)

    e = e_ref[...]
    rows = lax.broadcasted_iota(jnp.int32, (c, 1), 0)
    tri_b = lax.broadcasted_iota(jnp.int32, (c, c), 0) >= lax.broadcasted_iota(jnp.int32, (c, c), 1)
    lane_lo = lax.broadcasted_iota(jnp.int32, (1, LANES), 1) < SSM_P

    dt = jax.nn.softplus(dt_ref[...] + dtb_ref[...])
    if t % c:
        dt = jnp.where(ci * c + rows < t, dt, 0.0)
    da = dt * (-jnp.exp(alog_ref[...]))
    bcum = _prefix_sums(tri_b.astype(BF16), da)
    bcum_t = bcum.T
    dt_t = dt.T
    bl = bcum[c - 1:c, :]
    eb_x = _expand_heads(jnp.exp(bcum), e, 2)
    w_x = _expand_heads(jnp.exp(bl - bcum) * dt, e, 2)
    decay_x = _expand_heads(jnp.broadcast_to(jnp.exp(bl), (SUBLANES, LANES)), e, 3)[0:1, :]

    xs = xc_ref[:, 0:SSM_INNER]
    xdt = xs.astype(BF16)
    xw = (xs * w_x).astype(BF16)
    y = xs * dx_ref[...]
    zg = _silu(zx_ref[:, 0:SSM_INNER])
    for g in range(SSM_GROUPS):
        bg = xc_ref[:, SSM_INNER + g * SSM_N:SSM_INNER + (g + 1) * SSM_N]
        cg = xc_ref[:, SSM_INNER + (SSM_GROUPS + g) * SSM_N:SSM_INNER + (SSM_GROUPS + g + 1) * SSM_N].astype(BF16)
        cb = _nt(cg, bg.astype(BF16))
        st_g = st_ref[:, g * gw:(g + 1) * gw]
        yg = jnp.dot(cg, st_g.astype(BF16), preferred_element_type=F32) * eb_x[:, g * gw:(g + 1) * gw]
        parts = []
        for jp in range(SSM_HPG // 2):
            xpair = xdt[:, g * gw + jp * LANES:g * gw + (jp + 1) * LANES]
            acc = None
            for half in range(2):
                h = g * SSM_HPG + jp * 2 + half
                dec = jnp.exp(jnp.where(tri_b, bcum[:, h:h + 1] - bcum_t[h:h + 1, :], -jnp.inf))
                w = (cb * dec * dt_t[h:h + 1, :]).astype(BF16)
                xh = jnp.where(lane_lo if half == 0 else jnp.logical_not(lane_lo), xpair, 0.0).astype(BF16)
                r = jnp.dot(w, xh, preferred_element_type=F32)
                acc = r if acc is None else acc + r
            parts.append(acc)
        yg = yg + jnp.concatenate(parts, axis=1)
        st_ref[:, g * gw:(g + 1) * gw] = (st_g * decay_x[:, g * gw:(g + 1) * gw]
                                          + jnp.dot(bg.T.astype(BF16), xw[:, g * gw:(g + 1) * gw],
                                                    preferred_element_type=F32))
        yg = (yg + y[:, g * gw:(g + 1) * gw]) * zg[:, g * gw:(g + 1) * gw]
        yg = yg * lax.rsqrt(jnp.mean(yg * yg, axis=-1, keepdims=True) + RMS_EPS) * ng_ref[:, g * gw:(g + 1) * gw]
        o_ref[:, g * gw:(g + 1) * gw] = yg.astype(o_ref.dtype)

    @pl.when(ci == nc - 1)
    def _():
        for blk in range(SSM_INNER // LANES):
            so_ref[blk * LANES:(blk + 1) * LANES, :] = st_ref[:, blk * LANES:(blk + 1) * LANES].T


def _ssd(zx, dt_raw, cs_pad, conv_w, conv_b, expand, dt_bias, a_log, d_x, norm_g, s0, t, c=128):
    nb, tpad, _ = zx.shape
    nc = tpad // c
    return pl.pallas_call(
        functools.partial(_ssd_body, c=c, t=t, nc=nc),
        grid=(nb, nc),
        in_specs=[pl.BlockSpec((None, c, SSM_MAIN), lambda b, ci: (b, ci, 0)),
                  pl.BlockSpec((None, c, LANES), lambda b, ci: (b, ci, 0)),
                  pl.BlockSpec((None, SUBLANES, SSM_CH), lambda b, ci: (b, 0, 0)),
                  pl.BlockSpec((SSM_CONV, SSM_CH), lambda b, ci: (0, 0)),
                  pl.BlockSpec((1, SSM_CH), lambda b, ci: (0, 0)),
                  pl.BlockSpec((LANES, SSM_INNER), lambda b, ci: (0, 0)),
                  pl.BlockSpec((1, LANES), lambda b, ci: (0, 0)),
                  pl.BlockSpec((1, LANES), lambda b, ci: (0, 0)),
                  pl.BlockSpec((1, SSM_INNER), lambda b, ci: (0, 0)),
                  pl.BlockSpec((1, SSM_INNER), lambda b, ci: (0, 0)),
                  pl.BlockSpec((None, SSM_INNER, SSM_N), lambda b, ci: (b, 0, 0))],
        out_specs=[pl.BlockSpec((None, c, SSM_INNER), lambda b, ci: (b, ci, 0)),
                   pl.BlockSpec((None, SSM_INNER, SSM_N), lambda b, ci: (b, 0, 0)),
                   pl.BlockSpec((None, SUBLANES, SSM_CH), lambda b, ci: (b, 0, 0))],
        out_shape=[jax.ShapeDtypeStruct((nb, tpad, SSM_INNER), BF16),
                   jax.ShapeDtypeStruct((nb, SSM_INNER, SSM_N), F32),
                   jax.ShapeDtypeStruct((nb, SUBLANES, SSM_CH), F32)],
        scratch_shapes=[pltpu.VMEM((SSM_N, SSM_INNER), F32),
                        pltpu.VMEM((c, SSM_CH), F32),
                        pltpu.VMEM((SUBLANES, SSM_CH), F32)],
        compiler_params=_cp("parallel", "arbitrary"),
        name="ssd",
    )(zx, dt_raw, cs_pad, conv_w, conv_b.reshape(1, SSM_CH), expand, dt_bias, a_log, d_x,
      norm_g.reshape(1, SSM_INNER), s0)


def _pad_cols(w, n):
    return jnp.pad(w, ((0, 0), (0, n - w.shape[1])))


def _pad_time(a, tpad):
    return jnp.pad(a, ((0, 0), (0, tpad - a.shape[1]), (0, 0)))


def kernel(x_prompt, x_sample, cache_k, cache_v, cache_idx_k, state_hgrn, state_ssm, state_conv, page_table, p_prompt, p_sample, ln_g, ln_b, ffn_w_gate_up, ffn_w_down, ple_w_proj, ple_w_gate, att_w_in, att_idx_k_norm, att_w_o, hg_w_in, hg_lb_logits, hg_norm_g, hg_w_o, ssm_w_in, ssm_conv_w, ssm_conv_b, ssm_dt_bias, ssm_a_log, ssm_d, ssm_norm_g, ssm_w_o):
    nbp, seq, d = x_prompt.shape
    nbs = x_sample.shape[0]
    mp = nbp * seq
    ms = 16
    chunk = 128

    tm, tm_ln = 1024, 512
    x_p = x_prompt.reshape(mp, d)
    x_s = jnp.pad(x_sample.reshape(nbs, d), ((0, ms - nbs), (0, 0)))
    xb_p, xb_s = x_p.astype(BF16), x_s.astype(BF16)
    pl_p = p_prompt.reshape(DEPTH, mp, PLE_DIM)
    pl_s = jnp.pad(p_sample.reshape(DEPTH, nbs, PLE_DIM), ((0, 0), (0, ms - nbs), (0, 0)))
    pad_rows = lambda a: jnp.pad(a, ((0, ms - nbs), (0, 0)))

    expand = jnp.asarray(np.kron(np.eye(LANES, SSM_HEADS, dtype=np.float32),
                                 np.ones((1, SSM_P), np.float32)), BF16)
    outs = {}

    def ffn_ln(i, which, ln_idx, x_p, xb_p, x_s, xb_s):
        h_p, h_s, wd = _mm_swiglu(xb_p, xb_s, ffn_w_gate_up, ffn_w_down, (i, which), tm, 512)
        return _mm_ln(h_p, h_s, wd, x_p, x_s, ln_g[i, ln_idx], ln_b[i, ln_idx], 0.5, tm_ln, "ffn_down_ln")

    for i in range(DEPTH):
        j = i // N_MIXERS
        (x_p, xb_p), (x_s, xb_s) = ffn_ln(i, 0, 0, x_p, xb_p, x_s, xb_s)

        if i % N_MIXERS == 0:
            w_small = _pad_cols(att_w_in[j][:, ATT_MAIN:], LANES)
            w_o = _cast_w(att_w_o, (j,))
            qw, kvw = ATT_HEADS * ATT_HD, ATT_KV * ATT_HD
            (proj, projb), (proj_s, _) = _proj(xb_p, xb_s, att_w_in, (j,), ATT_MAIN, tm, 1024, with_bf16=True,
                                               name="att_in")
            proj2, proj2_s = _proj(xb_p, xb_s, w_small, (), LANES, tm, LANES, name="att_in_idx")
            ik, ikb = _ik_norm(proj2, att_idx_k_norm[j], tm)
            ik_s, _ = _ik_norm(proj2_s, att_idx_k_norm[j], ms)
            o_p = _dsa_prompt(proj, projb, proj2, ikb, nbp, seq)
            outs.setdefault("k_p", []).append(proj[:, qw:qw + kvw].reshape(nbp, seq, ATT_KV, ATT_HD))
            outs.setdefault("v_p", []).append(proj[:, qw + kvw:qw + 2 * kvw].reshape(nbp, seq, ATT_KV, ATT_HD))
            outs.setdefault("ik_p", []).append(ik.reshape(nbp, seq, IDX_DIM))
            pr = proj_s[:nbs]
            k_new = pr[:, qw:qw + kvw]
            v_new = pr[:, qw + kvw:qw + 2 * kvw]
            ik_new = ik_s[:nbs]
            scores = _dsa_s_scores(page_table,
                                   pr[:, qw + 2 * kvw:].reshape(nbs, IDX_HEADS, IDX_DIM),
                                   proj2_s[:nbs, IDX_DIM:IDX_DIM + IDX_HEADS].reshape(nbs, IDX_HEADS, 1),
                                   ik_new.reshape(nbs, 1, IDX_DIM), cache_idx_k, j)
            idx, meta = _dsa_s_select(scores, page_table)
            o_s = _dsa_s_attend(idx, meta, pr[:, :qw].reshape(nbs, ATT_HEADS, ATT_HD),
                                k_new.reshape(nbs, 1, kvw), v_new.reshape(nbs, 1, kvw), cache_k, cache_v, j)
            o_s = pad_rows(o_s.reshape(nbs, qw)).astype(BF16)
            outs.setdefault("k_s", []).append(k_new.reshape(nbs, 1, ATT_KV, ATT_HD))
            outs.setdefault("v_s", []).append(v_new.reshape(nbs, 1, ATT_KV, ATT_HD))
            outs.setdefault("ik_s", []).append(ik_new.reshape(nbs, 1, IDX_DIM))
        elif i % N_MIXERS == 1:
            w_o = _cast_w(hg_w_o, (j,))
            proj, proj_s = _proj(xb_p, xb_s, hg_w_in, (j,), hg_w_in.shape[-1], tm, 1024, name="hg_in")
            s0 = jnp.zeros((nbp, HG_HEADS, HG_DK, HG_DV), F32)
            o_p, s_fin = _hgrn(proj.reshape(nbp, seq, -1), hg_lb_logits, hg_norm_g[j], s0, i, seq, chunk)
            o_p = o_p.reshape(mp, -1)
            outs.setdefault("hg_p", []).append(s_fin)
            pr = _pad_time(proj_s[:nbs].reshape(nbs, 1, -1), chunk)
            o_s, s_fin = _hgrn(pr, hg_lb_logits, hg_norm_g[j], state_hgrn[j], i, 1, chunk)
            o_s = pad_rows(o_s[:, 0, :])
            outs.setdefault("hg_s", []).append(s_fin)
        else:
            w_small = _pad_cols(ssm_w_in[j][:, SSM_MAIN:], LANES)
            w_o = _cast_w(ssm_w_o, (j,))
            dt_bias = _pad_cols(ssm_dt_bias[j].reshape(1, SSM_HEADS), LANES)
            a_log = _pad_cols(ssm_a_log[j].reshape(1, SSM_HEADS), LANES)
            d_x = jnp.repeat(ssm_d[j], SSM_P).reshape(1, SSM_INNER)
            zx_p, zx_s = _proj(xb_p, xb_s, ssm_w_in, (j,), SSM_MAIN, tm, 1024, name="ssm_in")
            dtr_p, dtr_s = _proj(xb_p, xb_s, w_small, (), LANES, tm, LANES, name="ssm_in_dt")
            mix = {}
            for name, zx, dtr in (("p", zx_p, dtr_p), ("s", zx_s, dtr_s)):
                if name == "p":
                    nb_, t_ = nbp, seq
                    zx3 = zx.reshape(nbp, seq, -1)
                    dt3 = dtr.reshape(nbp, seq, LANES)
                    cs = jnp.zeros((nbp, SUBLANES, SSM_CH), F32)
                    s0 = jnp.zeros((nbp, SSM_INNER, SSM_N), F32)
                else:
                    nb_, t_ = nbs, 1
                    zx3 = _pad_time(zx[:nbs].reshape(nbs, 1, -1), chunk)
                    dt3 = _pad_time(dtr[:nbs].reshape(nbs, 1, LANES), chunk)
                    cs = jnp.pad(state_conv[j], ((0, 0), (SUBLANES - (SSM_CONV - 1), 0), (0, 0)))
                    s0 = state_ssm[j].reshape(nbs, SSM_INNER, SSM_N)
                y, s_fin, new_conv = _ssd(zx3, dt3, cs, ssm_conv_w[j], ssm_conv_b[j], expand, dt_bias, a_log, d_x,
                                          ssm_norm_g[j], s0, t_, chunk)
                s_fin = s_fin.reshape(nb_, SSM_HEADS, SSM_P, SSM_N)
                new_conv = new_conv[:, :SSM_CONV - 1, :]
                if name == "p":
                    mix[name] = y.reshape(mp, SSM_INNER)
                    outs.setdefault("ssm_p", []).append(s_fin)
                    outs.setdefault("conv_p", []).append(new_conv)
                else:
                    mix[name] = pad_rows(y[:, 0, :])
                    outs.setdefault("ssm_s", []).append(s_fin)
                    outs.setdefault("conv_s", []).append(new_conv)
            o_p, o_s = mix["p"], mix["s"]

        (x_p, xb_p), (x_s, xb_s) = _mm_ln(o_p, o_s, w_o, x_p, x_s, ln_g[i, 1], ln_b[i, 1], 1.0, tm_ln, "mixer_out_ln")
        (x_p, xb_p), (x_s, xb_s) = ffn_ln(i, 1, 2, x_p, xb_p, x_s, xb_s)
        (x_p, xb_p), (x_s, xb_s) = _mm_ple(x_p, xb_p, pl_p[i].astype(BF16), x_s, xb_s, pl_s[i].astype(BF16),
                                           ple_w_gate, ple_w_proj, (i,), tm_ln, 1024)

    y_prompt = x_p.reshape(nbp, seq, d)
    y_sample = x_s[:nbs].reshape(nbs, 1, d)
    stack = lambda key: jnp.stack(outs[key])
    return (y_prompt, y_sample, stack("k_p"), stack("v_p"), stack("ik_p"), stack("k_s"), stack("v_s"), stack("ik_s"),
            stack("hg_p"), stack("hg_s"), stack("ssm_p"), stack("ssm_s"), stack("conv_p"), stack("conv_s"))
```

```python
import functools
import math

import jax
import jax.numpy as jnp
import numpy as np
from jax import lax
from jax.experimental import pallas as pl
from jax.experimental.pallas import tpu as pltpu

F32 = jnp.float32
BF16 = jnp.bfloat16

D_MODEL = 2048
DEPTH = 4
N_MIXERS = 3
D_FF = 2 * D_MODEL
PLE_DIM = 256
ALPHA = (2 * DEPTH) ** 0.25
LN_EPS = 1e-5
RMS_EPS = 1e-6
PAGE = 128

ATT_HD = 128
ATT_HEADS = 16
ATT_KV = 4
ATT_GROUP = ATT_HEADS // ATT_KV
IDX_HEADS = 16
IDX_DIM = 64
IDX_W_SCALE = (IDX_HEADS ** -0.5) * (IDX_DIM ** -0.5)
TOPK_MAX = 256
ATT_SCALE = ATT_HD ** -0.5
LOG2E = math.log2(math.e)
ATT_MAIN = ATT_HEADS * ATT_HD + 2 * ATT_KV * ATT_HD + IDX_HEADS * IDX_DIM

HG_HEADS = 16
HG_DK = 128
HG_DV = 128

SSM_INNER = 2 * D_MODEL
SSM_P = 64
SSM_HEADS = SSM_INNER // SSM_P
SSM_GROUPS = 8
SSM_HPG = SSM_HEADS // SSM_GROUPS
SSM_N = 128
SSM_CONV = 4
SSM_CH = SSM_INNER + 2 * SSM_GROUPS * SSM_N
SSM_MAIN = SSM_INNER + SSM_CH

LANES = 128
SUBLANES = 8
VMEM_LIMIT_BYTES = 56 * 1024 * 1024

INT_MIN = np.int32(-2 ** 31)
NEG_BIG = -1e30


def _cp(*sem):
    return pltpu.CompilerParams(dimension_semantics=sem, vmem_limit_bytes=VMEM_LIMIT_BYTES)


def _nt(a, b):
    return lax.dot_general(a, b, (((1,), (1,)), ((), ())), preferred_element_type=F32)


def _tn(a, b):
    return lax.dot_general(a, b, (((0,), (0,)), ((), ())), preferred_element_type=F32)


def _silu(x):
    return x * jax.nn.sigmoid(x)


def _prefix_sums(tri, v):
    out = None
    rest = v
    for _ in range(3):
        part = rest.astype(BF16)
        rest = rest - part.astype(F32)
        d = jnp.dot(tri, part, preferred_element_type=F32)
        out = d if out is None else out + d
    return out


def _wspec(k, tn, idx, col_block):
    lead = (None,) * len(idx)
    return pl.BlockSpec(lead + (k, tn), lambda n, i: idx + (0, col_block(n)))


def _proj_body(x_ref, xs_ref, w_ref, *rest, with_bf16, w_is_nk):
    wb_ref = rest[-1]
    outs, outs_s = (rest[0:2], rest[2:4]) if with_bf16 else (rest[0:1], rest[1:2])
    mm = _nt if w_is_nk else functools.partial(jnp.dot, preferred_element_type=F32)

    def emit(refs, acc):
        refs[0][...] = acc
        if with_bf16:
            refs[1][...] = acc.astype(BF16)

    @pl.when(pl.program_id(1) == 0)
    def _():
        wb_ref[...] = w_ref[...].astype(BF16)
        emit(outs_s, mm(xs_ref[...], wb_ref[...]))

    emit(outs, mm(x_ref[...], wb_ref[...]))


def _proj(xb, xsb, w, idx, n_cols, tm, tn, with_bf16=False, w_is_nk=False, name="proj"):
    m, k = xb.shape
    ms = xsb.shape[0]
    dts = (F32, BF16) if with_bf16 else (F32,)
    if w_is_nk:
        w_spec = pl.BlockSpec((None,) * len(idx) + (tn, k), lambda n, i: idx + (n, 0))
        w_tile = (tn, k)
    else:
        w_spec = _wspec(k, tn, idx, lambda n: n)
        w_tile = (k, tn)
    res = pl.pallas_call(
        functools.partial(_proj_body, with_bf16=with_bf16, w_is_nk=w_is_nk),
        grid=(n_cols // tn, m // tm),
        in_specs=[pl.BlockSpec((tm, k), lambda n, i: (i, 0)),
                  pl.BlockSpec((ms, k), lambda n, i: (0, 0)),
                  w_spec],
        out_specs=[pl.BlockSpec((tm, tn), lambda n, i: (i, n)) for _ in dts]
                  + [pl.BlockSpec((ms, tn), lambda n, i: (0, n)) for _ in dts],
        out_shape=[jax.ShapeDtypeStruct((m, n_cols), dt) for dt in dts]
                  + [jax.ShapeDtypeStruct((ms, n_cols), dt) for dt in dts],
        scratch_shapes=[pltpu.VMEM(w_tile, BF16)],
        compiler_params=_cp("parallel", "arbitrary"),
        name=name,
    )(xb, xsb, w)
    nd = len(dts)
    return (res[:nd], res[nd:]) if with_bf16 else (res[0], res[1])


def _swiglu_body(x_ref, xs_ref, wg_ref, wu_ref, wd_ref, o_ref, os_ref, wdb_ref, wgb_ref, wub_ref):
    def swiglu(x):
        g = jnp.dot(x, wgb_ref[...], preferred_element_type=F32)
        u = jnp.dot(x, wub_ref[...], preferred_element_type=F32)
        return (_silu(g) * u).astype(BF16)

    @pl.when(pl.program_id(1) == 0)
    def _():
        wgb_ref[...] = wg_ref[...].astype(BF16)
        wub_ref[...] = wu_ref[...].astype(BF16)
        wdb_ref[...] = wd_ref[...].astype(BF16)
        os_ref[...] = swiglu(xs_ref[...])

    o_ref[...] = swiglu(x_ref[...])


def _mm_swiglu(xb, xsb, wgu, wd, idx, tm, tn):
    m, k = xb.shape
    ms = xsb.shape[0]
    f = wgu.shape[-1] // 2
    nj = f // tn
    n_out = wd.shape[-1]
    lead = (None,) * len(idx)
    return pl.pallas_call(
        _swiglu_body,
        grid=(nj, m // tm),
        in_specs=[pl.BlockSpec((tm, k), lambda n, i: (i, 0)),
                  pl.BlockSpec((ms, k), lambda n, i: (0, 0)),
                  _wspec(k, tn, idx, lambda n: n),
                  _wspec(k, tn, idx, lambda n: n + nj),
                  pl.BlockSpec(lead + (f // nj, n_out), lambda n, i: idx + (n, 0))],
        out_specs=[pl.BlockSpec((tm, tn), lambda n, i: (i, n)),
                   pl.BlockSpec((ms, tn), lambda n, i: (0, n)),
                   pl.BlockSpec((f // nj, n_out), lambda n, i: (n, 0))],
        out_shape=[jax.ShapeDtypeStruct((m, f), BF16), jax.ShapeDtypeStruct((ms, f), BF16),
                   jax.ShapeDtypeStruct((f, n_out), BF16)],
        scratch_shapes=[pltpu.VMEM((k, tn), BF16), pltpu.VMEM((k, tn), BF16)],
        compiler_params=_cp("parallel", "arbitrary"),
        name="ffn_up",
    )(xb, xsb, wgu, wgu, wd)


def _cast_body(w_ref, o_ref):
    o_ref[...] = w_ref[...].astype(BF16)


def _cast_w(w, idx, tk=512):
    k, n = w.shape[-2:]
    lead = (None,) * len(idx)
    return pl.pallas_call(
        _cast_body,
        grid=(k // tk,),
        in_specs=[pl.BlockSpec(lead + (tk, n), lambda i: idx + (i, 0))],
        out_specs=pl.BlockSpec((tk, n), lambda i: (i, 0)),
        out_shape=jax.ShapeDtypeStruct((k, n), BF16),
        compiler_params=_cp("parallel"),
        name="cast_w",
    )(w)


LN_SPLIT = 2


def _mm_ln_body(a_ref, as_ref, w_ref, r_ref, rs_ref, g_ref, b_ref, o_ref, ob_ref, os_ref, osb_ref, *, scale, tm):
    g = g_ref[...]
    b = b_ref[...]

    def ln_rows(a, r, o, ob, sl):
        y = ALPHA * r[sl, :] + scale * jnp.dot(a[sl, :], w_ref[...], preferred_element_type=F32)
        mu = jnp.mean(y, axis=-1, keepdims=True)
        yc = y - mu
        var = jnp.mean(yc * yc, axis=-1, keepdims=True)
        out = yc * lax.rsqrt(var + LN_EPS) * g + b
        o[sl, :] = out
        ob[sl, :] = out.astype(BF16)

    @pl.when(pl.program_id(0) == 0)
    def _():
        ln_rows(as_ref, rs_ref, os_ref, osb_ref, pl.ds(0, as_ref.shape[0]))

    rows = tm // LN_SPLIT
    for r in range(0, tm, rows):
        ln_rows(a_ref, r_ref, o_ref, ob_ref, pl.ds(r, rows))


def _mm_ln(ab, asb, wb, res, res_s, g, b, scale, tm, name):
    m, kdim = ab.shape
    ms = asb.shape[0]
    n = wb.shape[1]
    res = pl.pallas_call(
        functools.partial(_mm_ln_body, scale=scale, tm=tm),
        grid=(m // tm,),
        in_specs=[pl.BlockSpec((tm, kdim), lambda i: (i, 0)),
                  pl.BlockSpec((ms, kdim), lambda i: (0, 0)),
                  pl.BlockSpec((kdim, n), lambda i: (0, 0), pipeline_mode=pl.Buffered(1)),
                  pl.BlockSpec((tm, n), lambda i: (i, 0)),
                  pl.BlockSpec((ms, n), lambda i: (0, 0)),
                  pl.BlockSpec((1, n), lambda i: (0, 0)),
                  pl.BlockSpec((1, n), lambda i: (0, 0))],
        out_specs=[pl.BlockSpec((tm, n), lambda i: (i, 0)),
                   pl.BlockSpec((tm, n), lambda i: (i, 0)),
                   pl.BlockSpec((ms, n), lambda i: (0, 0)),
                   pl.BlockSpec((ms, n), lambda i: (0, 0))],
        out_shape=[jax.ShapeDtypeStruct((m, n), F32), jax.ShapeDtypeStruct((m, n), BF16),
                   jax.ShapeDtypeStruct((ms, n), F32), jax.ShapeDtypeStruct((ms, n), BF16)],
        compiler_params=_cp("arbitrary"),
        name=name,
    )(ab, asb, wb, res, res_s, g.reshape(1, n), b.reshape(1, n))
    return res[:2], res[2:]


def _ple_body(xb_ref, p_ref, x_ref, xsb_ref, ps_ref, xs_ref, wg_ref, wp_ref, o_ref, ob_ref, os_ref, osb_ref,
              wgb_ref, wpb_ref):
    def ple(xb, p, x, o, ob):
        gate = jax.nn.sigmoid(jnp.dot(xb[...], wgb_ref[...], preferred_element_type=F32))
        proj = jnp.dot(p[...], wpb_ref[...], preferred_element_type=F32)
        out = x[...] + gate * proj
        o[...] = out
        ob[...] = out.astype(BF16)

    @pl.when(pl.program_id(1) == 0)
    def _():
        wgb_ref[...] = wg_ref[...].astype(BF16)
        wpb_ref[...] = wp_ref[...].astype(BF16)
        ple(xsb_ref, ps_ref, xs_ref, os_ref, osb_ref)

    ple(xb_ref, p_ref, x_ref, o_ref, ob_ref)


def _mm_ple(x32, xb, pb, xs32, xsb, psb, wg, wp, idx, tm, tn):
    m, d = xb.shape
    ms = xsb.shape[0]
    pd = pb.shape[1]
    res = pl.pallas_call(
        _ple_body,
        grid=(d // tn, m // tm),
        in_specs=[pl.BlockSpec((tm, d), lambda n, i: (i, 0)),
                  pl.BlockSpec((tm, pd), lambda n, i: (i, 0)),
                  pl.BlockSpec((tm, tn), lambda n, i: (i, n)),
                  pl.BlockSpec((ms, d), lambda n, i: (0, 0)),
                  pl.BlockSpec((ms, pd), lambda n, i: (0, 0)),
                  pl.BlockSpec((ms, tn), lambda n, i: (0, n)),
                  _wspec(d, tn, idx, lambda n: n),
                  _wspec(pd, tn, idx, lambda n: n)],
        out_specs=[pl.BlockSpec((tm, tn), lambda n, i: (i, n)),
                   pl.BlockSpec((tm, tn), lambda n, i: (i, n)),
                   pl.BlockSpec((ms, tn), lambda n, i: (0, n)),
                   pl.BlockSpec((ms, tn), lambda n, i: (0, n))],
        out_shape=[jax.ShapeDtypeStruct((m, d), F32), jax.ShapeDtypeStruct((m, d), BF16),
                   jax.ShapeDtypeStruct((ms, d), F32), jax.ShapeDtypeStruct((ms, d), BF16)],
        scratch_shapes=[pltpu.VMEM((d, tn), BF16), pltpu.VMEM((pd, tn), BF16)],
        compiler_params=_cp("parallel", "arbitrary"),
        name="ple",
    )(xb, pb, x32, xsb, psb, xs32, wg, wp)
    return res[:2], res[2:]


def _ik_norm_body(p_ref, g_ref, o_ref, ob_ref):
    x = p_ref[...][:, :IDX_DIM]
    mu = jnp.mean(x, axis=-1, keepdims=True)
    xc = x - mu
    out = xc * lax.rsqrt(jnp.mean(xc * xc, axis=-1, keepdims=True) + LN_EPS) * g_ref[...]
    o_ref[...] = out
    ob_ref[...] = out.astype(BF16)


def _ik_norm(proj2, ik_g, tm):
    m = proj2.shape[0]
    return pl.pallas_call(
        _ik_norm_body,
        grid=(m // tm,),
        in_specs=[pl.BlockSpec((tm, LANES), lambda i: (i, 0)),
                  pl.BlockSpec((1, IDX_DIM), lambda i: (0, 0))],
        out_specs=[pl.BlockSpec((tm, IDX_DIM), lambda i: (i, 0)),
                   pl.BlockSpec((tm, IDX_DIM), lambda i: (i, 0))],
        out_shape=[jax.ShapeDtypeStruct((m, IDX_DIM), F32), jax.ShapeDtypeStruct((m, IDX_DIM), BF16)],
        compiler_params=_cp("parallel"),
        name="idx_k_norm",
    )(proj2, ik_g.reshape(1, IDX_DIM))


def _order_key(x):
    bits = pltpu.bitcast(x, jnp.int32)
    return jnp.where(bits < 0, bits ^ jnp.int32(0x7FFFFFFF), bits)


_RADIX_BITS = [INT_MIN] + [np.int32(1 << s) for s in range(30, -1, -1)]


RADIX_ROWS = 128
KNORM_ROWS = 512
SUM_FLOOR = 2.0 ** -100


def _dsa_prompt_body(q_ref, iq_ref, iw_ref, k_ref, v_ref, ik_ref, o_ref, keys_ref, knorm_ref, *, qb, kc, topk):
    i = pl.program_id(1)
    nck = ((i + 1) * qb + kc - 1) // kc
    row_pos = i * qb + lax.broadcasted_iota(jnp.int32, (qb, 1), 0)
    iq = iq_ref[...]
    iw = iw_ref[...][:, IDX_DIM:IDX_DIM + IDX_HEADS] * IDX_W_SCALE
    iq_h = [iq[:, h * IDX_DIM:(h + 1) * IDX_DIM] for h in range(IDX_HEADS)]
    iw_h = [iw[:, h:h + 1] for h in range(IDX_HEADS)]
    col0 = lax.broadcasted_iota(jnp.int32, (1, kc), 1)

    def score_chunk(c, carry):
        off = pl.multiple_of(c * kc, kc)
        ikc = ik_ref[pl.ds(off, kc), :]
        sc = jnp.zeros((qb, kc), F32)
        for h in range(IDX_HEADS):
            sc = sc + iw_h[h] * jnp.maximum(_nt(iq_h[h], ikc), 0.0)
        key = jnp.where(col0 + off <= row_pos, _order_key(sc), INT_MIN)
        keys_ref[:, pl.ds(off, kc)] = key
        return carry

    lax.fori_loop(0, nck, score_chunk, 0)

    count_kc = 2 * kc
    n_count = (nck + 1) // 2

    @pl.when(nck % 2 == 1)
    def _():
        keys_ref[:, pl.ds(pl.multiple_of(nck * kc, kc), kc)] = jnp.full((qb, kc), INT_MIN, jnp.int32)

    nblk = qb // RADIX_ROWS
    los = [jnp.full((RADIX_ROWS, 1), INT_MIN, jnp.int32) for _ in range(nblk)]
    for bit in _RADIX_BITS:
        cands = [lo + bit for lo in los]
        accs = []
        for blk in range(nblk):
            cand_b = jnp.broadcast_to(cands[blk], (RADIX_ROWS, LANES))

            def count_chunk(c, acc, cand_b=cand_b, r0=blk * RADIX_ROWS):
                off = pl.multiple_of(c * count_kc, count_kc)
                for t in range(count_kc // LANES):
                    tile = keys_ref[r0:r0 + RADIX_ROWS, pl.ds(off + t * LANES, LANES)]
                    acc = acc + jnp.where(tile >= cand_b, 1.0, 0.0)
                return acc

            accs.append(lax.fori_loop(0, n_count, count_chunk, jnp.zeros((RADIX_ROWS, LANES), F32)))
        cnts = [jnp.sum(acc, axis=-1, keepdims=True) for acc in accs]
        los = [jnp.where(cnts[blk] >= topk, cands[blk], los[blk]) for blk in range(nblk)]
    thr = jnp.maximum(jnp.concatenate(los, axis=0), INT_MIN + 1)

    @pl.when(i == 0)
    def _():
        for g in range(ATT_KV):
            best = jnp.zeros((1, 1), F32)
            for r0 in range(0, k_ref.shape[0], KNORM_ROWS):
                kk = k_ref[r0:r0 + KNORM_ROWS, g * ATT_HD:(g + 1) * ATT_HD].astype(F32)
                best = jnp.maximum(best, jnp.max(jnp.sum(kk * kk, axis=-1, keepdims=True), axis=0, keepdims=True))
            knorm_ref[g:g + 1, :] = jnp.broadcast_to(jnp.sqrt(best), (1, LANES))

    def store(g, out):
        for j in range(ATT_GROUP):
            h = g * ATT_GROUP + j
            o_ref[:, h * ATT_HD:(h + 1) * ATT_HD] = out[j].astype(o_ref.dtype)

    for g in range(ATT_KV):
        qg = jnp.concatenate(
            [q_ref[:, (g * ATT_GROUP + j) * ATT_HD:(g * ATT_GROUP + j + 1) * ATT_HD] for j in range(ATT_GROUP)], axis=0)
        qg = (qg * (ATT_SCALE * LOG2E)).astype(BF16)
        q32 = qg.astype(F32)
        shift = (jnp.sqrt(jnp.sum(q32 * q32, axis=-1, keepdims=True)) * knorm_ref[g:g + 1, 0:1]).reshape(
            ATT_GROUP, qb, 1)

        def chunk_operands(c, g=g):
            off = pl.multiple_of(c * kc, kc)
            kch = k_ref[pl.ds(off, kc), g * ATT_HD:(g + 1) * ATT_HD]
            vch = v_ref[pl.ds(off, kc), g * ATT_HD:(g + 1) * ATT_HD]
            sel = (keys_ref[:, pl.ds(off, kc)] >= thr)[None]
            return kch, vch, sel

        def fixed_chunk(c, carry, qg=qg, shift=shift):
            l, acc = carry
            kch, vch, sel = chunk_operands(c)
            p = jnp.exp2(jnp.where(sel, _nt(qg, kch).reshape(ATT_GROUP, qb, kc) - shift, -jnp.inf))
            l = l + jnp.sum(p, axis=-1, keepdims=True)
            pv = jnp.dot(p.reshape(ATT_GROUP * qb, kc).astype(BF16), vch, preferred_element_type=F32)
            return l, acc + pv.reshape(ATT_GROUP, qb, ATT_HD)

        l, acc = lax.fori_loop(0, nck, fixed_chunk, (jnp.zeros((ATT_GROUP, qb, 1), F32),
                                                     jnp.zeros((ATT_GROUP, qb, ATT_HD), F32)))
        healthy = jnp.min(l) > SUM_FLOOR

        @pl.when(healthy)
        def _(g=g, l=l, acc=acc):
            store(g, acc / l)

        @pl.when(jnp.logical_not(healthy))
        def _(g=g, qg=qg):
            def running_chunk(c, carry):
                m, l, acc = carry
                kch, vch, sel = chunk_operands(c)
                s = jnp.where(sel, _nt(qg, kch).reshape(ATT_GROUP, qb, kc), -jnp.inf)
                m_new = jnp.maximum(m, jnp.max(s, axis=-1, keepdims=True))
                p = jnp.exp2(s - m_new)
                a = jnp.exp2(m - m_new)
                l = a * l + jnp.sum(p, axis=-1, keepdims=True)
                pv = jnp.dot(p.reshape(ATT_GROUP * qb, kc).astype(BF16), vch, preferred_element_type=F32)
                return m_new, l, a * acc + pv.reshape(ATT_GROUP, qb, ATT_HD)

            init = (jnp.full((ATT_GROUP, qb, 1), NEG_BIG, F32), jnp.zeros((ATT_GROUP, qb, 1), F32),
                    jnp.zeros((ATT_GROUP, qb, ATT_HD), F32))
            _, l, acc = lax.fori_loop(0, nck, running_chunk, init)
            store(g, acc / l)


def _dsa_prompt(proj, projb, proj2, ikb, nb, s, qb=256, kc=512):
    nq = s // qb
    topk = min(TOPK_MAX, s // 4)
    assert (s // kc) % 2 == 0
    qw = ATT_HEADS * ATT_HD
    kvw = ATT_KV * ATT_HD
    iqw = IDX_HEADS * IDX_DIM
    return pl.pallas_call(
        functools.partial(_dsa_prompt_body, qb=qb, kc=kc, topk=topk),
        grid=(nb, nq),
        in_specs=[pl.BlockSpec((qb, qw), lambda b, i: (b * nq + i, 0)),
                  pl.BlockSpec((qb, iqw), lambda b, i: (b * nq + i, (qw + 2 * kvw) // iqw)),
                  pl.BlockSpec((qb, LANES), lambda b, i: (b * nq + i, 0)),
                  pl.BlockSpec((s, kvw), lambda b, i: (b, qw // kvw)),
                  pl.BlockSpec((s, kvw), lambda b, i: (b, qw // kvw + 1)),
                  pl.BlockSpec((s, IDX_DIM), lambda b, i: (b, 0))],
        out_specs=pl.BlockSpec((qb, qw), lambda b, i: (b * nq + i, 0)),
        out_shape=jax.ShapeDtypeStruct((nb * s, qw), BF16),
        scratch_shapes=[pltpu.VMEM((qb, s), jnp.int32), pltpu.VMEM((SUBLANES, LANES), F32)],
        compiler_params=_cp("parallel", "arbitrary"),
        name="dsa_prompt",
    )(proj, projb, proj2, projb, projb, ikb)


SCORE_PAGES = 16


def _dsa_s_scores_body(pt_ref, iq_ref, iw_ref, ikn_ref, *rest, n_pages):
    page_refs, o_ref = rest[:SCORE_PAGES], rest[SCORE_PAGES]
    p = pl.program_id(1)
    iq = iq_ref[...].astype(BF16)
    iw = iw_ref[...] * IDX_W_SCALE

    @pl.when(p == 0)
    def _():
        own = jnp.broadcast_to(ikn_ref[...], (SUBLANES, IDX_DIM)).astype(BF16)
        d = jnp.maximum(_nt(iq, own), 0.0)
        sc = jnp.sum(iw * d, axis=0, keepdims=True)
        o_ref[n_pages:n_pages + SUBLANES, :] = jnp.broadcast_to(sc[:, 0:1], (SUBLANES, LANES))

    for r in range(SCORE_PAGES):
        page_t = page_refs[r][...].astype(BF16)
        d = jnp.maximum(jnp.dot(iq, page_t, preferred_element_type=F32), 0.0)
        o_ref[pl.ds(p * SCORE_PAGES + r, 1), :] = jnp.sum(iw * d, axis=0, keepdims=True)


def _dsa_s_scores(page_table, iq, iw, ik_new, cache_ik_t, layer):
    nb, n_pages = page_table.shape
    page_spec = lambda r: pl.BlockSpec((None, None, IDX_DIM, PAGE),
                                       lambda b, p, pt: (layer, pt[b, p * SCORE_PAGES + r], 0, 0))
    return pl.pallas_call(
        functools.partial(_dsa_s_scores_body, n_pages=n_pages),
        grid_spec=pltpu.PrefetchScalarGridSpec(
            num_scalar_prefetch=1,
            grid=(nb, n_pages // SCORE_PAGES),
            in_specs=[pl.BlockSpec((None, IDX_HEADS, IDX_DIM), lambda b, p, pt: (b, 0, 0)),
                      pl.BlockSpec((None, IDX_HEADS, 1), lambda b, p, pt: (b, 0, 0)),
                      pl.BlockSpec((None, 1, IDX_DIM), lambda b, p, pt: (b, 0, 0))]
                     + [page_spec(r) for r in range(SCORE_PAGES)],
            out_specs=pl.BlockSpec((None, n_pages + SUBLANES, LANES), lambda b, p, pt: (b, 0, 0)),
        ),
        out_shape=jax.ShapeDtypeStruct((nb, n_pages + SUBLANES, LANES), F32),
        compiler_params=_cp("parallel", "arbitrary"),
        name="dsa_sample_scores",
    )(page_table, iq, iw, ik_new, *([cache_ik_t] * SCORE_PAGES))


def _dsa_s_select_body(sc_ref, ptc_ref, idx_ref, meta_ref, rank_ref, phys_ref, *, n_pages, topk):
    shape = (n_pages + SUBLANES, LANES)
    rows = lax.broadcasted_iota(jnp.int32, shape, 0)
    cols = lax.broadcasted_iota(jnp.int32, shape, 1)
    live = (rows < n_pages) | ((rows == n_pages) & (cols == 0))
    keys = jnp.where(live, _order_key(sc_ref[...]), INT_MIN)
    lo = jnp.full((1, 1), INT_MIN, jnp.int32)
    for bit in _RADIX_BITS:
        cand = lo + bit
        cnt = jnp.sum(jnp.sum(jnp.where(keys >= cand, 1.0, 0.0), axis=0, keepdims=True), axis=1, keepdims=True)
        lo = jnp.where(cnt >= topk, cand, lo)
    sel = keys >= jnp.maximum(lo, INT_MIN + 1)
    sel_c = sel[:n_pages, :]
    own = jnp.where(sel[n_pages:n_pages + 1, 0:1], 1, 0)

    r_i = lax.broadcasted_iota(jnp.int32, (LANES, LANES), 0)
    c_i = lax.broadcasted_iota(jnp.int32, (LANES, LANES), 1)
    ones_le = jnp.where(r_i <= c_i, 1.0, 0.0).astype(BF16)
    ones_gt = jnp.where(r_i > c_i, 1.0, 0.0).astype(BF16)
    m = jnp.where(sel_c, 1.0, 0.0).astype(BF16)
    within = jnp.dot(m, ones_le, preferred_element_type=F32)
    tot = jnp.broadcast_to(within[:, LANES - 1:LANES], (n_pages, LANES)).astype(BF16)
    before = jnp.dot(ones_gt, tot, preferred_element_type=F32)
    rank_ref[...] = jnp.where(sel_c, (within + before).astype(jnp.int32) - 1, -1)
    phys_ref[...] = (ptc_ref[...] * PAGE + c_i[:n_pages, :]).astype(F32)
    n_sel = (before[n_pages - 1:n_pages, 0:1] + within[n_pages - 1:n_pages, LANES - 1:LANES]).astype(jnp.int32)

    slot = lax.broadcasted_iota(jnp.int32, (topk, 1), 0)

    def place(r, acc):
        return acc + jnp.where(rank_ref[pl.ds(r, 1), :] == slot, phys_ref[pl.ds(r, 1), :], 0.0)

    acc = lax.fori_loop(0, n_pages, place, jnp.zeros((topk, LANES), F32), unroll=8)
    idx_ref[...] = jnp.sum(acc, axis=-1, keepdims=True).astype(jnp.int32)
    mrow = lax.broadcasted_iota(jnp.int32, (SUBLANES, LANES), 0)
    meta_ref[...] = jnp.where(mrow == 0, jnp.minimum(n_sel, topk), own)


def _dsa_s_select(scores, page_table):
    nb, n_pages = page_table.shape
    assert n_pages == LANES
    topk = min(TOPK_MAX, (n_pages * PAGE + 1) // 4)
    idx, meta = pl.pallas_call(
        functools.partial(_dsa_s_select_body, n_pages=n_pages, topk=topk),
        grid=(nb,),
        in_specs=[pl.BlockSpec((None, n_pages + SUBLANES, LANES), lambda b: (b, 0, 0)),
                  pl.BlockSpec((None, n_pages, 1), lambda b: (b, 0, 0))],
        out_specs=[pl.BlockSpec((None, topk, 1), lambda b: (b, 0, 0)),
                   pl.BlockSpec((None, SUBLANES, LANES), lambda b: (b, 0, 0))],
        out_shape=[jax.ShapeDtypeStruct((nb, topk, 1), jnp.int32),
                   jax.ShapeDtypeStruct((nb, SUBLANES, LANES), jnp.int32)],
        scratch_shapes=[pltpu.VMEM((n_pages, LANES), jnp.int32), pltpu.VMEM((n_pages, LANES), F32)],
        compiler_params=_cp("parallel"),
        name="dsa_sample_select",
    )(scores, page_table.reshape(nb, n_pages, 1))
    return idx.reshape(nb, topk), meta[:, 0:2, 0]


def _dsa_s_attend_body(idx_ref, meta_ref, q_ref, kn_ref, vn_ref, ck_ref, cv_ref, o_ref, kbuf, vbuf, sem,
                       *, layer, topk):
    b = pl.program_id(0)
    nb = pl.num_programs(0)
    slot = b % 2

    def row_copies(tok, buf_slot, j):
        row = idx_ref[tok, j]
        page, off = row // PAGE, row % PAGE
        dst = pl.ds(j * ATT_KV, ATT_KV)
        return (pltpu.make_async_copy(ck_ref.at[layer, page, off], kbuf.at[buf_slot, dst, :], sem.at[buf_slot, 0]),
                pltpu.make_async_copy(cv_ref.at[layer, page, off], vbuf.at[buf_slot, dst, :], sem.at[buf_slot, 1]))

    def start_gather(tok, buf_slot):
        def body(j, carry):
            for cp in row_copies(tok, buf_slot, j):
                cp.start()
            return carry
        lax.fori_loop(0, topk, body, 0)

    @pl.when(b == 0)
    def _():
        start_gather(0, 0)

    @pl.when(b + 1 < nb)
    def _():
        start_gather(b + 1, 1 - slot)

    def wait_body(j, carry):
        for cp in row_copies(b, slot, j):
            cp.wait()
        return carry

    lax.fori_loop(0, topk, wait_body, 0)

    n_sel = meta_ref[b, 0]
    own = meta_ref[b, 1] > 0
    qb = q_ref[...].astype(BF16)
    ncol = topk * ATT_KV
    col = lax.broadcasted_iota(jnp.int32, (ATT_HEADS, ncol), 1)
    head = lax.broadcasted_iota(jnp.int32, (ATT_HEADS, ncol), 0)
    valid = (col % ATT_KV == head // ATT_GROUP) & (col // ATT_KV < n_sel)
    s = jnp.where(valid, _nt(qb, kbuf[slot].astype(BF16)) * ATT_SCALE, NEG_BIG)
    kn = kn_ref[...].astype(BF16).astype(F32)
    vn = vn_ref[...].astype(BF16).astype(F32)
    expand = lambda a: jnp.concatenate(
        [jnp.broadcast_to(a[:, g * ATT_HD:(g + 1) * ATT_HD], (ATT_GROUP, ATT_HD)) for g in range(ATT_KV)], axis=0)
    s_own = jnp.where(own, jnp.sum(qb.astype(F32) * expand(kn), axis=-1, keepdims=True) * ATT_SCALE, NEG_BIG)
    m = jnp.maximum(jnp.max(s, axis=-1, keepdims=True), s_own)
    p = jnp.where(valid, jnp.exp(s - m), 0.0)
    p_own = jnp.where(own, jnp.exp(s_own - m), 0.0)
    l = jnp.sum(p, axis=-1, keepdims=True) + p_own
    acc = (jnp.dot(p.astype(BF16), vbuf[slot].astype(BF16), preferred_element_type=F32)
           + p_own.astype(BF16).astype(F32) * expand(vn))
    o_ref[...] = acc / l


def _dsa_s_attend(idx, meta, q, k_new, v_new, cache_k, cache_v, layer):
    nb, topk = idx.shape
    kvw = ATT_KV * ATT_HD
    return pl.pallas_call(
        functools.partial(_dsa_s_attend_body, layer=layer, topk=topk),
        grid_spec=pltpu.PrefetchScalarGridSpec(
            num_scalar_prefetch=2,
            grid=(nb,),
            in_specs=[pl.BlockSpec((None, ATT_HEADS, ATT_HD), lambda b, idx, meta: (b, 0, 0)),
                      pl.BlockSpec((None, 1, kvw), lambda b, idx, meta: (b, 0, 0)),
                      pl.BlockSpec((None, 1, kvw), lambda b, idx, meta: (b, 0, 0)),
                      pl.BlockSpec(memory_space=pl.ANY),
                      pl.BlockSpec(memory_space=pl.ANY)],
            out_specs=pl.BlockSpec((None, ATT_HEADS, ATT_HD), lambda b, idx, meta: (b, 0, 0)),
            scratch_shapes=[pltpu.VMEM((2, topk * ATT_KV, ATT_HD), F32),
                            pltpu.VMEM((2, topk * ATT_KV, ATT_HD), F32),
                            pltpu.SemaphoreType.DMA((2, 2))],
        ),
        out_shape=jax.ShapeDtypeStruct((nb, ATT_HEADS, ATT_HD), F32),
        compiler_params=_cp("arbitrary"),
        name="dsa_sample_attend",
    )(idx, meta, q, k_new, v_new, cache_k, cache_v)


HG_SUB = 16


HG_HB = 4
HG_EXP_LIMIT = 80.0


def _hgrn_body(lbl_ref, q_ref, f_ref, i_ref, g_ref, ng_ref, s0_ref, o_ref, so_ref, st_ref, *, layer, c, t, nc):
    ci = pl.program_id(2)
    mid = c // 2

    @pl.when(ci == 0)
    def _():
        for hh in range(HG_HB):
            st_ref[hh] = s0_ref[hh].T

    logits = lbl_ref[...]
    e = jnp.exp(logits - jnp.max(logits, axis=0, keepdims=True))
    soft = e / jnp.sum(e, axis=0, keepdims=True)
    lb_all = jnp.zeros((1, HG_HB * HG_DK), F32)
    for r in range(1, layer + 1):
        lb_all = lb_all + soft[r:r + 1, :]
    ng = ng_ref[...]

    def gates(hh, sl, row0):
        cs = slice(hh * HG_DK, (hh + 1) * HG_DK)
        lb = lb_all[:, cs]
        fg = lb + (1.0 - lb) * jax.nn.sigmoid(f_ref[sl, cs])
        lf = jnp.log(fg)
        kk = 1.0 - fg
        if t % c:
            n = lf.shape[0]
            valid = (ci * c + row0 + lax.broadcasted_iota(jnp.int32, (n, 1), 0)) < t
            lf = jnp.where(valid, lf, 0.0)
            kk = jnp.where(valid, kk, 0.0)
        return lf, kk

    def finish(hh, sl, o):
        cs = slice(hh * HG_DV, (hh + 1) * HG_DV)
        on = o * lax.rsqrt(jnp.mean(o * o, axis=-1, keepdims=True) + RMS_EPS) * ng
        o_ref[sl, cs] = (on * _silu(g_ref[sl, cs])).astype(o_ref.dtype)

    tri_c = lax.broadcasted_iota(jnp.int32, (c, c), 0) >= lax.broadcasted_iota(jnp.int32, (c, c), 1)
    full = pl.ds(0, c)
    lfs, kks, bs = [], [], []
    safe = None
    for hh in range(HG_HB):
        lf, kk = gates(hh, full, 0)
        b = _prefix_sums(tri_c.astype(BF16), lf)
        bm = b[mid - 1:mid, :]
        ok = jnp.min(jnp.minimum(bm, b[c - 1:c, :] - bm)) > -HG_EXP_LIMIT
        safe = ok if safe is None else jnp.logical_and(safe, ok)
        lfs.append(lf)
        kks.append(kk)
        bs.append(b)

    @pl.when(safe)
    def _():
        for hh in range(HG_HB):
            cs = slice(hh * HG_DK, (hh + 1) * HG_DK)
            b, kk = bs[hh], kks[hh]
            bm = b[mid - 1:mid, :]
            bl = b[c - 1:c, :]
            qq = _silu(q_ref[:, cs])
            vv = i_ref[:, cs].astype(BF16)
            att = jnp.where(tri_c, _nt((qq * jnp.exp(b - bm)).astype(BF16), (kk * jnp.exp(bm - b)).astype(BF16)), 0.0)
            st = st_ref[hh]
            o = (jnp.dot(att.astype(BF16), vv, preferred_element_type=F32)
                 + _nt((qq * jnp.exp(b)).astype(BF16), st.astype(BF16)))
            st_ref[hh] = st * jnp.exp(bl) + _tn(vv, (kk * jnp.exp(bl - b)).astype(BF16))
            finish(hh, full, o)

    @pl.when(jnp.logical_not(safe))
    def _():
        rows = lax.broadcasted_iota(jnp.int32, (HG_SUB, 1), 0)
        tri = (lax.broadcasted_iota(jnp.int32, (HG_SUB, HG_SUB), 0)
               >= lax.broadcasted_iota(jnp.int32, (HG_SUB, HG_SUB), 1)).astype(F32)
        for hh in range(HG_HB):
            cs = slice(hh * HG_DK, (hh + 1) * HG_DK)

            def sub_block(sb, carry, hh=hh, cs=cs):
                row0 = pl.multiple_of(sb * HG_SUB, HG_SUB)
                sl = pl.ds(row0, HG_SUB)
                lf, kk = gates(hh, sl, row0)
                qq = _silu(q_ref[sl, cs])
                vv = i_ref[sl, cs]
                b = _prefix_sums(tri.astype(BF16), lf)
                st = st_ref[hh]
                o = _nt((qq * jnp.exp(b)).astype(BF16), st.astype(BF16))
                for s in range(HG_SUB):
                    dec = jnp.exp(jnp.where(rows >= s, b - b[s:s + 1, :], -jnp.inf))
                    att = jnp.sum(qq * dec * kk[s:s + 1, :], axis=-1, keepdims=True)
                    o = o + att * vv[s:s + 1, :]
                bl = b[HG_SUB - 1:HG_SUB, :]
                st_ref[hh] = st * jnp.exp(bl) + _tn(vv.astype(BF16), (kk * jnp.exp(bl - b)).astype(BF16))
                finish(hh, sl, o)
                return carry

            lax.fori_loop(0, c // HG_SUB, sub_block, 0)

    @pl.when(ci == nc - 1)
    def _():
        for hh in range(HG_HB):
            so_ref[hh] = st_ref[hh].T


def _hgrn(proj, lb_logits, norm_g, s0, layer, t, c=128):
    nb, tpad, _ = proj.shape
    nc = tpad // c
    nhb = HG_HEADS // HG_HB
    w = HG_HB * HG_DK
    return pl.pallas_call(
        functools.partial(_hgrn_body, layer=layer, c=c, t=t, nc=nc),
        grid=(nb, nhb, nc),
        in_specs=[pl.BlockSpec((DEPTH, w), lambda b, h, ci: (0, h)),
                  pl.BlockSpec((None, c, w), lambda b, h, ci: (b, ci, h)),
                  pl.BlockSpec((None, c, w), lambda b, h, ci: (b, ci, nhb + h)),
                  pl.BlockSpec((None, c, w), lambda b, h, ci: (b, ci, 2 * nhb + h)),
                  pl.BlockSpec((None, c, w), lambda b, h, ci: (b, ci, 3 * nhb + h)),
                  pl.BlockSpec((1, HG_DV), lambda b, h, ci: (0, 0)),
                  pl.BlockSpec((None, HG_HB, HG_DK, HG_DV), lambda b, h, ci: (b, h, 0, 0))],
        out_specs=[pl.BlockSpec((None, c, w), lambda b, h, ci: (b, ci, h)),
                   pl.BlockSpec((None, HG_HB, HG_DK, HG_DV), lambda b, h, ci: (b, h, 0, 0))],
        out_shape=[jax.ShapeDtypeStruct((nb, tpad, HG_HEADS * HG_DV), BF16),
                   jax.ShapeDtypeStruct((nb, HG_HEADS, HG_DK, HG_DV), F32)],
        scratch_shapes=[pltpu.VMEM((HG_HB, HG_DV, HG_DK), F32)],
        compiler_params=_cp("parallel", "parallel", "arbitrary"),
        name="hgrn2",
    )(lb_logits, proj, proj, proj, proj, norm_g.reshape(1, HG_DV), s0)


def _expand_heads(v, e, terms):
    out = None
    rest = v
    for _ in range(terms):
        part = rest.astype(BF16)
        rest = rest - part.astype(F32)
        d = jnp.dot(part, e, preferred_element_type=F32)
        out = d if out is None else out + d
    return out


def _ssd_body(zx_ref, dt_ref, cs_ref, cw_ref, cbias_ref, e_ref, dtb_ref, alog_ref, dx_ref, ng_ref, s0_ref,
              o_ref, so_ref, nc_ref, st_ref, xc_ref, halo_ref, *, c, t, nc):
    ci = pl.program_id(1)
    gw = SSM_HPG * SSM_P

    @pl.when(ci == 0)
    def _():
        for blk in range(SSM_INNER // LANES):
            st_ref[:, blk * LANES:(blk + 1) * LANES] = s0_ref[blk * LANES:(blk + 1) * LANES, :].T
        halo_ref[...] = cs_ref[...]

    full = jnp.concatenate([halo_ref[...], zx_ref[:, SSM_INNER:SSM_INNER + SSM_CH]], axis=0)
    conv = cbias_ref[...]
    for k in range(SSM_CONV):
        lo = SUBLANES - (SSM_CONV - 1) + k
        conv = conv + full[lo:lo + c, :] * cw_ref[k:k + 1, :]
    xc_ref[...] = _silu(conv)
    halo_ref[...] = full[c:c + SUBLANES, :]

    @pl.when(ci == nc - 1)
    def _():
        tv = t - (nc - 1) * c
        tail = full[SUBLANES + tv - (SSM_CONV - 1):SUBLANES + tv, :]
        nc_ref[...] = jnp.concatenate([tail, jnp.zeros((SUBLANES - (SSM_CONV - 1), SSM_CH), F32)], axis=0)

    e = e_ref[...]
    rows = lax.broadcasted_iota(jnp.int32, (c, 1), 0)
    tri_b = lax.broadcasted_iota(jnp.int32, (c, c), 0) >= lax.broadcasted_iota(jnp.int32, (c, c), 1)
    lane_lo = lax.broadcasted_iota(jnp.int32, (1, LANES), 1) < SSM_P

    dt = jax.nn.softplus(dt_ref[...] + dtb_ref[...])
    if t % c:
        dt = jnp.where(ci * c + rows < t, dt, 0.0)
    da = dt * (-jnp.exp(alog_ref[...]))
    bcum = _prefix_sums(tri_b.astype(BF16), da)
    bcum_t = bcum.T
    dt_t = dt.T
    bl = bcum[c - 1:c, :]
    eb_x = _expand_heads(jnp.exp(bcum), e, 2)
    w_x = _expand_heads(jnp.exp(bl - bcum) * dt, e, 2)
    decay_x = _expand_heads(jnp.broadcast_to(jnp.exp(bl), (SUBLANES, LANES)), e, 3)[0:1, :]

    xs = xc_ref[:, 0:SSM_INNER]
    xdt = xs.astype(BF16)
    xw = (xs * w_x).astype(BF16)
    y = xs * dx_ref[...]
    zg = _silu(zx_ref[:, 0:SSM_INNER])
    for g in range(SSM_GROUPS):
        bg = xc_ref[:, SSM_INNER + g * SSM_N:SSM_INNER + (g + 1) * SSM_N]
        cg = xc_ref[:, SSM_INNER + (SSM_GROUPS + g) * SSM_N:SSM_INNER + (SSM_GROUPS + g + 1) * SSM_N].astype(BF16)
        cb = _nt(cg, bg.astype(BF16))
        st_g = st_ref[:, g * gw:(g + 1) * gw]
        yg = jnp.dot(cg, st_g.astype(BF16), preferred_element_type=F32) * eb_x[:, g * gw:(g + 1) * gw]
        parts = []
        for jp in range(SSM_HPG // 2):
            xpair = xdt[:, g * gw + jp * LANES:g * gw + (jp + 1) * LANES]
            acc = None
            for half in range(2):
                h = g * SSM_HPG + jp * 2 + half
                dec = jnp.exp(jnp.where(tri_b, bcum[:, h:h + 1] - bcum_t[h:h + 1, :], -jnp.inf))
                w = (cb * dec * dt_t[h:h + 1, :]).astype(BF16)
                xh = jnp.where(lane_lo if half == 0 else jnp.logical_not(lane_lo), xpair, 0.0).astype(BF16)
                r = jnp.dot(w, xh, preferred_element_type=F32)
                acc = r if acc is None else acc + r
            parts.append(acc)
        yg = yg + jnp.concatenate(parts, axis=1)
        st_ref[:, g * gw:(g + 1) * gw] = (st_g * decay_x[:, g * gw:(g + 1) * gw]
                                          + jnp.dot(bg.T.astype(BF16), xw[:, g * gw:(g + 1) * gw],
                                                    preferred_element_type=F32))
        yg = (yg + y[:, g * gw:(g + 1) * gw]) * zg[:, g * gw:(g + 1) * gw]
        yg = yg * lax.rsqrt(jnp.mean(yg * yg, axis=-1, keepdims=True) + RMS_EPS) * ng_ref[:, g * gw:(g + 1) * gw]
        o_ref[:, g * gw:(g + 1) * gw] = yg.astype(o_ref.dtype)

    @pl.when(ci == nc - 1)
    def _():
        for blk in range(SSM_INNER // LANES):
            so_ref[blk * LANES:(blk + 1) * LANES, :] = st_ref[:, blk * LANES:(blk + 1) * LANES].T


def _ssd(zx, dt_raw, cs_pad, conv_w, conv_b, expand, dt_bias, a_log, d_x, norm_g, s0, t, c=128):
    nb, tpad, _ = zx.shape
    nc = tpad // c
    return pl.pallas_call(
        functools.partial(_ssd_body, c=c, t=t, nc=nc),
        grid=(nb, nc),
        in_specs=[pl.BlockSpec((None, c, SSM_MAIN), lambda b, ci: (b, ci, 0)),
                  pl.BlockSpec((None, c, LANES), lambda b, ci: (b, ci, 0)),
                  pl.BlockSpec((None, SUBLANES, SSM_CH), lambda b, ci: (b, 0, 0)),
                  pl.BlockSpec((SSM_CONV, SSM_CH), lambda b, ci: (0, 0)),
                  pl.BlockSpec((1, SSM_CH), lambda b, ci: (0, 0)),
                  pl.BlockSpec((LANES, SSM_INNER), lambda b, ci: (0, 0)),
                  pl.BlockSpec((1, LANES), lambda b, ci: (0, 0)),
                  pl.BlockSpec((1, LANES), lambda b, ci: (0, 0)),
                  pl.BlockSpec((1, SSM_INNER), lambda b, ci: (0, 0)),
                  pl.BlockSpec((1, SSM_INNER), lambda b, ci: (0, 0)),
                  pl.BlockSpec((None, SSM_INNER, SSM_N), lambda b, ci: (b, 0, 0))],
        out_specs=[pl.BlockSpec((None, c, SSM_INNER), lambda b, ci: (b, ci, 0)),
                   pl.BlockSpec((None, SSM_INNER, SSM_N), lambda b, ci: (b, 0, 0)),
                   pl.BlockSpec((None, SUBLANES, SSM_CH), lambda b, ci: (b, 0, 0))],
        out_shape=[jax.ShapeDtypeStruct((nb, tpad, SSM_INNER), BF16),
                   jax.ShapeDtypeStruct((nb, SSM_INNER, SSM_N), F32),
                   jax.ShapeDtypeStruct((nb, SUBLANES, SSM_CH), F32)],
        scratch_shapes=[pltpu.VMEM((SSM_N, SSM_INNER), F32),
                        pltpu.VMEM((c, SSM_CH), F32),
                        pltpu.VMEM((SUBLANES, SSM_CH), F32)],
        compiler_params=_cp("parallel", "arbitrary"),
        name="ssd",
    )(zx, dt_raw, cs_pad, conv_w, conv_b.reshape(1, SSM_CH), expand, dt_bias, a_log, d_x,
      norm_g.reshape(1, SSM_INNER), s0)


def _pad_cols(w, n):
    return jnp.pad(w, ((0, 0), (0, n - w.shape[1])))


def _pad_rows_to(w, n):
    return jnp.pad(w, ((0, n - w.shape[0]), (0, 0)))


def _pad_time(a, tpad):
    return jnp.pad(a, ((0, 0), (0, tpad - a.shape[1]), (0, 0)))


def kernel(x_prompt, x_sample, cache_k, cache_v, cache_idx_k, state_hgrn, state_ssm, state_conv, page_table, p_prompt, p_sample, ln_g, ln_b, ffn_w_gate_up, ffn_w_down, ple_w_proj, ple_w_gate, att_w_in, att_idx_k_norm, att_w_o, hg_w_in, hg_lb_logits, hg_norm_g, hg_w_o, ssm_w_in, ssm_conv_w, ssm_conv_b, ssm_dt_bias, ssm_a_log, ssm_d, ssm_norm_g, ssm_w_o):
    nbp, seq, d = x_prompt.shape
    nbs = x_sample.shape[0]
    mp = nbp * seq
    ms = 16
    chunk = 128

    tm, tm_ln = 1024, 512
    x_p = x_prompt.reshape(mp, d)
    x_s = jnp.pad(x_sample.reshape(nbs, d), ((0, ms - nbs), (0, 0)))
    xb_p, xb_s = x_p.astype(BF16), x_s.astype(BF16)
    pl_p = p_prompt.reshape(DEPTH, mp, PLE_DIM)
    pl_s = jnp.pad(p_sample.reshape(DEPTH, nbs, PLE_DIM), ((0, 0), (0, ms - nbs), (0, 0)))
    pad_rows = lambda a: jnp.pad(a, ((0, ms - nbs), (0, 0)))
    att_w_in_t = jnp.swapaxes(att_w_in, 1, 2)
    ssm_w_in_t = jnp.swapaxes(ssm_w_in, 1, 2)
    cache_idx_k_t = jnp.swapaxes(cache_idx_k, 2, 3)

    expand = jnp.asarray(np.kron(np.eye(LANES, SSM_HEADS, dtype=np.float32),
                                 np.ones((1, SSM_P), np.float32)), BF16)
    outs = {}

    def ffn_ln(i, which, ln_idx, x_p, xb_p, x_s, xb_s):
        h_p, h_s, wd = _mm_swiglu(xb_p, xb_s, ffn_w_gate_up, ffn_w_down, (i, which), tm, 512)
        return _mm_ln(h_p, h_s, wd, x_p, x_s, ln_g[i, ln_idx], ln_b[i, ln_idx], 0.5, tm_ln, "ffn_down_ln")

    for i in range(DEPTH):
        j = i // N_MIXERS
        (x_p, xb_p), (x_s, xb_s) = ffn_ln(i, 0, 0, x_p, xb_p, x_s, xb_s)

        if i % N_MIXERS == 0:
            w_small = _pad_rows_to(att_w_in_t[j, ATT_MAIN:, :], LANES)
            w_o = _cast_w(att_w_o, (j,))
            qw, kvw = ATT_HEADS * ATT_HD, ATT_KV * ATT_HD
            (proj, projb), (proj_s, _) = _proj(xb_p, xb_s, att_w_in_t, (j,), ATT_MAIN, tm, 1024, with_bf16=True,
                                               w_is_nk=True, name="att_in")
            proj2, proj2_s = _proj(xb_p, xb_s, w_small, (), LANES, tm, LANES, w_is_nk=True, name="att_in_idx")
            ik, ikb = _ik_norm(proj2, att_idx_k_norm[j], tm)
            ik_s, _ = _ik_norm(proj2_s, att_idx_k_norm[j], ms)
            o_p = _dsa_prompt(proj, projb, proj2, ikb, nbp, seq)
            outs.setdefault("k_p", []).append(proj[:, qw:qw + kvw].reshape(nbp, seq, ATT_KV, ATT_HD))
            outs.setdefault("v_p", []).append(proj[:, qw + kvw:qw + 2 * kvw].reshape(nbp, seq, ATT_KV, ATT_HD))
            outs.setdefault("ik_p", []).append(ik.reshape(nbp, seq, IDX_DIM))
            pr = proj_s[:nbs]
            k_new = pr[:, qw:qw + kvw]
            v_new = pr[:, qw + kvw:qw + 2 * kvw]
            ik_new = ik_s[:nbs]
            scores = _dsa_s_scores(page_table,
                                   pr[:, qw + 2 * kvw:].reshape(nbs, IDX_HEADS, IDX_DIM),
                                   proj2_s[:nbs, IDX_DIM:IDX_DIM + IDX_HEADS].reshape(nbs, IDX_HEADS, 1),
                                   ik_new.reshape(nbs, 1, IDX_DIM), cache_idx_k_t, j)
            idx, meta = _dsa_s_select(scores, page_table)
            o_s = _dsa_s_attend(idx, meta, pr[:, :qw].reshape(nbs, ATT_HEADS, ATT_HD),
                                k_new.reshape(nbs, 1, kvw), v_new.reshape(nbs, 1, kvw), cache_k, cache_v, j)
            o_s = pad_rows(o_s.reshape(nbs, qw)).astype(BF16)
            outs.setdefault("k_s", []).append(k_new.reshape(nbs, 1, ATT_KV, ATT_HD))
            outs.setdefault("v_s", []).append(v_new.reshape(nbs, 1, ATT_KV, ATT_HD))
            outs.setdefault("ik_s", []).append(ik_new.reshape(nbs, 1, IDX_DIM))
        elif i % N_MIXERS == 1:
            w_o = _cast_w(hg_w_o, (j,))
            proj, proj_s = _proj(xb_p, xb_s, hg_w_in, (j,), hg_w_in.shape[-1], tm, 1024, name="hg_in")
            s0 = jnp.zeros((nbp, HG_HEADS, HG_DK, HG_DV), F32)
            o_p, s_fin = _hgrn(proj.reshape(nbp, seq, -1), hg_lb_logits, hg_norm_g[j], s0, i, seq, chunk)
            o_p = o_p.reshape(mp, -1)
            outs.setdefault("hg_p", []).append(s_fin)
            pr = _pad_time(proj_s[:nbs].reshape(nbs, 1, -1), chunk)
            o_s, s_fin = _hgrn(pr, hg_lb_logits, hg_norm_g[j], state_hgrn[j], i, 1, chunk)
            o_s = pad_rows(o_s[:, 0, :])
            outs.setdefault("hg_s", []).append(s_fin)
        else:
            w_small = _pad_rows_to(ssm_w_in_t[j, SSM_MAIN:, :], LANES)
            w_o = _cast_w(ssm_w_o, (j,))
            dt_bias = _pad_cols(ssm_dt_bias[j].reshape(1, SSM_HEADS), LANES)
            a_log = _pad_cols(ssm_a_log[j].reshape(1, SSM_HEADS), LANES)
            d_x = jnp.repeat(ssm_d[j], SSM_P).reshape(1, SSM_INNER)
            zx_p, zx_s = _proj(xb_p, xb_s, ssm_w_in_t, (j,), SSM_MAIN, tm, 1024, w_is_nk=True, name="ssm_in")
            dtr_p, dtr_s = _proj(xb_p, xb_s, w_small, (), LANES, tm, LANES, w_is_nk=True, name="ssm_in_dt")
            mix = {}
            for name, zx, dtr in (("p", zx_p, dtr_p), ("s", zx_s, dtr_s)):
                if name == "p":
                    nb_, t_ = nbp, seq
                    zx3 = zx.reshape(nbp, seq, -1)
                    dt3 = dtr.reshape(nbp, seq, LANES)
                    cs = jnp.zeros((nbp, SUBLANES, SSM_CH), F32)
                    s0 = jnp.zeros((nbp, SSM_INNER, SSM_N), F32)
                else:
                    nb_, t_ = nbs, 1
                    zx3 = _pad_time(zx[:nbs].reshape(nbs, 1, -1), chunk)
                    dt3 = _pad_time(dtr[:nbs].reshape(nbs, 1, LANES), chunk)
                    cs = jnp.pad(state_conv[j], ((0, 0), (SUBLANES - (SSM_CONV - 1), 0), (0, 0)))
                    s0 = state_ssm[j].reshape(nbs, SSM_INNER, SSM_N)
                y, s_fin, new_conv = _ssd(zx3, dt3, cs, ssm_conv_w[j], ssm_conv_b[j], expand, dt_bias, a_log, d_x,
                                          ssm_norm_g[j], s0, t_, chunk)
                s_fin = s_fin.reshape(nb_, SSM_HEADS, SSM_P, SSM_N)
                new_conv = new_conv[:, :SSM_CONV - 1, :]
                if name == "p":
                    mix[name] = y.reshape(mp, SSM_INNER)
                    outs.setdefault("ssm_p", []).append(s_fin)
                    outs.setdefault("conv_p", []).append(new_conv)
                else:
                    mix[name] = pad_rows(y[:, 0, :])
                    outs.setdefault("ssm_s", []).append(s_fin)
                    outs.setdefault("conv_s", []).append(new_conv)
            o_p, o_s = mix["p"], mix["s"]

        (x_p, xb_p), (x_s, xb_s) = _mm_ln(o_p, o_s, w_o, x_p, x_s, ln_g[i, 1], ln_b[i, 1], 1.0, tm_ln, "mixer_out_ln")
        (x_p, xb_p), (x_s, xb_s) = ffn_ln(i, 1, 2, x_p, xb_p, x_s, xb_s)
        (x_p, xb_p), (x_s, xb_s) = _mm_ple(x_p, xb_p, pl_p[i].astype(BF16), x_s, xb_s, pl_s[i].astype(BF16),
                                           ple_w_gate, ple_w_proj, (i,), tm_ln, 1024)

    y_prompt = x_p.reshape(nbp, seq, d)
    y_sample = x_s[:nbs].reshape(nbs, 1, d)
    stack = lambda key: jnp.stack(outs[key])
    return (y_prompt, y_sample, stack("k_p"), stack("v_p"), stack("ik_p"), stack("k_s"), stack("v_s"), stack("ik_s"),
            stack("hg_p"), stack("hg_s"), stack("ssm_p"), stack("ssm_s"), stack("conv_p"), stack("conv_s"))
```

```python
import functools
import math

import jax
import jax.numpy as jnp
import numpy as np
from jax import lax
from jax.experimental import pallas as pl
from jax.experimental.pallas import tpu as pltpu

F32 = jnp.float32
BF16 = jnp.bfloat16

D_MODEL = 2048
DEPTH = 4
N_MIXERS = 3
D_FF = 2 * D_MODEL
PLE_DIM = 256
ALPHA = (2 * DEPTH) ** 0.25
LN_EPS = 1e-5
RMS_EPS = 1e-6
PAGE = 128

ATT_HD = 128
ATT_HEADS = 16
ATT_KV = 4
ATT_GROUP = ATT_HEADS // ATT_KV
IDX_HEADS = 16
IDX_DIM = 64
IDX_W_SCALE = (IDX_HEADS ** -0.5) * (IDX_DIM ** -0.5)
TOPK_MAX = 256
ATT_SCALE = ATT_HD ** -0.5
LOG2E = math.log2(math.e)
ATT_MAIN = ATT_HEADS * ATT_HD + 2 * ATT_KV * ATT_HD + IDX_HEADS * IDX_DIM

HG_HEADS = 16
HG_DK = 128
HG_DV = 128

SSM_INNER = 2 * D_MODEL
SSM_P = 64
SSM_HEADS = SSM_INNER // SSM_P
SSM_GROUPS = 8
SSM_HPG = SSM_HEADS // SSM_GROUPS
SSM_N = 128
SSM_CONV = 4
SSM_CH = SSM_INNER + 2 * SSM_GROUPS * SSM_N
SSM_MAIN = SSM_INNER + SSM_CH

LANES = 128
SUBLANES = 8
VMEM_CAPACITY_BYTES = 64 * 1024 * 1024
VMEM_LIMIT_BYTES = VMEM_CAPACITY_BYTES * 7 // 8

ROW_TILE = 1024
ROW_TILE_LN = 512
COL_TILE_PROJ = 1024
COL_TILE_FFN = 512
COL_TILE_PLE = 1024
SEQ_CHUNK = 128
DSA_Q_ROWS = 256
DSA_KEY_CHUNK = 512
SAMPLE_ROWS = 16

INT_MIN = np.int32(-2 ** 31)
NEG_BIG = -1e30


def _cp(*sem):
    return pltpu.CompilerParams(dimension_semantics=sem, vmem_limit_bytes=VMEM_LIMIT_BYTES)


def _nt(a, b):
    return lax.dot_general(a, b, (((1,), (1,)), ((), ())), preferred_element_type=F32)


def _tn(a, b):
    return lax.dot_general(a, b, (((0,), (0,)), ((), ())), preferred_element_type=F32)


def _silu(x):
    return x * jax.nn.sigmoid(x)


def _prefix_sums(tri, v):
    out = None
    rest = v
    for _ in range(3):
        part = rest.astype(BF16)
        rest = rest - part.astype(F32)
        d = jnp.dot(tri, part, preferred_element_type=F32)
        out = d if out is None else out + d
    return out


def _wspec(k, tn, idx, col_block):
    lead = (None,) * len(idx)
    return pl.BlockSpec(lead + (k, tn), lambda n, i: idx + (0, col_block(n)))


def _proj_body(x_ref, xs_ref, w_ref, *rest, with_bf16, w_is_nk):
    wb_ref = rest[-1]
    outs, outs_s = (rest[0:2], rest[2:4]) if with_bf16 else (rest[0:1], rest[1:2])
    mm = _nt if w_is_nk else functools.partial(jnp.dot, preferred_element_type=F32)

    def emit(refs, acc):
        refs[0][...] = acc
        if with_bf16:
            refs[1][...] = acc.astype(BF16)

    @pl.when(pl.program_id(1) == 0)
    def _():
        wb_ref[...] = w_ref[...].astype(BF16)
        emit(outs_s, mm(xs_ref[...], wb_ref[...]))

    emit(outs, mm(x_ref[...], wb_ref[...]))


def _proj(xb, xsb, w, idx, n_cols, tm, tn, with_bf16=False, w_is_nk=False, name="proj"):
    m, k = xb.shape
    ms = xsb.shape[0]
    dts = (F32, BF16) if with_bf16 else (F32,)
    if w_is_nk:
        w_spec = pl.BlockSpec((None,) * len(idx) + (tn, k), lambda n, i: idx + (n, 0))
        w_tile = (tn, k)
    else:
        w_spec = _wspec(k, tn, idx, lambda n: n)
        w_tile = (k, tn)
    res = pl.pallas_call(
        functools.partial(_proj_body, with_bf16=with_bf16, w_is_nk=w_is_nk),
        grid=(n_cols // tn, m // tm),
        in_specs=[pl.BlockSpec((tm, k), lambda n, i: (i, 0)),
                  pl.BlockSpec((ms, k), lambda n, i: (0, 0)),
                  w_spec],
        out_specs=[pl.BlockSpec((tm, tn), lambda n, i: (i, n)) for _ in dts]
                  + [pl.BlockSpec((ms, tn), lambda n, i: (0, n)) for _ in dts],
        out_shape=[jax.ShapeDtypeStruct((m, n_cols), dt) for dt in dts]
                  + [jax.ShapeDtypeStruct((ms, n_cols), dt) for dt in dts],
        scratch_shapes=[pltpu.VMEM(w_tile, BF16)],
        compiler_params=_cp("parallel", "arbitrary"),
        name=name,
    )(xb, xsb, w)
    nd = len(dts)
    return (res[:nd], res[nd:]) if with_bf16 else (res[0], res[1])


def _swiglu_body(x_ref, xs_ref, wg_ref, wu_ref, wd_ref, o_ref, os_ref, wdb_ref, wgub_ref):
    tn = o_ref.shape[1]

    def swiglu(x):
        gu = jnp.dot(x, wgub_ref[...], preferred_element_type=F32)
        return (_silu(gu[:, :tn]) * gu[:, tn:]).astype(BF16)

    @pl.when(pl.program_id(1) == 0)
    def _():
        wgub_ref[:, :tn] = wg_ref[...].astype(BF16)
        wgub_ref[:, tn:] = wu_ref[...].astype(BF16)
        wdb_ref[...] = wd_ref[...].astype(BF16)
        os_ref[...] = swiglu(xs_ref[...])

    o_ref[...] = swiglu(x_ref[...])


def _mm_swiglu(xb, xsb, wgu, wd, idx, tm, tn):
    m, k = xb.shape
    ms = xsb.shape[0]
    f = wgu.shape[-1] // 2
    nj = f // tn
    n_out = wd.shape[-1]
    lead = (None,) * len(idx)
    return pl.pallas_call(
        _swiglu_body,
        grid=(nj, m // tm),
        in_specs=[pl.BlockSpec((tm, k), lambda n, i: (i, 0)),
                  pl.BlockSpec((ms, k), lambda n, i: (0, 0)),
                  _wspec(k, tn, idx, lambda n: n),
                  _wspec(k, tn, idx, lambda n: n + nj),
                  pl.BlockSpec(lead + (f // nj, n_out), lambda n, i: idx + (n, 0))],
        out_specs=[pl.BlockSpec((tm, tn), lambda n, i: (i, n)),
                   pl.BlockSpec((ms, tn), lambda n, i: (0, n)),
                   pl.BlockSpec((f // nj, n_out), lambda n, i: (n, 0))],
        out_shape=[jax.ShapeDtypeStruct((m, f), BF16), jax.ShapeDtypeStruct((ms, f), BF16),
                   jax.ShapeDtypeStruct((f, n_out), BF16)],
        scratch_shapes=[pltpu.VMEM((k, 2 * tn), BF16)],
        compiler_params=_cp("parallel", "arbitrary"),
        name="ffn_up",
    )(xb, xsb, wgu, wgu, wd)


def _cast_body(w_ref, o_ref):
    o_ref[...] = w_ref[...].astype(BF16)


def _cast_w(w, idx, tk=512):
    k, n = w.shape[-2:]
    lead = (None,) * len(idx)
    return pl.pallas_call(
        _cast_body,
        grid=(k // tk,),
        in_specs=[pl.BlockSpec(lead + (tk, n), lambda i: idx + (i, 0))],
        out_specs=pl.BlockSpec((tk, n), lambda i: (i, 0)),
        out_shape=jax.ShapeDtypeStruct((k, n), BF16),
        compiler_params=_cp("parallel"),
        name="cast_w",
    )(w)


LN_SPLIT = 2


def _mm_ln_body(a_ref, as_ref, w_ref, r_ref, rs_ref, g_ref, b_ref, o_ref, ob_ref, os_ref, osb_ref, *, scale, tm):
    g = g_ref[...]
    b = b_ref[...]

    def ln_rows(a, r, o, ob, sl):
        y = ALPHA * r[sl, :] + scale * jnp.dot(a[sl, :], w_ref[...], preferred_element_type=F32)
        mu = jnp.mean(y, axis=-1, keepdims=True)
        yc = y - mu
        var = jnp.mean(yc * yc, axis=-1, keepdims=True)
        out = yc * lax.rsqrt(var + LN_EPS) * g + b
        o[sl, :] = out
        ob[sl, :] = out.astype(BF16)

    @pl.when(pl.program_id(0) == 0)
    def _():
        ln_rows(as_ref, rs_ref, os_ref, osb_ref, pl.ds(0, as_ref.shape[0]))

    rows = tm // LN_SPLIT
    for r in range(0, tm, rows):
        ln_rows(a_ref, r_ref, o_ref, ob_ref, pl.ds(r, rows))


def _mm_ln(ab, asb, wb, res, res_s, g, b, scale, tm, name):
    m, kdim = ab.shape
    ms = asb.shape[0]
    n = wb.shape[1]
    res = pl.pallas_call(
        functools.partial(_mm_ln_body, scale=scale, tm=tm),
        grid=(m // tm,),
        in_specs=[pl.BlockSpec((tm, kdim), lambda i: (i, 0)),
                  pl.BlockSpec((ms, kdim), lambda i: (0, 0)),
                  pl.BlockSpec((kdim, n), lambda i: (0, 0), pipeline_mode=pl.Buffered(1)),
                  pl.BlockSpec((tm, n), lambda i: (i, 0)),
                  pl.BlockSpec((ms, n), lambda i: (0, 0)),
                  pl.BlockSpec((1, n), lambda i: (0, 0)),
                  pl.BlockSpec((1, n), lambda i: (0, 0))],
        out_specs=[pl.BlockSpec((tm, n), lambda i: (i, 0)),
                   pl.BlockSpec((tm, n), lambda i: (i, 0)),
                   pl.BlockSpec((ms, n), lambda i: (0, 0)),
                   pl.BlockSpec((ms, n), lambda i: (0, 0))],
        out_shape=[jax.ShapeDtypeStruct((m, n), F32), jax.ShapeDtypeStruct((m, n), BF16),
                   jax.ShapeDtypeStruct((ms, n), F32), jax.ShapeDtypeStruct((ms, n), BF16)],
        compiler_params=_cp("arbitrary"),
        name=name,
    )(ab, asb, wb, res, res_s, g.reshape(1, n), b.reshape(1, n))
    return res[:2], res[2:]


def _ple_body(xb_ref, p_ref, x_ref, xsb_ref, ps_ref, xs_ref, wg_ref, wp_ref, o_ref, ob_ref, os_ref, osb_ref,
              wgb_ref, wpb_ref):
    def ple(xb, p, x, o, ob):
        gate = jax.nn.sigmoid(jnp.dot(xb[...], wgb_ref[...], preferred_element_type=F32))
        proj = jnp.dot(p[...], wpb_ref[...], preferred_element_type=F32)
        out = x[...] + gate * proj
        o[...] = out
        ob[...] = out.astype(BF16)

    @pl.when(pl.program_id(1) == 0)
    def _():
        wgb_ref[...] = wg_ref[...].astype(BF16)
        wpb_ref[...] = wp_ref[...].astype(BF16)
        ple(xsb_ref, ps_ref, xs_ref, os_ref, osb_ref)

    ple(xb_ref, p_ref, x_ref, o_ref, ob_ref)


def _mm_ple(x32, xb, pb, xs32, xsb, psb, wg, wp, idx, tm, tn):
    m, d = xb.shape
    ms = xsb.shape[0]
    pd = pb.shape[1]
    res = pl.pallas_call(
        _ple_body,
        grid=(d // tn, m // tm),
        in_specs=[pl.BlockSpec((tm, d), lambda n, i: (i, 0)),
                  pl.BlockSpec((tm, pd), lambda n, i: (i, 0)),
                  pl.BlockSpec((tm, tn), lambda n, i: (i, n)),
                  pl.BlockSpec((ms, d), lambda n, i: (0, 0)),
                  pl.BlockSpec((ms, pd), lambda n, i: (0, 0)),
                  pl.BlockSpec((ms, tn), lambda n, i: (0, n)),
                  _wspec(d, tn, idx, lambda n: n),
                  _wspec(pd, tn, idx, lambda n: n)],
        out_specs=[pl.BlockSpec((tm, tn), lambda n, i: (i, n)),
                   pl.BlockSpec((tm, tn), lambda n, i: (i, n)),
                   pl.BlockSpec((ms, tn), lambda n, i: (0, n)),
                   pl.BlockSpec((ms, tn), lambda n, i: (0, n))],
        out_shape=[jax.ShapeDtypeStruct((m, d), F32), jax.ShapeDtypeStruct((m, d), BF16),
                   jax.ShapeDtypeStruct((ms, d), F32), jax.ShapeDtypeStruct((ms, d), BF16)],
        scratch_shapes=[pltpu.VMEM((d, tn), BF16), pltpu.VMEM((pd, tn), BF16)],
        compiler_params=_cp("parallel", "arbitrary"),
        name="ple",
    )(xb, pb, x32, xsb, psb, xs32, wg, wp)
    return res[:2], res[2:]


def _ik_norm_body(p_ref, g_ref, o_ref, ob_ref):
    x = p_ref[...][:, :IDX_DIM]
    mu = jnp.mean(x, axis=-1, keepdims=True)
    xc = x - mu
    out = xc * lax.rsqrt(jnp.mean(xc * xc, axis=-1, keepdims=True) + LN_EPS) * g_ref[...]
    o_ref[...] = out
    ob_ref[...] = out.astype(BF16)


def _ik_norm(proj2, ik_g, tm):
    m = proj2.shape[0]
    return pl.pallas_call(
        _ik_norm_body,
        grid=(m // tm,),
        in_specs=[pl.BlockSpec((tm, LANES), lambda i: (i, 0)),
                  pl.BlockSpec((1, IDX_DIM), lambda i: (0, 0))],
        out_specs=[pl.BlockSpec((tm, IDX_DIM), lambda i: (i, 0)),
                   pl.BlockSpec((tm, IDX_DIM), lambda i: (i, 0))],
        out_shape=[jax.ShapeDtypeStruct((m, IDX_DIM), F32), jax.ShapeDtypeStruct((m, IDX_DIM), BF16)],
        compiler_params=_cp("parallel"),
        name="idx_k_norm",
    )(proj2, ik_g.reshape(1, IDX_DIM))


def _order_key(x):
    bits = pltpu.bitcast(x, jnp.int32)
    return jnp.where(bits < 0, bits ^ jnp.int32(0x7FFFFFFF), bits)


_RADIX_BITS = [INT_MIN] + [np.int32(1 << s) for s in range(30, -1, -1)]


RADIX_ROWS = 128
KNORM_ROWS = 512
SUM_FLOOR = 2.0 ** -100


def _dsa_prompt_body(q_ref, iq_ref, iw_ref, k_ref, v_ref, ik_ref, o_ref, keys_ref, knorm_ref, *, qb, kc, topk):
    i = pl.program_id(1)
    nck = ((i + 1) * qb + kc - 1) // kc
    row_pos = i * qb + lax.broadcasted_iota(jnp.int32, (qb, 1), 0)
    iq = iq_ref[...]
    iw = iw_ref[...][:, IDX_DIM:IDX_DIM + IDX_HEADS] * IDX_W_SCALE
    iq_h = [iq[:, h * IDX_DIM:(h + 1) * IDX_DIM] for h in range(IDX_HEADS)]
    iw_h = [iw[:, h:h + 1] for h in range(IDX_HEADS)]
    col0 = lax.broadcasted_iota(jnp.int32, (1, kc), 1)

    def score_chunk(c, carry):
        off = pl.multiple_of(c * kc, kc)
        ikc = ik_ref[pl.ds(off, kc), :]
        sc = jnp.zeros((qb, kc), F32)
        for h in range(IDX_HEADS):
            sc = sc + iw_h[h] * jnp.maximum(_nt(iq_h[h], ikc), 0.0)
        key = jnp.where(col0 + off <= row_pos, _order_key(sc), INT_MIN)
        keys_ref[:, pl.ds(off, kc)] = key
        return carry

    lax.fori_loop(0, nck, score_chunk, 0)

    nblk = qb // RADIX_ROWS
    los = [jnp.full((RADIX_ROWS, 1), INT_MIN, jnp.int32) for _ in range(nblk)]
    for bit in _RADIX_BITS:
        cands = [lo + bit for lo in los]
        accs = []
        for blk in range(nblk):
            cand_b = jnp.broadcast_to(cands[blk], (RADIX_ROWS, LANES))

            def count_chunk(c, acc, cand_b=cand_b, r0=blk * RADIX_ROWS):
                off = pl.multiple_of(c * kc, kc)
                for t in range(kc // LANES):
                    tile = keys_ref[r0:r0 + RADIX_ROWS, pl.ds(off + t * LANES, LANES)]
                    acc = acc + jnp.where(tile >= cand_b, 1.0, 0.0)
                return acc

            accs.append(lax.fori_loop(0, nck, count_chunk, jnp.zeros((RADIX_ROWS, LANES), F32)))
        cnts = [jnp.sum(acc, axis=-1, keepdims=True) for acc in accs]
        los = [jnp.where(cnts[blk] >= topk, cands[blk], los[blk]) for blk in range(nblk)]
    thr = jnp.maximum(jnp.concatenate(los, axis=0), INT_MIN + 1)

    @pl.when(i == 0)
    def _():
        for g in range(ATT_KV):
            best = jnp.zeros((1, 1), F32)
            for r0 in range(0, k_ref.shape[0], KNORM_ROWS):
                kk = k_ref[r0:r0 + KNORM_ROWS, g * ATT_HD:(g + 1) * ATT_HD].astype(F32)
                best = jnp.maximum(best, jnp.max(jnp.sum(kk * kk, axis=-1, keepdims=True), axis=0, keepdims=True))
            knorm_ref[g:g + 1, :] = jnp.broadcast_to(jnp.sqrt(best), (1, LANES))

    def store(g, out):
        for j in range(ATT_GROUP):
            h = g * ATT_GROUP + j
            o_ref[:, h * ATT_HD:(h + 1) * ATT_HD] = out[j].astype(o_ref.dtype)

    for g in range(ATT_KV):
        qg = jnp.concatenate(
            [q_ref[:, (g * ATT_GROUP + j) * ATT_HD:(g * ATT_GROUP + j + 1) * ATT_HD] for j in range(ATT_GROUP)], axis=0)
        qg = (qg * (ATT_SCALE * LOG2E)).astype(BF16)
        q32 = qg.astype(F32)
        shift = (jnp.sqrt(jnp.sum(q32 * q32, axis=-1, keepdims=True)) * knorm_ref[g:g + 1, 0:1]).reshape(
            ATT_GROUP, qb, 1)

        def chunk_operands(c, g=g):
            off = pl.multiple_of(c * kc, kc)
            kch = k_ref[pl.ds(off, kc), g * ATT_HD:(g + 1) * ATT_HD]
            vch = v_ref[pl.ds(off, kc), g * ATT_HD:(g + 1) * ATT_HD]
            sel = (keys_ref[:, pl.ds(off, kc)] >= thr)[None]
            return kch, vch, sel

        def fixed_chunk(c, carry, qg=qg, shift=shift):
            l, acc = carry
            kch, vch, sel = chunk_operands(c)
            p = jnp.exp2(jnp.where(sel, _nt(qg, kch).reshape(ATT_GROUP, qb, kc) - shift, -jnp.inf))
            l = l + jnp.sum(p, axis=-1, keepdims=True)
            pv = jnp.dot(p.reshape(ATT_GROUP * qb, kc).astype(BF16), vch, preferred_element_type=F32)
            return l, acc + pv.reshape(ATT_GROUP, qb, ATT_HD)

        l, acc = lax.fori_loop(0, nck, fixed_chunk, (jnp.zeros((ATT_GROUP, qb, 1), F32),
                                                     jnp.zeros((ATT_GROUP, qb, ATT_HD), F32)))
        healthy = jnp.min(l) > SUM_FLOOR

        @pl.when(healthy)
        def _(g=g, l=l, acc=acc):
            store(g, acc / l)

        @pl.when(jnp.logical_not(healthy))
        def _(g=g, qg=qg):
            def running_chunk(c, carry):
                m, l, acc = carry
                kch, vch, sel = chunk_operands(c)
                s = jnp.where(sel, _nt(qg, kch).reshape(ATT_GROUP, qb, kc), -jnp.inf)
                m_new = jnp.maximum(m, jnp.max(s, axis=-1, keepdims=True))
                p = jnp.exp2(s - m_new)
                a = jnp.exp2(m - m_new)
                l = a * l + jnp.sum(p, axis=-1, keepdims=True)
                pv = jnp.dot(p.reshape(ATT_GROUP * qb, kc).astype(BF16), vch, preferred_element_type=F32)
                return m_new, l, a * acc + pv.reshape(ATT_GROUP, qb, ATT_HD)

            init = (jnp.full((ATT_GROUP, qb, 1), NEG_BIG, F32), jnp.zeros((ATT_GROUP, qb, 1), F32),
                    jnp.zeros((ATT_GROUP, qb, ATT_HD), F32))
            _, l, acc = lax.fori_loop(0, nck, running_chunk, init)
            store(g, acc / l)


def _dsa_prompt(proj, projb, proj2, ikb, nb, s, qb=DSA_Q_ROWS, kc=DSA_KEY_CHUNK):
    nq = s // qb
    topk = min(TOPK_MAX, s // 4)
    qw = ATT_HEADS * ATT_HD
    kvw = ATT_KV * ATT_HD
    iqw = IDX_HEADS * IDX_DIM
    return pl.pallas_call(
        functools.partial(_dsa_prompt_body, qb=qb, kc=kc, topk=topk),
        grid=(nb, nq),
        in_specs=[pl.BlockSpec((qb, qw), lambda b, i: (b * nq + i, 0)),
                  pl.BlockSpec((qb, iqw), lambda b, i: (b * nq + i, (qw + 2 * kvw) // iqw)),
                  pl.BlockSpec((qb, LANES), lambda b, i: (b * nq + i, 0)),
                  pl.BlockSpec((s, kvw), lambda b, i: (b, qw // kvw)),
                  pl.BlockSpec((s, kvw), lambda b, i: (b, qw // kvw + 1)),
                  pl.BlockSpec((s, IDX_DIM), lambda b, i: (b, 0))],
        out_specs=pl.BlockSpec((qb, qw), lambda b, i: (b * nq + i, 0)),
        out_shape=jax.ShapeDtypeStruct((nb * s, qw), BF16),
        scratch_shapes=[pltpu.VMEM((qb, s), jnp.int32), pltpu.VMEM((SUBLANES, LANES), F32)],
        compiler_params=_cp("parallel", "arbitrary"),
        name="dsa_prompt",
    )(proj, projb, proj2, projb, projb, ikb)


SCORE_PAGES = 32


def _dsa_s_scores_body(pt_ref, iq_ref, iw_ref, ikn_ref, *rest, n_pages):
    page_refs, o_ref = rest[:SCORE_PAGES], rest[SCORE_PAGES]
    p = pl.program_id(1)
    iq = iq_ref[...].astype(BF16)
    iw = iw_ref[...] * IDX_W_SCALE

    @pl.when(p == 0)
    def _():
        own = jnp.broadcast_to(ikn_ref[...], (SUBLANES, IDX_DIM)).astype(BF16)
        d = jnp.maximum(_nt(iq, own), 0.0)
        sc = jnp.sum(iw * d, axis=0, keepdims=True)
        o_ref[n_pages:n_pages + SUBLANES, :] = jnp.broadcast_to(sc[:, 0:1], (SUBLANES, LANES))

    for r in range(SCORE_PAGES):
        page_t = page_refs[r][...].astype(BF16)
        d = jnp.maximum(jnp.dot(iq, page_t, preferred_element_type=F32), 0.0)
        o_ref[pl.ds(p * SCORE_PAGES + r, 1), :] = jnp.sum(iw * d, axis=0, keepdims=True)


def _dsa_s_scores(page_table, iq, iw, ik_new, cache_ik_t, layer):
    nb, n_pages = page_table.shape
    page_spec = lambda r: pl.BlockSpec((None, None, IDX_DIM, PAGE),
                                       lambda b, p, pt: (layer, pt[b, p * SCORE_PAGES + r], 0, 0))
    return pl.pallas_call(
        functools.partial(_dsa_s_scores_body, n_pages=n_pages),
        grid_spec=pltpu.PrefetchScalarGridSpec(
            num_scalar_prefetch=1,
            grid=(nb, n_pages // SCORE_PAGES),
            in_specs=[pl.BlockSpec((None, IDX_HEADS, IDX_DIM), lambda b, p, pt: (b, 0, 0)),
                      pl.BlockSpec((None, IDX_HEADS, 1), lambda b, p, pt: (b, 0, 0)),
                      pl.BlockSpec((None, 1, IDX_DIM), lambda b, p, pt: (b, 0, 0))]
                     + [page_spec(r) for r in range(SCORE_PAGES)],
            out_specs=pl.BlockSpec((None, n_pages + SUBLANES, LANES), lambda b, p, pt: (b, 0, 0)),
        ),
        out_shape=jax.ShapeDtypeStruct((nb, n_pages + SUBLANES, LANES), F32),
        compiler_params=_cp("parallel", "arbitrary"),
        name="dsa_sample_scores",
    )(page_table, iq, iw, ik_new, *([cache_ik_t] * SCORE_PAGES))


def _dsa_s_select_body(sc_ref, ptc_ref, idx_ref, meta_ref, rank_ref, phys_ref, *, n_pages, topk):
    shape = (n_pages + SUBLANES, LANES)
    rows = lax.broadcasted_iota(jnp.int32, shape, 0)
    cols = lax.broadcasted_iota(jnp.int32, shape, 1)
    live = (rows < n_pages) | ((rows == n_pages) & (cols == 0))
    keys = jnp.where(live, _order_key(sc_ref[...]), INT_MIN)
    lo = jnp.full((1, 1), INT_MIN, jnp.int32)
    for bit in _RADIX_BITS:
        cand = lo + bit
        cnt = jnp.sum(jnp.sum(jnp.where(keys >= cand, 1.0, 0.0), axis=0, keepdims=True), axis=1, keepdims=True)
        lo = jnp.where(cnt >= topk, cand, lo)
    sel = keys >= jnp.maximum(lo, INT_MIN + 1)
    sel_c = sel[:n_pages, :]
    own = jnp.where(sel[n_pages:n_pages + 1, 0:1], 1, 0)

    r_i = lax.broadcasted_iota(jnp.int32, (LANES, LANES), 0)
    c_i = lax.broadcasted_iota(jnp.int32, (LANES, LANES), 1)
    ones_le = jnp.where(r_i <= c_i, 1.0, 0.0).astype(BF16)
    ones_gt = jnp.where(r_i > c_i, 1.0, 0.0).astype(BF16)
    m = jnp.where(sel_c, 1.0, 0.0).astype(BF16)
    within = jnp.dot(m, ones_le, preferred_element_type=F32)
    tot = jnp.broadcast_to(within[:, LANES - 1:LANES], (n_pages, LANES)).astype(BF16)
    before = jnp.dot(ones_gt, tot, preferred_element_type=F32)
    rank_ref[...] = jnp.where(sel_c, (within + before).astype(jnp.int32) - 1, -1)
    phys_ref[...] = (ptc_ref[...] * PAGE + c_i[:n_pages, :]).astype(F32)
    n_sel = (before[n_pages - 1:n_pages, 0:1] + within[n_pages - 1:n_pages, LANES - 1:LANES]).astype(jnp.int32)

    slot = lax.broadcasted_iota(jnp.int32, (topk, 1), 0)

    def place(r, acc):
        return acc + jnp.where(rank_ref[pl.ds(r, 1), :] == slot, phys_ref[pl.ds(r, 1), :], 0.0)

    acc = lax.fori_loop(0, n_pages, place, jnp.zeros((topk, LANES), F32), unroll=8)
    idx_ref[...] = jnp.sum(acc, axis=-1, keepdims=True).astype(jnp.int32)
    mrow = lax.broadcasted_iota(jnp.int32, (SUBLANES, LANES), 0)
    meta_ref[...] = jnp.where(mrow == 0, jnp.minimum(n_sel, topk), own)


def _dsa_s_select(scores, page_table):
    nb, n_pages = page_table.shape
    assert n_pages == LANES
    topk = min(TOPK_MAX, (n_pages * PAGE + 1) // 4)
    idx, meta = pl.pallas_call(
        functools.partial(_dsa_s_select_body, n_pages=n_pages, topk=topk),
        grid=(nb,),
        in_specs=[pl.BlockSpec((None, n_pages + SUBLANES, LANES), lambda b: (b, 0, 0)),
                  pl.BlockSpec((None, n_pages, 1), lambda b: (b, 0, 0))],
        out_specs=[pl.BlockSpec((None, topk, 1), lambda b: (b, 0, 0)),
                   pl.BlockSpec((None, SUBLANES, LANES), lambda b: (b, 0, 0))],
        out_shape=[jax.ShapeDtypeStruct((nb, topk, 1), jnp.int32),
                   jax.ShapeDtypeStruct((nb, SUBLANES, LANES), jnp.int32)],
        scratch_shapes=[pltpu.VMEM((n_pages, LANES), jnp.int32), pltpu.VMEM((n_pages, LANES), F32)],
        compiler_params=_cp("parallel"),
        name="dsa_sample_select",
    )(scores, page_table.reshape(nb, n_pages, 1))
    return idx.reshape(nb, topk), meta[:, 0:2, 0]


def _dsa_s_attend_body(idx_ref, meta_ref, q_ref, kn_ref, vn_ref, ck_ref, cv_ref, o_ref, kbuf, vbuf, sem,
                       *, layer, topk):
    b = pl.program_id(0)
    nb = pl.num_programs(0)
    slot = b % 2

    def row_copies(tok, buf_slot, j):
        row = idx_ref[tok, j]
        page, off = row // PAGE, row % PAGE
        dst = pl.ds(j * ATT_KV, ATT_KV)
        return (pltpu.make_async_copy(ck_ref.at[layer, page, off], kbuf.at[buf_slot, dst, :], sem.at[buf_slot, 0]),
                pltpu.make_async_copy(cv_ref.at[layer, page, off], vbuf.at[buf_slot, dst, :], sem.at[buf_slot, 1]))

    def start_gather(tok, buf_slot):
        def body(j, carry):
            for cp in row_copies(tok, buf_slot, j):
                cp.start()
            return carry
        lax.fori_loop(0, topk, body, 0)

    @pl.when(b == 0)
    def _():
        start_gather(0, 0)

    @pl.when(b + 1 < nb)
    def _():
        start_gather(b + 1, 1 - slot)

    def wait_body(j, carry):
        for cp in row_copies(b, slot, j):
            cp.wait()
        return carry

    lax.fori_loop(0, topk, wait_body, 0)

    n_sel = meta_ref[b, 0]
    own = meta_ref[b, 1] > 0
    qb = q_ref[...].astype(BF16)
    ncol = topk * ATT_KV
    col = lax.broadcasted_iota(jnp.int32, (ATT_HEADS, ncol), 1)
    head = lax.broadcasted_iota(jnp.int32, (ATT_HEADS, ncol), 0)
    valid = (col % ATT_KV == head // ATT_GROUP) & (col // ATT_KV < n_sel)
    s = jnp.where(valid, _nt(qb, kbuf[slot].astype(BF16)) * ATT_SCALE, NEG_BIG)
    kn = kn_ref[...].astype(BF16).astype(F32)
    vn = vn_ref[...].astype(BF16).astype(F32)
    expand = lambda a: jnp.concatenate(
        [jnp.broadcast_to(a[:, g * ATT_HD:(g + 1) * ATT_HD], (ATT_GROUP, ATT_HD)) for g in range(ATT_KV)], axis=0)
    s_own = jnp.where(own, jnp.sum(qb.astype(F32) * expand(kn), axis=-1, keepdims=True) * ATT_SCALE, NEG_BIG)
    m = jnp.maximum(jnp.max(s, axis=-1, keepdims=True), s_own)
    p = jnp.where(valid, jnp.exp(s - m), 0.0)
    p_own = jnp.where(own, jnp.exp(s_own - m), 0.0)
    l = jnp.sum(p, axis=-1, keepdims=True) + p_own
    acc = (jnp.dot(p.astype(BF16), vbuf[slot].astype(BF16), preferred_element_type=F32)
           + p_own.astype(BF16).astype(F32) * expand(vn))
    o_ref[...] = acc / l


def _dsa_s_attend(idx, meta, q, k_new, v_new, cache_k, cache_v, layer):
    nb, topk = idx.shape
    kvw = ATT_KV * ATT_HD
    return pl.pallas_call(
        functools.partial(_dsa_s_attend_body, layer=layer, topk=topk),
        grid_spec=pltpu.PrefetchScalarGridSpec(
            num_scalar_prefetch=2,
            grid=(nb,),
            in_specs=[pl.BlockSpec((None, ATT_HEADS, ATT_HD), lambda b, idx, meta: (b, 0, 0)),
                      pl.BlockSpec((None, 1, kvw), lambda b, idx, meta: (b, 0, 0)),
                      pl.BlockSpec((None, 1, kvw), lambda b, idx, meta: (b, 0, 0)),
                      pl.BlockSpec(memory_space=pl.ANY),
                      pl.BlockSpec(memory_space=pl.ANY)],
            out_specs=pl.BlockSpec((None, ATT_HEADS, ATT_HD), lambda b, idx, meta: (b, 0, 0)),
            scratch_shapes=[pltpu.VMEM((2, topk * ATT_KV, ATT_HD), F32),
                            pltpu.VMEM((2, topk * ATT_KV, ATT_HD), F32),
                            pltpu.SemaphoreType.DMA((2, 2))],
        ),
        out_shape=jax.ShapeDtypeStruct((nb, ATT_HEADS, ATT_HD), F32),
        compiler_params=_cp("arbitrary"),
        name="dsa_sample_attend",
    )(idx, meta, q, k_new, v_new, cache_k, cache_v)


HG_SUB = 16


HG_HB = 4
HG_EXP_LIMIT = 80.0


def _hgrn_body(lbl_ref, q_ref, f_ref, i_ref, g_ref, ng_ref, s0_ref, o_ref, so_ref, st_ref, *, layer, c, t, nc):
    ci = pl.program_id(2)
    mid = c // 2

    @pl.when(ci == 0)
    def _():
        for hh in range(HG_HB):
            st_ref[hh] = s0_ref[hh].T

    logits = lbl_ref[...]
    e = jnp.exp(logits - jnp.max(logits, axis=0, keepdims=True))
    soft = e / jnp.sum(e, axis=0, keepdims=True)
    lb_all = jnp.zeros((1, HG_HB * HG_DK), F32)
    for r in range(1, layer + 1):
        lb_all = lb_all + soft[r:r + 1, :]
    ng = ng_ref[...]

    def gates(hh, sl, row0):
        cs = slice(hh * HG_DK, (hh + 1) * HG_DK)
        lb = lb_all[:, cs]
        fg = lb + (1.0 - lb) * jax.nn.sigmoid(f_ref[sl, cs])
        lf = jnp.log(fg)
        kk = 1.0 - fg
        if t % c:
            n = lf.shape[0]
            valid = (ci * c + row0 + lax.broadcasted_iota(jnp.int32, (n, 1), 0)) < t
            lf = jnp.where(valid, lf, 0.0)
            kk = jnp.where(valid, kk, 0.0)
        return lf, kk

    def finish(hh, sl, o):
        cs = slice(hh * HG_DV, (hh + 1) * HG_DV)
        on = o * lax.rsqrt(jnp.mean(o * o, axis=-1, keepdims=True) + RMS_EPS) * ng
        o_ref[sl, cs] = (on * _silu(g_ref[sl, cs])).astype(o_ref.dtype)

    tri_c = lax.broadcasted_iota(jnp.int32, (c, c), 0) >= lax.broadcasted_iota(jnp.int32, (c, c), 1)
    full = pl.ds(0, c)
    lfs, kks, bs = [], [], []
    safe = None
    for hh in range(HG_HB):
        lf, kk = gates(hh, full, 0)
        b = _prefix_sums(tri_c.astype(BF16), lf)
        bm = b[mid - 1:mid, :]
        ok = jnp.min(jnp.minimum(bm, b[c - 1:c, :] - bm)) > -HG_EXP_LIMIT
        safe = ok if safe is None else jnp.logical_and(safe, ok)
        lfs.append(lf)
        kks.append(kk)
        bs.append(b)

    @pl.when(safe)
    def _():
        for hh in range(HG_HB):
            cs = slice(hh * HG_DK, (hh + 1) * HG_DK)
            b, kk = bs[hh], kks[hh]
            bm = b[mid - 1:mid, :]
            bl = b[c - 1:c, :]
            qq = _silu(q_ref[:, cs])
            vv = i_ref[:, cs].astype(BF16)
            att = jnp.where(tri_c, _nt((qq * jnp.exp(b - bm)).astype(BF16), (kk * jnp.exp(bm - b)).astype(BF16)), 0.0)
            st = st_ref[hh]
            o = (jnp.dot(att.astype(BF16), vv, preferred_element_type=F32)
                 + _nt((qq * jnp.exp(b)).astype(BF16), st.astype(BF16)))
            st_ref[hh] = st * jnp.exp(bl) + _tn(vv, (kk * jnp.exp(bl - b)).astype(BF16))
            finish(hh, full, o)

    @pl.when(jnp.logical_not(safe))
    def _():
        rows = lax.broadcasted_iota(jnp.int32, (HG_SUB, 1), 0)
        tri = (lax.broadcasted_iota(jnp.int32, (HG_SUB, HG_SUB), 0)
               >= lax.broadcasted_iota(jnp.int32, (HG_SUB, HG_SUB), 1)).astype(F32)
        for hh in range(HG_HB):
            cs = slice(hh * HG_DK, (hh + 1) * HG_DK)

            def sub_block(sb, carry, hh=hh, cs=cs):
                row0 = pl.multiple_of(sb * HG_SUB, HG_SUB)
                sl = pl.ds(row0, HG_SUB)
                lf, kk = gates(hh, sl, row0)
                qq = _silu(q_ref[sl, cs])
                vv = i_ref[sl, cs]
                b = _prefix_sums(tri.astype(BF16), lf)
                st = st_ref[hh]
                o = _nt((qq * jnp.exp(b)).astype(BF16), st.astype(BF16))
                for s in range(HG_SUB):
                    dec = jnp.exp(jnp.where(rows >= s, b - b[s:s + 1, :], -jnp.inf))
                    att = jnp.sum(qq * dec * kk[s:s + 1, :], axis=-1, keepdims=True)
                    o = o + att * vv[s:s + 1, :]
                bl = b[HG_SUB - 1:HG_SUB, :]
                st_ref[hh] = st * jnp.exp(bl) + _tn(vv.astype(BF16), (kk * jnp.exp(bl - b)).astype(BF16))
                finish(hh, sl, o)
                return carry

            lax.fori_loop(0, c // HG_SUB, sub_block, 0)

    @pl.when(ci == nc - 1)
    def _():
        for hh in range(HG_HB):
            so_ref[hh] = st_ref[hh].T


def _hgrn(proj, lb_logits, norm_g, s0, layer, t, c):
    nb, tpad, _ = proj.shape
    nc = tpad // c
    nhb = HG_HEADS // HG_HB
    w = HG_HB * HG_DK
    return pl.pallas_call(
        functools.partial(_hgrn_body, layer=layer, c=c, t=t, nc=nc),
        grid=(nb, nhb, nc),
        in_specs=[pl.BlockSpec((DEPTH, w), lambda b, h, ci: (0, h)),
                  pl.BlockSpec((None, c, w), lambda b, h, ci: (b, ci, h)),
                  pl.BlockSpec((None, c, w), lambda b, h, ci: (b, ci, nhb + h)),
                  pl.BlockSpec((None, c, w), lambda b, h, ci: (b, ci, 2 * nhb + h)),
                  pl.BlockSpec((None, c, w), lambda b, h, ci: (b, ci, 3 * nhb + h)),
                  pl.BlockSpec((1, HG_DV), lambda b, h, ci: (0, 0)),
                  pl.BlockSpec((None, HG_HB, HG_DK, HG_DV), lambda b, h, ci: (b, h, 0, 0))],
        out_specs=[pl.BlockSpec((None, c, w), lambda b, h, ci: (b, ci, h)),
                   pl.BlockSpec((None, HG_HB, HG_DK, HG_DV), lambda b, h, ci: (b, h, 0, 0))],
        out_shape=[jax.ShapeDtypeStruct((nb, tpad, HG_HEADS * HG_DV), BF16),
                   jax.ShapeDtypeStruct((nb, HG_HEADS, HG_DK, HG_DV), F32)],
        scratch_shapes=[pltpu.VMEM((HG_HB, HG_DV, HG_DK), F32)],
        compiler_params=_cp("parallel", "parallel", "arbitrary"),
        name="hgrn2",
    )(lb_logits, proj, proj, proj, proj, norm_g.reshape(1, HG_DV), s0)


def _expand_heads(v, e, terms):
    out = None
    rest = v
    for _ in range(terms):
        part = rest.astype(BF16)
        rest = rest - part.astype(F32)
        d = jnp.dot(part, e, preferred_element_type=F32)
        out = d if out is None else out + d
    return out


def _ssd_body(zx_ref, dt_ref, cs_ref, cw_ref, cbias_ref, e_ref, dtb_ref, alog_ref, dx_ref, ng_ref, s0_ref,
              o_ref, so_ref, nc_ref, st_ref, xc_ref, halo_ref, *, c, t, nc):
    ci = pl.program_id(1)
    gw = SSM_HPG * SSM_P

    @pl.when(ci == 0)
    def _():
        for blk in range(SSM_INNER // LANES):
            st_ref[:, blk * LANES:(blk + 1) * LANES] = s0_ref[blk * LANES:(blk + 1) * LANES, :].T
        halo_ref[...] = cs_ref[...]

    full = jnp.concatenate([halo_ref[...], zx_ref[:, SSM_INNER:SSM_INNER + SSM_CH]], axis=0)
    conv = cbias_ref[...]
    for k in range(SSM_CONV):
        lo = SUBLANES - (SSM_CONV - 1) + k
        conv = conv + full[lo:lo + c, :] * cw_ref[k:k + 1, :]
    xc_ref[...] = _silu(conv)
    halo_ref[...] = full[c:c + SUBLANES, :]

    @pl.when(ci == nc - 1)
    def _():
        tv = t - (nc - 1) * c
        tail = full[SUBLANES + tv - (SSM_CONV - 1):SUBLANES + tv, :]
        nc_ref[...] = jnp.concatenate([tail, jnp.zeros((SUBLANES - (SSM_CONV - 1), SSM_CH), F32)], axis=0)

    e = e_ref[...]
    rows = lax.broadcasted_iota(jnp.int32, (c, 1), 0)
    tri_b = lax.broadcasted_iota(jnp.int32, (c, c), 0) >= lax.broadcasted_iota(jnp.int32, (c, c), 1)
    lane_lo = lax.broadcasted_iota(jnp.int32, (1, LANES), 1) < SSM_P

    dt = jax.nn.softplus(dt_ref[...] + dtb_ref[...])
    if t % c:
        dt = jnp.where(ci * c + rows < t, dt, 0.0)
    da = dt * (-jnp.exp(alog_ref[...]))
    bcum = _prefix_sums(tri_b.astype(BF16), da)
    bcum_t = bcum.T
    dt_t = dt.T
    bl = bcum[c - 1:c, :]
    eb_x = _expand_heads(jnp.exp(bcum), e, 2)
    w_x = _expand_heads(jnp.exp(bl - bcum) * dt, e, 2)
    decay_x = _expand_heads(jnp.broadcast_to(jnp.exp(bl), (SUBLANES, LANES)), e, 3)[0:1, :]

    xs = xc_ref[:, 0:SSM_INNER]
    xdt = xs.astype(BF16)
    xw = (xs * w_x).astype(BF16)
    y = xs * dx_ref[...]
    zg = _silu(zx_ref[:, 0:SSM_INNER])
    for g in range(SSM_GROUPS):
        bg = xc_ref[:, SSM_INNER + g * SSM_N:SSM_INNER + (g + 1) * SSM_N]
        cg = xc_ref[:, SSM_INNER + (SSM_GROUPS + g) * SSM_N:SSM_INNER + (SSM_GROUPS + g + 1) * SSM_N].astype(BF16)
        cb = _nt(cg, bg.astype(BF16))
        st_g = st_ref[:, g * gw:(g + 1) * gw]
        yg = jnp.dot(cg, st_g.astype(BF16), preferred_element_type=F32) * eb_x[:, g * gw:(g + 1) * gw]
        parts = []
        for jp in range(SSM_HPG // 2):
            xpair = xdt[:, g * gw + jp * LANES:g * gw + (jp + 1) * LANES]
            acc = None
            for half in range(2):
                h = g * SSM_HPG + jp * 2 + half
                dec = jnp.exp(jnp.where(tri_b, bcum[:, h:h + 1] - bcum_t[h:h + 1, :], -jnp.inf))
                w = (cb * dec * dt_t[h:h + 1, :]).astype(BF16)
                xh = jnp.where(lane_lo if half == 0 else jnp.logical_not(lane_lo), xpair, 0.0).astype(BF16)
                r = jnp.dot(w, xh, preferred_element_type=F32)
                acc = r if acc is None else acc + r
            parts.append(acc)
        yg = yg + jnp.concatenate(parts, axis=1)
        st_ref[:, g * gw:(g + 1) * gw] = (st_g * decay_x[:, g * gw:(g + 1) * gw]
                                          + jnp.dot(bg.T.astype(BF16), xw[:, g * gw:(g + 1) * gw],
                                                    preferred_element_type=F32))
        yg = (yg + y[:, g * gw:(g + 1) * gw]) * zg[:, g * gw:(g + 1) * gw]
        yg = yg * lax.rsqrt(jnp.mean(yg * yg, axis=-1, keepdims=True) + RMS_EPS) * ng_ref[:, g * gw:(g + 1) * gw]
        o_ref[:, g * gw:(g + 1) * gw] = yg.astype(o_ref.dtype)

    @pl.when(ci == nc - 1)
    def _():
        for blk in range(SSM_INNER // LANES):
            so_ref[blk * LANES:(blk + 1) * LANES, :] = st_ref[:, blk * LANES:(blk + 1) * LANES].T


def _ssd(zx, dt_raw, cs_pad, conv_w, conv_b, expand, dt_bias, a_log, d_x, norm_g, s0, t, c):
    nb, tpad, _ = zx.shape
    nc = tpad // c
    return pl.pallas_call(
        functools.partial(_ssd_body, c=c, t=t, nc=nc),
        grid=(nb, nc),
        in_specs=[pl.BlockSpec((None, c, SSM_MAIN), lambda b, ci: (b, ci, 0)),
                  pl.BlockSpec((None, c, LANES), lambda b, ci: (b, ci, 0)),
                  pl.BlockSpec((None, SUBLANES, SSM_CH), lambda b, ci: (b, 0, 0)),
                  pl.BlockSpec((SSM_CONV, SSM_CH), lambda b, ci: (0, 0)),
                  pl.BlockSpec((1, SSM_CH), lambda b, ci: (0, 0)),
                  pl.BlockSpec((LANES, SSM_INNER), lambda b, ci: (0, 0)),
                  pl.BlockSpec((1, LANES), lambda b, ci: (0, 0)),
                  pl.BlockSpec((1, LANES), lambda b, ci: (0, 0)),
                  pl.BlockSpec((1, SSM_INNER), lambda b, ci: (0, 0)),
                  pl.BlockSpec((1, SSM_INNER), lambda b, ci: (0, 0)),
                  pl.BlockSpec((None, SSM_INNER, SSM_N), lambda b, ci: (b, 0, 0))],
        out_specs=[pl.BlockSpec((None, c, SSM_INNER), lambda b, ci: (b, ci, 0)),
                   pl.BlockSpec((None, SSM_INNER, SSM_N), lambda b, ci: (b, 0, 0)),
                   pl.BlockSpec((None, SUBLANES, SSM_CH), lambda b, ci: (b, 0, 0))],
        out_shape=[jax.ShapeDtypeStruct((nb, tpad, SSM_INNER), BF16),
                   jax.ShapeDtypeStruct((nb, SSM_INNER, SSM_N), F32),
                   jax.ShapeDtypeStruct((nb, SUBLANES, SSM_CH), F32)],
        scratch_shapes=[pltpu.VMEM((SSM_N, SSM_INNER), F32),
                        pltpu.VMEM((c, SSM_CH), F32),
                        pltpu.VMEM((SUBLANES, SSM_CH), F32)],
        compiler_params=_cp("parallel", "arbitrary"),
        name="ssd",
    )(zx, dt_raw, cs_pad, conv_w, conv_b.reshape(1, SSM_CH), expand, dt_bias, a_log, d_x,
      norm_g.reshape(1, SSM_INNER), s0)


def _pad_cols(w, n):
    return jnp.pad(w, ((0, 0), (0, n - w.shape[1])))


def _pad_rows_to(w, n):
    return jnp.pad(w, ((0, n - w.shape[0]), (0, 0)))


def _pad_time(a, tpad):
    return jnp.pad(a, ((0, 0), (0, tpad - a.shape[1]), (0, 0)))


def kernel(x_prompt, x_sample, cache_k, cache_v, cache_idx_k, state_hgrn, state_ssm, state_conv, page_table, p_prompt, p_sample, ln_g, ln_b, ffn_w_gate_up, ffn_w_down, ple_w_proj, ple_w_gate, att_w_in, att_idx_k_norm, att_w_o, hg_w_in, hg_lb_logits, hg_norm_g, hg_w_o, ssm_w_in, ssm_conv_w, ssm_conv_b, ssm_dt_bias, ssm_a_log, ssm_d, ssm_norm_g, ssm_w_o):
    nbp, seq, d = x_prompt.shape
    nbs = x_sample.shape[0]
    mp = nbp * seq
    ms, chunk, tm, tm_ln = SAMPLE_ROWS, SEQ_CHUNK, ROW_TILE, ROW_TILE_LN
    assert nbs <= ms and mp % tm == 0 and seq % chunk == 0

    x_p = x_prompt.reshape(mp, d)
    x_s = jnp.pad(x_sample.reshape(nbs, d), ((0, ms - nbs), (0, 0)))
    xb_p, xb_s = x_p.astype(BF16), x_s.astype(BF16)
    pl_p = p_prompt.reshape(DEPTH, mp, PLE_DIM)
    pl_s = jnp.pad(p_sample.reshape(DEPTH, nbs, PLE_DIM), ((0, 0), (0, ms - nbs), (0, 0)))
    pad_rows = lambda a: jnp.pad(a, ((0, ms - nbs), (0, 0)))
    att_w_in_t = jnp.swapaxes(att_w_in, 1, 2)
    ssm_w_in_t = jnp.swapaxes(ssm_w_in, 1, 2)
    cache_idx_k_t = jnp.swapaxes(cache_idx_k, 2, 3)

    expand = jnp.asarray(np.kron(np.eye(LANES, SSM_HEADS, dtype=np.float32),
                                 np.ones((1, SSM_P), np.float32)), BF16)
    outs = {}

    def ffn_ln(i, which, ln_idx, x_p, xb_p, x_s, xb_s):
        h_p, h_s, wd = _mm_swiglu(xb_p, xb_s, ffn_w_gate_up, ffn_w_down, (i, which), tm, COL_TILE_FFN)
        return _mm_ln(h_p, h_s, wd, x_p, x_s, ln_g[i, ln_idx], ln_b[i, ln_idx], 0.5, tm_ln, "ffn_down_ln")

    for i in range(DEPTH):
        j = i // N_MIXERS
        (x_p, xb_p), (x_s, xb_s) = ffn_ln(i, 0, 0, x_p, xb_p, x_s, xb_s)

        if i % N_MIXERS == 0:
            w_small = _pad_rows_to(att_w_in_t[j, ATT_MAIN:, :], LANES)
            w_o = _cast_w(att_w_o, (j,))
            qw, kvw = ATT_HEADS * ATT_HD, ATT_KV * ATT_HD
            (proj, projb), (proj_s, _) = _proj(xb_p, xb_s, att_w_in_t, (j,), ATT_MAIN, tm, COL_TILE_PROJ, with_bf16=True,
                                               w_is_nk=True, name="att_in")
            proj2, proj2_s = _proj(xb_p, xb_s, w_small, (), LANES, tm, LANES, w_is_nk=True, name="att_in_idx")
            ik, ikb = _ik_norm(proj2, att_idx_k_norm[j], tm)
            ik_s, _ = _ik_norm(proj2_s, att_idx_k_norm[j], ms)
            o_p = _dsa_prompt(proj, projb, proj2, ikb, nbp, seq)
            outs.setdefault("k_p", []).append(proj[:, qw:qw + kvw].reshape(nbp, seq, ATT_KV, ATT_HD))
            outs.setdefault("v_p", []).append(proj[:, qw + kvw:qw + 2 * kvw].reshape(nbp, seq, ATT_KV, ATT_HD))
            outs.setdefault("ik_p", []).append(ik.reshape(nbp, seq, IDX_DIM))
            pr = proj_s[:nbs]
            k_new = pr[:, qw:qw + kvw]
            v_new = pr[:, qw + kvw:qw + 2 * kvw]
            ik_new = ik_s[:nbs]
            scores = _dsa_s_scores(page_table,
                                   pr[:, qw + 2 * kvw:].reshape(nbs, IDX_HEADS, IDX_DIM),
                                   proj2_s[:nbs, IDX_DIM:IDX_DIM + IDX_HEADS].reshape(nbs, IDX_HEADS, 1),
                                   ik_new.reshape(nbs, 1, IDX_DIM), cache_idx_k_t, j)
            idx, meta = _dsa_s_select(scores, page_table)
            o_s = _dsa_s_attend(idx, meta, pr[:, :qw].reshape(nbs, ATT_HEADS, ATT_HD),
                                k_new.reshape(nbs, 1, kvw), v_new.reshape(nbs, 1, kvw), cache_k, cache_v, j)
            o_s = pad_rows(o_s.reshape(nbs, qw)).astype(BF16)
            outs.setdefault("k_s", []).append(k_new.reshape(nbs, 1, ATT_KV, ATT_HD))
            outs.setdefault("v_s", []).append(v_new.reshape(nbs, 1, ATT_KV, ATT_HD))
            outs.setdefault("ik_s", []).append(ik_new.reshape(nbs, 1, IDX_DIM))
        elif i % N_MIXERS == 1:
            w_o = _cast_w(hg_w_o, (j,))
            proj, proj_s = _proj(xb_p, xb_s, hg_w_in, (j,), hg_w_in.shape[-1], tm, COL_TILE_PROJ, name="hg_in")
            s0 = jnp.zeros((nbp, HG_HEADS, HG_DK, HG_DV), F32)
            o_p, s_fin = _hgrn(proj.reshape(nbp, seq, -1), hg_lb_logits, hg_norm_g[j], s0, i, seq, chunk)
            o_p = o_p.reshape(mp, -1)
            outs.setdefault("hg_p", []).append(s_fin)
            pr = _pad_time(proj_s[:nbs].reshape(nbs, 1, -1), chunk)
            o_s, s_fin = _hgrn(pr, hg_lb_logits, hg_norm_g[j], state_hgrn[j], i, 1, chunk)
            o_s = pad_rows(o_s[:, 0, :])
            outs.setdefault("hg_s", []).append(s_fin)
        else:
            w_small = _pad_rows_to(ssm_w_in_t[j, SSM_MAIN:, :], LANES)
            w_o = _cast_w(ssm_w_o, (j,))
            dt_bias = _pad_cols(ssm_dt_bias[j].reshape(1, SSM_HEADS), LANES)
            a_log = _pad_cols(ssm_a_log[j].reshape(1, SSM_HEADS), LANES)
            d_x = jnp.repeat(ssm_d[j], SSM_P).reshape(1, SSM_INNER)
            zx_p, zx_s = _proj(xb_p, xb_s, ssm_w_in_t, (j,), SSM_MAIN, tm, COL_TILE_PROJ, w_is_nk=True, name="ssm_in")
            dtr_p, dtr_s = _proj(xb_p, xb_s, w_small, (), LANES, tm, LANES, w_is_nk=True, name="ssm_in_dt")
            mix = {}
            for name, zx, dtr in (("p", zx_p, dtr_p), ("s", zx_s, dtr_s)):
                if name == "p":
                    nb_, t_ = nbp, seq
                    zx3 = zx.reshape(nbp, seq, -1)
                    dt3 = dtr.reshape(nbp, seq, LANES)
                    cs = jnp.zeros((nbp, SUBLANES, SSM_CH), F32)
                    s0 = jnp.zeros((nbp, SSM_INNER, SSM_N), F32)
                else:
                    nb_, t_ = nbs, 1
                    zx3 = _pad_time(zx[:nbs].reshape(nbs, 1, -1), chunk)
                    dt3 = _pad_time(dtr[:nbs].reshape(nbs, 1, LANES), chunk)
                    cs = jnp.pad(state_conv[j], ((0, 0), (SUBLANES - (SSM_CONV - 1), 0), (0, 0)))
                    s0 = state_ssm[j].reshape(nbs, SSM_INNER, SSM_N)
                y, s_fin, new_conv = _ssd(zx3, dt3, cs, ssm_conv_w[j], ssm_conv_b[j], expand, dt_bias, a_log, d_x,
                                          ssm_norm_g[j], s0, t_, chunk)
                s_fin = s_fin.reshape(nb_, SSM_HEADS, SSM_P, SSM_N)
                new_conv = new_conv[:, :SSM_CONV - 1, :]
                if name == "p":
                    mix[name] = y.reshape(mp, SSM_INNER)
                    outs.setdefault("ssm_p", []).append(s_fin)
                    outs.setdefault("conv_p", []).append(new_conv)
                else:
                    mix[name] = pad_rows(y[:, 0, :])
                    outs.setdefault("ssm_s", []).append(s_fin)
                    outs.setdefault("conv_s", []).append(new_conv)
            o_p, o_s = mix["p"], mix["s"]

        (x_p, xb_p), (x_s, xb_s) = _mm_ln(o_p, o_s, w_o, x_p, x_s, ln_g[i, 1], ln_b[i, 1], 1.0, tm_ln, "mixer_out_ln")
        (x_p, xb_p), (x_s, xb_s) = ffn_ln(i, 1, 2, x_p, xb_p, x_s, xb_s)
        (x_p, xb_p), (x_s, xb_s) = _mm_ple(x_p, xb_p, pl_p[i].astype(BF16), x_s, xb_s, pl_s[i].astype(BF16),
                                           ple_w_gate, ple_w_proj, (i,), tm_ln, COL_TILE_PLE)

    y_prompt = x_p.reshape(nbp, seq, d)
    y_sample = x_s[:nbs].reshape(nbs, 1, d)
    stack = lambda key: jnp.stack(outs[key])
    return (y_prompt, y_sample, stack("k_p"), stack("v_p"), stack("ik_p"), stack("k_s"), stack("v_s"), stack("ik_s"),
            stack("hg_p"), stack("hg_s"), stack("ssm_p"), stack("ssm_s"), stack("conv_p"), stack("conv_s"))
```

```python
import functools
import math

import jax
import jax.numpy as jnp
import numpy as np
from jax import lax
from jax.experimental import pallas as pl
from jax.experimental.pallas import tpu as pltpu

F32 = jnp.float32
BF16 = jnp.bfloat16

D_MODEL = 2048
DEPTH = 4
N_MIXERS = 3
D_FF = 2 * D_MODEL
PLE_DIM = 256
ALPHA = (2 * DEPTH) ** 0.25
LN_EPS = 1e-5
RMS_EPS = 1e-6
PAGE = 128

ATT_HD = 128
ATT_HEADS = 16
ATT_KV = 4
ATT_GROUP = ATT_HEADS // ATT_KV
IDX_HEADS = 16
IDX_DIM = 64
IDX_W_SCALE = (IDX_HEADS ** -0.5) * (IDX_DIM ** -0.5)
TOPK_MAX = 256
ATT_SCALE = ATT_HD ** -0.5
LOG2E = math.log2(math.e)
ATT_MAIN = ATT_HEADS * ATT_HD + 2 * ATT_KV * ATT_HD + IDX_HEADS * IDX_DIM

HG_HEADS = 16
HG_DK = 128
HG_DV = 128

SSM_INNER = 2 * D_MODEL
SSM_P = 64
SSM_HEADS = SSM_INNER // SSM_P
SSM_GROUPS = 8
SSM_HPG = SSM_HEADS // SSM_GROUPS
SSM_N = 128
SSM_CONV = 4
SSM_CH = SSM_INNER + 2 * SSM_GROUPS * SSM_N
SSM_MAIN = SSM_INNER + SSM_CH

LANES = 128
SUBLANES = 8
VMEM_CAPACITY_BYTES = 64 * 1024 * 1024
VMEM_LIMIT_BYTES = VMEM_CAPACITY_BYTES * 7 // 8

ROW_TILE = 1024
ROW_TILE_LN = 512
COL_TILE_PROJ = 1024
COL_TILE_FFN = 512
COL_TILE_PLE = 1024
SEQ_CHUNK = 128
DSA_Q_ROWS = 256
DSA_KEY_CHUNK = 512
SAMPLE_ROWS = 16

INT_MIN = np.int32(-2 ** 31)
NEG_BIG = -1e30


def _cp(*sem):
    return pltpu.CompilerParams(dimension_semantics=sem, vmem_limit_bytes=VMEM_LIMIT_BYTES)


def _nt(a, b):
    return lax.dot_general(a, b, (((1,), (1,)), ((), ())), preferred_element_type=F32)


def _tn(a, b):
    return lax.dot_general(a, b, (((0,), (0,)), ((), ())), preferred_element_type=F32)


def _silu(x):
    return x * jax.nn.sigmoid(x)


def _prefix_sums(tri, v):
    out = None
    rest = v
    for _ in range(3):
        part = rest.astype(BF16)
        rest = rest - part.astype(F32)
        d = jnp.dot(tri, part, preferred_element_type=F32)
        out = d if out is None else out + d
    return out


def _wspec(k, tn, idx, col_block):
    lead = (None,) * len(idx)
    return pl.BlockSpec(lead + (k, tn), lambda n, i: idx + (0, col_block(n)))


def _proj_body(x_ref, xs_ref, w_ref, *rest, with_bf16, w_is_nk):
    wb_ref = rest[-1]
    outs, outs_s = (rest[0:2], rest[2:4]) if with_bf16 else (rest[0:1], rest[1:2])
    mm = _nt if w_is_nk else functools.partial(jnp.dot, preferred_element_type=F32)

    def emit(refs, acc):
        refs[0][...] = acc
        if with_bf16:
            refs[1][...] = acc.astype(BF16)

    @pl.when(pl.program_id(1) == 0)
    def _():
        wb_ref[...] = w_ref[...].astype(BF16)
        emit(outs_s, mm(xs_ref[...], wb_ref[...]))

    emit(outs, mm(x_ref[...], wb_ref[...]))


def _proj(xb, xsb, w, idx, n_cols, tm, tn, with_bf16=False, w_is_nk=False, name="proj"):
    m, k = xb.shape
    ms = xsb.shape[0]
    dts = (F32, BF16) if with_bf16 else (F32,)
    if w_is_nk:
        w_spec = pl.BlockSpec((None,) * len(idx) + (tn, k), lambda n, i: idx + (n, 0))
        w_tile = (tn, k)
    else:
        w_spec = _wspec(k, tn, idx, lambda n: n)
        w_tile = (k, tn)
    res = pl.pallas_call(
        functools.partial(_proj_body, with_bf16=with_bf16, w_is_nk=w_is_nk),
        grid=(n_cols // tn, m // tm),
        in_specs=[pl.BlockSpec((tm, k), lambda n, i: (i, 0)),
                  pl.BlockSpec((ms, k), lambda n, i: (0, 0)),
                  w_spec],
        out_specs=[pl.BlockSpec((tm, tn), lambda n, i: (i, n)) for _ in dts]
                  + [pl.BlockSpec((ms, tn), lambda n, i: (0, n)) for _ in dts],
        out_shape=[jax.ShapeDtypeStruct((m, n_cols), dt) for dt in dts]
                  + [jax.ShapeDtypeStruct((ms, n_cols), dt) for dt in dts],
        scratch_shapes=[pltpu.VMEM(w_tile, BF16)],
        compiler_params=_cp("parallel", "arbitrary"),
        name=name,
    )(xb, xsb, w)
    nd = len(dts)
    return (res[:nd], res[nd:]) if with_bf16 else (res[0], res[1])


def _swiglu_body(x_ref, xs_ref, wg_ref, wu_ref, wd_ref, o_ref, os_ref, wdb_ref, wgub_ref):
    tn = o_ref.shape[1]

    def swiglu(x):
        gu = jnp.dot(x, wgub_ref[...], preferred_element_type=F32)
        return (_silu(gu[:, :tn]) * gu[:, tn:]).astype(BF16)

    @pl.when(pl.program_id(1) == 0)
    def _():
        wgub_ref[:, :tn] = wg_ref[...].astype(BF16)
        wgub_ref[:, tn:] = wu_ref[...].astype(BF16)
        wdb_ref[...] = wd_ref[...].astype(BF16)
        os_ref[...] = swiglu(xs_ref[...])

    o_ref[...] = swiglu(x_ref[...])


def _mm_swiglu(xb, xsb, wgu, wd, idx, tm, tn):
    m, k = xb.shape
    ms = xsb.shape[0]
    f = wgu.shape[-1] // 2
    nj = f // tn
    n_out = wd.shape[-1]
    lead = (None,) * len(idx)
    return pl.pallas_call(
        _swiglu_body,
        grid=(nj, m // tm),
        in_specs=[pl.BlockSpec((tm, k), lambda n, i: (i, 0)),
                  pl.BlockSpec((ms, k), lambda n, i: (0, 0)),
                  _wspec(k, tn, idx, lambda n: n),
                  _wspec(k, tn, idx, lambda n: n + nj),
                  pl.BlockSpec(lead + (f // nj, n_out), lambda n, i: idx + (n, 0))],
        out_specs=[pl.BlockSpec((tm, tn), lambda n, i: (i, n)),
                   pl.BlockSpec((ms, tn), lambda n, i: (0, n)),
                   pl.BlockSpec((f // nj, n_out), lambda n, i: (n, 0))],
        out_shape=[jax.ShapeDtypeStruct((m, f), BF16), jax.ShapeDtypeStruct((ms, f), BF16),
                   jax.ShapeDtypeStruct((f, n_out), BF16)],
        scratch_shapes=[pltpu.VMEM((k, 2 * tn), BF16)],
        compiler_params=_cp("parallel", "arbitrary"),
        name="ffn_up",
    )(xb, xsb, wgu, wgu, wd)


def _cast_body(w_ref, o_ref):
    o_ref[...] = w_ref[...].astype(BF16)


def _cast_w(w, idx, tk=512):
    k, n = w.shape[-2:]
    lead = (None,) * len(idx)
    return pl.pallas_call(
        _cast_body,
        grid=(k // tk,),
        in_specs=[pl.BlockSpec(lead + (tk, n), lambda i: idx + (i, 0))],
        out_specs=pl.BlockSpec((tk, n), lambda i: (i, 0)),
        out_shape=jax.ShapeDtypeStruct((k, n), BF16),
        compiler_params=_cp("parallel"),
        name="cast_w",
    )(w)


LN_SPLIT = 2


def _mm_ln_body(a_ref, as_ref, w_ref, r_ref, rs_ref, g_ref, b_ref, o_ref, ob_ref, os_ref, osb_ref, *, scale, tm):
    g = g_ref[...]
    b = b_ref[...]

    def ln_rows(a, r, o, ob, sl):
        y = ALPHA * r[sl, :] + scale * jnp.dot(a[sl, :], w_ref[...], preferred_element_type=F32)
        mu = jnp.mean(y, axis=-1, keepdims=True)
        yc = y - mu
        var = jnp.mean(yc * yc, axis=-1, keepdims=True)
        out = yc * lax.rsqrt(var + LN_EPS) * g + b
        o[sl, :] = out
        ob[sl, :] = out.astype(BF16)

    @pl.when(pl.program_id(0) == 0)
    def _():
        ln_rows(as_ref, rs_ref, os_ref, osb_ref, pl.ds(0, as_ref.shape[0]))

    rows = tm // LN_SPLIT
    for r in range(0, tm, rows):
        ln_rows(a_ref, r_ref, o_ref, ob_ref, pl.ds(r, rows))


def _mm_ln(ab, asb, wb, res, res_s, g, b, scale, tm, name):
    m, kdim = ab.shape
    ms = asb.shape[0]
    n = wb.shape[1]
    res = pl.pallas_call(
        functools.partial(_mm_ln_body, scale=scale, tm=tm),
        grid=(m // tm,),
        in_specs=[pl.BlockSpec((tm, kdim), lambda i: (i, 0)),
                  pl.BlockSpec((ms, kdim), lambda i: (0, 0)),
                  pl.BlockSpec((kdim, n), lambda i: (0, 0), pipeline_mode=pl.Buffered(1)),
                  pl.BlockSpec((tm, n), lambda i: (i, 0)),
                  pl.BlockSpec((ms, n), lambda i: (0, 0)),
                  pl.BlockSpec((1, n), lambda i: (0, 0)),
                  pl.BlockSpec((1, n), lambda i: (0, 0))],
        out_specs=[pl.BlockSpec((tm, n), lambda i: (i, 0)),
                   pl.BlockSpec((tm, n), lambda i: (i, 0)),
                   pl.BlockSpec((ms, n), lambda i: (0, 0)),
                   pl.BlockSpec((ms, n), lambda i: (0, 0))],
        out_shape=[jax.ShapeDtypeStruct((m, n), F32), jax.ShapeDtypeStruct((m, n), BF16),
                   jax.ShapeDtypeStruct((ms, n), F32), jax.ShapeDtypeStruct((ms, n), BF16)],
        compiler_params=_cp("arbitrary"),
        name=name,
    )(ab, asb, wb, res, res_s, g.reshape(1, n), b.reshape(1, n))
    return res[:2], res[2:]


def _ple_body(xb_ref, p_ref, x_ref, xsb_ref, ps_ref, xs_ref, wg_ref, wp_ref, o_ref, ob_ref, os_ref, osb_ref,
              wgb_ref, wpb_ref):
    def ple(xb, p, x, o, ob):
        gate = jax.nn.sigmoid(jnp.dot(xb[...], wgb_ref[...], preferred_element_type=F32))
        proj = jnp.dot(p[...], wpb_ref[...], preferred_element_type=F32)
        out = x[...] + gate * proj
        o[...] = out
        ob[...] = out.astype(BF16)

    @pl.when(pl.program_id(1) == 0)
    def _():
        wgb_ref[...] = wg_ref[...].astype(BF16)
        wpb_ref[...] = wp_ref[...].astype(BF16)
        ple(xsb_ref, ps_ref, xs_ref, os_ref, osb_ref)

    ple(xb_ref, p_ref, x_ref, o_ref, ob_ref)


def _mm_ple(x32, xb, pb, xs32, xsb, psb, wg, wp, idx, tm, tn):
    m, d = xb.shape
    ms = xsb.shape[0]
    pd = pb.shape[1]
    res = pl.pallas_call(
        _ple_body,
        grid=(d // tn, m // tm),
        in_specs=[pl.BlockSpec((tm, d), lambda n, i: (i, 0)),
                  pl.BlockSpec((tm, pd), lambda n, i: (i, 0)),
                  pl.BlockSpec((tm, tn), lambda n, i: (i, n)),
                  pl.BlockSpec((ms, d), lambda n, i: (0, 0)),
                  pl.BlockSpec((ms, pd), lambda n, i: (0, 0)),
                  pl.BlockSpec((ms, tn), lambda n, i: (0, n)),
                  _wspec(d, tn, idx, lambda n: n),
                  _wspec(pd, tn, idx, lambda n: n)],
        out_specs=[pl.BlockSpec((tm, tn), lambda n, i: (i, n)),
                   pl.BlockSpec((tm, tn), lambda n, i: (i, n)),
                   pl.BlockSpec((ms, tn), lambda n, i: (0, n)),
                   pl.BlockSpec((ms, tn), lambda n, i: (0, n))],
        out_shape=[jax.ShapeDtypeStruct((m, d), F32), jax.ShapeDtypeStruct((m, d), BF16),
                   jax.ShapeDtypeStruct((ms, d), F32), jax.ShapeDtypeStruct((ms, d), BF16)],
        scratch_shapes=[pltpu.VMEM((d, tn), BF16), pltpu.VMEM((pd, tn), BF16)],
        compiler_params=_cp("parallel", "arbitrary"),
        name="ple",
    )(xb, pb, x32, xsb, psb, xs32, wg, wp)
    return res[:2], res[2:]


def _ik_norm_body(p_ref, g_ref, o_ref, ob_ref):
    x = p_ref[...][:, :IDX_DIM]
    mu = jnp.mean(x, axis=-1, keepdims=True)
    xc = x - mu
    out = xc * lax.rsqrt(jnp.mean(xc * xc, axis=-1, keepdims=True) + LN_EPS) * g_ref[...]
    o_ref[...] = out
    ob_ref[...] = out.astype(BF16)


def _ik_norm(proj2, ik_g, tm):
    m = proj2.shape[0]
    return pl.pallas_call(
        _ik_norm_body,
        grid=(m // tm,),
        in_specs=[pl.BlockSpec((tm, LANES), lambda i: (i, 0)),
                  pl.BlockSpec((1, IDX_DIM), lambda i: (0, 0))],
        out_specs=[pl.BlockSpec((tm, IDX_DIM), lambda i: (i, 0)),
                   pl.BlockSpec((tm, IDX_DIM), lambda i: (i, 0))],
        out_shape=[jax.ShapeDtypeStruct((m, IDX_DIM), F32), jax.ShapeDtypeStruct((m, IDX_DIM), BF16)],
        compiler_params=_cp("parallel"),
        name="idx_k_norm",
    )(proj2, ik_g.reshape(1, IDX_DIM))


def _order_key(x):
    bits = pltpu.bitcast(x, jnp.int32)
    return jnp.where(bits < 0, bits ^ jnp.int32(0x7FFFFFFF), bits)


_RADIX_BITS = [INT_MIN] + [np.int32(1 << s) for s in range(30, -1, -1)]


RADIX_ROWS = 128
KNORM_ROWS = 512
SUM_FLOOR = 2.0 ** -100


def _dsa_prompt_body(q_ref, iq_ref, iw_ref, k_ref, v_ref, ik_ref, o_ref, keys_ref, knorm_ref, *, qb, kc, topk):
    i = pl.program_id(1)
    nck = ((i + 1) * qb + kc - 1) // kc
    row_pos = i * qb + lax.broadcasted_iota(jnp.int32, (qb, 1), 0)
    iq = iq_ref[...]
    iw = iw_ref[...][:, IDX_DIM:IDX_DIM + IDX_HEADS] * IDX_W_SCALE
    iq_h = [iq[:, h * IDX_DIM:(h + 1) * IDX_DIM] for h in range(IDX_HEADS)]
    iw_h = [iw[:, h:h + 1] for h in range(IDX_HEADS)]
    col0 = lax.broadcasted_iota(jnp.int32, (1, kc), 1)

    def score_chunk(c, carry):
        off = pl.multiple_of(c * kc, kc)
        ikc = ik_ref[pl.ds(off, kc), :]
        sc = jnp.zeros((qb, kc), F32)
        for h in range(IDX_HEADS):
            sc = sc + iw_h[h] * jnp.maximum(_nt(iq_h[h], ikc), 0.0)
        key = jnp.where(col0 + off <= row_pos, _order_key(sc), INT_MIN)
        keys_ref[:, pl.ds(off, kc)] = key
        return carry

    lax.fori_loop(0, nck, score_chunk, 0)

    nblk = qb // RADIX_ROWS
    los = [jnp.full((RADIX_ROWS, 1), INT_MIN, jnp.int32) for _ in range(nblk)]
    n_ge = [jnp.zeros((RADIX_ROWS, 1), F32) for _ in range(nblk)]
    for bit in _RADIX_BITS:
        cands = [lo + bit for lo in los]
        accs = []
        for blk in range(nblk):
            cand_b = jnp.broadcast_to(cands[blk], (RADIX_ROWS, LANES))

            def count_chunk(c, acc, cand_b=cand_b, r0=blk * RADIX_ROWS):
                off = pl.multiple_of(c * kc, kc)
                for t in range(kc // LANES):
                    tile = keys_ref[r0:r0 + RADIX_ROWS, pl.ds(off + t * LANES, LANES)]
                    acc = acc + jnp.where(tile >= cand_b, 1.0, 0.0)
                return acc

            accs.append(lax.fori_loop(0, nck, count_chunk, jnp.zeros((RADIX_ROWS, LANES), F32)))
        cnts = [jnp.sum(acc, axis=-1, keepdims=True) for acc in accs]
        n_ge = [jnp.where(cnts[blk] >= topk, cnts[blk], n_ge[blk]) for blk in range(nblk)]
        los = [jnp.where(cnts[blk] >= topk, cands[blk], los[blk]) for blk in range(nblk)]
    thr = jnp.maximum(jnp.concatenate(los, axis=0), INT_MIN + 1)

    @pl.when(jnp.max(jnp.concatenate(n_ge, axis=0)) > topk)
    def _():
        lane = lax.broadcasted_iota(jnp.int32, (1, LANES), 1)

        def count_rows(pred):
            def body(c, acc):
                off = pl.multiple_of(c * kc, kc)
                for t in range(kc // LANES):
                    tile = keys_ref[:, pl.ds(off + t * LANES, LANES)]
                    acc = acc + jnp.where(pred(tile, off + t * LANES + lane), 1.0, 0.0)
                return acc
            return jnp.sum(lax.fori_loop(0, nck, body, jnp.zeros((qb, LANES), F32)), axis=-1, keepdims=True)

        keep = topk - count_rows(lambda tile, pos: tile > thr)
        last_short = jnp.full((qb, 1), -1, jnp.int32)
        step = keys_ref.shape[1] // 2
        while step >= 1:
            cand = last_short + step
            short = count_rows(lambda tile, pos, cand=cand: (tile == thr) & (pos <= cand)) < keep
            last_short = jnp.where(short, cand, last_short)
            step //= 2
        cut = last_short + 1

        def demote(c, carry):
            off = pl.multiple_of(c * kc, kc)
            for t in range(kc // LANES):
                sl = pl.ds(off + t * LANES, LANES)
                tile = keys_ref[:, sl]
                keys_ref[:, sl] = jnp.where((tile == thr) & (off + t * LANES + lane > cut), thr - 1, tile)
            return carry

        lax.fori_loop(0, nck, demote, 0)

    @pl.when(i == 0)
    def _():
        for g in range(ATT_KV):
            best = jnp.zeros((1, 1), F32)
            for r0 in range(0, k_ref.shape[0], KNORM_ROWS):
                kk = k_ref[r0:r0 + KNORM_ROWS, g * ATT_HD:(g + 1) * ATT_HD].astype(F32)
                best = jnp.maximum(best, jnp.max(jnp.sum(kk * kk, axis=-1, keepdims=True), axis=0, keepdims=True))
            knorm_ref[g:g + 1, :] = jnp.broadcast_to(jnp.sqrt(best), (1, LANES))

    def store(g, out):
        for j in range(ATT_GROUP):
            h = g * ATT_GROUP + j
            o_ref[:, h * ATT_HD:(h + 1) * ATT_HD] = out[j].astype(o_ref.dtype)

    for g in range(ATT_KV):
        qg = jnp.concatenate(
            [q_ref[:, (g * ATT_GROUP + j) * ATT_HD:(g * ATT_GROUP + j + 1) * ATT_HD] for j in range(ATT_GROUP)], axis=0)
        qg = (qg * (ATT_SCALE * LOG2E)).astype(BF16)
        q32 = qg.astype(F32)
        shift = (jnp.sqrt(jnp.sum(q32 * q32, axis=-1, keepdims=True)) * knorm_ref[g:g + 1, 0:1]).reshape(
            ATT_GROUP, qb, 1)

        def chunk_operands(c, g=g):
            off = pl.multiple_of(c * kc, kc)
            kch = k_ref[pl.ds(off, kc), g * ATT_HD:(g + 1) * ATT_HD]
            vch = v_ref[pl.ds(off, kc), g * ATT_HD:(g + 1) * ATT_HD]
            sel = (keys_ref[:, pl.ds(off, kc)] >= thr)[None]
            return kch, vch, sel

        def fixed_chunk(c, carry, qg=qg, shift=shift):
            l, acc = carry
            kch, vch, sel = chunk_operands(c)
            p = jnp.exp2(jnp.where(sel, _nt(qg, kch).reshape(ATT_GROUP, qb, kc) - shift, -jnp.inf))
            l = l + jnp.sum(p, axis=-1, keepdims=True)
            pv = jnp.dot(p.reshape(ATT_GROUP * qb, kc).astype(BF16), vch, preferred_element_type=F32)
            return l, acc + pv.reshape(ATT_GROUP, qb, ATT_HD)

        l, acc = lax.fori_loop(0, nck, fixed_chunk, (jnp.zeros((ATT_GROUP, qb, 1), F32),
                                                     jnp.zeros((ATT_GROUP, qb, ATT_HD), F32)))
        healthy = jnp.min(l) > SUM_FLOOR

        @pl.when(healthy)
        def _(g=g, l=l, acc=acc):
            store(g, acc / l)

        @pl.when(jnp.logical_not(healthy))
        def _(g=g, qg=qg):
            def running_chunk(c, carry):
                m, l, acc = carry
                kch, vch, sel = chunk_operands(c)
                s = jnp.where(sel, _nt(qg, kch).reshape(ATT_GROUP, qb, kc), -jnp.inf)
                m_new = jnp.maximum(m, jnp.max(s, axis=-1, keepdims=True))
                p = jnp.exp2(s - m_new)
                a = jnp.exp2(m - m_new)
                l = a * l + jnp.sum(p, axis=-1, keepdims=True)
                pv = jnp.dot(p.reshape(ATT_GROUP * qb, kc).astype(BF16), vch, preferred_element_type=F32)
                return m_new, l, a * acc + pv.reshape(ATT_GROUP, qb, ATT_HD)

            init = (jnp.full((ATT_GROUP, qb, 1), NEG_BIG, F32), jnp.zeros((ATT_GROUP, qb, 1), F32),
                    jnp.zeros((ATT_GROUP, qb, ATT_HD), F32))
            _, l, acc = lax.fori_loop(0, nck, running_chunk, init)
            store(g, acc / l)


def _dsa_prompt(proj, projb, proj2, ikb, nb, s, qb=DSA_Q_ROWS, kc=DSA_KEY_CHUNK):
    nq = s // qb
    topk = min(TOPK_MAX, s // 4)
    qw = ATT_HEADS * ATT_HD
    kvw = ATT_KV * ATT_HD
    iqw = IDX_HEADS * IDX_DIM
    return pl.pallas_call(
        functools.partial(_dsa_prompt_body, qb=qb, kc=kc, topk=topk),
        grid=(nb, nq),
        in_specs=[pl.BlockSpec((qb, qw), lambda b, i: (b * nq + i, 0)),
                  pl.BlockSpec((qb, iqw), lambda b, i: (b * nq + i, (qw + 2 * kvw) // iqw)),
                  pl.BlockSpec((qb, LANES), lambda b, i: (b * nq + i, 0)),
                  pl.BlockSpec((s, kvw), lambda b, i: (b, qw // kvw)),
                  pl.BlockSpec((s, kvw), lambda b, i: (b, qw // kvw + 1)),
                  pl.BlockSpec((s, IDX_DIM), lambda b, i: (b, 0))],
        out_specs=pl.BlockSpec((qb, qw), lambda b, i: (b * nq + i, 0)),
        out_shape=jax.ShapeDtypeStruct((nb * s, qw), BF16),
        scratch_shapes=[pltpu.VMEM((qb, s), jnp.int32), pltpu.VMEM((SUBLANES, LANES), F32)],
        compiler_params=_cp("parallel", "arbitrary"),
        name="dsa_prompt",
    )(proj, projb, proj2, projb, projb, ikb)


SCORE_PAGES = 32


def _dsa_s_scores_body(pt_ref, iq_ref, iw_ref, ikn_ref, *rest, n_pages):
    page_refs, o_ref = rest[:SCORE_PAGES], rest[SCORE_PAGES]
    p = pl.program_id(1)
    iq = iq_ref[...].astype(BF16)
    iw = iw_ref[...] * IDX_W_SCALE

    @pl.when(p == 0)
    def _():
        own = jnp.broadcast_to(ikn_ref[...], (SUBLANES, IDX_DIM)).astype(BF16)
        d = jnp.maximum(_nt(iq, own), 0.0)
        sc = jnp.sum(iw * d, axis=0, keepdims=True)
        o_ref[n_pages:n_pages + SUBLANES, :] = jnp.broadcast_to(sc[:, 0:1], (SUBLANES, LANES))

    for r in range(SCORE_PAGES):
        page_t = page_refs[r][...].astype(BF16)
        d = jnp.maximum(jnp.dot(iq, page_t, preferred_element_type=F32), 0.0)
        o_ref[pl.ds(p * SCORE_PAGES + r, 1), :] = jnp.sum(iw * d, axis=0, keepdims=True)


def _dsa_s_scores(page_table, iq, iw, ik_new, cache_ik_t, layer):
    nb, n_pages = page_table.shape
    page_spec = lambda r: pl.BlockSpec((None, None, IDX_DIM, PAGE),
                                       lambda b, p, pt: (layer, pt[b, p * SCORE_PAGES + r], 0, 0))
    return pl.pallas_call(
        functools.partial(_dsa_s_scores_body, n_pages=n_pages),
        grid_spec=pltpu.PrefetchScalarGridSpec(
            num_scalar_prefetch=1,
            grid=(nb, n_pages // SCORE_PAGES),
            in_specs=[pl.BlockSpec((None, IDX_HEADS, IDX_DIM), lambda b, p, pt: (b, 0, 0)),
                      pl.BlockSpec((None, IDX_HEADS, 1), lambda b, p, pt: (b, 0, 0)),
                      pl.BlockSpec((None, 1, IDX_DIM), lambda b, p, pt: (b, 0, 0))]
                     + [page_spec(r) for r in range(SCORE_PAGES)],
            out_specs=pl.BlockSpec((None, n_pages + SUBLANES, LANES), lambda b, p, pt: (b, 0, 0)),
        ),
        out_shape=jax.ShapeDtypeStruct((nb, n_pages + SUBLANES, LANES), F32),
        compiler_params=_cp("parallel", "arbitrary"),
        name="dsa_sample_scores",
    )(page_table, iq, iw, ik_new, *([cache_ik_t] * SCORE_PAGES))


def _dsa_s_select_body(sc_ref, ptc_ref, idx_ref, meta_ref, rank_ref, phys_ref, *, n_pages, topk):
    shape = (n_pages + SUBLANES, LANES)
    rows = lax.broadcasted_iota(jnp.int32, shape, 0)
    cols = lax.broadcasted_iota(jnp.int32, shape, 1)
    live = (rows < n_pages) | ((rows == n_pages) & (cols == 0))
    keys = jnp.where(live, _order_key(sc_ref[...]), INT_MIN)
    lo = jnp.full((1, 1), INT_MIN, jnp.int32)
    for bit in _RADIX_BITS:
        cand = lo + bit
        cnt = jnp.sum(jnp.sum(jnp.where(keys >= cand, 1.0, 0.0), axis=0, keepdims=True), axis=1, keepdims=True)
        lo = jnp.where(cnt >= topk, cand, lo)
    thr = jnp.maximum(lo, INT_MIN + 1)
    above = keys > thr
    tied = keys == thr
    n_above = jnp.sum(jnp.sum(jnp.where(above, 1.0, 0.0), axis=0, keepdims=True), axis=1, keepdims=True)
    keep_tied = topk - n_above

    r_i = lax.broadcasted_iota(jnp.int32, (LANES, LANES), 0)
    c_i = lax.broadcasted_iota(jnp.int32, (LANES, LANES), 1)
    ones_le = jnp.where(r_i <= c_i, 1.0, 0.0).astype(BF16)
    ones_gt = jnp.where(r_i > c_i, 1.0, 0.0).astype(BF16)

    def prefix(mask):
        inside = jnp.dot(jnp.where(mask, 1.0, 0.0).astype(BF16), ones_le, preferred_element_type=F32)
        tot = jnp.broadcast_to(inside[:, LANES - 1:LANES], (n_pages, LANES)).astype(BF16)
        return inside, jnp.dot(ones_gt, tot, preferred_element_type=F32)

    tied_in, tied_before = prefix(tied[:n_pages, :])
    n_tied_cache = tied_before[n_pages - 1:n_pages, 0:1] + tied_in[n_pages - 1:n_pages, LANES - 1:LANES]
    sel_c = above[:n_pages, :] | (tied[:n_pages, :] & (tied_in + tied_before <= keep_tied))
    own_key = keys[n_pages:n_pages + 1, 0:1]
    own = jnp.where((own_key > thr) | ((own_key == thr) & (n_tied_cache < keep_tied)), 1, 0)
    within, before = prefix(sel_c)
    rank_ref[...] = jnp.where(sel_c, (within + before).astype(jnp.int32) - 1, -1)
    phys_ref[...] = (ptc_ref[...] * PAGE + c_i[:n_pages, :]).astype(F32)
    n_sel = (before[n_pages - 1:n_pages, 0:1] + within[n_pages - 1:n_pages, LANES - 1:LANES]).astype(jnp.int32)

    slot = lax.broadcasted_iota(jnp.int32, (topk, 1), 0)

    def place(r, acc):
        return acc + jnp.where(rank_ref[pl.ds(r, 1), :] == slot, phys_ref[pl.ds(r, 1), :], 0.0)

    acc = lax.fori_loop(0, n_pages, place, jnp.zeros((topk, LANES), F32), unroll=8)
    idx_ref[...] = jnp.sum(acc, axis=-1, keepdims=True).astype(jnp.int32)
    mrow = lax.broadcasted_iota(jnp.int32, (SUBLANES, LANES), 0)
    meta_ref[...] = jnp.where(mrow == 0, jnp.minimum(n_sel, topk), own)


def _dsa_s_select(scores, page_table):
    nb, n_pages = page_table.shape
    assert n_pages == LANES
    topk = min(TOPK_MAX, (n_pages * PAGE + 1) // 4)
    idx, meta = pl.pallas_call(
        functools.partial(_dsa_s_select_body, n_pages=n_pages, topk=topk),
        grid=(nb,),
        in_specs=[pl.BlockSpec((None, n_pages + SUBLANES, LANES), lambda b: (b, 0, 0)),
                  pl.BlockSpec((None, n_pages, 1), lambda b: (b, 0, 0))],
        out_specs=[pl.BlockSpec((None, topk, 1), lambda b: (b, 0, 0)),
                   pl.BlockSpec((None, SUBLANES, LANES), lambda b: (b, 0, 0))],
        out_shape=[jax.ShapeDtypeStruct((nb, topk, 1), jnp.int32),
                   jax.ShapeDtypeStruct((nb, SUBLANES, LANES), jnp.int32)],
        scratch_shapes=[pltpu.VMEM((n_pages, LANES), jnp.int32), pltpu.VMEM((n_pages, LANES), F32)],
        compiler_params=_cp("parallel"),
        name="dsa_sample_select",
    )(scores, page_table.reshape(nb, n_pages, 1))
    return idx.reshape(nb, topk), meta[:, 0:2, 0]


def _dsa_s_attend_body(idx_ref, meta_ref, q_ref, kn_ref, vn_ref, ck_ref, cv_ref, o_ref, kbuf, vbuf, sem,
                       *, layer, topk):
    b = pl.program_id(0)
    nb = pl.num_programs(0)
    slot = b % 2

    def row_copies(tok, buf_slot, j):
        row = idx_ref[tok, j]
        page, off = row // PAGE, row % PAGE
        dst = pl.ds(j * ATT_KV, ATT_KV)
        return (pltpu.make_async_copy(ck_ref.at[layer, page, off], kbuf.at[buf_slot, dst, :], sem.at[buf_slot, 0]),
                pltpu.make_async_copy(cv_ref.at[layer, page, off], vbuf.at[buf_slot, dst, :], sem.at[buf_slot, 1]))

    def start_gather(tok, buf_slot):
        def body(j, carry):
            for cp in row_copies(tok, buf_slot, j):
                cp.start()
            return carry
        lax.fori_loop(0, topk, body, 0)

    @pl.when(b == 0)
    def _():
        start_gather(0, 0)

    @pl.when(b + 1 < nb)
    def _():
        start_gather(b + 1, 1 - slot)

    def wait_body(j, carry):
        for cp in row_copies(b, slot, j):
            cp.wait()
        return carry

    lax.fori_loop(0, topk, wait_body, 0)

    n_sel = meta_ref[b, 0]
    own = meta_ref[b, 1] > 0
    qb = q_ref[...].astype(BF16)
    ncol = topk * ATT_KV
    col = lax.broadcasted_iota(jnp.int32, (ATT_HEADS, ncol), 1)
    head = lax.broadcasted_iota(jnp.int32, (ATT_HEADS, ncol), 0)
    valid = (col % ATT_KV == head // ATT_GROUP) & (col // ATT_KV < n_sel)
    s = jnp.where(valid, _nt(qb, kbuf[slot].astype(BF16)) * ATT_SCALE, NEG_BIG)
    kn = kn_ref[...].astype(BF16).astype(F32)
    vn = vn_ref[...].astype(BF16).astype(F32)
    expand = lambda a: jnp.concatenate(
        [jnp.broadcast_to(a[:, g * ATT_HD:(g + 1) * ATT_HD], (ATT_GROUP, ATT_HD)) for g in range(ATT_KV)], axis=0)
    s_own = jnp.where(own, jnp.sum(qb.astype(F32) * expand(kn), axis=-1, keepdims=True) * ATT_SCALE, NEG_BIG)
    m = jnp.maximum(jnp.max(s, axis=-1, keepdims=True), s_own)
    p = jnp.where(valid, jnp.exp(s - m), 0.0)
    p_own = jnp.where(own, jnp.exp(s_own - m), 0.0)
    l = jnp.sum(p, axis=-1, keepdims=True) + p_own
    acc = (jnp.dot(p.astype(BF16), vbuf[slot].astype(BF16), preferred_element_type=F32)
           + p_own.astype(BF16).astype(F32) * expand(vn))
    o_ref[...] = acc / l


def _dsa_s_attend(idx, meta, q, k_new, v_new, cache_k, cache_v, layer):
    nb, topk = idx.shape
    kvw = ATT_KV * ATT_HD
    return pl.pallas_call(
        functools.partial(_dsa_s_attend_body, layer=layer, topk=topk),
        grid_spec=pltpu.PrefetchScalarGridSpec(
            num_scalar_prefetch=2,
            grid=(nb,),
            in_specs=[pl.BlockSpec((None, ATT_HEADS, ATT_HD), lambda b, idx, meta: (b, 0, 0)),
                      pl.BlockSpec((None, 1, kvw), lambda b, idx, meta: (b, 0, 0)),
                      pl.BlockSpec((None, 1, kvw), lambda b, idx, meta: (b, 0, 0)),
                      pl.BlockSpec(memory_space=pl.ANY),
                      pl.BlockSpec(memory_space=pl.ANY)],
            out_specs=pl.BlockSpec((None, ATT_HEADS, ATT_HD), lambda b, idx, meta: (b, 0, 0)),
            scratch_shapes=[pltpu.VMEM((2, topk * ATT_KV, ATT_HD), F32),
                            pltpu.VMEM((2, topk * ATT_KV, ATT_HD), F32),
                            pltpu.SemaphoreType.DMA((2, 2))],
        ),
        out_shape=jax.ShapeDtypeStruct((nb, ATT_HEADS, ATT_HD), F32),
        compiler_params=_cp("arbitrary"),
        name="dsa_sample_attend",
    )(idx, meta, q, k_new, v_new, cache_k, cache_v)


HG_SUB = 16


HG_HB = 4
HG_EXP_LIMIT = 80.0


def _hgrn_body(lbl_ref, q_ref, f_ref, i_ref, g_ref, ng_ref, s0_ref, o_ref, so_ref, st_ref, *, layer, c, t, nc):
    ci = pl.program_id(2)
    mid = c // 2

    @pl.when(ci == 0)
    def _():
        for hh in range(HG_HB):
            st_ref[hh] = s0_ref[hh].T

    logits = lbl_ref[...]
    e = jnp.exp(logits - jnp.max(logits, axis=0, keepdims=True))
    soft = e / jnp.sum(e, axis=0, keepdims=True)
    lb_all = jnp.zeros((1, HG_HB * HG_DK), F32)
    for r in range(1, layer + 1):
        lb_all = lb_all + soft[r:r + 1, :]
    ng = ng_ref[...]

    def gates(hh, sl, row0):
        cs = slice(hh * HG_DK, (hh + 1) * HG_DK)
        lb = lb_all[:, cs]
        fg = lb + (1.0 - lb) * jax.nn.sigmoid(f_ref[sl, cs])
        lf = jnp.log(fg)
        kk = 1.0 - fg
        if t % c:
            n = lf.shape[0]
            valid = (ci * c + row0 + lax.broadcasted_iota(jnp.int32, (n, 1), 0)) < t
            lf = jnp.where(valid, lf, 0.0)
            kk = jnp.where(valid, kk, 0.0)
        return lf, kk

    def finish(hh, sl, o):
        cs = slice(hh * HG_DV, (hh + 1) * HG_DV)
        on = o * lax.rsqrt(jnp.mean(o * o, axis=-1, keepdims=True) + RMS_EPS) * ng
        o_ref[sl, cs] = (on * _silu(g_ref[sl, cs])).astype(o_ref.dtype)

    tri_c = lax.broadcasted_iota(jnp.int32, (c, c), 0) >= lax.broadcasted_iota(jnp.int32, (c, c), 1)
    full = pl.ds(0, c)
    lfs, kks, bs = [], [], []
    safe = None
    for hh in range(HG_HB):
        lf, kk = gates(hh, full, 0)
        b = _prefix_sums(tri_c.astype(BF16), lf)
        bm = b[mid - 1:mid, :]
        ok = jnp.min(jnp.minimum(bm, b[c - 1:c, :] - bm)) > -HG_EXP_LIMIT
        safe = ok if safe is None else jnp.logical_and(safe, ok)
        lfs.append(lf)
        kks.append(kk)
        bs.append(b)

    @pl.when(safe)
    def _():
        for hh in range(HG_HB):
            cs = slice(hh * HG_DK, (hh + 1) * HG_DK)
            b, kk = bs[hh], kks[hh]
            bm = b[mid - 1:mid, :]
            bl = b[c - 1:c, :]
            qq = _silu(q_ref[:, cs])
            vv = i_ref[:, cs].astype(BF16)
            att = jnp.where(tri_c, _nt((qq * jnp.exp(b - bm)).astype(BF16), (kk * jnp.exp(bm - b)).astype(BF16)), 0.0)
            st = st_ref[hh]
            o = (jnp.dot(att.astype(BF16), vv, preferred_element_type=F32)
                 + _nt((qq * jnp.exp(b)).astype(BF16), st.astype(BF16)))
            st_ref[hh] = st * jnp.exp(bl) + _tn(vv, (kk * jnp.exp(bl - b)).astype(BF16))
            finish(hh, full, o)

    @pl.when(jnp.logical_not(safe))
    def _():
        rows = lax.broadcasted_iota(jnp.int32, (HG_SUB, 1), 0)
        tri = (lax.broadcasted_iota(jnp.int32, (HG_SUB, HG_SUB), 0)
               >= lax.broadcasted_iota(jnp.int32, (HG_SUB, HG_SUB), 1)).astype(F32)
        for hh in range(HG_HB):
            cs = slice(hh * HG_DK, (hh + 1) * HG_DK)

            def sub_block(sb, carry, hh=hh, cs=cs):
                row0 = pl.multiple_of(sb * HG_SUB, HG_SUB)
                sl = pl.ds(row0, HG_SUB)
                lf, kk = gates(hh, sl, row0)
                qq = _silu(q_ref[sl, cs])
                vv = i_ref[sl, cs]
                b = _prefix_sums(tri.astype(BF16), lf)
                st = st_ref[hh]
                o = _nt((qq * jnp.exp(b)).astype(BF16), st.astype(BF16))
                for s in range(HG_SUB):
                    dec = jnp.exp(jnp.where(rows >= s, b - b[s:s + 1, :], -jnp.inf))
                    att = jnp.sum(qq * dec * kk[s:s + 1, :], axis=-1, keepdims=True)
                    o = o + att * vv[s:s + 1, :]
                bl = b[HG_SUB - 1:HG_SUB, :]
                st_ref[hh] = st * jnp.exp(bl) + _tn(vv.astype(BF16), (kk * jnp.exp(bl - b)).astype(BF16))
                finish(hh, sl, o)
                return carry

            lax.fori_loop(0, c // HG_SUB, sub_block, 0)

    @pl.when(ci == nc - 1)
    def _():
        for hh in range(HG_HB):
            so_ref[hh] = st_ref[hh].T


def _hgrn(proj, lb_logits, norm_g, s0, layer, t, c):
    nb, tpad, _ = proj.shape
    nc = tpad // c
    nhb = HG_HEADS // HG_HB
    w = HG_HB * HG_DK
    return pl.pallas_call(
        functools.partial(_hgrn_body, layer=layer, c=c, t=t, nc=nc),
        grid=(nb, nhb, nc),
        in_specs=[pl.BlockSpec((DEPTH, w), lambda b, h, ci: (0, h)),
                  pl.BlockSpec((None, c, w), lambda b, h, ci: (b, ci, h)),
                  pl.BlockSpec((None, c, w), lambda b, h, ci: (b, ci, nhb + h)),
                  pl.BlockSpec((None, c, w), lambda b, h, ci: (b, ci, 2 * nhb + h)),
                  pl.BlockSpec((None, c, w), lambda b, h, ci: (b, ci, 3 * nhb + h)),
                  pl.BlockSpec((1, HG_DV), lambda b, h, ci: (0, 0)),
                  pl.BlockSpec((None, HG_HB, HG_DK, HG_DV), lambda b, h, ci: (b, h, 0, 0))],
        out_specs=[pl.BlockSpec((None, c, w), lambda b, h, ci: (b, ci, h)),
                   pl.BlockSpec((None, HG_HB, HG_DK, HG_DV), lambda b, h, ci: (b, h, 0, 0))],
        out_shape=[jax.ShapeDtypeStruct((nb, tpad, HG_HEADS * HG_DV), BF16),
                   jax.ShapeDtypeStruct((nb, HG_HEADS, HG_DK, HG_DV), F32)],
        scratch_shapes=[pltpu.VMEM((HG_HB, HG_DV, HG_DK), F32)],
        compiler_params=_cp("parallel", "parallel", "arbitrary"),
        name="hgrn2",
    )(lb_logits, proj, proj, proj, proj, norm_g.reshape(1, HG_DV), s0)


def _expand_heads(v, e, terms):
    out = None
    rest = v
    for _ in range(terms):
        part = rest.astype(BF16)
        rest = rest - part.astype(F32)
        d = jnp.dot(part, e, preferred_element_type=F32)
        out = d if out is None else out + d
    return out


def _ssd_body(zx_ref, dt_ref, cs_ref, cw_ref, cbias_ref, e_ref, dtb_ref, alog_ref, dx_ref, ng_ref, s0_ref,
              o_ref, so_ref, nc_ref, st_ref, xc_ref, halo_ref, *, c, t, nc):
    ci = pl.program_id(1)
    gw = SSM_HPG * SSM_P

    @pl.when(ci == 0)
    def _():
        for blk in range(SSM_INNER // LANES):
            st_ref[:, blk * LANES:(blk + 1) * LANES] = s0_ref[blk * LANES:(blk + 1) * LANES, :].T
        halo_ref[...] = cs_ref[...]

    full = jnp.concatenate([halo_ref[...], zx_ref[:, SSM_INNER:SSM_INNER + SSM_CH]], axis=0)
    conv = cbias_ref[...]
    for k in range(SSM_CONV):
        lo = SUBLANES - (SSM_CONV - 1) + k
        conv = conv + full[lo:lo + c, :] * cw_ref[k:k + 1, :]
    xc_ref[...] = _silu(conv)
    halo_ref[...] = full[c:c + SUBLANES, :]

    @pl.when(ci == nc - 1)
    def _():
        tv = t - (nc - 1) * c
        tail = full[SUBLANES + tv - (SSM_CONV - 1):SUBLANES + tv, :]
        nc_ref[...] = jnp.concatenate([tail, jnp.zeros((SUBLANES - (SSM_CONV - 1), SSM_CH), F32)], axis=0)

    e = e_ref[...]
    rows = lax.broadcasted_iota(jnp.int32, (c, 1), 0)
    tri_b = lax.broadcasted_iota(jnp.int32, (c, c), 0) >= lax.broadcasted_iota(jnp.int32, (c, c), 1)
    lane_lo = lax.broadcasted_iota(jnp.int32, (1, LANES), 1) < SSM_P

    dt = jax.nn.softplus(dt_ref[...] + dtb_ref[...])
    if t % c:
        dt = jnp.where(ci * c + rows < t, dt, 0.0)
    da = dt * (-jnp.exp(alog_ref[...]))
    bcum = _prefix_sums(tri_b.astype(BF16), da)
    bcum_t = bcum.T
    dt_t = dt.T
    bl = bcum[c - 1:c, :]
    eb_x = _expand_heads(jnp.exp(bcum), e, 2)
    w_x = _expand_heads(jnp.exp(bl - bcum) * dt, e, 2)
    decay_x = _expand_heads(jnp.broadcast_to(jnp.exp(bl), (SUBLANES, LANES)), e, 3)[0:1, :]

    xs = xc_ref[:, 0:SSM_INNER]
    xdt = xs.astype(BF16)
    xw = (xs * w_x).astype(BF16)
    y = xs * dx_ref[...]
    zg = _silu(zx_ref[:, 0:SSM_INNER])
    for g in range(SSM_GROUPS):
        bg = xc_ref[:, SSM_INNER + g * SSM_N:SSM_INNER + (g + 1) * SSM_N]
        cg = xc_ref[:, SSM_INNER + (SSM_GROUPS + g) * SSM_N:SSM_INNER + (SSM_GROUPS + g + 1) * SSM_N].astype(BF16)
        cb = _nt(cg, bg.astype(BF16))
        st_g = st_ref[:, g * gw:(g + 1) * gw]
        yg = jnp.dot(cg, st_g.astype(BF16), preferred_element_type=F32) * eb_x[:, g * gw:(g + 1) * gw]
        parts = []
        for jp in range(SSM_HPG // 2):
            xpair = xdt[:, g * gw + jp * LANES:g * gw + (jp + 1) * LANES]
            acc = None
            for half in range(2):
                h = g * SSM_HPG + jp * 2 + half
                dec = jnp.exp(jnp.where(tri_b, bcum[:, h:h + 1] - bcum_t[h:h + 1, :], -jnp.inf))
                w = (cb * dec * dt_t[h:h + 1, :]).astype(BF16)
                xh = jnp.where(lane_lo if half == 0 else jnp.logical_not(lane_lo), xpair, 0.0).astype(BF16)
                r = jnp.dot(w, xh, preferred_element_type=F32)
                acc = r if acc is None else acc + r
            parts.append(acc)
        yg = yg + jnp.concatenate(parts, axis=1)
        st_ref[:, g * gw:(g + 1) * gw] = (st_g * decay_x[:, g * gw:(g + 1) * gw]
                                          + jnp.dot(bg.T.astype(BF16), xw[:, g * gw:(g + 1) * gw],
                                                    preferred_element_type=F32))
        yg = (yg + y[:, g * gw:(g + 1) * gw]) * zg[:, g * gw:(g + 1) * gw]
        yg = yg * lax.rsqrt(jnp.mean(yg * yg, axis=-1, keepdims=True) + RMS_EPS) * ng_ref[:, g * gw:(g + 1) * gw]
        o_ref[:, g * gw:(g + 1) * gw] = yg.astype(o_ref.dtype)

    @pl.when(ci == nc - 1)
    def _():
        for blk in range(SSM_INNER // LANES):
            so_ref[blk * LANES:(blk + 1) * LANES, :] = st_ref[:, blk * LANES:(blk + 1) * LANES].T


def _ssd(zx, dt_raw, cs_pad, conv_w, conv_b, expand, dt_bias, a_log, d_x, norm_g, s0, t, c):
    nb, tpad, _ = zx.shape
    nc = tpad // c
    return pl.pallas_call(
        functools.partial(_ssd_body, c=c, t=t, nc=nc),
        grid=(nb, nc),
        in_specs=[pl.BlockSpec((None, c, SSM_MAIN), lambda b, ci: (b, ci, 0)),
                  pl.BlockSpec((None, c, LANES), lambda b, ci: (b, ci, 0)),
                  pl.BlockSpec((None, SUBLANES, SSM_CH), lambda b, ci: (b, 0, 0)),
                  pl.BlockSpec((SSM_CONV, SSM_CH), lambda b, ci: (0, 0)),
                  pl.BlockSpec((1, SSM_CH), lambda b, ci: (0, 0)),
                  pl.BlockSpec((LANES, SSM_INNER), lambda b, ci: (0, 0)),
                  pl.BlockSpec((1, LANES), lambda b, ci: (0, 0)),
                  pl.BlockSpec((1, LANES), lambda b, ci: (0, 0)),
                  pl.BlockSpec((1, SSM_INNER), lambda b, ci: (0, 0)),
                  pl.BlockSpec((1, SSM_INNER), lambda b, ci: (0, 0)),
                  pl.BlockSpec((None, SSM_INNER, SSM_N), lambda b, ci: (b, 0, 0))],
        out_specs=[pl.BlockSpec((None, c, SSM_INNER), lambda b, ci: (b, ci, 0)),
                   pl.BlockSpec((None, SSM_INNER, SSM_N), lambda b, ci: (b, 0, 0)),
                   pl.BlockSpec((None, SUBLANES, SSM_CH), lambda b, ci: (b, 0, 0))],
        out_shape=[jax.ShapeDtypeStruct((nb, tpad, SSM_INNER), BF16),
                   jax.ShapeDtypeStruct((nb, SSM_INNER, SSM_N), F32),
                   jax.ShapeDtypeStruct((nb, SUBLANES, SSM_CH), F32)],
        scratch_shapes=[pltpu.VMEM((SSM_N, SSM_INNER), F32),
                        pltpu.VMEM((c, SSM_CH), F32),
                        pltpu.VMEM((SUBLANES, SSM_CH), F32)],
        compiler_params=_cp("parallel", "arbitrary"),
        name="ssd",
    )(zx, dt_raw, cs_pad, conv_w, conv_b.reshape(1, SSM_CH), expand, dt_bias, a_log, d_x,
      norm_g.reshape(1, SSM_INNER), s0)


def _pad_cols(w, n):
    return jnp.pad(w, ((0, 0), (0, n - w.shape[1])))


def _pad_rows_to(w, n):
    return jnp.pad(w, ((0, n - w.shape[0]), (0, 0)))


def _pad_time(a, tpad):
    return jnp.pad(a, ((0, 0), (0, tpad - a.shape[1]), (0, 0)))


def kernel(x_prompt, x_sample, cache_k, cache_v, cache_idx_k, state_hgrn, state_ssm, state_conv, page_table, p_prompt, p_sample, ln_g, ln_b, ffn_w_gate_up, ffn_w_down, ple_w_proj, ple_w_gate, att_w_in, att_idx_k_norm, att_w_o, hg_w_in, hg_lb_logits, hg_norm_g, hg_w_o, ssm_w_in, ssm_conv_w, ssm_conv_b, ssm_dt_bias, ssm_a_log, ssm_d, ssm_norm_g, ssm_w_o):
    nbp, seq, d = x_prompt.shape
    nbs = x_sample.shape[0]
    mp = nbp * seq
    ms, chunk, tm, tm_ln = SAMPLE_ROWS, SEQ_CHUNK, ROW_TILE, ROW_TILE_LN
    assert nbs <= ms and mp % tm == 0 and seq % chunk == 0

    x_p = x_prompt.reshape(mp, d)
    x_s = jnp.pad(x_sample.reshape(nbs, d), ((0, ms - nbs), (0, 0)))
    xb_p, xb_s = x_p.astype(BF16), x_s.astype(BF16)
    pl_p = p_prompt.reshape(DEPTH, mp, PLE_DIM)
    pl_s = jnp.pad(p_sample.reshape(DEPTH, nbs, PLE_DIM), ((0, 0), (0, ms - nbs), (0, 0)))
    pad_rows = lambda a: jnp.pad(a, ((0, ms - nbs), (0, 0)))
    att_w_in_t = jnp.swapaxes(att_w_in, 1, 2)
    ssm_w_in_t = jnp.swapaxes(ssm_w_in, 1, 2)
    cache_idx_k_t = jnp.swapaxes(cache_idx_k, 2, 3)

    expand = jnp.asarray(np.kron(np.eye(LANES, SSM_HEADS, dtype=np.float32),
                                 np.ones((1, SSM_P), np.float32)), BF16)
    outs = {}

    def ffn_ln(i, which, ln_idx, x_p, xb_p, x_s, xb_s):
        h_p, h_s, wd = _mm_swiglu(xb_p, xb_s, ffn_w_gate_up, ffn_w_down, (i, which), tm, COL_TILE_FFN)
        return _mm_ln(h_p, h_s, wd, x_p, x_s, ln_g[i, ln_idx], ln_b[i, ln_idx], 0.5, tm_ln, "ffn_down_ln")

    for i in range(DEPTH):
        j = i // N_MIXERS
        (x_p, xb_p), (x_s, xb_s) = ffn_ln(i, 0, 0, x_p, xb_p, x_s, xb_s)

        if i % N_MIXERS == 0:
            w_small = _pad_rows_to(att_w_in_t[j, ATT_MAIN:, :], LANES)
            w_o = _cast_w(att_w_o, (j,))
            qw, kvw = ATT_HEADS * ATT_HD, ATT_KV * ATT_HD
            (proj, projb), (proj_s, _) = _proj(xb_p, xb_s, att_w_in_t, (j,), ATT_MAIN, tm, COL_TILE_PROJ, with_bf16=True,
                                               w_is_nk=True, name="att_in")
            proj2, proj2_s = _proj(xb_p, xb_s, w_small, (), LANES, tm, LANES, w_is_nk=True, name="att_in_idx")
            ik, ikb = _ik_norm(proj2, att_idx_k_norm[j], tm)
            ik_s, _ = _ik_norm(proj2_s, att_idx_k_norm[j], ms)
            o_p = _dsa_prompt(proj, projb, proj2, ikb, nbp, seq)
            outs.setdefault("k_p", []).append(proj[:, qw:qw + kvw].reshape(nbp, seq, ATT_KV, ATT_HD))
            outs.setdefault("v_p", []).append(proj[:, qw + kvw:qw + 2 * kvw].reshape(nbp, seq, ATT_KV, ATT_HD))
            outs.setdefault("ik_p", []).append(ik.reshape(nbp, seq, IDX_DIM))
            pr = proj_s[:nbs]
            k_new = pr[:, qw:qw + kvw]
            v_new = pr[:, qw + kvw:qw + 2 * kvw]
            ik_new = ik_s[:nbs]
            scores = _dsa_s_scores(page_table,
                                   pr[:, qw + 2 * kvw:].reshape(nbs, IDX_HEADS, IDX_DIM),
                                   proj2_s[:nbs, IDX_DIM:IDX_DIM + IDX_HEADS].reshape(nbs, IDX_HEADS, 1),
                                   ik_new.reshape(nbs, 1, IDX_DIM), cache_idx_k_t, j)
            idx, meta = _dsa_s_select(scores, page_table)
            o_s = _dsa_s_attend(idx, meta, pr[:, :qw].reshape(nbs, ATT_HEADS, ATT_HD),
                                k_new.reshape(nbs, 1, kvw), v_new.reshape(nbs, 1, kvw), cache_k, cache_v, j)
            o_s = pad_rows(o_s.reshape(nbs, qw)).astype(BF16)
            outs.setdefault("k_s", []).append(k_new.reshape(nbs, 1, ATT_KV, ATT_HD))
            outs.setdefault("v_s", []).append(v_new.reshape(nbs, 1, ATT_KV, ATT_HD))
            outs.setdefault("ik_s", []).append(ik_new.reshape(nbs, 1, IDX_DIM))
        elif i % N_MIXERS == 1:
            w_o = _cast_w(hg_w_o, (j,))
            proj, proj_s = _proj(xb_p, xb_s, hg_w_in, (j,), hg_w_in.shape[-1], tm, COL_TILE_PROJ, name="hg_in")
            s0 = jnp.zeros((nbp, HG_HEADS, HG_DK, HG_DV), F32)
            o_p, s_fin = _hgrn(proj.reshape(nbp, seq, -1), hg_lb_logits, hg_norm_g[j], s0, i, seq, chunk)
            o_p = o_p.reshape(mp, -1)
            outs.setdefault("hg_p", []).append(s_fin)
            pr = _pad_time(proj_s[:nbs].reshape(nbs, 1, -1), chunk)
            o_s, s_fin = _hgrn(pr, hg_lb_logits, hg_norm_g[j], state_hgrn[j], i, 1, chunk)
            o_s = pad_rows(o_s[:, 0, :])
            outs.setdefault("hg_s", []).append(s_fin)
        else:
            w_small = _pad_rows_to(ssm_w_in_t[j, SSM_MAIN:, :], LANES)
            w_o = _cast_w(ssm_w_o, (j,))
            dt_bias = _pad_cols(ssm_dt_bias[j].reshape(1, SSM_HEADS), LANES)
            a_log = _pad_cols(ssm_a_log[j].reshape(1, SSM_HEADS), LANES)
            d_x = jnp.repeat(ssm_d[j], SSM_P).reshape(1, SSM_INNER)
            zx_p, zx_s = _proj(xb_p, xb_s, ssm_w_in_t, (j,), SSM_MAIN, tm, COL_TILE_PROJ, w_is_nk=True, name="ssm_in")
            dtr_p, dtr_s = _proj(xb_p, xb_s, w_small, (), LANES, tm, LANES, w_is_nk=True, name="ssm_in_dt")
            mix = {}
            for name, zx, dtr in (("p", zx_p, dtr_p), ("s", zx_s, dtr_s)):
                if name == "p":
                    nb_, t_ = nbp, seq
                    zx3 = zx.reshape(nbp, seq, -1)
                    dt3 = dtr.reshape(nbp, seq, LANES)
                    cs = jnp.zeros((nbp, SUBLANES, SSM_CH), F32)
                    s0 = jnp.zeros((nbp, SSM_INNER, SSM_N), F32)
                else:
                    nb_, t_ = nbs, 1
                    zx3 = _pad_time(zx[:nbs].reshape(nbs, 1, -1), chunk)
                    dt3 = _pad_time(dtr[:nbs].reshape(nbs, 1, LANES), chunk)
                    cs = jnp.pad(state_conv[j], ((0, 0), (SUBLANES - (SSM_CONV - 1), 0), (0, 0)))
                    s0 = state_ssm[j].reshape(nbs, SSM_INNER, SSM_N)
                y, s_fin, new_conv = _ssd(zx3, dt3, cs, ssm_conv_w[j], ssm_conv_b[j], expand, dt_bias, a_log, d_x,
                                          ssm_norm_g[j], s0, t_, chunk)
                s_fin = s_fin.reshape(nb_, SSM_HEADS, SSM_P, SSM_N)
                new_conv = new_conv[:, :SSM_CONV - 1, :]
                if name == "p":
                    mix[name] = y.reshape(mp, SSM_INNER)
                    outs.setdefault("ssm_p", []).append(s_fin)
                    outs.setdefault("conv_p", []).append(new_conv)
                else:
                    mix[name] = pad_rows(y[:, 0, :])
                    outs.setdefault("ssm_s", []).append(s_fin)
                    outs.setdefault("conv_s", []).append(new_conv)
            o_p, o_s = mix["p"], mix["s"]

        (x_p, xb_p), (x_s, xb_s) = _mm_ln(o_p, o_s, w_o, x_p, x_s, ln_g[i, 1], ln_b[i, 1], 1.0, tm_ln, "mixer_out_ln")
        (x_p, xb_p), (x_s, xb_s) = ffn_ln(i, 1, 2, x_p, xb_p, x_s, xb_s)
        (x_p, xb_p), (x_s, xb_s) = _mm_ple(x_p, xb_p, pl_p[i].astype(BF16), x_s, xb_s, pl_s[i].astype(BF16),
                                           ple_w_gate, ple_w_proj, (i,), tm_ln, COL_TILE_PLE)

    y_prompt = x_p.reshape(nbp, seq, d)
    y_sample = x_s[:nbs].reshape(nbs, 1, d)
    stack = lambda key: jnp.stack(outs[key])
    return (y_prompt, y_sample, stack("k_p"), stack("v_p"), stack("ik_p"), stack("k_s"), stack("v_s"), stack("ik_s"),
            stack("hg_p"), stack("hg_s"), stack("ssm_p"), stack("ssm_s"), stack("conv_p"), stack("conv_s"))
```

```python
import functools
import math

import jax
import jax.numpy as jnp
import numpy as np
from jax import lax
from jax.experimental import pallas as pl
from jax.experimental.pallas import tpu as pltpu

F32 = jnp.float32
BF16 = jnp.bfloat16

D_MODEL = 2048
DEPTH = 4
N_MIXERS = 3
D_FF = 2 * D_MODEL
PLE_DIM = 256
ALPHA = (2 * DEPTH) ** 0.25
LN_EPS = 1e-5
RMS_EPS = 1e-6
PAGE = 128

ATT_HD = 128
ATT_HEADS = 16
ATT_KV = 4
ATT_GROUP = ATT_HEADS // ATT_KV
IDX_HEADS = 16
IDX_DIM = 64
IDX_W_SCALE = (IDX_HEADS ** -0.5) * (IDX_DIM ** -0.5)
TOPK_MAX = 256
ATT_SCALE = ATT_HD ** -0.5
LOG2E = math.log2(math.e)
ATT_MAIN = ATT_HEADS * ATT_HD + 2 * ATT_KV * ATT_HD + IDX_HEADS * IDX_DIM

HG_HEADS = 16
HG_DK = 128
HG_DV = 128

SSM_INNER = 2 * D_MODEL
SSM_P = 64
SSM_HEADS = SSM_INNER // SSM_P
SSM_GROUPS = 8
SSM_HPG = SSM_HEADS // SSM_GROUPS
SSM_N = 128
SSM_CONV = 4
SSM_CH = SSM_INNER + 2 * SSM_GROUPS * SSM_N
SSM_MAIN = SSM_INNER + SSM_CH

LANES = 128
SUBLANES = 8
VMEM_CAPACITY_BYTES = 64 * 1024 * 1024
VMEM_LIMIT_BYTES = VMEM_CAPACITY_BYTES * 7 // 8

ROW_TILE = 1024
ROW_TILE_LN = 512
COL_TILE_PROJ = 1024
COL_TILE_FFN = 512
COL_TILE_PLE = 1024
SEQ_CHUNK = 128
DSA_Q_ROWS = 256
DSA_KEY_CHUNK = 512
SAMPLE_ROWS = 16

INT_MIN = np.int32(-2 ** 31)
NEG_BIG = -1e30


def _cp(*sem):
    return pltpu.CompilerParams(dimension_semantics=sem, vmem_limit_bytes=VMEM_LIMIT_BYTES)


def _nt(a, b):
    return lax.dot_general(a, b, (((1,), (1,)), ((), ())), preferred_element_type=F32)


def _tn(a, b):
    return lax.dot_general(a, b, (((0,), (0,)), ((), ())), preferred_element_type=F32)


def _silu(x):
    return x * jax.nn.sigmoid(x)


def _prefix_sums(tri, v):
    out = None
    rest = v
    for _ in range(3):
        part = rest.astype(BF16)
        rest = rest - part.astype(F32)
        d = jnp.dot(tri, part, preferred_element_type=F32)
        out = d if out is None else out + d
    return out


def _wspec(k, tn, idx, col_block):
    lead = (None,) * len(idx)
    return pl.BlockSpec(lead + (k, tn), lambda n, i: idx + (0, col_block(n)))


def _proj_body(x_ref, xs_ref, w_ref, *rest, with_bf16, w_is_nk):
    wb_ref = rest[-1]
    outs, outs_s = (rest[0:2], rest[2:4]) if with_bf16 else (rest[0:1], rest[1:2])
    mm = _nt if w_is_nk else functools.partial(jnp.dot, preferred_element_type=F32)

    def emit(refs, acc):
        refs[0][...] = acc
        if with_bf16:
            refs[1][...] = acc.astype(BF16)

    @pl.when(pl.program_id(1) == 0)
    def _():
        wb_ref[...] = w_ref[...].astype(BF16)
        emit(outs_s, mm(xs_ref[...], wb_ref[...]))

    emit(outs, mm(x_ref[...], wb_ref[...]))


def _proj(xb, xsb, w, idx, n_cols, tm, tn, with_bf16=False, w_is_nk=False, name="proj"):
    m, k = xb.shape
    ms = xsb.shape[0]
    dts = (F32, BF16) if with_bf16 else (F32,)
    if w_is_nk:
        w_spec = pl.BlockSpec((None,) * len(idx) + (tn, k), lambda n, i: idx + (n, 0))
        w_tile = (tn, k)
    else:
        w_spec = _wspec(k, tn, idx, lambda n: n)
        w_tile = (k, tn)
    res = pl.pallas_call(
        functools.partial(_proj_body, with_bf16=with_bf16, w_is_nk=w_is_nk),
        grid=(n_cols // tn, m // tm),
        in_specs=[pl.BlockSpec((tm, k), lambda n, i: (i, 0)),
                  pl.BlockSpec((ms, k), lambda n, i: (0, 0)),
                  w_spec],
        out_specs=[pl.BlockSpec((tm, tn), lambda n, i: (i, n)) for _ in dts]
                  + [pl.BlockSpec((ms, tn), lambda n, i: (0, n)) for _ in dts],
        out_shape=[jax.ShapeDtypeStruct((m, n_cols), dt) for dt in dts]
                  + [jax.ShapeDtypeStruct((ms, n_cols), dt) for dt in dts],
        scratch_shapes=[pltpu.VMEM(w_tile, BF16)],
        compiler_params=_cp("parallel", "arbitrary"),
        name=name,
    )(xb, xsb, w)
    nd = len(dts)
    return (res[:nd], res[nd:]) if with_bf16 else (res[0], res[1])


def _swiglu_body(x_ref, xs_ref, wg_ref, wu_ref, wd_ref, o_ref, os_ref, wdb_ref, wgub_ref):
    tn = o_ref.shape[1]

    def swiglu(x):
        gu = jnp.dot(x, wgub_ref[...], preferred_element_type=F32)
        return (_silu(gu[:, :tn]) * gu[:, tn:]).astype(BF16)

    @pl.when(pl.program_id(1) == 0)
    def _():
        wgub_ref[:, :tn] = wg_ref[...].astype(BF16)
        wgub_ref[:, tn:] = wu_ref[...].astype(BF16)
        wdb_ref[...] = wd_ref[...].astype(BF16)
        os_ref[...] = swiglu(xs_ref[...])

    o_ref[...] = swiglu(x_ref[...])


def _mm_swiglu(xb, xsb, wgu, wd, idx, tm, tn):
    m, k = xb.shape
    ms = xsb.shape[0]
    f = wgu.shape[-1] // 2
    nj = f // tn
    n_out = wd.shape[-1]
    lead = (None,) * len(idx)
    return pl.pallas_call(
        _swiglu_body,
        grid=(nj, m // tm),
        in_specs=[pl.BlockSpec((tm, k), lambda n, i: (i, 0)),
                  pl.BlockSpec((ms, k), lambda n, i: (0, 0)),
                  _wspec(k, tn, idx, lambda n: n),
                  _wspec(k, tn, idx, lambda n: n + nj),
                  pl.BlockSpec(lead + (f // nj, n_out), lambda n, i: idx + (n, 0))],
        out_specs=[pl.BlockSpec((tm, tn), lambda n, i: (i, n)),
                   pl.BlockSpec((ms, tn), lambda n, i: (0, n)),
                   pl.BlockSpec((f // nj, n_out), lambda n, i: (n, 0))],
        out_shape=[jax.ShapeDtypeStruct((m, f), BF16), jax.ShapeDtypeStruct((ms, f), BF16),
                   jax.ShapeDtypeStruct((f, n_out), BF16)],
        scratch_shapes=[pltpu.VMEM((k, 2 * tn), BF16)],
        compiler_params=_cp("parallel", "arbitrary"),
        name="ffn_up",
    )(xb, xsb, wgu, wgu, wd)


def _cast_body(w_ref, o_ref):
    o_ref[...] = w_ref[...].astype(BF16)


def _cast_w(w, idx, tk=512):
    k, n = w.shape[-2:]
    lead = (None,) * len(idx)
    return pl.pallas_call(
        _cast_body,
        grid=(k // tk,),
        in_specs=[pl.BlockSpec(lead + (tk, n), lambda i: idx + (i, 0))],
        out_specs=pl.BlockSpec((tk, n), lambda i: (i, 0)),
        out_shape=jax.ShapeDtypeStruct((k, n), BF16),
        compiler_params=_cp("parallel"),
        name="cast_w",
    )(w)


LN_SPLIT = 2


def _mm_ln_body(a_ref, as_ref, w_ref, r_ref, rs_ref, g_ref, b_ref, o_ref, ob_ref, os_ref, osb_ref, *, scale, tm):
    g = g_ref[...]
    b = b_ref[...]

    def ln_rows(a, r, o, ob, sl):
        y = ALPHA * r[sl, :] + scale * jnp.dot(a[sl, :], w_ref[...], preferred_element_type=F32)
        mu = jnp.mean(y, axis=-1, keepdims=True)
        yc = y - mu
        var = jnp.mean(yc * yc, axis=-1, keepdims=True)
        out = yc * lax.rsqrt(var + LN_EPS) * g + b
        o[sl, :] = out
        ob[sl, :] = out.astype(BF16)

    @pl.when(pl.program_id(0) == 0)
    def _():
        ln_rows(as_ref, rs_ref, os_ref, osb_ref, pl.ds(0, as_ref.shape[0]))

    rows = tm // LN_SPLIT
    for r in range(0, tm, rows):
        ln_rows(a_ref, r_ref, o_ref, ob_ref, pl.ds(r, rows))


def _mm_ln(ab, asb, wb, res, res_s, g, b, scale, tm, name):
    m, kdim = ab.shape
    ms = asb.shape[0]
    n = wb.shape[1]
    res = pl.pallas_call(
        functools.partial(_mm_ln_body, scale=scale, tm=tm),
        grid=(m // tm,),
        in_specs=[pl.BlockSpec((tm, kdim), lambda i: (i, 0)),
                  pl.BlockSpec((ms, kdim), lambda i: (0, 0)),
                  pl.BlockSpec((kdim, n), lambda i: (0, 0), pipeline_mode=pl.Buffered(1)),
                  pl.BlockSpec((tm, n), lambda i: (i, 0)),
                  pl.BlockSpec((ms, n), lambda i: (0, 0)),
                  pl.BlockSpec((1, n), lambda i: (0, 0)),
                  pl.BlockSpec((1, n), lambda i: (0, 0))],
        out_specs=[pl.BlockSpec((tm, n), lambda i: (i, 0)),
                   pl.BlockSpec((tm, n), lambda i: (i, 0)),
                   pl.BlockSpec((ms, n), lambda i: (0, 0)),
                   pl.BlockSpec((ms, n), lambda i: (0, 0))],
        out_shape=[jax.ShapeDtypeStruct((m, n), F32), jax.ShapeDtypeStruct((m, n), BF16),
                   jax.ShapeDtypeStruct((ms, n), F32), jax.ShapeDtypeStruct((ms, n), BF16)],
        compiler_params=_cp("arbitrary"),
        name=name,
    )(ab, asb, wb, res, res_s, g.reshape(1, n), b.reshape(1, n))
    return res[:2], res[2:]


def _ple_body(xb_ref, p_ref, x_ref, xsb_ref, ps_ref, xs_ref, wg_ref, wp_ref, o_ref, ob_ref, os_ref, osb_ref,
              wgb_ref, wpb_ref):
    def ple(xb, p, x, o, ob):
        gate = jax.nn.sigmoid(jnp.dot(xb[...], wgb_ref[...], preferred_element_type=F32))
        proj = jnp.dot(p[...], wpb_ref[...], preferred_element_type=F32)
        out = x[...] + gate * proj
        o[...] = out
        ob[...] = out.astype(BF16)

    @pl.when(pl.program_id(1) == 0)
    def _():
        wgb_ref[...] = wg_ref[...].astype(BF16)
        wpb_ref[...] = wp_ref[...].astype(BF16)
        ple(xsb_ref, ps_ref, xs_ref, os_ref, osb_ref)

    ple(xb_ref, p_ref, x_ref, o_ref, ob_ref)


def _mm_ple(x32, xb, pb, xs32, xsb, psb, wg, wp, idx, tm, tn):
    m, d = xb.shape
    ms = xsb.shape[0]
    pd = pb.shape[1]
    res = pl.pallas_call(
        _ple_body,
        grid=(d // tn, m // tm),
        in_specs=[pl.BlockSpec((tm, d), lambda n, i: (i, 0)),
                  pl.BlockSpec((tm, pd), lambda n, i: (i, 0)),
                  pl.BlockSpec((tm, tn), lambda n, i: (i, n)),
                  pl.BlockSpec((ms, d), lambda n, i: (0, 0)),
                  pl.BlockSpec((ms, pd), lambda n, i: (0, 0)),
                  pl.BlockSpec((ms, tn), lambda n, i: (0, n)),
                  _wspec(d, tn, idx, lambda n: n),
                  _wspec(pd, tn, idx, lambda n: n)],
        out_specs=[pl.BlockSpec((tm, tn), lambda n, i: (i, n)),
                   pl.BlockSpec((tm, tn), lambda n, i: (i, n)),
                   pl.BlockSpec((ms, tn), lambda n, i: (0, n)),
                   pl.BlockSpec((ms, tn), lambda n, i: (0, n))],
        out_shape=[jax.ShapeDtypeStruct((m, d), F32), jax.ShapeDtypeStruct((m, d), BF16),
                   jax.ShapeDtypeStruct((ms, d), F32), jax.ShapeDtypeStruct((ms, d), BF16)],
        scratch_shapes=[pltpu.VMEM((d, tn), BF16), pltpu.VMEM((pd, tn), BF16)],
        compiler_params=_cp("parallel", "arbitrary"),
        name="ple",
    )(xb, pb, x32, xsb, psb, xs32, wg, wp)
    return res[:2], res[2:]


def _ik_norm_body(p_ref, g_ref, o_ref, ob_ref):
    x = p_ref[...][:, :IDX_DIM]
    mu = jnp.mean(x, axis=-1, keepdims=True)
    xc = x - mu
    out = xc * lax.rsqrt(jnp.mean(xc * xc, axis=-1, keepdims=True) + LN_EPS) * g_ref[...]
    o_ref[...] = out
    ob_ref[...] = out.astype(BF16)


def _ik_norm(proj2, ik_g, tm):
    m = proj2.shape[0]
    return pl.pallas_call(
        _ik_norm_body,
        grid=(m // tm,),
        in_specs=[pl.BlockSpec((tm, LANES), lambda i: (i, 0)),
                  pl.BlockSpec((1, IDX_DIM), lambda i: (0, 0))],
        out_specs=[pl.BlockSpec((tm, IDX_DIM), lambda i: (i, 0)),
                   pl.BlockSpec((tm, IDX_DIM), lambda i: (i, 0))],
        out_shape=[jax.ShapeDtypeStruct((m, IDX_DIM), F32), jax.ShapeDtypeStruct((m, IDX_DIM), BF16)],
        compiler_params=_cp("parallel"),
        name="idx_k_norm",
    )(proj2, ik_g.reshape(1, IDX_DIM))


def _order_key(x):
    bits = pltpu.bitcast(x, jnp.int32)
    return jnp.where(bits < 0, bits ^ jnp.int32(0x7FFFFFFF), bits)


_RADIX_BITS = [INT_MIN] + [np.int32(1 << s) for s in range(30, -1, -1)]


RADIX_ROWS = 128
KNORM_ROWS = 512
SUM_FLOOR = 2.0 ** -100


def _dsa_prompt_body(q_ref, iq_ref, iw_ref, k_ref, v_ref, ik_ref, o_ref, keys_ref, knorm_ref, *, qb, kc, topk):
    i = pl.program_id(1)
    nck = ((i + 1) * qb + kc - 1) // kc
    row_pos = i * qb + lax.broadcasted_iota(jnp.int32, (qb, 1), 0)
    iq = iq_ref[...]
    iw = iw_ref[...][:, IDX_DIM:IDX_DIM + IDX_HEADS] * IDX_W_SCALE
    iq_h = [iq[:, h * IDX_DIM:(h + 1) * IDX_DIM] for h in range(IDX_HEADS)]
    iw_h = [iw[:, h:h + 1] for h in range(IDX_HEADS)]
    col0 = lax.broadcasted_iota(jnp.int32, (1, kc), 1)

    def score_chunk(c, carry):
        off = pl.multiple_of(c * kc, kc)
        ikc = ik_ref[pl.ds(off, kc), :]
        sc = jnp.zeros((qb, kc), F32)
        for h in range(IDX_HEADS):
            sc = sc + iw_h[h] * jnp.maximum(_nt(iq_h[h], ikc), 0.0)
        key = jnp.where(col0 + off <= row_pos, _order_key(sc), INT_MIN)
        keys_ref[:, pl.ds(off, kc)] = key
        return carry

    lax.fori_loop(0, nck, score_chunk, 0)

    nblk = qb // RADIX_ROWS
    los = [jnp.full((RADIX_ROWS, 1), INT_MIN, jnp.int32) for _ in range(nblk)]
    n_ge = [jnp.zeros((RADIX_ROWS, 1), F32) for _ in range(nblk)]
    for bit in _RADIX_BITS:
        cands = [lo + bit for lo in los]
        accs = []
        for blk in range(nblk):
            cand_b = jnp.broadcast_to(cands[blk], (RADIX_ROWS, LANES))

            def count_chunk(c, acc, cand_b=cand_b, r0=blk * RADIX_ROWS):
                off = pl.multiple_of(c * kc, kc)
                for t in range(kc // LANES):
                    tile = keys_ref[r0:r0 + RADIX_ROWS, pl.ds(off + t * LANES, LANES)]
                    acc = acc + jnp.where(tile >= cand_b, 1.0, 0.0)
                return acc

            accs.append(lax.fori_loop(0, nck, count_chunk, jnp.zeros((RADIX_ROWS, LANES), F32)))
        cnts = [jnp.sum(acc, axis=-1, keepdims=True) for acc in accs]
        n_ge = [jnp.where(cnts[blk] >= topk, cnts[blk], n_ge[blk]) for blk in range(nblk)]
        los = [jnp.where(cnts[blk] >= topk, cands[blk], los[blk]) for blk in range(nblk)]
    thr = jnp.maximum(jnp.concatenate(los, axis=0), INT_MIN + 1)

    @pl.when(jnp.max(jnp.concatenate(n_ge, axis=0)) > topk)
    def _():
        lane = lax.broadcasted_iota(jnp.int32, (1, LANES), 1)

        def count_rows(pred):
            def body(c, acc):
                off = pl.multiple_of(c * kc, kc)
                for t in range(kc // LANES):
                    tile = keys_ref[:, pl.ds(off + t * LANES, LANES)]
                    acc = acc + jnp.where(pred(tile, off + t * LANES + lane), 1.0, 0.0)
                return acc
            return jnp.sum(lax.fori_loop(0, nck, body, jnp.zeros((qb, LANES), F32)), axis=-1, keepdims=True)

        keep = topk - count_rows(lambda tile, pos: tile > thr)
        last_short = jnp.full((qb, 1), -1, jnp.int32)
        step = keys_ref.shape[1] // 2
        while step >= 1:
            cand = last_short + step
            short = count_rows(lambda tile, pos, cand=cand: (tile == thr) & (pos <= cand)) < keep
            last_short = jnp.where(short, cand, last_short)
            step //= 2
        cut = last_short + 1

        def demote(c, carry):
            off = pl.multiple_of(c * kc, kc)
            for t in range(kc // LANES):
                sl = pl.ds(off + t * LANES, LANES)
                tile = keys_ref[:, sl]
                keys_ref[:, sl] = jnp.where((tile == thr) & (off + t * LANES + lane > cut), thr - 1, tile)
            return carry

        lax.fori_loop(0, nck, demote, 0)

    @pl.when(i == 0)
    def _():
        for g in range(ATT_KV):
            best = jnp.zeros((1, 1), F32)
            for r0 in range(0, k_ref.shape[0], KNORM_ROWS):
                kk = k_ref[r0:r0 + KNORM_ROWS, g * ATT_HD:(g + 1) * ATT_HD].astype(F32)
                best = jnp.maximum(best, jnp.max(jnp.sum(kk * kk, axis=-1, keepdims=True), axis=0, keepdims=True))
            knorm_ref[g:g + 1, :] = jnp.broadcast_to(jnp.sqrt(best), (1, LANES))

    def store(g, out):
        for j in range(ATT_GROUP):
            h = g * ATT_GROUP + j
            o_ref[:, h * ATT_HD:(h + 1) * ATT_HD] = out[j].astype(o_ref.dtype)

    for g in range(ATT_KV):
        qg = jnp.concatenate(
            [q_ref[:, (g * ATT_GROUP + j) * ATT_HD:(g * ATT_GROUP + j + 1) * ATT_HD] for j in range(ATT_GROUP)], axis=0)
        qg = (qg * (ATT_SCALE * LOG2E)).astype(BF16)
        q32 = qg.astype(F32)
        shift = (jnp.sqrt(jnp.sum(q32 * q32, axis=-1, keepdims=True)) * knorm_ref[g:g + 1, 0:1]).reshape(
            ATT_GROUP, qb, 1)

        def chunk_operands(c, g=g):
            off = pl.multiple_of(c * kc, kc)
            kch = k_ref[pl.ds(off, kc), g * ATT_HD:(g + 1) * ATT_HD]
            vch = v_ref[pl.ds(off, kc), g * ATT_HD:(g + 1) * ATT_HD]
            sel = (keys_ref[:, pl.ds(off, kc)] >= thr)[None]
            return kch, vch, sel

        def fixed_chunk(c, carry, qg=qg, shift=shift):
            l, acc = carry
            kch, vch, sel = chunk_operands(c)
            p = jnp.exp2(jnp.where(sel, _nt(qg, kch).reshape(ATT_GROUP, qb, kc) - shift, -jnp.inf))
            l = l + jnp.sum(p, axis=-1, keepdims=True)
            pv = jnp.dot(p.reshape(ATT_GROUP * qb, kc).astype(BF16), vch, preferred_element_type=F32)
            return l, acc + pv.reshape(ATT_GROUP, qb, ATT_HD)

        l, acc = lax.fori_loop(0, nck, fixed_chunk, (jnp.zeros((ATT_GROUP, qb, 1), F32),
                                                     jnp.zeros((ATT_GROUP, qb, ATT_HD), F32)))
        healthy = jnp.min(l) > SUM_FLOOR

        @pl.when(healthy)
        def _(g=g, l=l, acc=acc):
            store(g, acc / l)

        @pl.when(jnp.logical_not(healthy))
        def _(g=g, qg=qg):
            def running_chunk(c, carry):
                m, l, acc = carry
                kch, vch, sel = chunk_operands(c)
                s = jnp.where(sel, _nt(qg, kch).reshape(ATT_GROUP, qb, kc), -jnp.inf)
                m_new = jnp.maximum(m, jnp.max(s, axis=-1, keepdims=True))
                p = jnp.exp2(s - m_new)
                a = jnp.exp2(m - m_new)
                l = a * l + jnp.sum(p, axis=-1, keepdims=True)
                pv = jnp.dot(p.reshape(ATT_GROUP * qb, kc).astype(BF16), vch, preferred_element_type=F32)
                return m_new, l, a * acc + pv.reshape(ATT_GROUP, qb, ATT_HD)

            init = (jnp.full((ATT_GROUP, qb, 1), NEG_BIG, F32), jnp.zeros((ATT_GROUP, qb, 1), F32),
                    jnp.zeros((ATT_GROUP, qb, ATT_HD), F32))
            _, l, acc = lax.fori_loop(0, nck, running_chunk, init)
            store(g, acc / l)


def _dsa_prompt(proj, projb, proj2, ikb, nb, s, qb=DSA_Q_ROWS, kc=DSA_KEY_CHUNK):
    nq = s // qb
    topk = min(TOPK_MAX, s // 4)
    qw = ATT_HEADS * ATT_HD
    kvw = ATT_KV * ATT_HD
    iqw = IDX_HEADS * IDX_DIM
    return pl.pallas_call(
        functools.partial(_dsa_prompt_body, qb=qb, kc=kc, topk=topk),
        grid=(nb, nq),
        in_specs=[pl.BlockSpec((qb, qw), lambda b, i: (b * nq + i, 0)),
                  pl.BlockSpec((qb, iqw), lambda b, i: (b * nq + i, (qw + 2 * kvw) // iqw)),
                  pl.BlockSpec((qb, LANES), lambda b, i: (b * nq + i, 0)),
                  pl.BlockSpec((s, kvw), lambda b, i: (b, qw // kvw)),
                  pl.BlockSpec((s, kvw), lambda b, i: (b, qw // kvw + 1)),
                  pl.BlockSpec((s, IDX_DIM), lambda b, i: (b, 0))],
        out_specs=pl.BlockSpec((qb, qw), lambda b, i: (b * nq + i, 0)),
        out_shape=jax.ShapeDtypeStruct((nb * s, qw), BF16),
        scratch_shapes=[pltpu.VMEM((qb, s), jnp.int32), pltpu.VMEM((SUBLANES, LANES), F32)],
        compiler_params=_cp("parallel", "arbitrary"),
        name="dsa_prompt",
    )(proj, projb, proj2, projb, projb, ikb)


SCORE_PAGES = 32


def _dsa_s_scores_body(pt_ref, iq_ref, iw_ref, ikn_ref, *rest, n_pages):
    page_refs, o_ref = rest[:SCORE_PAGES], rest[SCORE_PAGES]
    p = pl.program_id(1)
    iq = iq_ref[...].astype(BF16)
    iw = iw_ref[...] * IDX_W_SCALE

    @pl.when(p == 0)
    def _():
        own = jnp.broadcast_to(ikn_ref[...], (SUBLANES, IDX_DIM)).astype(BF16)
        d = jnp.maximum(_nt(iq, own), 0.0)
        sc = jnp.sum(iw * d, axis=0, keepdims=True)
        o_ref[n_pages:n_pages + SUBLANES, :] = jnp.broadcast_to(sc[:, 0:1], (SUBLANES, LANES))

    for r in range(SCORE_PAGES):
        page_t = page_refs[r][...].astype(BF16)
        d = jnp.maximum(jnp.dot(iq, page_t, preferred_element_type=F32), 0.0)
        o_ref[pl.ds(p * SCORE_PAGES + r, 1), :] = jnp.sum(iw * d, axis=0, keepdims=True)


def _dsa_s_scores(page_table, iq, iw, ik_new, cache_ik_t, layer):
    nb, n_pages = page_table.shape
    page_spec = lambda r: pl.BlockSpec((None, None, IDX_DIM, PAGE),
                                       lambda b, p, pt: (layer, pt[b, p * SCORE_PAGES + r], 0, 0))
    return pl.pallas_call(
        functools.partial(_dsa_s_scores_body, n_pages=n_pages),
        grid_spec=pltpu.PrefetchScalarGridSpec(
            num_scalar_prefetch=1,
            grid=(nb, n_pages // SCORE_PAGES),
            in_specs=[pl.BlockSpec((None, IDX_HEADS, IDX_DIM), lambda b, p, pt: (b, 0, 0)),
                      pl.BlockSpec((None, IDX_HEADS, 1), lambda b, p, pt: (b, 0, 0)),
                      pl.BlockSpec((None, 1, IDX_DIM), lambda b, p, pt: (b, 0, 0))]
                     + [page_spec(r) for r in range(SCORE_PAGES)],
            out_specs=pl.BlockSpec((None, n_pages + SUBLANES, LANES), lambda b, p, pt: (b, 0, 0)),
        ),
        out_shape=jax.ShapeDtypeStruct((nb, n_pages + SUBLANES, LANES), F32),
        compiler_params=_cp("parallel", "arbitrary"),
        name="dsa_sample_scores",
    )(page_table, iq, iw, ik_new, *([cache_ik_t] * SCORE_PAGES))


def _dsa_s_select_body(sc_ref, ptc_ref, idx_ref, meta_ref, rank_ref, phys_ref, *, n_pages, topk):
    shape = (n_pages + SUBLANES, LANES)
    rows = lax.broadcasted_iota(jnp.int32, shape, 0)
    cols = lax.broadcasted_iota(jnp.int32, shape, 1)
    live = (rows < n_pages) | ((rows == n_pages) & (cols == 0))
    keys = jnp.where(live, _order_key(sc_ref[...]), INT_MIN)
    lo = jnp.full((1, 1), INT_MIN, jnp.int32)
    for bit in _RADIX_BITS:
        cand = lo + bit
        cnt = jnp.sum(jnp.sum(jnp.where(keys >= cand, 1.0, 0.0), axis=0, keepdims=True), axis=1, keepdims=True)
        lo = jnp.where(cnt >= topk, cand, lo)
    thr = jnp.maximum(lo, INT_MIN + 1)
    above = keys > thr
    tied = keys == thr
    n_above = jnp.sum(jnp.sum(jnp.where(above, 1.0, 0.0), axis=0, keepdims=True), axis=1, keepdims=True)
    keep_tied = topk - n_above

    r_i = lax.broadcasted_iota(jnp.int32, (LANES, LANES), 0)
    c_i = lax.broadcasted_iota(jnp.int32, (LANES, LANES), 1)
    ones_le = jnp.where(r_i <= c_i, 1.0, 0.0).astype(BF16)
    ones_gt = jnp.where(r_i > c_i, 1.0, 0.0).astype(BF16)

    def prefix(mask):
        inside = jnp.dot(jnp.where(mask, 1.0, 0.0).astype(BF16), ones_le, preferred_element_type=F32)
        tot = jnp.broadcast_to(inside[:, LANES - 1:LANES], (n_pages, LANES)).astype(BF16)
        return inside, jnp.dot(ones_gt, tot, preferred_element_type=F32)

    tied_in, tied_before = prefix(tied[:n_pages, :])
    n_tied_cache = tied_before[n_pages - 1:n_pages, 0:1] + tied_in[n_pages - 1:n_pages, LANES - 1:LANES]
    sel_c = above[:n_pages, :] | (tied[:n_pages, :] & (tied_in + tied_before <= keep_tied))
    own_key = keys[n_pages:n_pages + 1, 0:1]
    own = jnp.where((own_key > thr) | ((own_key == thr) & (n_tied_cache < keep_tied)), 1, 0)
    within, before = prefix(sel_c)
    rank_ref[...] = jnp.where(sel_c, (within + before).astype(jnp.int32) - 1, -1)
    phys_ref[...] = (ptc_ref[...] * PAGE + c_i[:n_pages, :]).astype(F32)
    n_sel = (before[n_pages - 1:n_pages, 0:1] + within[n_pages - 1:n_pages, LANES - 1:LANES]).astype(jnp.int32)

    slot = lax.broadcasted_iota(jnp.int32, (topk, 1), 0)

    def place(r, acc):
        return acc + jnp.where(rank_ref[pl.ds(r, 1), :] == slot, phys_ref[pl.ds(r, 1), :], 0.0)

    acc = lax.fori_loop(0, n_pages, place, jnp.zeros((topk, LANES), F32), unroll=8)
    idx_ref[...] = jnp.sum(acc, axis=-1, keepdims=True).astype(jnp.int32)
    mrow = lax.broadcasted_iota(jnp.int32, (SUBLANES, LANES), 0)
    meta_ref[...] = jnp.where(mrow == 0, jnp.minimum(n_sel, topk), own)


def _dsa_s_select(scores, page_table):
    nb, n_pages = page_table.shape
    assert n_pages == LANES
    topk = min(TOPK_MAX, (n_pages * PAGE + 1) // 4)
    idx, meta = pl.pallas_call(
        functools.partial(_dsa_s_select_body, n_pages=n_pages, topk=topk),
        grid=(nb,),
        in_specs=[pl.BlockSpec((None, n_pages + SUBLANES, LANES), lambda b: (b, 0, 0)),
                  pl.BlockSpec((None, n_pages, 1), lambda b: (b, 0, 0))],
        out_specs=[pl.BlockSpec((None, topk, 1), lambda b: (b, 0, 0)),
                   pl.BlockSpec((None, SUBLANES, LANES), lambda b: (b, 0, 0))],
        out_shape=[jax.ShapeDtypeStruct((nb, topk, 1), jnp.int32),
                   jax.ShapeDtypeStruct((nb, SUBLANES, LANES), jnp.int32)],
        scratch_shapes=[pltpu.VMEM((n_pages, LANES), jnp.int32), pltpu.VMEM((n_pages, LANES), F32)],
        compiler_params=_cp("parallel"),
        name="dsa_sample_select",
    )(scores, page_table.reshape(nb, n_pages, 1))
    return idx.reshape(nb, topk), meta[:, 0:2, 0]


def _dsa_s_attend_body(idx_ref, meta_ref, q_ref, kn_ref, vn_ref, ck_ref, cv_ref, o_ref, kbuf, vbuf, sem,
                       *, layer, topk):
    b = pl.program_id(0)
    nb = pl.num_programs(0)
    slot = b % 2

    def row_copies(tok, buf_slot, j):
        row = idx_ref[tok, j]
        page, off = row // PAGE, row % PAGE
        dst = pl.ds(j * ATT_KV, ATT_KV)
        return (pltpu.make_async_copy(ck_ref.at[layer, page, off], kbuf.at[buf_slot, dst, :], sem.at[buf_slot, 0]),
                pltpu.make_async_copy(cv_ref.at[layer, page, off], vbuf.at[buf_slot, dst, :], sem.at[buf_slot, 1]))

    def start_gather(tok, buf_slot):
        def body(j, carry):
            for cp in row_copies(tok, buf_slot, j):
                cp.start()
            return carry
        lax.fori_loop(0, topk, body, 0)

    @pl.when(b == 0)
    def _():
        start_gather(0, 0)

    @pl.when(b + 1 < nb)
    def _():
        start_gather(b + 1, 1 - slot)

    def wait_body(j, carry):
        for cp in row_copies(b, slot, j):
            cp.wait()
        return carry

    lax.fori_loop(0, topk, wait_body, 0)

    n_sel = meta_ref[b, 0]
    own = meta_ref[b, 1] > 0
    qb = q_ref[...].astype(BF16)
    ncol = topk * ATT_KV
    col = lax.broadcasted_iota(jnp.int32, (ATT_HEADS, ncol), 1)
    head = lax.broadcasted_iota(jnp.int32, (ATT_HEADS, ncol), 0)
    valid = (col % ATT_KV == head // ATT_GROUP) & (col // ATT_KV < n_sel)
    s = jnp.where(valid, _nt(qb, kbuf[slot].astype(BF16)) * ATT_SCALE, NEG_BIG)
    kn = kn_ref[...].astype(BF16).astype(F32)
    vn = vn_ref[...].astype(BF16).astype(F32)
    expand = lambda a: jnp.concatenate(
        [jnp.broadcast_to(a[:, g * ATT_HD:(g + 1) * ATT_HD], (ATT_GROUP, ATT_HD)) for g in range(ATT_KV)], axis=0)
    s_own = jnp.where(own, jnp.sum(qb.astype(F32) * expand(kn), axis=-1, keepdims=True) * ATT_SCALE, NEG_BIG)
    m = jnp.maximum(jnp.max(s, axis=-1, keepdims=True), s_own)
    p = jnp.where(valid, jnp.exp(s - m), 0.0)
    p_own = jnp.where(own, jnp.exp(s_own - m), 0.0)
    l = jnp.sum(p, axis=-1, keepdims=True) + p_own
    acc = (jnp.dot(p.astype(BF16), vbuf[slot].astype(BF16), preferred_element_type=F32)
           + p_own.astype(BF16).astype(F32) * expand(vn))
    o_ref[...] = acc / l


def _dsa_s_attend(idx, meta, q, k_new, v_new, cache_k, cache_v, layer):
    nb, topk = idx.shape
    kvw = ATT_KV * ATT_HD
    return pl.pallas_call(
        functools.partial(_dsa_s_attend_body, layer=layer, topk=topk),
        grid_spec=pltpu.PrefetchScalarGridSpec(
            num_scalar_prefetch=2,
            grid=(nb,),
            in_specs=[pl.BlockSpec((None, ATT_HEADS, ATT_HD), lambda b, idx, meta: (b, 0, 0)),
                      pl.BlockSpec((None, 1, kvw), lambda b, idx, meta: (b, 0, 0)),
                      pl.BlockSpec((None, 1, kvw), lambda b, idx, meta: (b, 0, 0)),
                      pl.BlockSpec(memory_space=pl.ANY),
                      pl.BlockSpec(memory_space=pl.ANY)],
            out_specs=pl.BlockSpec((None, ATT_HEADS, ATT_HD), lambda b, idx, meta: (b, 0, 0)),
            scratch_shapes=[pltpu.VMEM((2, topk * ATT_KV, ATT_HD), F32),
                            pltpu.VMEM((2, topk * ATT_KV, ATT_HD), F32),
                            pltpu.SemaphoreType.DMA((2, 2))],
        ),
        out_shape=jax.ShapeDtypeStruct((nb, ATT_HEADS, ATT_HD), F32),
        compiler_params=_cp("arbitrary"),
        name="dsa_sample_attend",
    )(idx, meta, q, k_new, v_new, cache_k, cache_v)


HG_SUB = 16


HG_HB = 4
HG_EXP_LIMIT = 80.0


def _hgrn_body(lbl_ref, q_ref, f_ref, i_ref, g_ref, ng_ref, s0_ref, o_ref, so_ref, st_ref, *, layer, c, t, nc):
    ci = pl.program_id(2)
    mid = c // 2

    @pl.when(ci == 0)
    def _():
        for hh in range(HG_HB):
            st_ref[hh] = s0_ref[hh].T

    logits = lbl_ref[...]
    e = jnp.exp(logits - jnp.max(logits, axis=0, keepdims=True))
    soft = e / jnp.sum(e, axis=0, keepdims=True)
    lb_all = jnp.zeros((1, HG_HB * HG_DK), F32)
    for r in range(1, layer + 1):
        lb_all = lb_all + soft[r:r + 1, :]
    ng = ng_ref[...]

    def gates(hh, sl, row0):
        cs = slice(hh * HG_DK, (hh + 1) * HG_DK)
        lb = lb_all[:, cs]
        fg = lb + (1.0 - lb) * jax.nn.sigmoid(f_ref[sl, cs])
        lf = jnp.log(fg)
        kk = 1.0 - fg
        if t % c:
            n = lf.shape[0]
            valid = (ci * c + row0 + lax.broadcasted_iota(jnp.int32, (n, 1), 0)) < t
            lf = jnp.where(valid, lf, 0.0)
            kk = jnp.where(valid, kk, 0.0)
        return lf, kk

    def finish(hh, sl, o):
        cs = slice(hh * HG_DV, (hh + 1) * HG_DV)
        on = o * lax.rsqrt(jnp.mean(o * o, axis=-1, keepdims=True) + RMS_EPS) * ng
        o_ref[sl, cs] = (on * _silu(g_ref[sl, cs])).astype(o_ref.dtype)

    tri_c = lax.broadcasted_iota(jnp.int32, (c, c), 0) >= lax.broadcasted_iota(jnp.int32, (c, c), 1)
    full = pl.ds(0, c)
    lfs, kks, bs = [], [], []
    safe = None
    for hh in range(HG_HB):
        lf, kk = gates(hh, full, 0)
        b = _prefix_sums(tri_c.astype(BF16), lf)
        bm = b[mid - 1:mid, :]
        ok = jnp.min(jnp.minimum(bm, b[c - 1:c, :] - bm)) > -HG_EXP_LIMIT
        safe = ok if safe is None else jnp.logical_and(safe, ok)
        lfs.append(lf)
        kks.append(kk)
        bs.append(b)

    @pl.when(safe)
    def _():
        for hh in range(HG_HB):
            cs = slice(hh * HG_DK, (hh + 1) * HG_DK)
            b, kk = bs[hh], kks[hh]
            bm = b[mid - 1:mid, :]
            bl = b[c - 1:c, :]
            qq = _silu(q_ref[:, cs])
            vv = i_ref[:, cs].astype(BF16)
            att = jnp.where(tri_c, _nt((qq * jnp.exp(b - bm)).astype(BF16), (kk * jnp.exp(bm - b)).astype(BF16)), 0.0)
            st = st_ref[hh]
            o = (jnp.dot(att.astype(BF16), vv, preferred_element_type=F32)
                 + _nt((qq * jnp.exp(b)).astype(BF16), st.astype(BF16)))
            st_ref[hh] = st * jnp.exp(bl) + _tn(vv, (kk * jnp.exp(bl - b)).astype(BF16))
            finish(hh, full, o)

    @pl.when(jnp.logical_not(safe))
    def _():
        rows = lax.broadcasted_iota(jnp.int32, (HG_SUB, 1), 0)
        tri = (lax.broadcasted_iota(jnp.int32, (HG_SUB, HG_SUB), 0)
               >= lax.broadcasted_iota(jnp.int32, (HG_SUB, HG_SUB), 1)).astype(F32)
        for hh in range(HG_HB):
            cs = slice(hh * HG_DK, (hh + 1) * HG_DK)

            def sub_block(sb, carry, hh=hh, cs=cs):
                row0 = pl.multiple_of(sb * HG_SUB, HG_SUB)
                sl = pl.ds(row0, HG_SUB)
                lf, kk = gates(hh, sl, row0)
                qq = _silu(q_ref[sl, cs])
                vv = i_ref[sl, cs]
                b = _prefix_sums(tri.astype(BF16), lf)
                st = st_ref[hh]
                o = _nt((qq * jnp.exp(b)).astype(BF16), st.astype(BF16))
                for s in range(HG_SUB):
                    dec = jnp.exp(jnp.where(rows >= s, b - b[s:s + 1, :], -jnp.inf))
                    att = jnp.sum(qq * dec * kk[s:s + 1, :], axis=-1, keepdims=True)
                    o = o + att * vv[s:s + 1, :]
                bl = b[HG_SUB - 1:HG_SUB, :]
                st_ref[hh] = st * jnp.exp(bl) + _tn(vv.astype(BF16), (kk * jnp.exp(bl - b)).astype(BF16))
                finish(hh, sl, o)
                return carry

            lax.fori_loop(0, c // HG_SUB, sub_block, 0)

    @pl.when(ci == nc - 1)
    def _():
        for hh in range(HG_HB):
            so_ref[hh] = st_ref[hh].T


def _hgrn(proj, lb_logits, norm_g, s0, layer, t, c):
    nb, tpad, _ = proj.shape
    nc = tpad // c
    nhb = HG_HEADS // HG_HB
    w = HG_HB * HG_DK
    return pl.pallas_call(
        functools.partial(_hgrn_body, layer=layer, c=c, t=t, nc=nc),
        grid=(nb, nhb, nc),
        in_specs=[pl.BlockSpec((DEPTH, w), lambda b, h, ci: (0, h)),
                  pl.BlockSpec((None, c, w), lambda b, h, ci: (b, ci, h)),
                  pl.BlockSpec((None, c, w), lambda b, h, ci: (b, ci, nhb + h)),
                  pl.BlockSpec((None, c, w), lambda b, h, ci: (b, ci, 2 * nhb + h)),
                  pl.BlockSpec((None, c, w), lambda b, h, ci: (b, ci, 3 * nhb + h)),
                  pl.BlockSpec((1, HG_DV), lambda b, h, ci: (0, 0)),
                  pl.BlockSpec((None, HG_HB, HG_DK, HG_DV), lambda b, h, ci: (b, h, 0, 0))],
        out_specs=[pl.BlockSpec((None, c, w), lambda b, h, ci: (b, ci, h)),
                   pl.BlockSpec((None, HG_HB, HG_DK, HG_DV), lambda b, h, ci: (b, h, 0, 0))],
        out_shape=[jax.ShapeDtypeStruct((nb, tpad, HG_HEADS * HG_DV), BF16),
                   jax.ShapeDtypeStruct((nb, HG_HEADS, HG_DK, HG_DV), F32)],
        scratch_shapes=[pltpu.VMEM((HG_HB, HG_DV, HG_DK), F32)],
        compiler_params=_cp("parallel", "parallel", "arbitrary"),
        name="hgrn2",
    )(lb_logits, proj, proj, proj, proj, norm_g.reshape(1, HG_DV), s0)


def _expand_heads(v, e, terms):
    out = None
    rest = v
    for _ in range(terms):
        part = rest.astype(BF16)
        rest = rest - part.astype(F32)
        d = jnp.dot(part, e, preferred_element_type=F32)
        out = d if out is None else out + d
    return out


def _ssd_body(zx_ref, dt_ref, cs_ref, cw_ref, cbias_ref, e_ref, dtb_ref, alog_ref, dx_ref, ng_ref, s0_ref,
              o_ref, so_ref, nc_ref, st_ref, xc_ref, halo_ref, *, c, t, nc):
    ci = pl.program_id(1)
    gw = SSM_HPG * SSM_P

    @pl.when(ci == 0)
    def _():
        for blk in range(SSM_INNER // LANES):
            st_ref[:, blk * LANES:(blk + 1) * LANES] = s0_ref[blk * LANES:(blk + 1) * LANES, :].T
        halo_ref[...] = cs_ref[...]

    full = jnp.concatenate([halo_ref[...], zx_ref[:, SSM_INNER:SSM_INNER + SSM_CH]], axis=0)
    conv = cbias_ref[...]
    for k in range(SSM_CONV):
        lo = SUBLANES - (SSM_CONV - 1) + k
        conv = conv + full[lo:lo + c, :] * cw_ref[k:k + 1, :]
    xc_ref[...] = _silu(conv)
    halo_ref[...] = full[c:c + SUBLANES, :]

    @pl.when(ci == nc - 1)
    def _():
        tv = t - (nc - 1) * c
        tail = full[SUBLANES + tv - (SSM_CONV - 1):SUBLANES + tv, :]
        nc_ref[...] = jnp.concatenate([tail, jnp.zeros((SUBLANES - (SSM_CONV - 1), SSM_CH), F32)], axis=0)

    e = e_ref[...]
    rows = lax.broadcasted_iota(jnp.int32, (c, 1), 0)
    tri_b = lax.broadcasted_iota(jnp.int32, (c, c), 0) >= lax.broadcasted_iota(jnp.int32, (c, c), 1)
    lane_lo = lax.broadcasted_iota(jnp.int32, (1, LANES), 1) < SSM_P

    dt = jax.nn.softplus(dt_ref[...] + dtb_ref[...])
    if t % c:
        dt = jnp.where(ci * c + rows < t, dt, 0.0)
    da = dt * (-jnp.exp(alog_ref[...]))
    bcum = _prefix_sums(tri_b.astype(BF16), da)
    bcum_t = bcum.T
    dt_t = dt.T
    bl = bcum[c - 1:c, :]
    eb_x = _expand_heads(jnp.exp(bcum), e, 2)
    w_x = _expand_heads(jnp.exp(bl - bcum) * dt, e, 2)
    decay_x = _expand_heads(jnp.broadcast_to(jnp.exp(bl), (SUBLANES, LANES)), e, 3)[0:1, :]

    xs = xc_ref[:, 0:SSM_INNER]
    xdt = xs.astype(BF16)
    xw = (xs * w_x).astype(BF16)
    y = xs * dx_ref[...]
    zg = _silu(zx_ref[:, 0:SSM_INNER])
    for g in range(SSM_GROUPS):
        bg = xc_ref[:, SSM_INNER + g * SSM_N:SSM_INNER + (g + 1) * SSM_N]
        cg = xc_ref[:, SSM_INNER + (SSM_GROUPS + g) * SSM_N:SSM_INNER + (SSM_GROUPS + g + 1) * SSM_N].astype(BF16)
        cb = _nt(cg, bg.astype(BF16))
        st_g = st_ref[:, g * gw:(g + 1) * gw]
        yg = jnp.dot(cg, st_g.astype(BF16), preferred_element_type=F32) * eb_x[:, g * gw:(g + 1) * gw]
        parts = []
        for jp in range(SSM_HPG // 2):
            xpair = xdt[:, g * gw + jp * LANES:g * gw + (jp + 1) * LANES]
            acc = None
            for half in range(2):
                h = g * SSM_HPG + jp * 2 + half
                dec = jnp.exp(jnp.where(tri_b, bcum[:, h:h + 1] - bcum_t[h:h + 1, :], -jnp.inf))
                w = (cb * dec * dt_t[h:h + 1, :]).astype(BF16)
                xh = jnp.where(lane_lo if half == 0 else jnp.logical_not(lane_lo), xpair, 0.0).astype(BF16)
                r = jnp.dot(w, xh, preferred_element_type=F32)
                acc = r if acc is None else acc + r
            parts.append(acc)
        yg = yg + jnp.concatenate(parts, axis=1)
        st_ref[:, g * gw:(g + 1) * gw] = (st_g * decay_x[:, g * gw:(g + 1) * gw]
                                          + jnp.dot(bg.T.astype(BF16), xw[:, g * gw:(g + 1) * gw],
                                                    preferred_element_type=F32))
        yg = (yg + y[:, g * gw:(g + 1) * gw]) * zg[:, g * gw:(g + 1) * gw]
        yg = yg * lax.rsqrt(jnp.mean(yg * yg, axis=-1, keepdims=True) + RMS_EPS) * ng_ref[:, g * gw:(g + 1) * gw]
        o_ref[:, g * gw:(g + 1) * gw] = yg.astype(o_ref.dtype)

    @pl.when(ci == nc - 1)
    def _():
        for blk in range(SSM_INNER // LANES):
            so_ref[blk * LANES:(blk + 1) * LANES, :] = st_ref[:, blk * LANES:(blk + 1) * LANES].T


def _ssd(zx, dt_raw, cs_pad, conv_w, conv_b, expand, dt_bias, a_log, d_x, norm_g, s0, t, c):
    nb, tpad, _ = zx.shape
    nc = tpad // c
    return pl.pallas_call(
        functools.partial(_ssd_body, c=c, t=t, nc=nc),
        grid=(nb, nc),
        in_specs=[pl.BlockSpec((None, c, SSM_MAIN), lambda b, ci: (b, ci, 0)),
                  pl.BlockSpec((None, c, LANES), lambda b, ci: (b, ci, 0)),
                  pl.BlockSpec((None, SUBLANES, SSM_CH), lambda b, ci: (b, 0, 0)),
                  pl.BlockSpec((SSM_CONV, SSM_CH), lambda b, ci: (0, 0)),
                  pl.BlockSpec((1, SSM_CH), lambda b, ci: (0, 0)),
                  pl.BlockSpec((LANES, SSM_INNER), lambda b, ci: (0, 0)),
                  pl.BlockSpec((1, LANES), lambda b, ci: (0, 0)),
                  pl.BlockSpec((1, LANES), lambda b, ci: (0, 0)),
                  pl.BlockSpec((1, SSM_INNER), lambda b, ci: (0, 0)),
                  pl.BlockSpec((1, SSM_INNER), lambda b, ci: (0, 0)),
                  pl.BlockSpec((None, SSM_INNER, SSM_N), lambda b, ci: (b, 0, 0))],
        out_specs=[pl.BlockSpec((None, c, SSM_INNER), lambda b, ci: (b, ci, 0)),
                   pl.BlockSpec((None, SSM_INNER, SSM_N), lambda b, ci: (b, 0, 0)),
                   pl.BlockSpec((None, SUBLANES, SSM_CH), lambda b, ci: (b, 0, 0))],
        out_shape=[jax.ShapeDtypeStruct((nb, tpad, SSM_INNER), BF16),
                   jax.ShapeDtypeStruct((nb, SSM_INNER, SSM_N), F32),
                   jax.ShapeDtypeStruct((nb, SUBLANES, SSM_CH), F32)],
        scratch_shapes=[pltpu.VMEM((SSM_N, SSM_INNER), F32),
                        pltpu.VMEM((c, SSM_CH), F32),
                        pltpu.VMEM((SUBLANES, SSM_CH), F32)],
        compiler_params=_cp("parallel", "arbitrary"),
        name="ssd",
    )(zx, dt_raw, cs_pad, conv_w, conv_b.reshape(1, SSM_CH), expand, dt_bias, a_log, d_x,
      norm_g.reshape(1, SSM_INNER), s0)


def _pad_cols(w, n):
    return jnp.pad(w, ((0, 0), (0, n - w.shape[1])))


def _pad_rows_to(w, n):
    return jnp.pad(w, ((0, n - w.shape[0]), (0, 0)))


def _pad_time(a, tpad):
    return jnp.pad(a, ((0, 0), (0, tpad - a.shape[1]), (0, 0)))


def kernel(x_prompt, x_sample, cache_k, cache_v, cache_idx_k, state_hgrn, state_ssm, state_conv, page_table, p_prompt, p_sample, ln_g, ln_b, ffn_w_gate_up, ffn_w_down, ple_w_proj, ple_w_gate, att_w_in, att_idx_k_norm, att_w_o, hg_w_in, hg_lb_logits, hg_norm_g, hg_w_o, ssm_w_in, ssm_conv_w, ssm_conv_b, ssm_dt_bias, ssm_a_log, ssm_d, ssm_norm_g, ssm_w_o):
    nbp, seq, d = x_prompt.shape
    nbs = x_sample.shape[0]
    mp = nbp * seq
    ms, chunk, tm, tm_ln = SAMPLE_ROWS, SEQ_CHUNK, ROW_TILE, ROW_TILE_LN
    assert nbs <= ms and mp % tm == 0 and seq % chunk == 0

    x_p = x_prompt.reshape(mp, d)
    x_s = jnp.pad(x_sample.reshape(nbs, d), ((0, ms - nbs), (0, 0)))
    xb_p, xb_s = x_p.astype(BF16), x_s.astype(BF16)
    pl_p = p_prompt.reshape(DEPTH, mp, PLE_DIM)
    pl_s = jnp.pad(p_sample.reshape(DEPTH, nbs, PLE_DIM), ((0, 0), (0, ms - nbs), (0, 0)))
    pad_rows = lambda a: jnp.pad(a, ((0, ms - nbs), (0, 0)))
    att_w_in_t = jnp.swapaxes(att_w_in, 1, 2)
    ssm_w_in_t = jnp.swapaxes(ssm_w_in, 1, 2)
    cache_idx_k_t = jnp.swapaxes(cache_idx_k, 2, 3)

    expand = jnp.asarray(np.kron(np.eye(LANES, SSM_HEADS, dtype=np.float32),
                                 np.ones((1, SSM_P), np.float32)), BF16)
    outs = {}

    def ffn_ln(i, which, ln_idx, x_p, xb_p, x_s, xb_s):
        h_p, h_s, wd = _mm_swiglu(xb_p, xb_s, ffn_w_gate_up, ffn_w_down, (i, which), tm, COL_TILE_FFN)
        return _mm_ln(h_p, h_s, wd, x_p, x_s, ln_g[i, ln_idx], ln_b[i, ln_idx], 0.5, tm_ln, "ffn_down_ln")

    for i in range(DEPTH):
        j = i // N_MIXERS
        (x_p, xb_p), (x_s, xb_s) = ffn_ln(i, 0, 0, x_p, xb_p, x_s, xb_s)

        if i % N_MIXERS == 0:
            w_small = _pad_rows_to(att_w_in_t[j, ATT_MAIN:, :], LANES)
            w_o = _cast_w(att_w_o, (j,))
            qw, kvw = ATT_HEADS * ATT_HD, ATT_KV * ATT_HD
            (proj, projb), (proj_s, _) = _proj(xb_p, xb_s, att_w_in_t, (j,), ATT_MAIN, tm, COL_TILE_PROJ, with_bf16=True,
                                               w_is_nk=True, name="att_in")
            proj2, proj2_s = _proj(xb_p, xb_s, w_small, (), LANES, tm, LANES, w_is_nk=True, name="att_in_idx")
            ik, ikb = _ik_norm(proj2, att_idx_k_norm[j], tm)
            ik_s, _ = _ik_norm(proj2_s, att_idx_k_norm[j], ms)
            o_p = _dsa_prompt(proj, projb, proj2, ikb, nbp, seq)
            outs.setdefault("k_p", []).append(proj[:, qw:qw + kvw].reshape(nbp, seq, ATT_KV, ATT_HD))
            outs.setdefault("v_p", []).append(proj[:, qw + kvw:qw + 2 * kvw].reshape(nbp, seq, ATT_KV, ATT_HD))
            outs.setdefault("ik_p", []).append(ik.reshape(nbp, seq, IDX_DIM))
            pr = proj_s[:nbs]
            k_new = pr[:, qw:qw + kvw]
            v_new = pr[:, qw + kvw:qw + 2 * kvw]
            ik_new = ik_s[:nbs]
            scores = _dsa_s_scores(page_table,
                                   pr[:, qw + 2 * kvw:].reshape(nbs, IDX_HEADS, IDX_DIM),
                                   proj2_s[:nbs, IDX_DIM:IDX_DIM + IDX_HEADS].reshape(nbs, IDX_HEADS, 1),
                                   ik_new.reshape(nbs, 1, IDX_DIM), cache_idx_k_t, j)
            idx, meta = _dsa_s_select(scores, page_table)
            o_s = _dsa_s_attend(idx, meta, pr[:, :qw].reshape(nbs, ATT_HEADS, ATT_HD),
                                k_new.reshape(nbs, 1, kvw), v_new.reshape(nbs, 1, kvw), cache_k, cache_v, j)
            o_s = pad_rows(o_s.reshape(nbs, qw)).astype(BF16)
            outs.setdefault("k_s", []).append(k_new.reshape(nbs, 1, ATT_KV, ATT_HD))
            outs.setdefault("v_s", []).append(v_new.reshape(nbs, 1, ATT_KV, ATT_HD))
            outs.setdefault("ik_s", []).append(ik_new.reshape(nbs, 1, IDX_DIM))
        elif i % N_MIXERS == 1:
            w_o = _cast_w(hg_w_o, (j,))
            proj, proj_s = _proj(xb_p, xb_s, hg_w_in, (j,), hg_w_in.shape[-1], tm, COL_TILE_PROJ, name="hg_in")
            s0 = jnp.zeros((nbp, HG_HEADS, HG_DK, HG_DV), F32)
            o_p, s_fin = _hgrn(proj.reshape(nbp, seq, -1), hg_lb_logits, hg_norm_g[j], s0, i, seq, chunk)
            o_p = o_p.reshape(mp, -1)
            outs.setdefault("hg_p", []).append(s_fin)
            pr = _pad_time(proj_s[:nbs].reshape(nbs, 1, -1), HG_SUB)
            o_s, s_fin = _hgrn(pr, hg_lb_logits, hg_norm_g[j], state_hgrn[j], i, 1, HG_SUB)
            o_s = pad_rows(o_s[:, 0, :])
            outs.setdefault("hg_s", []).append(s_fin)
        else:
            w_small = _pad_rows_to(ssm_w_in_t[j, SSM_MAIN:, :], LANES)
            w_o = _cast_w(ssm_w_o, (j,))
            dt_bias = _pad_cols(ssm_dt_bias[j].reshape(1, SSM_HEADS), LANES)
            a_log = _pad_cols(ssm_a_log[j].reshape(1, SSM_HEADS), LANES)
            d_x = jnp.repeat(ssm_d[j], SSM_P).reshape(1, SSM_INNER)
            zx_p, zx_s = _proj(xb_p, xb_s, ssm_w_in_t, (j,), SSM_MAIN, tm, COL_TILE_PROJ, w_is_nk=True, name="ssm_in")
            dtr_p, dtr_s = _proj(xb_p, xb_s, w_small, (), LANES, tm, LANES, w_is_nk=True, name="ssm_in_dt")
            mix = {}
            for name, zx, dtr in (("p", zx_p, dtr_p), ("s", zx_s, dtr_s)):
                if name == "p":
                    nb_, t_ = nbp, seq
                    zx3 = zx.reshape(nbp, seq, -1)
                    dt3 = dtr.reshape(nbp, seq, LANES)
                    cs = jnp.zeros((nbp, SUBLANES, SSM_CH), F32)
                    s0 = jnp.zeros((nbp, SSM_INNER, SSM_N), F32)
                else:
                    nb_, t_ = nbs, 1
                    zx3 = _pad_time(zx[:nbs].reshape(nbs, 1, -1), chunk)
                    dt3 = _pad_time(dtr[:nbs].reshape(nbs, 1, LANES), chunk)
                    cs = jnp.pad(state_conv[j], ((0, 0), (SUBLANES - (SSM_CONV - 1), 0), (0, 0)))
                    s0 = state_ssm[j].reshape(nbs, SSM_INNER, SSM_N)
                y, s_fin, new_conv = _ssd(zx3, dt3, cs, ssm_conv_w[j], ssm_conv_b[j], expand, dt_bias, a_log, d_x,
                                          ssm_norm_g[j], s0, t_, chunk)
                s_fin = s_fin.reshape(nb_, SSM_HEADS, SSM_P, SSM_N)
                new_conv = new_conv[:, :SSM_CONV - 1, :]
                if name == "p":
                    mix[name] = y.reshape(mp, SSM_INNER)
                    outs.setdefault("ssm_p", []).append(s_fin)
                    outs.setdefault("conv_p", []).append(new_conv)
                else:
                    mix[name] = pad_rows(y[:, 0, :])
                    outs.setdefault("ssm_s", []).append(s_fin)
                    outs.setdefault("conv_s", []).append(new_conv)
            o_p, o_s = mix["p"], mix["s"]

        (x_p, xb_p), (x_s, xb_s) = _mm_ln(o_p, o_s, w_o, x_p, x_s, ln_g[i, 1], ln_b[i, 1], 1.0, tm_ln, "mixer_out_ln")
        (x_p, xb_p), (x_s, xb_s) = ffn_ln(i, 1, 2, x_p, xb_p, x_s, xb_s)
        (x_p, xb_p), (x_s, xb_s) = _mm_ple(x_p, xb_p, pl_p[i].astype(BF16), x_s, xb_s, pl_s[i].astype(BF16),
                                           ple_w_gate, ple_w_proj, (i,), tm_ln, COL_TILE_PLE)

    y_prompt = x_p.reshape(nbp, seq, d)
    y_sample = x_s[:nbs].reshape(nbs, 1, d)
    stack = lambda key: jnp.stack(outs[key])
    return (y_prompt, y_sample, stack("k_p"), stack("v_p"), stack("ik_p"), stack("k_s"), stack("v_s"), stack("ik_s"),
            stack("hg_p"), stack("hg_s"), stack("ssm_p"), stack("ssm_s"), stack("conv_p"), stack("conv_s"))
```
